```python
import jax, jax.numpy as jnp
from jax import lax
import numpy as np

D_MODEL = 2048
BATCH = 2
SEQ = 4096
DEPTH = 1

MIX_WIDTH = D_MODEL
NSA_HEADS = 8
NSA_KV_HEADS = 2
HEAD_DIM = MIX_WIDTH // 2 // NSA_HEADS
NSA_WIDTH = NSA_HEADS * HEAD_DIM
GQA_GROUP = NSA_HEADS // NSA_KV_HEADS
KV_WIDTH = NSA_KV_HEADS * HEAD_DIM
ROPE_DIM = HEAD_DIM // 4
ROPE_THETA = 500000.0
CMP_LEN = 32
CMP_STRIDE = 16
CMP_HIDDEN = 2 * HEAD_DIM
SEL_LEN = 64
SEL_TOPK = 16
WINDOW = 512
Q_BLOCK = 128
N_GATES = 3
LRU_WIDTH = MIX_WIDTH - NSA_WIDTH
LRU_BLOCKS = 8
LRU_BLOCK_DIM = LRU_WIDTH // LRU_BLOCKS
CONV_WIDTH = 4
LRU_C = 8.0
D_FF = 5632
PLE_DIM = 256
IN_WIDTH = NSA_WIDTH + 6 * KV_WIDTH + NSA_HEADS * N_GATES + 2 * LRU_WIDTH
RMS_EPS = 1e-6
NEG = -1e30
SEL_FORCE = 1e4

kernel_name = 'hybrid_nsa_rglru_macaron_block'


def rmsnorm(x, g):
    xf = x.astype(jnp.float32)
    y = xf * lax.rsqrt(jnp.mean(xf * xf, axis=-1, keepdims=True) + RMS_EPS)
    return (y * g.astype(jnp.float32)).astype(x.dtype)


def swiglu(x, w_gate, w_up, w_down):
    return (jax.nn.silu(x @ w_gate) * (x @ w_up)) @ w_down


def partial_rope(x, positions):
    half = ROPE_DIM // 2
    inv_freq = ROPE_THETA ** (-jnp.arange(half, dtype=jnp.float32) / half)
    ang = positions.astype(jnp.float32)[..., None] * inv_freq
    cos = jnp.cos(ang)[:, :, None, :]
    sin = jnp.sin(ang)[:, :, None, :]
    xr = x[..., :ROPE_DIM].astype(jnp.float32)
    x1, x2 = xr[..., :half], xr[..., half:]
    rot = jnp.concatenate([x1 * cos - x2 * sin, x2 * cos + x1 * sin], axis=-1)
    return jnp.concatenate([rot.astype(x.dtype), x[..., ROPE_DIM:]], axis=-1)


def masked_softmax(s, mask):
    s = jnp.where(mask, s.astype(jnp.float32), NEG)
    return jax.nn.softmax(s, axis=-1) * mask


def compress(kv, pos_emb, w1, w2):
    B, T, G, dh = kv.shape
    n_cmp = (T - CMP_LEN) // CMP_STRIDE + 1
    idx = jnp.arange(n_cmp)[:, None] * CMP_STRIDE + jnp.arange(CMP_LEN)[None, :]
    blocks = kv[:, idx] + pos_emb[:, None, :]
    blocks = blocks.transpose(0, 1, 3, 2, 4).reshape(B, n_cmp, G, CMP_LEN * dh)
    return jax.nn.gelu(blocks @ w1) @ w2


def nsa_attention(q, k_cmp, v_cmp, k_sel, v_sel, k_win, v_win, gates,
                  pos_k, pos_v, k_w1, k_w2, v_w1, v_w2):
    B, T = q.shape[:2]
    G, R, dh = NSA_KV_HEADS, GQA_GROUP, HEAD_DIM
    scale = HEAD_DIM ** -0.5
    kc = compress(k_cmp, pos_k, k_w1, k_w2)
    vc = compress(v_cmp, pos_v, v_w1, v_w2)
    n_cmp = kc.shape[1]
    cmp_start = jnp.arange(n_cmp) * CMP_STRIDE
    cmp_end = cmp_start + CMP_LEN - 1
    n_sel = T // SEL_LEN
    top_k = min(SEL_TOPK, n_sel)
    sel_start = jnp.arange(n_sel) * SEL_LEN
    overlap = ((cmp_start[:, None] < sel_start[None, :] + SEL_LEN)
               & (cmp_end[:, None] >= sel_start[None, :])).astype(jnp.float32)
    kb = k_sel.reshape(B, n_sel, SEL_LEN, G, dh).transpose(0, 3, 1, 2, 4)
    vb = v_sel.reshape(B, n_sel, SEL_LEN, G, dh).transpose(0, 3, 1, 2, 4)
    kw = jnp.pad(k_win, ((0, 0), (WINDOW, 0), (0, 0), (0, 0)))
    vw = jnp.pad(v_win, ((0, 0), (WINDOW, 0), (0, 0), (0, 0)))
    qg = q.reshape(B, T, G, R, dh)
    bi = jnp.arange(B)[:, None, None, None]
    gi = jnp.arange(G)[None, :, None, None]
    blk = jnp.arange(n_sel)

    def query_block(i):
        t0 = i * Q_BLOCK
        qb = lax.dynamic_slice_in_dim(qg, t0, Q_BLOCK, 1)
        gb = lax.dynamic_slice_in_dim(gates, t0, Q_BLOCK, 1).reshape(B, Q_BLOCK, G, R, N_GATES)
        t = t0 + jnp.arange(Q_BLOCK)
        s = jnp.einsum('btgrd,bcgd->bgrtc', qb, kc) * scale
        pc = masked_softmax(s, cmp_end[None, :] <= t[:, None])
        o_cmp = jnp.einsum('bgrtc,bcgd->btgrd', pc.astype(vc.dtype), vc)
        imp = jnp.einsum('bgrtc,cs->bgts', pc, overlap)
        forced = (blk[None, :] == (t // SEL_LEN)[:, None]) | (blk[None, :] == 0)
        valid = sel_start[None, :] <= t[:, None]
        imp = jnp.where(forced, SEL_FORCE, jnp.where(valid, imp, -SEL_FORCE))
        _, idx = lax.top_k(imp, top_k)
        ks = kb[bi, gi, idx]
        vs = vb[bi, gi, idx]
        kpos = idx[..., None] * SEL_LEN + jnp.arange(SEL_LEN)
        ms = (kpos <= t[:, None, None]).reshape(B, G, 1, Q_BLOCK, top_k * SEL_LEN)
        s = jnp.einsum('btgrd,bgtkjd->bgrtkj', qb, ks).reshape(B, G, R, Q_BLOCK, top_k * SEL_LEN) * scale
        ps = masked_softmax(s, ms).reshape(B, G, R, Q_BLOCK, top_k, SEL_LEN)
        o_sel = jnp.einsum('bgrtkj,bgtkjd->btgrd', ps.astype(vs.dtype), vs)
        kwb = lax.dynamic_slice_in_dim(kw, t0, WINDOW + Q_BLOCK, 1)
        vwb = lax.dynamic_slice_in_dim(vw, t0, WINDOW + Q_BLOCK, 1)
        kp = t0 - WINDOW + jnp.arange(WINDOW + Q_BLOCK)
        mw = (kp[None, :] <= t[:, None]) & (kp[None, :] > t[:, None] - WINDOW) & (kp[None, :] >= 0)
        s = jnp.einsum('btgrd,bsgd->bgrts', qb, kwb) * scale
        pw = masked_softmax(s, mw)
        o_win = jnp.einsum('bgrts,bsgd->btgrd', pw.astype(vwb.dtype), vwb)
        o = gb[..., 0, None] * o_cmp + gb[..., 1, None] * o_sel + gb[..., 2, None] * o_win
        return o.reshape(B, Q_BLOCK, NSA_WIDTH)

    out = lax.map(query_block, jnp.arange(T // Q_BLOCK))
    return out.transpose(1, 0, 2, 3).reshape(B, T, NSA_WIDTH)


def rglru_block(xb, yb, conv_w, conv_b, w_a, b_a, w_i, b_i, lam):
    B, T, C = xb.shape
    xc = lax.conv_general_dilated(xb, conv_w[:, None, :], window_strides=(1,),
                                  padding=[(CONV_WIDTH - 1, 0)],
                                  dimension_numbers=('NWC', 'WIO', 'NWC'),
                                  feature_group_count=C) + conv_b
    xh = xc.reshape(B, T, LRU_BLOCKS, LRU_BLOCK_DIM)
    r = jax.nn.sigmoid(jnp.einsum('bthi,hij->bthj', xh, w_a).reshape(B, T, C) + b_a)
    ig = jax.nn.sigmoid(jnp.einsum('bthi,hij->bthj', xh, w_i).reshape(B, T, C) + b_i)
    log_a = -LRU_C * r.astype(jnp.float32) * jax.nn.softplus(-lam.astype(jnp.float32))
    a = jnp.exp(log_a)
    u = jnp.sqrt(-jnp.expm1(2.0 * log_a)) * (ig * xc).astype(jnp.float32)

    def combine(left, right):
        a1, b1 = left
        a2, b2 = right
        return a1 * a2, a2 * b1 + b2

    _, h = lax.associative_scan(combine, (a, u), axis=1)
    return h.astype(xb.dtype) * jax.nn.gelu(yb)


def setup_inputs(seed: int = 0) -> dict:
    key = jax.random.key(seed)
    ks = iter(jax.random.split(key, 64))
    L = DEPTH
    f32 = jnp.float32

    def nrm(shape, fan_in):
        return jax.random.normal(next(ks), shape, f32) * fan_in ** -0.5

    def gain(shape):
        return 1.0 + 0.02 * jax.random.normal(next(ks), shape, f32)

    def small(shape, s=0.01):
        return s * jax.random.normal(next(ks), shape, f32)

    x = jax.random.normal(next(ks), (BATCH, SEQ, D_MODEL), f32)
    p = jax.random.normal(next(ks), (DEPTH, BATCH, SEQ, PLE_DIM), f32)
    positions = (jax.random.randint(next(ks), (BATCH, 1), 0, 1024) + jnp.arange(SEQ)[None, :]).astype(jnp.int32)
    d = {'x': x, 'p': p, 'positions': positions}
    d['ff1_pre_g'] = gain((L, D_MODEL))
    d['ff1_post_g'] = gain((L, D_MODEL))
    d['ff1_w_gate'] = nrm((L, D_MODEL, D_FF), D_MODEL)
    d['ff1_w_up'] = nrm((L, D_MODEL, D_FF), D_MODEL)
    d['ff1_w_down'] = nrm((L, D_FF, D_MODEL), D_FF)
    d['mix_pre_g'] = gain((L, D_MODEL))
    d['mix_post_g'] = gain((L, D_MODEL))
    d['w_in'] = nrm((L, D_MODEL, IN_WIDTH), D_MODEL)
    d['cmp_pos_k'] = small((L, CMP_LEN, HEAD_DIM), 0.1)
    d['cmp_pos_v'] = small((L, CMP_LEN, HEAD_DIM), 0.1)
    d['cmp_k_w1'] = nrm((L, CMP_LEN * HEAD_DIM, CMP_HIDDEN), CMP_LEN * HEAD_DIM)
    d['cmp_k_w2'] = nrm((L, CMP_HIDDEN, HEAD_DIM), CMP_HIDDEN)
    d['cmp_v_w1'] = nrm((L, CMP_LEN * HEAD_DIM, CMP_HIDDEN), CMP_LEN * HEAD_DIM)
    d['cmp_v_w2'] = nrm((L, CMP_HIDDEN, HEAD_DIM), CMP_HIDDEN)
    d['nsa_gate_b'] = small((L, NSA_HEADS, N_GATES))
    d['conv_w'] = nrm((L, CONV_WIDTH, LRU_WIDTH), CONV_WIDTH)
    d['conv_b'] = small((L, LRU_WIDTH))
    d['rg_w_a'] = nrm((L, LRU_BLOCKS, LRU_BLOCK_DIM, LRU_BLOCK_DIM), LRU_BLOCK_DIM)
    d['rg_b_a'] = small((L, LRU_WIDTH))
    d['rg_w_i'] = nrm((L, LRU_BLOCKS, LRU_BLOCK_DIM, LRU_BLOCK_DIM), LRU_BLOCK_DIM)
    d['rg_b_i'] = small((L, LRU_WIDTH))
    a_c = jax.random.uniform(next(ks), (L, LRU_WIDTH), f32, 0.9, 0.999)
    a0 = a_c ** (1.0 / LRU_C)
    d['rg_lambda'] = jnp.log(a0) - jnp.log1p(-a0)
    d['attn_out_g'] = gain((L, NSA_WIDTH))
    d['rec_out_g'] = gain((L, LRU_WIDTH))
    d['w_out'] = nrm((L, MIX_WIDTH, D_MODEL), MIX_WIDTH)
    d['ff2_pre_g'] = gain((L, D_MODEL))
    d['ff2_post_g'] = gain((L, D_MODEL))
    d['ff2_w_gate'] = nrm((L, D_MODEL, D_FF), D_MODEL)
    d['ff2_w_up'] = nrm((L, D_MODEL, D_FF), D_MODEL)
    d['ff2_w_down'] = nrm((L, D_FF, D_MODEL), D_FF)
    d['ple_pre_g'] = gain((L, D_MODEL))
    d['ple_post_g'] = gain((L, D_MODEL))
    d['w_ple_gate'] = nrm((L, D_MODEL, D_MODEL), D_MODEL)
    d['w_ple_proj'] = nrm((L, PLE_DIM, D_MODEL), PLE_DIM)
    return d


def reference(x, p, positions,
              ff1_pre_g, ff1_post_g, ff1_w_gate, ff1_w_up, ff1_w_down,
              mix_pre_g, mix_post_g, w_in,
              cmp_pos_k, cmp_pos_v, cmp_k_w1, cmp_k_w2, cmp_v_w1, cmp_v_w2, nsa_gate_b,
              conv_w, conv_b, rg_w_a, rg_b_a, rg_w_i, rg_b_i, rg_lambda,
              attn_out_g, rec_out_g, w_out,
              ff2_pre_g, ff2_post_g, ff2_w_gate, ff2_w_up, ff2_w_down,
              ple_pre_g, ple_post_g, w_ple_gate, w_ple_proj):
    B, T, _ = x.shape
    G = NSA_KV_HEADS
    offs = np.cumsum([NSA_WIDTH] + [KV_WIDTH] * 6 + [NSA_HEADS * N_GATES, LRU_WIDTH]).tolist()

    def heads(t, n):
        return t.reshape(B, T, n, HEAD_DIM)

    h = x
    for i in range(DEPTH):
        u = rmsnorm(h, ff1_pre_g[i])
        h = h + 0.5 * rmsnorm(swiglu(u, ff1_w_gate[i], ff1_w_up[i], ff1_w_down[i]), ff1_post_g[i])
        u = rmsnorm(h, mix_pre_g[i])
        z = u @ w_in[i]
        zq, zkc, zvc, zks, zvs, zkw, zvw, zg, zx, zy = jnp.split(z, offs, axis=-1)
        q = partial_rope(heads(zq, NSA_HEADS), positions)
        k_cmp = partial_rope(heads(zkc, G), positions)
        k_sel = partial_rope(heads(zks, G), positions)
        k_win = partial_rope(heads(zkw, G), positions)
        gates = jax.nn.sigmoid(zg.reshape(B, T, NSA_HEADS, N_GATES) + nsa_gate_b[i])
        o_attn = nsa_attention(q, k_cmp, heads(zvc, G), k_sel, heads(zvs, G), k_win, heads(zvw, G), gates,
                               cmp_pos_k[i], cmp_pos_v[i], cmp_k_w1[i], cmp_k_w2[i], cmp_v_w1[i], cmp_v_w2[i])
        o_rec = rglru_block(zx, zy, conv_w[i], conv_b[i], rg_w_a[i], rg_b_a[i],
                            rg_w_i[i], rg_b_i[i], rg_lambda[i])
        o = jnp.concatenate([rmsnorm(o_attn, attn_out_g[i]), rmsnorm(o_rec, rec_out_g[i])], axis=-1)
        h = h + rmsnorm(o @ w_out[i], mix_post_g[i])
        u = rmsnorm(h, ff2_pre_g[i])
        h = h + 0.5 * rmsnorm(swiglu(u, ff2_w_gate[i], ff2_w_up[i], ff2_w_down[i]), ff2_post_g[i])
        gate = jax.nn.sigmoid(rmsnorm(h, ple_pre_g[i]) @ w_ple_gate[i])
        h = h + rmsnorm(gate * (p[i] @ w_ple_proj[i]), ple_post_g[i])
    return h
```

```python
import functools

import numpy as np
import jax
import jax.numpy as jnp
from jax import lax
from jax.experimental import pallas as pl
from jax.experimental.pallas import tpu as pltpu

F32 = jnp.float32
BF16 = jnp.bfloat16

D_MODEL = 2048
NSA_HEADS = 8
NSA_KV_HEADS = 2
GQA_GROUP = NSA_HEADS // NSA_KV_HEADS
HEAD_DIM = 128
NSA_WIDTH = NSA_HEADS * HEAD_DIM
KV_WIDTH = NSA_KV_HEADS * HEAD_DIM
ROPE_DIM = HEAD_DIM // 4
ROPE_HALF = ROPE_DIM // 2
ROPE_THETA = 500000.0
CMP_LEN = 32
CMP_STRIDE = 16
SEL_LEN = 64
SEL_TOPK = 16
WINDOW = 512
Q_BLOCK = 128
N_GATES = 3
LRU_WIDTH = 1024
LRU_BLOCKS = 8
LRU_BLOCK_DIM = LRU_WIDTH // LRU_BLOCKS
CONV_WIDTH = 4
LRU_C = 8.0
RMS_EPS = 1e-6
NEG = -1e30
SEL_FORCE = 1e4

LANE = 128
SUBLANE = 8
VMEM_LIMIT = 56 * 1024 * 1024

ZT_Q = 0
ZT_X = 8
ZT_Y = 16
ZT_KC, ZT_VC, ZT_KS, ZT_VS, ZT_KW, ZT_VW = 24, 26, 28, 30, 32, 34
ZT_GATE = 36
Z_TILES = 38
Z_WIDTH = Z_TILES * LANE
PROJ_TN = 256
ROPE_PROJ_TILES = (0, 1, 2, 3, ZT_KC // 2, ZT_KS // 2, ZT_KW // 2)
GATE_PROJ_TILE = ZT_GATE // 2

SEL_KT = 512
SEL_KT_BLOCKS = SEL_KT // SEL_LEN


def _rms(x, g):
    return x * lax.rsqrt(jnp.mean(x * x, axis=-1, keepdims=True) + RMS_EPS) * g


def _dot(a, b):
    return jnp.dot(a, b, preferred_element_type=F32)


def _dot_nt(a, b):
    return lax.dot_general(a, b, (((1,), (1,)), ((), ())), preferred_element_type=F32)


def _ffn_kernel(h_ref, pre_ref, wg_ref, wu_ref, wd_ref, post_ref, next_ref, o_ref, un_ref, u_scr, acc_scr):
    j = pl.program_id(1)

    @pl.when(j == 0)
    def _():
        u_scr[...] = _rms(h_ref[...], pre_ref[...]).astype(BF16)
        acc_scr[...] = jnp.zeros_like(acc_scr)

    u = u_scr[...]
    g = _dot(u, wg_ref[...])
    up = _dot(u, wu_ref[...])
    a = (g * jax.nn.sigmoid(g)) * up
    acc_scr[...] += _dot(a.astype(BF16), wd_ref[...])

    @pl.when(j == pl.num_programs(1) - 1)
    def _():
        hn = h_ref[...] + 0.5 * _rms(acc_scr[...], post_ref[...])
        o_ref[...] = hn
        un_ref[...] = _rms(hn, next_ref[...]).astype(BF16)


def _ffn(h, pre_g, wg, wu, wd, post_g, next_g, tm=512, tf=512):
    n, d = h.shape
    dff = wg.shape[1]
    tf = min(tf, dff)
    row = lambda i, j: (i, 0)
    vec = pl.BlockSpec((1, d), lambda i, j: (0, 0))
    return pl.pallas_call(
        _ffn_kernel,
        name="ffn",
        grid=(n // tm, dff // tf),
        in_specs=[
            pl.BlockSpec((tm, d), row), vec,
            pl.BlockSpec((d, tf), lambda i, j: (0, j)),
            pl.BlockSpec((d, tf), lambda i, j: (0, j)),
            pl.BlockSpec((tf, d), lambda i, j: (j, 0)),
            vec, vec,
        ],
        out_specs=[pl.BlockSpec((tm, d), row), pl.BlockSpec((tm, d), row)],
        out_shape=[jax.ShapeDtypeStruct((n, d), F32), jax.ShapeDtypeStruct((n, d), BF16)],
        scratch_shapes=[pltpu.VMEM((tm, d), BF16), pltpu.VMEM((tm, d), F32)],
        compiler_params=pltpu.CompilerParams(
            dimension_semantics=("arbitrary", "arbitrary"), vmem_limit_bytes=VMEM_LIMIT),
    )(h, pre_g, wg, wu, wd, post_g, next_g)


def _proj_kernel(pos_ref, u_ref, w_ref, gb_ref, invf_ref, z_ref, cos_scr, s1_scr, s2_scr):
    j = pl.program_id(1)

    @pl.when(j == 0)
    def _():
        ang = pos_ref[...].astype(F32) * invf_ref[...]
        lane = lax.broadcasted_iota(jnp.int32, ang.shape, 1)
        sin = jnp.sin(ang)
        cos_scr[...] = jnp.cos(ang)
        s1_scr[...] = jnp.where((lane >= ROPE_HALF) & (lane < ROPE_DIM), sin, 0.0)
        s2_scr[...] = jnp.where(lane < ROPE_HALF, -sin, 0.0)

    z = _dot(u_ref[...], w_ref[...])
    is_rope = functools.reduce(jnp.logical_or, [j == t for t in ROPE_PROJ_TILES])
    is_gate = j == GATE_PROJ_TILE

    @pl.when(is_rope)
    def _():
        for hd in range(PROJ_TN // LANE):
            x = z[:, hd * LANE:(hd + 1) * LANE]
            z_ref[:, hd * LANE:(hd + 1) * LANE] = (
                x * cos_scr[...]
                + pltpu.roll(x, ROPE_HALF, axis=1) * s1_scr[...]
                + pltpu.roll(x, LANE - ROPE_HALF, axis=1) * s2_scr[...])

    @pl.when(is_gate)
    def _():
        z_ref[...] = jax.nn.sigmoid(z + gb_ref[...])

    @pl.when(jnp.logical_not(jnp.logical_or(is_rope, is_gate)))
    def _():
        z_ref[...] = z


def _proj(pos, u, w, gb, invf, tm=1024):
    n, d = u.shape
    return pl.pallas_call(
        _proj_kernel,
        name="mix_proj",
        grid=(n // tm, Z_WIDTH // PROJ_TN),
        in_specs=[
            pl.BlockSpec((tm, 1), lambda i, j: (i, 0)),
            pl.BlockSpec((tm, d), lambda i, j: (i, 0)),
            pl.BlockSpec((d, PROJ_TN), lambda i, j: (0, j)),
            pl.BlockSpec((1, PROJ_TN), lambda i, j: (0, 0)),
            pl.BlockSpec((1, LANE), lambda i, j: (0, 0)),
        ],
        out_specs=pl.BlockSpec((tm, PROJ_TN), lambda i, j: (i, j)),
        out_shape=jax.ShapeDtypeStruct((n, Z_WIDTH), F32),
        scratch_shapes=[pltpu.VMEM((tm, LANE), F32)] * 3,
        compiler_params=pltpu.CompilerParams(
            dimension_semantics=("arbitrary", "arbitrary"), vmem_limit_bytes=VMEM_LIMIT),
    )(pos, u, w, gb, invf)


def _cmp_kernel(x_ref, pos_ref, w1_ref, w2_ref, o_ref):
    ncp = o_ref.shape[0]
    hidden_w = w1_ref.shape[-1]
    acc_a = jnp.zeros((ncp, hidden_w), F32)
    acc_b = jnp.zeros((ncp, hidden_w), F32)
    for l in range(CMP_STRIDE):
        xl = x_ref[pl.ds(l, ncp, stride=CMP_STRIDE), :]
        acc_a += _dot((xl + pos_ref[l:l + 1, :]).astype(BF16), w1_ref[l])
        acc_b += _dot((xl + pos_ref[CMP_STRIDE + l:CMP_STRIDE + l + 1, :]).astype(BF16), w1_ref[CMP_STRIDE + l])
    hidden = acc_a + pltpu.roll(acc_b, ncp - 1, axis=0)
    o_ref[...] = _dot(jax.nn.gelu(hidden).astype(BF16), w2_ref[...])


def _compress(z, pos_kv, w1, w2, batch, seq):
    ncp = seq // CMP_STRIDE
    hidden_w = w1.shape[-1]
    return pl.pallas_call(
        _cmp_kernel,
        name="compress",
        grid=(2, batch, NSA_KV_HEADS),
        in_specs=[
            pl.BlockSpec((seq, LANE), lambda k, b, g: (b, ZT_KC + 2 * k + g)),
            pl.BlockSpec((None, CMP_LEN, HEAD_DIM), lambda k, b, g: (k, 0, 0)),
            pl.BlockSpec((None, CMP_LEN, HEAD_DIM, hidden_w), lambda k, b, g: (k, 0, 0, 0)),
            pl.BlockSpec((None, hidden_w, HEAD_DIM), lambda k, b, g: (k, 0, 0)),
        ],
        out_specs=pl.BlockSpec((None, None, None, ncp, HEAD_DIM), lambda k, b, g: (k, b, g, 0, 0)),
        out_shape=jax.ShapeDtypeStruct((2, batch, NSA_KV_HEADS, ncp, HEAD_DIM), F32),
        compiler_params=pltpu.CompilerParams(
            dimension_semantics=("arbitrary",) * 3, vmem_limit_bytes=VMEM_LIMIT),
    )(z, pos_kv, w1, w2)


def _softmax_rows(s, mask):
    sm = jnp.where(mask, s, NEG)
    m = jnp.max(sm, axis=-1, keepdims=True)
    e = jnp.where(mask, jnp.exp(sm - m), 0.0)
    l = jnp.sum(e, axis=-1, keepdims=True)
    return e * (1.0 / jnp.where(l > 0.0, l, 1.0))


def _attn_kernel(q_ref, kc_ref, vc_ref, ks_ref, vs_ref, kw_ref, vw_ref, gz_ref, ovl_ref, e8_ref,
                 o_ref, sel_scr):
    R = GQA_GROUP
    tq = Q_BLOCK
    seq = ks_ref.shape[0]
    n_sel = seq // SEL_LEN
    top_k = min(SEL_TOPK, n_sel)
    i = pl.program_id(2)
    t0 = i * tq

    q = q_ref[...] * (HEAD_DIM ** -0.5)
    q4 = jnp.concatenate([q[:, r * HEAD_DIM:(r + 1) * HEAD_DIM] for r in range(R)], axis=0).astype(BF16)
    t_idx = t0 + lax.broadcasted_iota(jnp.int32, (1, tq, 1), 1)

    ncp = kc_ref.shape[0]
    s = _dot_nt(q4, kc_ref[...].astype(BF16)).reshape(R, tq, ncp)
    c_idx = lax.broadcasted_iota(jnp.int32, (1, 1, ncp), 2)
    pc = _softmax_rows(s, (c_idx * CMP_STRIDE + (CMP_LEN - 1)) <= t_idx)
    o_cmp = _dot(pc.reshape(R * tq, ncp).astype(BF16), vc_ref[...].astype(BF16)).reshape(R, tq, HEAD_DIM)

    pcs = pc[0]
    for r in range(1, R):
        pcs = pcs + pc[r]
    imp = lax.dot_general(ovl_ref[...], pcs, (((1,), (1,)), ((), ())),
                          preferred_element_type=F32, precision=lax.Precision.HIGHEST)
    blk = lax.broadcasted_iota(jnp.int32, (n_sel, tq), 0)
    tcol = t0 + lax.broadcasted_iota(jnp.int32, (n_sel, tq), 1)
    forced = (blk == lax.shift_right_logical(tcol, int(np.log2(SEL_LEN)))) | (blk == 0)
    v = jnp.where(forced, SEL_FORCE, jnp.where(blk * SEL_LEN <= tcol, imp, -SEL_FORCE))
    rank = jnp.zeros((n_sel, tq), F32)
    for sp in range(n_sel):
        other = v[sp:sp + 1, :]
        beats = jnp.where(other > v, 1.0, jnp.where((other == v) & (blk > sp), 1.0, 0.0))
        rank = rank + beats
    sel_scr[...] = jnp.where(rank < float(top_k), 1.0, 0.0)

    def sel_step(kt, carry):
        m, l, acc = carry
        k0 = pl.multiple_of(kt * SEL_KT, SEL_KT)
        ks = ks_ref[pl.ds(k0, SEL_KT), :].astype(BF16)
        vs = vs_ref[pl.ds(k0, SEL_KT), :].astype(BF16)
        s = _dot_nt(q4, ks).reshape(R, tq, SEL_KT)
        sel8 = sel_scr[pl.ds(pl.multiple_of(kt * SEL_KT_BLOCKS, SEL_KT_BLOCKS), SEL_KT_BLOCKS), :]
        chosen = lax.dot_general(sel8, e8_ref[...], (((0,), (0,)), ((), ())),
                                 preferred_element_type=F32)
        kp = k0 + lax.broadcasted_iota(jnp.int32, (1, 1, SEL_KT), 2)
        mask = (chosen[None] > 0.5) & (kp <= t_idx)
        sm = jnp.where(mask, s, NEG)
        m_new = jnp.maximum(m, jnp.max(sm, axis=-1, keepdims=True))
        p = jnp.where(mask, jnp.exp(sm - m_new), 0.0)
        alpha = jnp.exp(m - m_new)
        l = alpha * l + jnp.sum(p, axis=-1, keepdims=True)
        pv = _dot(p.reshape(R * tq, SEL_KT).astype(BF16), vs).reshape(R, tq, HEAD_DIM)
        return m_new, l, alpha * acc + pv

    n_kt = (t0 + tq - 1) // SEL_KT + 1
    init = (jnp.full((R, tq, 1), NEG, F32), jnp.zeros((R, tq, 1), F32), jnp.zeros((R, tq, HEAD_DIM), F32))
    _, l_sel, acc_sel = lax.fori_loop(0, n_kt, sel_step, init)
    o_sel = acc_sel * (1.0 / jnp.where(l_sel > 0.0, l_sel, 1.0))

    wk = WINDOW + tq
    w0 = pl.multiple_of(jnp.maximum(t0 - WINDOW, 0), tq)
    kw = kw_ref[pl.ds(w0, wk), :].astype(BF16)
    vw = vw_ref[pl.ds(w0, wk), :].astype(BF16)
    s = _dot_nt(q4, kw).reshape(R, tq, wk)
    kp = w0 + lax.broadcasted_iota(jnp.int32, (1, 1, wk), 2)
    pw = _softmax_rows(s, (kp <= t_idx) & (kp > t_idx - WINDOW))
    o_win = _dot(pw.reshape(R * tq, wk).astype(BF16), vw).reshape(R, tq, HEAD_DIM)

    gz = gz_ref[...]
    for r in range(R):
        c = r * N_GATES
        o_ref[:, r * HEAD_DIM:(r + 1) * HEAD_DIM] = (
            gz[:, c:c + 1] * o_cmp[r] + gz[:, c + 1:c + 2] * o_sel[r] + gz[:, c + 2:c + 3] * o_win[r])


def _attention(z, kcv, ovl_t, e8, batch, seq):
    nq = seq // Q_BLOCK
    ncp = seq // CMP_STRIDE
    n_sel = seq // SEL_LEN
    gw = GQA_GROUP * HEAD_DIM
    kv_spec = lambda tile: pl.BlockSpec((seq, LANE), lambda b, g, i: (b, tile + g))
    return pl.pallas_call(
        _attn_kernel,
        name="attention",
        grid=(batch, NSA_KV_HEADS, nq),
        in_specs=[
            pl.BlockSpec((Q_BLOCK, gw), lambda b, g, i: (b * nq + i, g)),
            pl.BlockSpec((None, None, None, ncp, HEAD_DIM), lambda b, g, i: (0, b, g, 0, 0)),
            pl.BlockSpec((None, None, None, ncp, HEAD_DIM), lambda b, g, i: (1, b, g, 0, 0)),
            kv_spec(ZT_KS), kv_spec(ZT_VS), kv_spec(ZT_KW), kv_spec(ZT_VW),
            pl.BlockSpec((Q_BLOCK, LANE), lambda b, g, i: (b * nq + i, ZT_GATE + g)),
            pl.BlockSpec((n_sel, ncp), lambda b, g, i: (0, 0)),
            pl.BlockSpec((SEL_KT_BLOCKS, SEL_KT), lambda b, g, i: (0, 0)),
        ],
        out_specs=pl.BlockSpec((Q_BLOCK, gw), lambda b, g, i: (b * nq + i, g)),
        out_shape=jax.ShapeDtypeStruct((batch * seq, NSA_WIDTH), F32),
        scratch_shapes=[pltpu.VMEM((n_sel, Q_BLOCK), F32)],
        compiler_params=pltpu.CompilerParams(
            dimension_semantics=("arbitrary",) * 3, vmem_limit_bytes=VMEM_LIMIT),
    )(z, kcv, kcv, z, z, z, z, z, ovl_t, e8)


def _lru_kernel(zx_ref, zy_ref, cw_ref, cb_ref, wa_ref, ba_ref, wi_ref, bi_ref, lam_ref,
                o_ref, xpad, a_scr, u_scr, h_scr, carry):
    tt, width = zx_ref.shape
    ti = pl.program_id(1)

    @pl.when(ti == 0)
    def _():
        xpad[0:SUBLANE, :] = jnp.zeros((SUBLANE, width), F32)
        carry[...] = jnp.zeros_like(carry)

    xb = zx_ref[...]
    xpad[SUBLANE:SUBLANE + tt, :] = xb
    xc = cb_ref[...] + cw_ref[CONV_WIDTH - 1:CONV_WIDTH, :] * xb
    for k in range(1, CONV_WIDTH):
        xc = xc + cw_ref[CONV_WIDTH - 1 - k:CONV_WIDTH - k, :] * xpad[SUBLANE - k:SUBLANE - k + tt, :]
    xpad[0:SUBLANE, :] = xb[tt - SUBLANE:tt, :]

    sp = jax.nn.softplus(-lam_ref[...])
    row8 = lax.broadcasted_iota(jnp.int32, (tt, LRU_BLOCK_DIM), 0) % SUBLANE
    for hb in range(LRU_BLOCKS):
        sl = slice(hb * LRU_BLOCK_DIM, (hb + 1) * LRU_BLOCK_DIM)
        xs = xc[:, sl]
        xs16 = xs.astype(BF16)
        r = jax.nn.sigmoid(_dot(xs16, wa_ref[hb]) + ba_ref[:, sl])
        ig = jax.nn.sigmoid(_dot(xs16, wi_ref[hb]) + bi_ref[:, sl])
        log_a = -LRU_C * r * sp[:, sl]
        a = jnp.exp(log_a)
        u = jnp.sqrt(-jnp.tanh(log_a) * (a * a + 1.0)) * (ig * xs)
        for s in (1, 2, 4):
            ok = row8 >= s
            a_sh = pltpu.roll(a, s, axis=0)
            u_sh = pltpu.roll(u, s, axis=0)
            u = jnp.where(ok, u + a * u_sh, u)
            a = jnp.where(ok, a * a_sh, a)
        a_scr[:, sl] = a
        u_scr[:, sl] = u

    def group(gi, c):
        r0 = pl.multiple_of(gi * SUBLANE, SUBLANE)
        h = u_scr[pl.ds(r0, SUBLANE), :] + a_scr[pl.ds(r0, SUBLANE), :] * c
        h_scr[pl.ds(r0, SUBLANE), :] = h
        return jnp.broadcast_to(h[SUBLANE - 1:SUBLANE, :], (SUBLANE, width))

    c = lax.fori_loop(0, tt // SUBLANE, group, carry[...], unroll=4)
    carry[...] = c
    o_ref[...] = h_scr[...] * jax.nn.gelu(zy_ref[...])


def _rglru(z, conv_w, conv_b, w_a, b_a, w_i, b_i, lam, batch, seq, tt=256):
    nt = seq // tt
    wt = LRU_WIDTH // LANE
    full = lambda a: pl.BlockSpec(a.shape, lambda b, t: (0,) * a.ndim)
    zspec = lambda tile: pl.BlockSpec((tt, LRU_WIDTH), lambda b, t: (b * nt + t, tile // wt))
    return pl.pallas_call(
        _lru_kernel,
        name="rglru",
        grid=(batch, nt),
        in_specs=[zspec(ZT_X), zspec(ZT_Y), full(conv_w), full(conv_b), full(w_a), full(b_a),
                  full(w_i), full(b_i), full(lam)],
        out_specs=pl.BlockSpec((tt, LRU_WIDTH), lambda b, t: (b * nt + t, 0)),
        out_shape=jax.ShapeDtypeStruct((batch * seq, LRU_WIDTH), F32),
        scratch_shapes=[pltpu.VMEM((tt + SUBLANE, LRU_WIDTH), F32), pltpu.VMEM((tt, LRU_WIDTH), F32),
                        pltpu.VMEM((tt, LRU_WIDTH), F32), pltpu.VMEM((tt, LRU_WIDTH), F32),
                        pltpu.VMEM((SUBLANE, LRU_WIDTH), F32)],
        compiler_params=pltpu.CompilerParams(
            dimension_semantics=("arbitrary", "arbitrary"), vmem_limit_bytes=VMEM_LIMIT),
    )(z, z, conv_w, conv_b, w_a, b_a, w_i, b_i, lam)


def _out_kernel(oa_ref, or_ref, ga_ref, gr_ref, w_ref, post_ref, h_ref, o_ref):
    a = _rms(oa_ref[...], ga_ref[...]).astype(BF16)
    r = _rms(or_ref[...], gr_ref[...]).astype(BF16)
    y = _dot(a, w_ref[0:NSA_WIDTH, :]) + _dot(r, w_ref[NSA_WIDTH:NSA_WIDTH + LRU_WIDTH, :])
    o_ref[...] = h_ref[...] + _rms(y, post_ref[...])


def _out_proj(o_attn, o_rec, g_attn, g_rec, w_out, post_g, h, tm=512):
    n, d = h.shape
    row = lambda i: (i, 0)
    full = lambda a: pl.BlockSpec(a.shape, lambda i: (0,) * a.ndim)
    return pl.pallas_call(
        _out_kernel,
        name="out_proj",
        grid=(n // tm,),
        in_specs=[pl.BlockSpec((tm, NSA_WIDTH), row), pl.BlockSpec((tm, LRU_WIDTH), row),
                  full(g_attn), full(g_rec), full(w_out), full(post_g), pl.BlockSpec((tm, d), row)],
        out_specs=pl.BlockSpec((tm, d), row),
        out_shape=jax.ShapeDtypeStruct((n, d), F32),
        compiler_params=pltpu.CompilerParams(
            dimension_semantics=("arbitrary",), vmem_limit_bytes=VMEM_LIMIT),
    )(o_attn, o_rec, g_attn, g_rec, w_out, post_g, h)


def _ple_kernel(h_ref, p_ref, pre_ref, wg_ref, wp_ref, post_ref, o_ref):
    h = h_ref[...]
    gate = jax.nn.sigmoid(_dot(_rms(h, pre_ref[...]).astype(BF16), wg_ref[...]))
    pp = _dot(p_ref[...].astype(BF16), wp_ref[...])
    o_ref[...] = h + _rms(gate * pp, post_ref[...])


def _ple(h, p, pre_g, w_gate, w_proj, post_g, tm=512):
    n, d = h.shape
    row = lambda i: (i, 0)
    full = lambda a: pl.BlockSpec(a.shape, lambda i: (0,) * a.ndim)
    return pl.pallas_call(
        _ple_kernel,
        name="ple",
        grid=(n // tm,),
        in_specs=[pl.BlockSpec((tm, d), row), pl.BlockSpec((tm, p.shape[1]), row),
                  full(pre_g), full(w_gate), full(w_proj), full(post_g)],
        out_specs=pl.BlockSpec((tm, d), row),
        out_shape=jax.ShapeDtypeStruct((n, d), F32),
        compiler_params=pltpu.CompilerParams(
            dimension_semantics=("arbitrary",), vmem_limit_bytes=VMEM_LIMIT),
    )(h, p, pre_g, w_gate, w_proj, post_g)


def _selection_constants(seq):
    ncp = seq // CMP_STRIDE
    n_cmp = (seq - CMP_LEN) // CMP_STRIDE + 1
    n_sel = seq // SEL_LEN
    c = np.arange(ncp)[None, :]
    s = np.arange(n_sel)[:, None]
    ovl_t = ((c * CMP_STRIDE < s * SEL_LEN + SEL_LEN) & (c * CMP_STRIDE + CMP_LEN - 1 >= s * SEL_LEN) & (c < n_cmp))
    e8 = (np.arange(SEL_KT)[None, :] // SEL_LEN) == np.arange(SEL_KT_BLOCKS)[:, None]
    return jnp.asarray(ovl_t, F32), jnp.asarray(e8, F32)


def _pad_cols(a, width):
    return jnp.pad(a, ((0, 0), (0, width - a.shape[1])))


def kernel(x, p, positions, ff1_pre_g, ff1_post_g, ff1_w_gate, ff1_w_up, ff1_w_down, mix_pre_g, mix_post_g, w_in, cmp_pos_k, cmp_pos_v, cmp_k_w1, cmp_k_w2, cmp_v_w1, cmp_v_w2, nsa_gate_b, conv_w, conv_b, rg_w_a, rg_b_a, rg_w_i, rg_b_i, rg_lambda, attn_out_g, rec_out_g, w_out, ff2_pre_g, ff2_post_g, ff2_w_gate, ff2_w_up, ff2_w_down, ple_pre_g, ple_post_g, w_ple_gate, w_ple_proj):
    batch, seq, d = x.shape
    depth = p.shape[0]
    n = batch * seq
    vec = lambda a: a.reshape(1, -1)
    gate_cols = NSA_HEADS * N_GATES
    grp_gates = GQA_GROUP * N_GATES
    o_q, o_kv, o_g = 0, NSA_WIDTH, NSA_WIDTH + 6 * KV_WIDTH
    o_x = o_g + gate_cols
    o_y = o_x + LRU_WIDTH

    half = jnp.arange(ROPE_HALF, dtype=F32)
    inv_freq = ROPE_THETA ** (-half / ROPE_HALF)
    invf = jnp.concatenate([inv_freq, inv_freq, jnp.zeros((LANE - ROPE_DIM,), F32)]).reshape(1, LANE)
    ovl_t, e8 = _selection_constants(seq)
    pos = positions.reshape(n, 1)

    h = x.reshape(n, d)
    for i in range(depth):
        wi = w_in[i]
        w_in_z = jnp.concatenate(
            [wi[:, o_q:o_q + NSA_WIDTH], wi[:, o_x:o_x + LRU_WIDTH], wi[:, o_y:o_y + LRU_WIDTH],
             wi[:, o_kv:o_kv + 6 * KV_WIDTH]]
            + [_pad_cols(wi[:, o_g + g * grp_gates:o_g + (g + 1) * grp_gates], LANE) for g in range(NSA_KV_HEADS)],
            axis=1).astype(BF16)
        gb = nsa_gate_b[i].reshape(NSA_KV_HEADS, grp_gates)
        gate_bias = jnp.pad(gb, ((0, 0), (0, LANE - grp_gates))).reshape(1, NSA_KV_HEADS * LANE)

        h, u = _ffn(h, vec(ff1_pre_g[i]), ff1_w_gate[i].astype(BF16), ff1_w_up[i].astype(BF16),
                    ff1_w_down[i].astype(BF16), vec(ff1_post_g[i]), vec(mix_pre_g[i]))
        z = _proj(pos, u, w_in_z, gate_bias, invf)
        kcv = _compress(
            z, jnp.stack([cmp_pos_k[i], cmp_pos_v[i]]),
            jnp.stack([cmp_k_w1[i], cmp_v_w1[i]]).reshape(2, CMP_LEN, HEAD_DIM, -1).astype(BF16),
            jnp.stack([cmp_k_w2[i], cmp_v_w2[i]]).astype(BF16), batch, seq)
        o_attn = _attention(z, kcv, ovl_t, e8, batch, seq)
        o_rec = _rglru(z, conv_w[i], vec(conv_b[i]), rg_w_a[i].astype(BF16), vec(rg_b_a[i]),
                       rg_w_i[i].astype(BF16), vec(rg_b_i[i]), vec(rg_lambda[i]), batch, seq)
        h = _out_proj(o_attn, o_rec, vec(attn_out_g[i]), vec(rec_out_g[i]), w_out[i].astype(BF16),
                      vec(mix_post_g[i]), h)
        h, _ = _ffn(h, vec(ff2_pre_g[i]), ff2_w_gate[i].astype(BF16), ff2_w_up[i].astype(BF16),
                    ff2_w_down[i].astype(BF16), vec(ff2_post_g[i]), vec(ple_pre_g[i]))
        h = _ple(h, p[i].reshape(n, -1), vec(ple_pre_g[i]), w_ple_gate[i].astype(BF16),
                 w_ple_proj[i].astype(BF16), vec(ple_post_g[i]))
    return h.reshape(batch, seq, d)
```

```python
import numpy as np
import jax
import jax.numpy as jnp
from jax import lax
from jax.experimental import pallas as pl
from jax.experimental.pallas import tpu as pltpu

F32 = jnp.float32
BF16 = jnp.bfloat16

D_MODEL = 2048
NSA_HEADS = 8
NSA_KV_HEADS = 2
GQA_GROUP = NSA_HEADS // NSA_KV_HEADS
HEAD_DIM = 128
NSA_WIDTH = NSA_HEADS * HEAD_DIM
KV_WIDTH = NSA_KV_HEADS * HEAD_DIM
ROPE_DIM = HEAD_DIM // 4
ROPE_HALF = ROPE_DIM // 2
ROPE_THETA = 500000.0
CMP_LEN = 32
CMP_STRIDE = 16
SEL_LEN = 64
SEL_TOPK = 16
WINDOW = 512
Q_BLOCK = 128
N_GATES = 3
LRU_WIDTH = 1024
LRU_BLOCKS = 8
LRU_BLOCK_DIM = LRU_WIDTH // LRU_BLOCKS
CONV_WIDTH = 4
LRU_C = 8.0
RMS_EPS = 1e-6
NEG = -1e30
SEL_FORCE = 1e4

LANE = 128
SUBLANE = 8
VMEM_LIMIT = 56 * 1024 * 1024

AT_Q, AT_KS, AT_KW, AT_VS, AT_VW = 0, 8, 10, 12, 14
A_WIDTH = 16 * LANE
BT_X, BT_Y, BT_KC, BT_VC, BT_GATE = 0, 8, 16, 18, 20
B_WIDTH = 24 * LANE
PROJ_TN = 512
PROJ_A_STEPS = A_WIDTH // PROJ_TN
PROJ_STEPS = (A_WIDTH + B_WIDTH) // PROJ_TN
PROJ_HEADS = PROJ_TN // LANE

SEL_KT = 512
MASK_BIG = 2.0 ** 100


def _rms(x, g):
    return x * lax.rsqrt(jnp.mean(x * x, axis=-1, keepdims=True) + RMS_EPS) * g


def _dot(a, b):
    return jnp.dot(a, b, preferred_element_type=F32)


def _dot_nt(a, b):
    return lax.dot_general(a, b, (((1,), (1,)), ((), ())), preferred_element_type=F32)


def _ffn_kernel(h_ref, pre_ref, wg_ref, wu_ref, wd_ref, post_ref, next_ref, o_ref, un_ref, u_scr, acc_scr):
    j = pl.program_id(1)

    @pl.when(j == 0)
    def _():
        u_scr[...] = _rms(h_ref[...], pre_ref[...]).astype(BF16)
        acc_scr[...] = jnp.zeros_like(acc_scr)

    u = u_scr[...]
    g = _dot(u, wg_ref[...])
    up = _dot(u, wu_ref[...])
    a = (g * jax.nn.sigmoid(g)) * up
    acc_scr[...] += _dot(a.astype(BF16), wd_ref[...])

    @pl.when(j == pl.num_programs(1) - 1)
    def _():
        hn = h_ref[...] + 0.5 * _rms(acc_scr[...], post_ref[...])
        o_ref[...] = hn
        un_ref[...] = _rms(hn, next_ref[...]).astype(BF16)


def _ffn(h, pre_g, wg, wu, wd, post_g, next_g, tm=512, tf=512):
    n, d = h.shape
    dff = wg.shape[1]
    tf = min(tf, dff)
    row = lambda i, j: (i, 0)
    vec = pl.BlockSpec((1, d), lambda i, j: (0, 0))
    return pl.pallas_call(
        _ffn_kernel,
        name="ffn",
        grid=(n // tm, dff // tf),
        in_specs=[
            pl.BlockSpec((tm, d), row), vec,
            pl.BlockSpec((d, tf), lambda i, j: (0, j)),
            pl.BlockSpec((d, tf), lambda i, j: (0, j)),
            pl.BlockSpec((tf, d), lambda i, j: (j, 0)),
            vec, vec,
        ],
        out_specs=[pl.BlockSpec((tm, d), row), pl.BlockSpec((tm, d), row)],
        out_shape=[jax.ShapeDtypeStruct((n, d), F32), jax.ShapeDtypeStruct((n, d), BF16)],
        scratch_shapes=[pltpu.VMEM((tm, d), BF16), pltpu.VMEM((tm, d), F32)],
        compiler_params=pltpu.CompilerParams(
            dimension_semantics=("arbitrary", "arbitrary"), vmem_limit_bytes=VMEM_LIMIT),
    )(h, pre_g, wg, wu, wd, post_g, next_g)


def _proj_kernel(pos_ref, u_ref, w_ref, gb_ref, invf_ref, za_ref, zb_ref, cos_scr, s1_scr, s2_scr):
    j = pl.program_id(1)

    @pl.when(j == 0)
    def _():
        ang = pos_ref[...].astype(F32) * invf_ref[...]
        lane = lax.broadcasted_iota(jnp.int32, ang.shape, 1)
        sin = jnp.sin(ang)
        cos_scr[...] = jnp.cos(ang)
        s1_scr[...] = jnp.where((lane >= ROPE_HALF) & (lane < ROPE_DIM), sin, 0.0)
        s2_scr[...] = jnp.where(lane < ROPE_HALF, -sin, 0.0)

    z = _dot(u_ref[...], w_ref[...])

    def head(hd, rope, mul=None):
        x = z[:, hd * LANE:(hd + 1) * LANE]
        if rope:
            x = (x * cos_scr[...]
                 + pltpu.roll(x, ROPE_HALF, axis=1) * s1_scr[...]
                 + pltpu.roll(x, LANE - ROPE_HALF, axis=1) * s2_scr[...])
        return x if mul is None else x * mul

    def store(ref, rope_heads, mul=None):
        for hd in range(PROJ_HEADS):
            ref[:, hd * LANE:(hd + 1) * LANE] = head(hd, hd < rope_heads, mul).astype(ref.dtype)

    @pl.when(j < AT_KS // PROJ_HEADS)
    def _():
        store(za_ref, PROJ_HEADS, HEAD_DIM ** -0.5)

    @pl.when(j == AT_KS // PROJ_HEADS)
    def _():
        store(za_ref, PROJ_HEADS)

    @pl.when(j == AT_VS // PROJ_HEADS)
    def _():
        store(za_ref, 0)

    @pl.when((j >= PROJ_A_STEPS) & (j < PROJ_A_STEPS + BT_KC // PROJ_HEADS))
    def _():
        store(zb_ref, 0)

    @pl.when(j == PROJ_A_STEPS + BT_KC // PROJ_HEADS)
    def _():
        store(zb_ref, NSA_KV_HEADS)

    @pl.when(j == PROJ_A_STEPS + BT_GATE // PROJ_HEADS)
    def _():
        zb_ref[...] = jax.nn.sigmoid(z + gb_ref[...])


def _proj(pos, u, w, gb, invf, tm=1024):
    n, d = u.shape
    return pl.pallas_call(
        _proj_kernel,
        name="mix_proj",
        grid=(n // tm, PROJ_STEPS),
        in_specs=[
            pl.BlockSpec((tm, 1), lambda i, j: (i, 0)),
            pl.BlockSpec((tm, d), lambda i, j: (i, 0)),
            pl.BlockSpec((d, PROJ_TN), lambda i, j: (0, j)),
            pl.BlockSpec((1, PROJ_TN), lambda i, j: (0, 0)),
            pl.BlockSpec((1, LANE), lambda i, j: (0, 0)),
        ],
        out_specs=[pl.BlockSpec((tm, PROJ_TN), lambda i, j: (i, jnp.minimum(j, PROJ_A_STEPS - 1))),
                   pl.BlockSpec((tm, PROJ_TN), lambda i, j: (i, jnp.maximum(j - PROJ_A_STEPS, 0)))],
        out_shape=[jax.ShapeDtypeStruct((n, A_WIDTH), BF16), jax.ShapeDtypeStruct((n, B_WIDTH), F32)],
        scratch_shapes=[pltpu.VMEM((tm, LANE), F32)] * 3,
        compiler_params=pltpu.CompilerParams(
            dimension_semantics=("arbitrary", "arbitrary"), vmem_limit_bytes=VMEM_LIMIT),
    )(pos, u, w, gb, invf)


def _cmp_kernel(x_ref, pos_ref, w1_ref, w2_ref, o_ref):
    ncp = o_ref.shape[0]
    hidden_w = w1_ref.shape[-1]
    acc_a = jnp.zeros((ncp, hidden_w), F32)
    acc_b = jnp.zeros((ncp, hidden_w), F32)
    for l in range(CMP_STRIDE):
        xl = x_ref[pl.ds(l, ncp, stride=CMP_STRIDE), :]
        acc_a += _dot((xl + pos_ref[l:l + 1, :]).astype(BF16), w1_ref[l])
        acc_b += _dot((xl + pos_ref[CMP_STRIDE + l:CMP_STRIDE + l + 1, :]).astype(BF16), w1_ref[CMP_STRIDE + l])
    hidden = acc_a + pltpu.roll(acc_b, ncp - 1, axis=0)
    o_ref[...] = _dot(jax.nn.gelu(hidden).astype(BF16), w2_ref[...])


def _compress(zb, pos_kv, w1, w2, batch, seq):
    ncp = seq // CMP_STRIDE
    hidden_w = w1.shape[-1]
    return pl.pallas_call(
        _cmp_kernel,
        name="compress",
        grid=(2, batch, NSA_KV_HEADS),
        in_specs=[
            pl.BlockSpec((seq, LANE), lambda k, b, g: (b, BT_KC + 2 * k + g)),
            pl.BlockSpec((None, CMP_LEN, HEAD_DIM), lambda k, b, g: (k, 0, 0)),
            pl.BlockSpec((None, CMP_LEN, HEAD_DIM, hidden_w), lambda k, b, g: (k, 0, 0, 0)),
            pl.BlockSpec((None, hidden_w, HEAD_DIM), lambda k, b, g: (k, 0, 0)),
        ],
        out_specs=pl.BlockSpec((None, None, None, ncp, HEAD_DIM), lambda k, b, g: (k, b, g, 0, 0)),
        out_shape=jax.ShapeDtypeStruct((2, batch, NSA_KV_HEADS, ncp, HEAD_DIM), F32),
        compiler_params=pltpu.CompilerParams(
            dimension_semantics=("arbitrary",) * 3, vmem_limit_bytes=VMEM_LIMIT),
    )(zb, pos_kv, w1, w2)


def _masked_exp(s, mask):
    s = s + jnp.where(mask, 0.0, -MASK_BIG)
    e = jnp.exp(s - jnp.max(s, axis=-1, keepdims=True))
    return e, jnp.sum(e, axis=-1, keepdims=True)


def _attn_kernel(q_ref, kc_ref, vc_ref, ks_ref, vs_ref, kw_ref, vw_ref, gz_ref, ovl_ref, oh_ref, o_ref):
    G, R, tq = NSA_KV_HEADS, GQA_GROUP, Q_BLOCK
    seq = ks_ref.shape[0]
    n_sel = seq // SEL_LEN
    top_k = min(SEL_TOPK, n_sel)
    ncp = kc_ref.shape[1]
    t0 = pl.program_id(1) * tq
    t_idx = t0 + lax.broadcasted_iota(jnp.int32, (1, tq, 1), 1)
    gsl = lambda g: slice(g * HEAD_DIM, (g + 1) * HEAD_DIM)

    def front(g):
        q = q_ref[:, g * R * HEAD_DIM:(g + 1) * R * HEAD_DIM]
        q4 = jnp.concatenate([q[:, r * HEAD_DIM:(r + 1) * HEAD_DIM] for r in range(R)], axis=0)

        s = _dot_nt(q4, kc_ref[g].astype(BF16)).reshape(R, tq, ncp)
        c_idx = lax.broadcasted_iota(jnp.int32, (1, 1, ncp), 2)
        e, l = _masked_exp(s, (c_idx * CMP_STRIDE + (CMP_LEN - 1)) <= t_idx)
        pc = e * (jnp.where(t_idx >= CMP_LEN - 1, 1.0, 0.0) / l)
        o_cmp = _dot(pc.reshape(R * tq, ncp).astype(BF16), vc_ref[g].astype(BF16)).reshape(R, tq, HEAD_DIM)

        pcs = pc[0]
        for r in range(1, R):
            pcs = pcs + pc[r]
        imp = lax.dot_general(ovl_ref[...], pcs, (((1,), (1,)), ((), ())),
                              preferred_element_type=F32, precision=lax.Precision.HIGHEST)
        blk = lax.broadcasted_iota(jnp.int32, (n_sel, tq), 0)
        tcol = t0 + lax.broadcasted_iota(jnp.int32, (n_sel, tq), 1)
        forced = (blk == lax.shift_right_logical(tcol, int(np.log2(SEL_LEN)))) | (blk == 0)
        v = jnp.where(forced, SEL_FORCE, jnp.where(blk * SEL_LEN <= tcol, imp, -SEL_FORCE))
        rank = jnp.zeros((n_sel, tq), F32)
        for sp in range(n_sel):
            other = v[sp:sp + 1, :]
            beats = jnp.where(other > v, 1.0, jnp.where((other == v) & (blk > sp), 1.0, 0.0))
            rank = rank + beats
        unsel = jnp.where(rank < float(top_k), 0.0, -1.0)
        unsel = jnp.concatenate([unsel, jnp.zeros((LANE - n_sel, tq), F32)], axis=0).T.astype(BF16)
        q_aug = jnp.concatenate([q4, jnp.concatenate([unsel] * R, axis=0)], axis=1)
        return q4, q_aug, o_cmp

    fronts = [front(g) for g in range(G)]

    def sel_tile(g, kt, carry, causal):
        m, l, acc = carry
        k0 = pl.multiple_of(kt * SEL_KT, SEL_KT)
        ks_aug = jnp.concatenate([ks_ref[pl.ds(k0, SEL_KT), gsl(g)], oh_ref[pl.ds(k0, SEL_KT), :]], axis=1)
        s = _dot_nt(fronts[g][1], ks_aug).reshape(R, tq, SEL_KT)
        if causal:
            kp = k0 + lax.broadcasted_iota(jnp.int32, (1, 1, SEL_KT), 2)
            s = s + jnp.where(kp <= t_idx, 0.0, -MASK_BIG)
        m_new = jnp.maximum(m, jnp.max(s, axis=-1, keepdims=True))
        p = jnp.exp(s - m_new)
        alpha = jnp.exp(m - m_new)
        l = alpha * l + jnp.sum(p, axis=-1, keepdims=True)
        pv = _dot(p.reshape(R * tq, SEL_KT).astype(BF16), vs_ref[pl.ds(k0, SEL_KT), gsl(g)])
        return m_new, l, alpha * acc + pv.reshape(R, tq, HEAD_DIM)

    last_kt = (t0 + tq - 1) // SEL_KT
    init = (jnp.full((R, tq, 1), NEG, F32), jnp.zeros((R, tq, 1), F32), jnp.zeros((R, tq, HEAD_DIM), F32))
    carry = lax.fori_loop(0, last_kt, lambda kt, c: tuple(sel_tile(g, kt, c[g], False) for g in range(G)),
                          (init,) * G)

    wk = WINDOW + tq
    w0 = pl.multiple_of(jnp.maximum(t0 - WINDOW, 0), tq)
    kp = w0 + lax.broadcasted_iota(jnp.int32, (1, 1, wk), 2)
    win_mask = (kp <= t_idx) & (kp > t_idx - WINDOW)

    for g in range(G):
        q4, _, o_cmp = fronts[g]
        _, l_sel, acc_sel = sel_tile(g, last_kt, carry[g], True)
        o_sel = acc_sel * (1.0 / l_sel)

        s = _dot_nt(q4, kw_ref[pl.ds(w0, wk), gsl(g)]).reshape(R, tq, wk)
        e, l = _masked_exp(s, win_mask)
        o_win = _dot(e.reshape(R * tq, wk).astype(BF16), vw_ref[pl.ds(w0, wk), gsl(g)]).reshape(R, tq, HEAD_DIM)
        o_win = o_win * (1.0 / l)

        gz = gz_ref[:, gsl(g)]
        for r in range(R):
            c = r * N_GATES
            o_ref[:, (g * R + r) * HEAD_DIM:(g * R + r + 1) * HEAD_DIM] = (
                gz[:, c:c + 1] * o_cmp[r] + gz[:, c + 1:c + 2] * o_sel[r] + gz[:, c + 2:c + 3] * o_win[r])


def _attention(za, zb, kcv, ovl_t, onehot, batch, seq):
    nq = seq // Q_BLOCK
    ncp = seq // CMP_STRIDE
    n_sel = seq // SEL_LEN
    G = NSA_KV_HEADS
    kv_spec = lambda tile: pl.BlockSpec((seq, G * LANE), lambda b, i: (b, tile // G))
    return pl.pallas_call(
        _attn_kernel,
        name="attention",
        grid=(batch, nq),
        in_specs=[
            pl.BlockSpec((Q_BLOCK, NSA_WIDTH), lambda b, i: (b * nq + i, 0)),
            pl.BlockSpec((None, None, G, ncp, HEAD_DIM), lambda b, i: (0, b, 0, 0, 0)),
            pl.BlockSpec((None, None, G, ncp, HEAD_DIM), lambda b, i: (1, b, 0, 0, 0)),
            kv_spec(AT_KS), kv_spec(AT_VS), kv_spec(AT_KW), kv_spec(AT_VW),
            pl.BlockSpec((Q_BLOCK, G * LANE), lambda b, i: (b * nq + i, BT_GATE // G)),
            pl.BlockSpec((n_sel, ncp), lambda b, i: (0, 0)),
            pl.BlockSpec((seq, LANE), lambda b, i: (0, 0)),
        ],
        out_specs=pl.BlockSpec((Q_BLOCK, NSA_WIDTH), lambda b, i: (b * nq + i, 0)),
        out_shape=jax.ShapeDtypeStruct((batch * seq, NSA_WIDTH), F32),
        compiler_params=pltpu.CompilerParams(
            dimension_semantics=("arbitrary",) * 2, vmem_limit_bytes=VMEM_LIMIT),
    )(za, kcv, kcv, za, za, za, za, zb, ovl_t, onehot)


def _lru_kernel(zx_ref, zy_ref, cw_ref, cb_ref, wa_ref, ba_ref, wi_ref, bi_ref, lam_ref,
                o_ref, xpad, a_scr, u_scr, h_scr, carry):
    tt, width = zx_ref.shape
    ti = pl.program_id(1)

    @pl.when(ti == 0)
    def _():
        xpad[0:SUBLANE, :] = jnp.zeros((SUBLANE, width), F32)
        carry[...] = jnp.zeros_like(carry)

    xb = zx_ref[...]
    xpad[SUBLANE:SUBLANE + tt, :] = xb
    xc = cb_ref[...] + cw_ref[CONV_WIDTH - 1:CONV_WIDTH, :] * xb
    for k in range(1, CONV_WIDTH):
        xc = xc + cw_ref[CONV_WIDTH - 1 - k:CONV_WIDTH - k, :] * xpad[SUBLANE - k:SUBLANE - k + tt, :]
    xpad[0:SUBLANE, :] = xb[tt - SUBLANE:tt, :]

    sp = jax.nn.softplus(-lam_ref[...])
    row8 = lax.broadcasted_iota(jnp.int32, (tt, LRU_BLOCK_DIM), 0) % SUBLANE
    for hb in range(LRU_BLOCKS):
        sl = slice(hb * LRU_BLOCK_DIM, (hb + 1) * LRU_BLOCK_DIM)
        xs = xc[:, sl]
        xs16 = xs.astype(BF16)
        r = jax.nn.sigmoid(_dot(xs16, wa_ref[hb]) + ba_ref[:, sl])
        ig = jax.nn.sigmoid(_dot(xs16, wi_ref[hb]) + bi_ref[:, sl])
        log_a = -LRU_C * r * sp[:, sl]
        a = jnp.exp(log_a)
        u = jnp.sqrt(-jnp.tanh(log_a) * (a * a + 1.0)) * (ig * xs)
        for s in (1, 2, 4):
            ok = row8 >= s
            a_sh = pltpu.roll(a, s, axis=0)
            u_sh = pltpu.roll(u, s, axis=0)
            u = jnp.where(ok, u + a * u_sh, u)
            a = jnp.where(ok, a * a_sh, a)
        a_scr[:, sl] = a
        u_scr[:, sl] = u

    def group(gi, c):
        r0 = pl.multiple_of(gi * SUBLANE, SUBLANE)
        h = u_scr[pl.ds(r0, SUBLANE), :] + a_scr[pl.ds(r0, SUBLANE), :] * c
        h_scr[pl.ds(r0, SUBLANE), :] = h
        return jnp.broadcast_to(h[SUBLANE - 1:SUBLANE, :], (SUBLANE, width))

    c = lax.fori_loop(0, tt // SUBLANE, group, carry[...], unroll=4)
    carry[...] = c
    o_ref[...] = h_scr[...] * jax.nn.gelu(zy_ref[...])


def _rglru(zb, conv_w, conv_b, w_a, b_a, w_i, b_i, lam, batch, seq, tt=256):
    nt = seq // tt
    wt = LRU_WIDTH // LANE
    full = lambda a: pl.BlockSpec(a.shape, lambda b, t: (0,) * a.ndim)
    zspec = lambda tile: pl.BlockSpec((tt, LRU_WIDTH), lambda b, t: (b * nt + t, tile // wt))
    return pl.pallas_call(
        _lru_kernel,
        name="rglru",
        grid=(batch, nt),
        in_specs=[zspec(BT_X), zspec(BT_Y), full(conv_w), full(conv_b), full(w_a), full(b_a),
                  full(w_i), full(b_i), full(lam)],
        out_specs=pl.BlockSpec((tt, LRU_WIDTH), lambda b, t: (b * nt + t, 0)),
        out_shape=jax.ShapeDtypeStruct((batch * seq, LRU_WIDTH), F32),
        scratch_shapes=[pltpu.VMEM((tt + SUBLANE, LRU_WIDTH), F32), pltpu.VMEM((tt, LRU_WIDTH), F32),
                        pltpu.VMEM((tt, LRU_WIDTH), F32), pltpu.VMEM((tt, LRU_WIDTH), F32),
                        pltpu.VMEM((SUBLANE, LRU_WIDTH), F32)],
        compiler_params=pltpu.CompilerParams(
            dimension_semantics=("arbitrary", "arbitrary"), vmem_limit_bytes=VMEM_LIMIT),
    )(zb, zb, conv_w, conv_b, w_a, b_a, w_i, b_i, lam)


def _out_kernel(oa_ref, or_ref, ga_ref, gr_ref, w_ref, post_ref, h_ref, o_ref):
    a = _rms(oa_ref[...], ga_ref[...]).astype(BF16)
    r = _rms(or_ref[...], gr_ref[...]).astype(BF16)
    y = _dot(a, w_ref[0:NSA_WIDTH, :]) + _dot(r, w_ref[NSA_WIDTH:NSA_WIDTH + LRU_WIDTH, :])
    o_ref[...] = h_ref[...] + _rms(y, post_ref[...])


def _out_proj(o_attn, o_rec, g_attn, g_rec, w_out, post_g, h, tm=512):
    n, d = h.shape
    row = lambda i: (i, 0)
    full = lambda a: pl.BlockSpec(a.shape, lambda i: (0,) * a.ndim)
    return pl.pallas_call(
        _out_kernel,
        name="out_proj",
        grid=(n // tm,),
        in_specs=[pl.BlockSpec((tm, NSA_WIDTH), row), pl.BlockSpec((tm, LRU_WIDTH), row),
                  full(g_attn), full(g_rec), full(w_out), full(post_g), pl.BlockSpec((tm, d), row)],
        out_specs=pl.BlockSpec((tm, d), row),
        out_shape=jax.ShapeDtypeStruct((n, d), F32),
        compiler_params=pltpu.CompilerParams(
            dimension_semantics=("arbitrary",), vmem_limit_bytes=VMEM_LIMIT),
    )(o_attn, o_rec, g_attn, g_rec, w_out, post_g, h)


def _ple_kernel(h_ref, p_ref, pre_ref, wg_ref, wp_ref, post_ref, o_ref):
    h = h_ref[...]
    gate = jax.nn.sigmoid(_dot(_rms(h, pre_ref[...]).astype(BF16), wg_ref[...]))
    pp = _dot(p_ref[...].astype(BF16), wp_ref[...])
    o_ref[...] = h + _rms(gate * pp, post_ref[...])


def _ple(h, p, pre_g, w_gate, w_proj, post_g, tm=512):
    n, d = h.shape
    row = lambda i: (i, 0)
    full = lambda a: pl.BlockSpec(a.shape, lambda i: (0,) * a.ndim)
    return pl.pallas_call(
        _ple_kernel,
        name="ple",
        grid=(n // tm,),
        in_specs=[pl.BlockSpec((tm, d), row), pl.BlockSpec((tm, p.shape[1]), row),
                  full(pre_g), full(w_gate), full(w_proj), full(post_g)],
        out_specs=pl.BlockSpec((tm, d), row),
        out_shape=jax.ShapeDtypeStruct((n, d), F32),
        compiler_params=pltpu.CompilerParams(
            dimension_semantics=("arbitrary",), vmem_limit_bytes=VMEM_LIMIT),
    )(h, p, pre_g, w_gate, w_proj, post_g)


def _selection_constants(seq):
    ncp = seq // CMP_STRIDE
    n_cmp = (seq - CMP_LEN) // CMP_STRIDE + 1
    n_sel = seq // SEL_LEN
    c = np.arange(ncp)[None, :]
    s = np.arange(n_sel)[:, None]
    ovl_t = ((c * CMP_STRIDE < s * SEL_LEN + SEL_LEN) & (c * CMP_STRIDE + CMP_LEN - 1 >= s * SEL_LEN) & (c < n_cmp))
    onehot = (np.arange(seq)[:, None] // SEL_LEN) == np.arange(LANE)[None, :]
    return jnp.asarray(ovl_t, F32), jnp.asarray(onehot * MASK_BIG, BF16)


def _pad_cols(a, width):
    return jnp.pad(a, ((0, 0), (0, width - a.shape[1])))


def kernel(x, p, positions, ff1_pre_g, ff1_post_g, ff1_w_gate, ff1_w_up, ff1_w_down, mix_pre_g, mix_post_g, w_in, cmp_pos_k, cmp_pos_v, cmp_k_w1, cmp_k_w2, cmp_v_w1, cmp_v_w2, nsa_gate_b, conv_w, conv_b, rg_w_a, rg_b_a, rg_w_i, rg_b_i, rg_lambda, attn_out_g, rec_out_g, w_out, ff2_pre_g, ff2_post_g, ff2_w_gate, ff2_w_up, ff2_w_down, ple_pre_g, ple_post_g, w_ple_gate, w_ple_proj):
    batch, seq, d = x.shape
    depth = p.shape[0]
    n = batch * seq
    vec = lambda a: a.reshape(1, -1)
    gate_cols = NSA_HEADS * N_GATES
    grp_gates = GQA_GROUP * N_GATES
    o_q, o_kc, o_vc, o_ks, o_vs, o_kw, o_vw = (NSA_WIDTH * 0,) + tuple(NSA_WIDTH + k * KV_WIDTH for k in range(6))
    o_g = NSA_WIDTH + 6 * KV_WIDTH
    o_x = o_g + gate_cols
    o_y = o_x + LRU_WIDTH

    half = jnp.arange(ROPE_HALF, dtype=F32)
    inv_freq = ROPE_THETA ** (-half / ROPE_HALF)
    invf = jnp.concatenate([inv_freq, inv_freq, jnp.zeros((LANE - ROPE_DIM,), F32)]).reshape(1, LANE)
    ovl_t, onehot = _selection_constants(seq)
    pos = positions.reshape(n, 1)

    h = x.reshape(n, d)
    for i in range(depth):
        wi = w_in[i]
        cols = lambda o, w: wi[:, o:o + w]
        w_in_z = jnp.concatenate(
            [cols(o_q, NSA_WIDTH), cols(o_ks, KV_WIDTH), cols(o_kw, KV_WIDTH), cols(o_vs, KV_WIDTH),
             cols(o_vw, KV_WIDTH), cols(o_x, LRU_WIDTH), cols(o_y, LRU_WIDTH), cols(o_kc, KV_WIDTH),
             cols(o_vc, KV_WIDTH)]
            + [_pad_cols(cols(o_g + g * grp_gates, grp_gates), LANE) for g in range(NSA_KV_HEADS)]
            + [jnp.zeros((d, B_WIDTH - (BT_GATE + NSA_KV_HEADS) * LANE), F32)],
            axis=1).astype(BF16)
        gb = nsa_gate_b[i].reshape(NSA_KV_HEADS, grp_gates)
        gate_bias = _pad_cols(jnp.pad(gb, ((0, 0), (0, LANE - grp_gates))).reshape(1, NSA_KV_HEADS * LANE), PROJ_TN)

        h, u = _ffn(h, vec(ff1_pre_g[i]), ff1_w_gate[i].astype(BF16), ff1_w_up[i].astype(BF16),
                    ff1_w_down[i].astype(BF16), vec(ff1_post_g[i]), vec(mix_pre_g[i]))
        za, zb = _proj(pos, u, w_in_z, gate_bias, invf)
        kcv = _compress(
            zb, jnp.stack([cmp_pos_k[i], cmp_pos_v[i]]),
            jnp.stack([cmp_k_w1[i], cmp_v_w1[i]]).reshape(2, CMP_LEN, HEAD_DIM, -1).astype(BF16),
            jnp.stack([cmp_k_w2[i], cmp_v_w2[i]]).astype(BF16), batch, seq)
        o_attn = _attention(za, zb, kcv, ovl_t, onehot, batch, seq)
        o_rec = _rglru(zb, conv_w[i], vec(conv_b[i]), rg_w_a[i].astype(BF16), vec(rg_b_a[i]),
                       rg_w_i[i].astype(BF16), vec(rg_b_i[i]), vec(rg_lambda[i]), batch, seq)
        h = _out_proj(o_attn, o_rec, vec(attn_out_g[i]), vec(rec_out_g[i]), w_out[i].astype(BF16),
                      vec(mix_post_g[i]), h)
        h, _ = _ffn(h, vec(ff2_pre_g[i]), ff2_w_gate[i].astype(BF16), ff2_w_up[i].astype(BF16),
                    ff2_w_down[i].astype(BF16), vec(ff2_post_g[i]), vec(ple_pre_g[i]))
        h = _ple(h, p[i].reshape(n, -1), vec(ple_pre_g[i]), w_ple_gate[i].astype(BF16),
                 w_ple_proj[i].astype(BF16), vec(ple_post_g[i]))
    return h.reshape(batch, seq, d)
```

```python
import functools

import numpy as np
import jax
import jax.numpy as jnp
from jax import lax
from jax.experimental import pallas as pl
from jax.experimental.pallas import tpu as pltpu

F32 = jnp.float32
BF16 = jnp.bfloat16

D_MODEL = 2048
NSA_HEADS = 8
NSA_KV_HEADS = 2
GQA_GROUP = NSA_HEADS // NSA_KV_HEADS
HEAD_DIM = 128
NSA_WIDTH = NSA_HEADS * HEAD_DIM
KV_WIDTH = NSA_KV_HEADS * HEAD_DIM
ROPE_DIM = HEAD_DIM // 4
ROPE_HALF = ROPE_DIM // 2
ROPE_THETA = 500000.0
CMP_LEN = 32
CMP_STRIDE = 16
SEL_LEN = 64
SEL_TOPK = 16
WINDOW = 512
Q_BLOCK = 128
N_GATES = 3
LRU_WIDTH = 1024
LRU_BLOCKS = 8
LRU_BLOCK_DIM = LRU_WIDTH // LRU_BLOCKS
CONV_WIDTH = 4
LRU_C = 8.0
RMS_EPS = 1e-6
NEG = -1e30
SEL_FORCE = 1e4

LANE = 128
SUBLANE = 8
VMEM_LIMIT = 62 * 1024 * 1024

AT_Q, AT_KS, AT_KW, AT_VS, AT_VW = 0, 8, 10, 12, 14
A_WIDTH = 16 * LANE
BT_X, BT_Y, BT_KC, BT_VC, BT_GATE = 0, 8, 16, 18, 20
B_WIDTH = 24 * LANE
PROJ_TN = 512
PROJ_A_STEPS = A_WIDTH // PROJ_TN
PROJ_STEPS = (A_WIDTH + B_WIDTH) // PROJ_TN
PROJ_HEADS = PROJ_TN // LANE
ROW_CHUNK = 256
PROJ_CHUNK = ROW_CHUNK
FFN_CHUNK = ROW_CHUNK

Q_SCALE = HEAD_DIM ** -0.5 * 1.4426950408889634
SEL_KT = 512
MASK_BIG = 2.0 ** 100


def _rms(x, g):
    return x * lax.rsqrt(jnp.mean(x * x, axis=-1, keepdims=True) + RMS_EPS) * g


def _dot(a, b):
    return jnp.dot(a, b, preferred_element_type=F32)


def _dot_nt(a, b):
    return lax.dot_general(a, b, (((1,), (1,)), ((), ())), preferred_element_type=F32)


def _ffn_kernel(h_ref, pre_ref, wg_ref, wu_ref, wd_ref, post_ref, o_ref, u_scr, *, n_steps):
    j = pl.program_id(1)

    def step(first, last):
        for c in range(h_ref.shape[0] // FFN_CHUNK):
            rows = slice(c * FFN_CHUNK, (c + 1) * FFN_CHUNK)
            if first:
                u = _rms(h_ref[rows, :], pre_ref[...]).astype(BF16)
                u_scr[rows, :] = u
            else:
                u = u_scr[rows, :]
            g = _dot(u, wg_ref[...])
            up = _dot(u, wu_ref[...])
            acc = _dot(((g * jax.nn.sigmoid(g)) * up).astype(BF16), wd_ref[...])
            if not first:
                acc = o_ref[rows, :] + acc
            if last:
                acc = h_ref[rows, :] + 0.5 * _rms(acc, post_ref[...])
            o_ref[rows, :] = acc

    if n_steps == 1:
        step(True, True)
    else:
        pl.when(j == 0)(lambda: step(True, False))
        pl.when((j > 0) & (j < n_steps - 1))(lambda: step(False, False))
        pl.when(j == n_steps - 1)(lambda: step(False, True))


def _ffn(h, pre_g, wg, wu, wd, post_g, tm=1024, tf=512):
    n, d = h.shape
    dff = wg.shape[1]
    tf = min(tf, dff)
    row = lambda i, j: (i, 0)
    vec = pl.BlockSpec((1, d), lambda i, j: (0, 0))
    return pl.pallas_call(
        functools.partial(_ffn_kernel, n_steps=dff // tf),
        name="ffn",
        grid=(n // tm, dff // tf),
        in_specs=[
            pl.BlockSpec((tm, d), row), vec,
            pl.BlockSpec((d, tf), lambda i, j: (0, j)),
            pl.BlockSpec((d, tf), lambda i, j: (0, j)),
            pl.BlockSpec((tf, d), lambda i, j: (j, 0)),
            vec,
        ],
        out_specs=pl.BlockSpec((tm, d), row),
        out_shape=jax.ShapeDtypeStruct((n, d), F32),
        scratch_shapes=[pltpu.VMEM((tm, d), BF16)],
        compiler_params=pltpu.CompilerParams(
            dimension_semantics=("arbitrary", "arbitrary"), vmem_limit_bytes=VMEM_LIMIT),
    )(h, pre_g, wg, wu, wd, post_g)


def _proj_kernel(pos_ref, h_ref, pre_ref, w_ref, gb_ref, invf_ref, za_ref, zb_ref, u_scr, cos_scr, s1_scr, s2_scr):
    j = pl.program_id(1)

    def prepare(rows):
        u_scr[rows, :] = _rms(h_ref[rows, :], pre_ref[...]).astype(BF16)
        ang = pos_ref[rows, :].astype(F32) * invf_ref[...]
        lane = lax.broadcasted_iota(jnp.int32, ang.shape, 1)
        sin = jnp.sin(ang)
        cos_scr[rows, :] = jnp.cos(ang)
        s1_scr[rows, :] = jnp.where((lane >= ROPE_HALF) & (lane < ROPE_DIM), sin, 0.0)
        s2_scr[rows, :] = jnp.where(lane < ROPE_HALF, -sin, 0.0)

    def head(z, rows, hd, rope, mul=None):
        x = z[:, hd * LANE:(hd + 1) * LANE]
        if rope:
            x = (x * cos_scr[rows, :]
                 + pltpu.roll(x, ROPE_HALF, axis=1) * s1_scr[rows, :]
                 + pltpu.roll(x, LANE - ROPE_HALF, axis=1) * s2_scr[rows, :])
        return x if mul is None else x * mul

    def chunks(first=False):
        for c in range(u_scr.shape[0] // PROJ_CHUNK):
            rows = slice(c * PROJ_CHUNK, (c + 1) * PROJ_CHUNK)
            if first:
                prepare(rows)
            yield rows, _dot(u_scr[rows, :], w_ref[...])

    def store(ref, rope_heads, mul=None, first=False):
        for rows, z in chunks(first):
            for hd in range(PROJ_HEADS):
                ref[rows, hd * LANE:(hd + 1) * LANE] = head(z, rows, hd, hd < rope_heads, mul).astype(ref.dtype)

    @pl.when(j == 0)
    def _():
        store(za_ref, PROJ_HEADS, Q_SCALE, first=True)

    @pl.when((j > 0) & (j < AT_KS // PROJ_HEADS))
    def _():
        store(za_ref, PROJ_HEADS, Q_SCALE)

    @pl.when(j == AT_KS // PROJ_HEADS)
    def _():
        store(za_ref, PROJ_HEADS)

    @pl.when(j == AT_VS // PROJ_HEADS)
    def _():
        store(za_ref, 0)

    @pl.when((j >= PROJ_A_STEPS) & (j < PROJ_A_STEPS + BT_KC // PROJ_HEADS))
    def _():
        store(zb_ref, 0)

    @pl.when(j == PROJ_A_STEPS + BT_KC // PROJ_HEADS)
    def _():
        store(zb_ref, NSA_KV_HEADS)

    @pl.when(j == PROJ_A_STEPS + BT_GATE // PROJ_HEADS)
    def _():
        for rows, z in chunks():
            zb_ref[rows, :] = jax.nn.sigmoid(z + gb_ref[...])


def _proj(pos, h, pre_g, w, gb, invf, tm=1024):
    n, d = h.shape
    return pl.pallas_call(
        _proj_kernel,
        name="mix_proj",
        grid=(n // tm, PROJ_STEPS),
        in_specs=[
            pl.BlockSpec((tm, 1), lambda i, j: (i, 0)),
            pl.BlockSpec((tm, d), lambda i, j: (i, 0)),
            pl.BlockSpec((1, d), lambda i, j: (0, 0)),
            pl.BlockSpec((d, PROJ_TN), lambda i, j: (0, j)),
            pl.BlockSpec((1, PROJ_TN), lambda i, j: (0, 0)),
            pl.BlockSpec((1, LANE), lambda i, j: (0, 0)),
        ],
        out_specs=[pl.BlockSpec((tm, PROJ_TN), lambda i, j: (i, jnp.minimum(j, PROJ_A_STEPS - 1))),
                   pl.BlockSpec((tm, PROJ_TN), lambda i, j: (i, jnp.maximum(j - PROJ_A_STEPS, 0)))],
        out_shape=[jax.ShapeDtypeStruct((n, A_WIDTH), BF16), jax.ShapeDtypeStruct((n, B_WIDTH), F32)],
        scratch_shapes=[pltpu.VMEM((tm, d), BF16)] + [pltpu.VMEM((tm, LANE), F32)] * 3,
        compiler_params=pltpu.CompilerParams(
            dimension_semantics=("arbitrary", "arbitrary"), vmem_limit_bytes=VMEM_LIMIT),
    )(pos, h, pre_g, w, gb, invf)


def _cmp_kernel(x_ref, pos_ref, w1_ref, w2_ref, o_ref):
    ncp = o_ref.shape[0]
    hidden_w = w1_ref.shape[-1]
    acc_a = jnp.zeros((ncp, hidden_w), F32)
    acc_b = jnp.zeros((ncp, hidden_w), F32)
    for l in range(CMP_STRIDE):
        xl = x_ref[pl.ds(l, ncp, stride=CMP_STRIDE), :]
        acc_a += _dot((xl + pos_ref[l:l + 1, :]).astype(BF16), w1_ref[l])
        acc_b += _dot((xl + pos_ref[CMP_STRIDE + l:CMP_STRIDE + l + 1, :]).astype(BF16), w1_ref[CMP_STRIDE + l])
    hidden = acc_a + pltpu.roll(acc_b, ncp - 1, axis=0)
    o_ref[...] = _dot(jax.nn.gelu(hidden).astype(BF16), w2_ref[...])


def _compress(zb, pos_kv, w1, w2, batch, seq):
    ncp = seq // CMP_STRIDE
    hidden_w = w1.shape[-1]
    return pl.pallas_call(
        _cmp_kernel,
        name="compress",
        grid=(2, batch, NSA_KV_HEADS),
        in_specs=[
            pl.BlockSpec((seq, LANE), lambda k, b, g: (b, BT_KC + 2 * k + g)),
            pl.BlockSpec((None, CMP_LEN, HEAD_DIM), lambda k, b, g: (k, 0, 0)),
            pl.BlockSpec((None, CMP_LEN, HEAD_DIM, hidden_w), lambda k, b, g: (k, 0, 0, 0)),
            pl.BlockSpec((None, hidden_w, HEAD_DIM), lambda k, b, g: (k, 0, 0)),
        ],
        out_specs=pl.BlockSpec((None, None, None, ncp, HEAD_DIM), lambda k, b, g: (k, b, g, 0, 0)),
        out_shape=jax.ShapeDtypeStruct((2, batch, NSA_KV_HEADS, ncp, HEAD_DIM), F32),
        compiler_params=pltpu.CompilerParams(
            dimension_semantics=("arbitrary",) * 3, vmem_limit_bytes=VMEM_LIMIT),
    )(zb, pos_kv, w1, w2)


def _masked_exp(s, mask):
    s = s + jnp.where(mask, 0.0, -MASK_BIG)
    e = jnp.exp2(s - jnp.max(s, axis=-1, keepdims=True))
    return e, jnp.sum(e, axis=-1, keepdims=True)


def _attn_kernel(q_ref, kc_ref, vc_ref, ks_ref, vs_ref, kw_ref, vw_ref, gz_ref, ovl_ref, oh_ref, o_ref):
    G, R, tq = NSA_KV_HEADS, GQA_GROUP, Q_BLOCK
    seq = ks_ref.shape[0]
    n_sel = seq // SEL_LEN
    top_k = min(SEL_TOPK, n_sel)
    ncp = kc_ref.shape[1]
    t0 = pl.program_id(1) * tq
    t_idx = t0 + lax.broadcasted_iota(jnp.int32, (1, tq, 1), 1)
    gsl = lambda g: slice(g * HEAD_DIM, (g + 1) * HEAD_DIM)

    def front(g):
        q = q_ref[:, g * R * HEAD_DIM:(g + 1) * R * HEAD_DIM]
        q4 = jnp.concatenate([q[:, r * HEAD_DIM:(r + 1) * HEAD_DIM] for r in range(R)], axis=0)

        s = _dot_nt(q4, kc_ref[g].astype(BF16)).reshape(R, tq, ncp)
        c_idx = lax.broadcasted_iota(jnp.int32, (1, 1, ncp), 2)
        e, l = _masked_exp(s, (c_idx * CMP_STRIDE + (CMP_LEN - 1)) <= t_idx)
        pc = e * (jnp.where(t_idx >= CMP_LEN - 1, 1.0, 0.0) / l)
        o_cmp = _dot(pc.reshape(R * tq, ncp).astype(BF16), vc_ref[g].astype(BF16)).reshape(R, tq, HEAD_DIM)

        pcs = pc[0]
        for r in range(1, R):
            pcs = pcs + pc[r]
        imp = lax.dot_general(ovl_ref[...], pcs, (((1,), (1,)), ((), ())),
                              preferred_element_type=F32, precision=lax.Precision.HIGHEST)
        blk = lax.broadcasted_iota(jnp.int32, (n_sel, tq), 0)
        tcol = t0 + lax.broadcasted_iota(jnp.int32, (n_sel, tq), 1)
        forced = (blk == lax.shift_right_logical(tcol, int(np.log2(SEL_LEN)))) | (blk == 0)
        v = jnp.where(forced, SEL_FORCE, jnp.where(blk * SEL_LEN <= tcol, imp, -SEL_FORCE))
        sub = lax.broadcasted_iota(jnp.int32, (SUBLANE, tq), 0)
        groups = [v[k:k + SUBLANE, :] for k in range(0, n_sel, SUBLANE)]
        ranks = [jnp.zeros((SUBLANE, tq), F32) for _ in groups]
        for sp in range(n_sel):
            other = v[sp:sp + 1, :]
            for gi, vg in enumerate(groups):
                first = gi * SUBLANE
                if first > sp:
                    beats = other >= vg
                elif first + SUBLANE - 1 <= sp:
                    beats = other > vg
                else:
                    beats = (other > vg) | ((other == vg) & (sub > sp - first))
                ranks[gi] = ranks[gi] + jnp.where(beats, 1.0, 0.0)
        rank = jnp.concatenate(ranks, axis=0)
        unsel = jnp.where(rank < float(top_k), 0.0, -1.0)
        unsel = jnp.concatenate([unsel, jnp.zeros((LANE - n_sel, tq), F32)], axis=0).T.astype(BF16)
        q_aug = jnp.concatenate([q4, jnp.concatenate([unsel] * R, axis=0)], axis=1)
        return q4, q_aug, o_cmp

    fronts = [front(g) for g in range(G)]

    def sel_tile(g, kt, carry, causal):
        m, l, acc = carry
        k0 = pl.multiple_of(kt * SEL_KT, SEL_KT)
        ks_aug = jnp.concatenate([ks_ref[pl.ds(k0, SEL_KT), gsl(g)], oh_ref[pl.ds(k0, SEL_KT), :]], axis=1)
        s = _dot_nt(fronts[g][1], ks_aug).reshape(R, tq, SEL_KT)
        if causal:
            kp = k0 + lax.broadcasted_iota(jnp.int32, (1, 1, SEL_KT), 2)
            s = s + jnp.where(kp <= t_idx, 0.0, -MASK_BIG)
        m_new = jnp.maximum(m, jnp.max(s, axis=-1, keepdims=True))
        p = jnp.exp2(s - m_new)
        alpha = jnp.exp2(m - m_new)
        l = alpha * l + jnp.sum(p, axis=-1, keepdims=True)
        pv = _dot(p.reshape(R * tq, SEL_KT).astype(BF16), vs_ref[pl.ds(k0, SEL_KT), gsl(g)])
        return m_new, l, alpha * acc + pv.reshape(R, tq, HEAD_DIM)

    last_kt = (t0 + tq - 1) // SEL_KT
    init = (jnp.full((R, tq, 1), NEG, F32), jnp.zeros((R, tq, 1), F32), jnp.zeros((R, tq, HEAD_DIM), F32))
    carry = lax.fori_loop(0, last_kt, lambda kt, c: tuple(sel_tile(g, kt, c[g], False) for g in range(G)),
                          (init,) * G)

    wk = WINDOW + tq
    w0 = pl.multiple_of(jnp.maximum(t0 - WINDOW, 0), tq)
    kp = w0 + lax.broadcasted_iota(jnp.int32, (1, 1, wk), 2)
    win_mask = (kp <= t_idx) & (kp > t_idx - WINDOW)

    for g in range(G):
        q4, _, o_cmp = fronts[g]
        _, l_sel, acc_sel = sel_tile(g, last_kt, carry[g], True)
        o_sel = acc_sel * (1.0 / l_sel)

        s = _dot_nt(q4, kw_ref[pl.ds(w0, wk), gsl(g)]).reshape(R, tq, wk)
        e, l = _masked_exp(s, win_mask)
        o_win = _dot(e.reshape(R * tq, wk).astype(BF16), vw_ref[pl.ds(w0, wk), gsl(g)]).reshape(R, tq, HEAD_DIM)
        o_win = o_win * (1.0 / l)

        gz = gz_ref[:, gsl(g)]
        for r in range(R):
            c = r * N_GATES
            o_ref[:, (g * R + r) * HEAD_DIM:(g * R + r + 1) * HEAD_DIM] = (
                gz[:, c:c + 1] * o_cmp[r] + gz[:, c + 1:c + 2] * o_sel[r] + gz[:, c + 2:c + 3] * o_win[r])


def _attention(za, zb, kcv, ovl_t, onehot, batch, seq):
    nq = seq // Q_BLOCK
    ncp = seq // CMP_STRIDE
    n_sel = seq // SEL_LEN
    G = NSA_KV_HEADS
    kv_spec = lambda tile: pl.BlockSpec((seq, G * LANE), lambda b, i: (b, tile // G))
    return pl.pallas_call(
        _attn_kernel,
        name="attention",
        grid=(batch, nq),
        in_specs=[
            pl.BlockSpec((Q_BLOCK, NSA_WIDTH), lambda b, i: (b * nq + i, 0)),
            pl.BlockSpec((None, None, G, ncp, HEAD_DIM), lambda b, i: (0, b, 0, 0, 0)),
            pl.BlockSpec((None, None, G, ncp, HEAD_DIM), lambda b, i: (1, b, 0, 0, 0)),
            kv_spec(AT_KS), kv_spec(AT_VS), kv_spec(AT_KW), kv_spec(AT_VW),
            pl.BlockSpec((Q_BLOCK, G * LANE), lambda b, i: (b * nq + i, BT_GATE // G)),
            pl.BlockSpec((n_sel, ncp), lambda b, i: (0, 0)),
            pl.BlockSpec((seq, LANE), lambda b, i: (0, 0)),
        ],
        out_specs=pl.BlockSpec((Q_BLOCK, NSA_WIDTH), lambda b, i: (b * nq + i, 0)),
        out_shape=jax.ShapeDtypeStruct((batch * seq, NSA_WIDTH), F32),
        compiler_params=pltpu.CompilerParams(
            dimension_semantics=("arbitrary",) * 2, vmem_limit_bytes=VMEM_LIMIT),
    )(za, kcv, kcv, za, za, za, za, zb, ovl_t, onehot)


def _lru_kernel(zx_ref, zy_ref, cw_ref, cb_ref, wa_ref, ba_ref, wi_ref, bi_ref, lam_ref,
                o_ref, xpad, a_scr, u_scr, h_scr, carry):
    tt, width = zx_ref.shape
    ti = pl.program_id(1)

    @pl.when(ti == 0)
    def _():
        xpad[0:SUBLANE, :] = jnp.zeros((SUBLANE, width), F32)
        carry[...] = jnp.zeros_like(carry)

    xb = zx_ref[...]
    xpad[SUBLANE:SUBLANE + tt, :] = xb
    xc = cb_ref[...] + cw_ref[CONV_WIDTH - 1:CONV_WIDTH, :] * xb
    for k in range(1, CONV_WIDTH):
        xc = xc + cw_ref[CONV_WIDTH - 1 - k:CONV_WIDTH - k, :] * xpad[SUBLANE - k:SUBLANE - k + tt, :]
    xpad[0:SUBLANE, :] = xb[tt - SUBLANE:tt, :]

    sp = jax.nn.softplus(-lam_ref[...])
    row8 = lax.broadcasted_iota(jnp.int32, (tt, LRU_BLOCK_DIM), 0) % SUBLANE
    for hb in range(LRU_BLOCKS):
        sl = slice(hb * LRU_BLOCK_DIM, (hb + 1) * LRU_BLOCK_DIM)
        xs = xc[:, sl]
        xs16 = xs.astype(BF16)
        r = jax.nn.sigmoid(_dot(xs16, wa_ref[hb]) + ba_ref[:, sl])
        ig = jax.nn.sigmoid(_dot(xs16, wi_ref[hb]) + bi_ref[:, sl])
        log_a = -LRU_C * r * sp[:, sl]
        a = jnp.exp(log_a)
        u = jnp.sqrt(-jnp.tanh(log_a) * (a * a + 1.0)) * (ig * xs)
        for s in (1, 2, 4):
            ok = row8 >= s
            a_sh = pltpu.roll(a, s, axis=0)
            u_sh = pltpu.roll(u, s, axis=0)
            u = jnp.where(ok, u + a * u_sh, u)
            a = jnp.where(ok, a * a_sh, a)
        a_scr[:, sl] = a
        u_scr[:, sl] = u

    def group(gi, c):
        r0 = pl.multiple_of(gi * SUBLANE, SUBLANE)
        h = u_scr[pl.ds(r0, SUBLANE), :] + a_scr[pl.ds(r0, SUBLANE), :] * c
        h_scr[pl.ds(r0, SUBLANE), :] = h
        return jnp.broadcast_to(h[SUBLANE - 1:SUBLANE, :], (SUBLANE, width))

    c = lax.fori_loop(0, tt // SUBLANE, group, carry[...], unroll=4)
    carry[...] = c
    o_ref[...] = h_scr[...] * jax.nn.gelu(zy_ref[...])


def _rglru(zb, conv_w, conv_b, w_a, b_a, w_i, b_i, lam, batch, seq, tt=256):
    nt = seq // tt
    wt = LRU_WIDTH // LANE
    full = lambda a: pl.BlockSpec(a.shape, lambda b, t: (0,) * a.ndim)
    zspec = lambda tile: pl.BlockSpec((tt, LRU_WIDTH), lambda b, t: (b * nt + t, tile // wt))
    return pl.pallas_call(
        _lru_kernel,
        name="rglru",
        grid=(batch, nt),
        in_specs=[zspec(BT_X), zspec(BT_Y), full(conv_w), full(conv_b), full(w_a), full(b_a),
                  full(w_i), full(b_i), full(lam)],
        out_specs=pl.BlockSpec((tt, LRU_WIDTH), lambda b, t: (b * nt + t, 0)),
        out_shape=jax.ShapeDtypeStruct((batch * seq, LRU_WIDTH), F32),
        scratch_shapes=[pltpu.VMEM((tt + SUBLANE, LRU_WIDTH), F32), pltpu.VMEM((tt, LRU_WIDTH), F32),
                        pltpu.VMEM((tt, LRU_WIDTH), F32), pltpu.VMEM((tt, LRU_WIDTH), F32),
                        pltpu.VMEM((SUBLANE, LRU_WIDTH), F32)],
        compiler_params=pltpu.CompilerParams(
            dimension_semantics=("arbitrary", "arbitrary"), vmem_limit_bytes=VMEM_LIMIT),
    )(zb, zb, conv_w, conv_b, w_a, b_a, w_i, b_i, lam)


def _out_kernel(oa_ref, or_ref, ga_ref, gr_ref, w_ref, post_ref, h_ref, o_ref):
    for c in range(h_ref.shape[0] // ROW_CHUNK):
        rows = slice(c * ROW_CHUNK, (c + 1) * ROW_CHUNK)
        a = _rms(oa_ref[rows, :], ga_ref[...]).astype(BF16)
        r = _rms(or_ref[rows, :], gr_ref[...]).astype(BF16)
        y = _dot(a, w_ref[0:NSA_WIDTH, :]) + _dot(r, w_ref[NSA_WIDTH:NSA_WIDTH + LRU_WIDTH, :])
        o_ref[rows, :] = h_ref[rows, :] + _rms(y, post_ref[...])


def _out_proj(o_attn, o_rec, g_attn, g_rec, w_out, post_g, h, tm=512):
    n, d = h.shape
    row = lambda i: (i, 0)
    full = lambda a: pl.BlockSpec(a.shape, lambda i: (0,) * a.ndim)
    return pl.pallas_call(
        _out_kernel,
        name="out_proj",
        grid=(n // tm,),
        in_specs=[pl.BlockSpec((tm, NSA_WIDTH), row), pl.BlockSpec((tm, LRU_WIDTH), row),
                  full(g_attn), full(g_rec), full(w_out), full(post_g), pl.BlockSpec((tm, d), row)],
        out_specs=pl.BlockSpec((tm, d), row),
        out_shape=jax.ShapeDtypeStruct((n, d), F32),
        compiler_params=pltpu.CompilerParams(
            dimension_semantics=("arbitrary",), vmem_limit_bytes=VMEM_LIMIT),
    )(o_attn, o_rec, g_attn, g_rec, w_out, post_g, h)


def _ple_kernel(h_ref, p_ref, pre_ref, wg_ref, wp_ref, post_ref, o_ref):
    for c in range(h_ref.shape[0] // ROW_CHUNK):
        rows = slice(c * ROW_CHUNK, (c + 1) * ROW_CHUNK)
        h = h_ref[rows, :]
        gate = jax.nn.sigmoid(_dot(_rms(h, pre_ref[...]).astype(BF16), wg_ref[...]))
        pp = _dot(p_ref[rows, :].astype(BF16), wp_ref[...])
        o_ref[rows, :] = h + _rms(gate * pp, post_ref[...])


def _ple(h, p, pre_g, w_gate, w_proj, post_g, tm=512):
    n, d = h.shape
    row = lambda i: (i, 0)
    full = lambda a: pl.BlockSpec(a.shape, lambda i: (0,) * a.ndim)
    return pl.pallas_call(
        _ple_kernel,
        name="ple",
        grid=(n // tm,),
        in_specs=[pl.BlockSpec((tm, d), row), pl.BlockSpec((tm, p.shape[1]), row),
                  full(pre_g), full(w_gate), full(w_proj), full(post_g)],
        out_specs=pl.BlockSpec((tm, d), row),
        out_shape=jax.ShapeDtypeStruct((n, d), F32),
        compiler_params=pltpu.CompilerParams(
            dimension_semantics=("arbitrary",), vmem_limit_bytes=VMEM_LIMIT),
    )(h, p, pre_g, w_gate, w_proj, post_g)


def _selection_constants(seq):
    ncp = seq // CMP_STRIDE
    n_cmp = (seq - CMP_LEN) // CMP_STRIDE + 1
    n_sel = seq // SEL_LEN
    c = np.arange(ncp)[None, :]
    s = np.arange(n_sel)[:, None]
    ovl_t = ((c * CMP_STRIDE < s * SEL_LEN + SEL_LEN) & (c * CMP_STRIDE + CMP_LEN - 1 >= s * SEL_LEN) & (c < n_cmp))
    onehot = (np.arange(seq)[:, None] // SEL_LEN) == np.arange(LANE)[None, :]
    return jnp.asarray(ovl_t, F32), jnp.asarray(onehot * MASK_BIG, BF16)


def _pad_cols(a, width):
    return jnp.pad(a, ((0, 0), (0, width - a.shape[1])))


def kernel(x, p, positions, ff1_pre_g, ff1_post_g, ff1_w_gate, ff1_w_up, ff1_w_down, mix_pre_g, mix_post_g, w_in, cmp_pos_k, cmp_pos_v, cmp_k_w1, cmp_k_w2, cmp_v_w1, cmp_v_w2, nsa_gate_b, conv_w, conv_b, rg_w_a, rg_b_a, rg_w_i, rg_b_i, rg_lambda, attn_out_g, rec_out_g, w_out, ff2_pre_g, ff2_post_g, ff2_w_gate, ff2_w_up, ff2_w_down, ple_pre_g, ple_post_g, w_ple_gate, w_ple_proj):
    batch, seq, d = x.shape
    depth = p.shape[0]
    n = batch * seq
    vec = lambda a: a.reshape(1, -1)
    gate_cols = NSA_HEADS * N_GATES
    grp_gates = GQA_GROUP * N_GATES
    o_q, o_kc, o_vc, o_ks, o_vs, o_kw, o_vw = (NSA_WIDTH * 0,) + tuple(NSA_WIDTH + k * KV_WIDTH for k in range(6))
    o_g = NSA_WIDTH + 6 * KV_WIDTH
    o_x = o_g + gate_cols
    o_y = o_x + LRU_WIDTH

    half = jnp.arange(ROPE_HALF, dtype=F32)
    inv_freq = ROPE_THETA ** (-half / ROPE_HALF)
    invf = jnp.concatenate([inv_freq, inv_freq, jnp.zeros((LANE - ROPE_DIM,), F32)]).reshape(1, LANE)
    ovl_t, onehot = _selection_constants(seq)
    pos = positions.reshape(n, 1)

    h = x.reshape(n, d)
    for i in range(depth):
        wi = w_in[i]
        cols = lambda o, w: wi[:, o:o + w]
        w_in_z = jnp.concatenate(
            [cols(o_q, NSA_WIDTH), cols(o_ks, KV_WIDTH), cols(o_kw, KV_WIDTH), cols(o_vs, KV_WIDTH),
             cols(o_vw, KV_WIDTH), cols(o_x, LRU_WIDTH), cols(o_y, LRU_WIDTH), cols(o_kc, KV_WIDTH),
             cols(o_vc, KV_WIDTH)]
            + [_pad_cols(cols(o_g + g * grp_gates, grp_gates), LANE) for g in range(NSA_KV_HEADS)]
            + [jnp.zeros((d, B_WIDTH - (BT_GATE + NSA_KV_HEADS) * LANE), F32)],
            axis=1).astype(BF16)
        gb = nsa_gate_b[i].reshape(NSA_KV_HEADS, grp_gates)
        gate_bias = _pad_cols(jnp.pad(gb, ((0, 0), (0, LANE - grp_gates))).reshape(1, NSA_KV_HEADS * LANE), PROJ_TN)

        h = _ffn(h, vec(ff1_pre_g[i]), ff1_w_gate[i].astype(BF16), ff1_w_up[i].astype(BF16),
                 ff1_w_down[i].astype(BF16), vec(ff1_post_g[i]))
        za, zb = _proj(pos, h, vec(mix_pre_g[i]), w_in_z, gate_bias, invf)
        kcv = _compress(
            zb, jnp.stack([cmp_pos_k[i], cmp_pos_v[i]]),
            jnp.stack([cmp_k_w1[i], cmp_v_w1[i]]).reshape(2, CMP_LEN, HEAD_DIM, -1).astype(BF16),
            jnp.stack([cmp_k_w2[i], cmp_v_w2[i]]).astype(BF16), batch, seq)
        o_attn = _attention(za, zb, kcv, ovl_t, onehot, batch, seq)
        o_rec = _rglru(zb, conv_w[i], vec(conv_b[i]), rg_w_a[i].astype(BF16), vec(rg_b_a[i]),
                       rg_w_i[i].astype(BF16), vec(rg_b_i[i]), vec(rg_lambda[i]), batch, seq)
        h = _out_proj(o_attn, o_rec, vec(attn_out_g[i]), vec(rec_out_g[i]), w_out[i].astype(BF16),
                      vec(mix_post_g[i]), h)
        h = _ffn(h, vec(ff2_pre_g[i]), ff2_w_gate[i].astype(BF16), ff2_w_up[i].astype(BF16),
                 ff2_w_down[i].astype(BF16), vec(ff2_post_g[i]))
        h = _ple(h, p[i].reshape(n, -1), vec(ple_pre_g[i]), w_ple_gate[i].astype(BF16),
                 w_ple_proj[i].astype(BF16), vec(ple_post_g[i]))
    return h.reshape(batch, seq, d)
```

```python
import functools

import numpy as np
import jax
import jax.numpy as jnp
from jax import lax
from jax.experimental import pallas as pl
from jax.experimental.pallas import tpu as pltpu

F32 = jnp.float32
BF16 = jnp.bfloat16

D_MODEL = 2048
NSA_HEADS = 8
NSA_KV_HEADS = 2
GQA_GROUP = NSA_HEADS // NSA_KV_HEADS
HEAD_DIM = 128
NSA_WIDTH = NSA_HEADS * HEAD_DIM
KV_WIDTH = NSA_KV_HEADS * HEAD_DIM
ROPE_DIM = HEAD_DIM // 4
ROPE_HALF = ROPE_DIM // 2
ROPE_THETA = 500000.0
CMP_LEN = 32
CMP_STRIDE = 16
SEL_LEN = 64
SEL_TOPK = 16
WINDOW = 512
Q_BLOCK = 128
N_GATES = 3
LRU_WIDTH = 1024
LRU_BLOCKS = 8
LRU_BLOCK_DIM = LRU_WIDTH // LRU_BLOCKS
CONV_WIDTH = 4
LRU_C = 8.0
RMS_EPS = 1e-6
NEG = -1e30
SEL_FORCE = 1e4

LANE = 128
SUBLANE = 8
VMEM_LIMIT = 62 * 1024 * 1024

AT_Q, AT_KS, AT_KW, AT_VS, AT_VW = 0, 8, 10, 12, 14
A_WIDTH = 16 * LANE
BT_X, BT_Y, BT_KC, BT_VC, BT_GATE = 0, 8, 16, 18, 20
B_WIDTH = 24 * LANE
PROJ_TN = 512
PROJ_A_STEPS = A_WIDTH // PROJ_TN
PROJ_STEPS = (A_WIDTH + B_WIDTH) // PROJ_TN
PROJ_HEADS = PROJ_TN // LANE
ROW_CHUNK = 256
PROJ_CHUNK = ROW_CHUNK
FFN_CHUNK = ROW_CHUNK

Q_SCALE = HEAD_DIM ** -0.5 * 1.4426950408889634
SEL_KT = 512
MASK_BIG = 2.0 ** 100


def _rms(x, g):
    return x * lax.rsqrt(jnp.mean(x * x, axis=-1, keepdims=True) + RMS_EPS) * g


def _dot(a, b):
    return jnp.dot(a, b, preferred_element_type=F32)


def _dot_nt(a, b):
    return lax.dot_general(a, b, (((1,), (1,)), ((), ())), preferred_element_type=F32)


def _ffn_kernel(h_ref, pre_ref, wg_ref, wu_ref, wd_ref, post_ref, o_ref, u_scr, *, n_steps):
    j = pl.program_id(1)

    def step(first, last):
        for c in range(h_ref.shape[0] // FFN_CHUNK):
            rows = slice(c * FFN_CHUNK, (c + 1) * FFN_CHUNK)
            if first:
                u = _rms(h_ref[rows, :], pre_ref[...]).astype(BF16)
                u_scr[rows, :] = u
            else:
                u = u_scr[rows, :]
            g = _dot(u, wg_ref[...])
            up = _dot(u, wu_ref[...])
            acc = _dot(((g * jax.nn.sigmoid(g)) * up).astype(BF16), wd_ref[...])
            if not first:
                acc = o_ref[rows, :] + acc
            if last:
                acc = h_ref[rows, :] + 0.5 * _rms(acc, post_ref[...])
            o_ref[rows, :] = acc

    if n_steps == 1:
        step(True, True)
    else:
        pl.when(j == 0)(lambda: step(True, False))
        pl.when((j > 0) & (j < n_steps - 1))(lambda: step(False, False))
        pl.when(j == n_steps - 1)(lambda: step(False, True))


def _ffn(h, pre_g, wg, wu, wd, post_g, tm=1024, tf=512):
    n, d = h.shape
    dff = wg.shape[1]
    tf = min(tf, dff)
    row = lambda i, j: (i, 0)
    vec = pl.BlockSpec((1, d), lambda i, j: (0, 0))
    return pl.pallas_call(
        functools.partial(_ffn_kernel, n_steps=dff // tf),
        name="ffn",
        grid=(n // tm, dff // tf),
        in_specs=[
            pl.BlockSpec((tm, d), row), vec,
            pl.BlockSpec((d, tf), lambda i, j: (0, j)),
            pl.BlockSpec((d, tf), lambda i, j: (0, j)),
            pl.BlockSpec((tf, d), lambda i, j: (j, 0)),
            vec,
        ],
        out_specs=pl.BlockSpec((tm, d), row),
        out_shape=jax.ShapeDtypeStruct((n, d), F32),
        scratch_shapes=[pltpu.VMEM((tm, d), BF16)],
        compiler_params=pltpu.CompilerParams(
            dimension_semantics=("arbitrary", "arbitrary"), vmem_limit_bytes=VMEM_LIMIT),
    )(h, pre_g, wg, wu, wd, post_g)


def _proj_kernel(pos_ref, h_ref, pre_ref, w_ref, gb_ref, invf_ref, za_ref, zb_ref, u_scr, cos_scr, s1_scr, s2_scr):
    j = pl.program_id(1)

    def prepare(rows):
        u_scr[rows, :] = _rms(h_ref[rows, :], pre_ref[...]).astype(BF16)
        ang = pos_ref[rows, :].astype(F32) * invf_ref[...]
        lane = lax.broadcasted_iota(jnp.int32, ang.shape, 1)
        sin = jnp.sin(ang)
        cos_scr[rows, :] = jnp.cos(ang)
        s1_scr[rows, :] = jnp.where((lane >= ROPE_HALF) & (lane < ROPE_DIM), sin, 0.0)
        s2_scr[rows, :] = jnp.where(lane < ROPE_HALF, -sin, 0.0)

    def head(z, rows, hd, rope, mul=None):
        x = z[:, hd * LANE:(hd + 1) * LANE]
        if rope:
            x = (x * cos_scr[rows, :]
                 + pltpu.roll(x, ROPE_HALF, axis=1) * s1_scr[rows, :]
                 + pltpu.roll(x, LANE - ROPE_HALF, axis=1) * s2_scr[rows, :])
        return x if mul is None else x * mul

    def chunks(first=False):
        for c in range(u_scr.shape[0] // PROJ_CHUNK):
            rows = slice(c * PROJ_CHUNK, (c + 1) * PROJ_CHUNK)
            if first:
                prepare(rows)
            yield rows, _dot(u_scr[rows, :], w_ref[...])

    def store(ref, rope_heads, mul=None, first=False):
        for rows, z in chunks(first):
            for hd in range(PROJ_HEADS):
                ref[rows, hd * LANE:(hd + 1) * LANE] = head(z, rows, hd, hd < rope_heads, mul).astype(ref.dtype)

    @pl.when(j == 0)
    def _():
        store(za_ref, PROJ_HEADS, Q_SCALE, first=True)

    @pl.when((j > 0) & (j < AT_KS // PROJ_HEADS))
    def _():
        store(za_ref, PROJ_HEADS, Q_SCALE)

    @pl.when(j == AT_KS // PROJ_HEADS)
    def _():
        store(za_ref, PROJ_HEADS)

    @pl.when(j == AT_VS // PROJ_HEADS)
    def _():
        store(za_ref, 0)

    @pl.when((j >= PROJ_A_STEPS) & (j < PROJ_A_STEPS + BT_KC // PROJ_HEADS))
    def _():
        store(zb_ref, 0)

    @pl.when(j == PROJ_A_STEPS + BT_KC // PROJ_HEADS)
    def _():
        store(zb_ref, NSA_KV_HEADS)

    @pl.when(j == PROJ_A_STEPS + BT_GATE // PROJ_HEADS)
    def _():
        for rows, z in chunks():
            zb_ref[rows, :] = jax.nn.sigmoid(z + gb_ref[...])


def _proj(pos, h, pre_g, w, gb, invf, tm=1024):
    n, d = h.shape
    return pl.pallas_call(
        _proj_kernel,
        name="mix_proj",
        grid=(n // tm, PROJ_STEPS),
        in_specs=[
            pl.BlockSpec((tm, 1), lambda i, j: (i, 0)),
            pl.BlockSpec((tm, d), lambda i, j: (i, 0)),
            pl.BlockSpec((1, d), lambda i, j: (0, 0)),
            pl.BlockSpec((d, PROJ_TN), lambda i, j: (0, j)),
            pl.BlockSpec((1, PROJ_TN), lambda i, j: (0, 0)),
            pl.BlockSpec((1, LANE), lambda i, j: (0, 0)),
        ],
        out_specs=[pl.BlockSpec((tm, PROJ_TN), lambda i, j: (i, jnp.minimum(j, PROJ_A_STEPS - 1))),
                   pl.BlockSpec((tm, PROJ_TN), lambda i, j: (i, jnp.maximum(j - PROJ_A_STEPS, 0)))],
        out_shape=[jax.ShapeDtypeStruct((n, A_WIDTH), BF16), jax.ShapeDtypeStruct((n, B_WIDTH), F32)],
        scratch_shapes=[pltpu.VMEM((tm, d), BF16)] + [pltpu.VMEM((tm, LANE), F32)] * 3,
        compiler_params=pltpu.CompilerParams(
            dimension_semantics=("arbitrary", "arbitrary"), vmem_limit_bytes=VMEM_LIMIT),
    )(pos, h, pre_g, w, gb, invf)


def _cmp_kernel(x_ref, pos_ref, w1_ref, w2_ref, o_ref):
    ncp = o_ref.shape[0]
    hidden_w = w1_ref.shape[-1]
    acc_a = jnp.zeros((ncp, hidden_w), F32)
    acc_b = jnp.zeros((ncp, hidden_w), F32)
    for l in range(CMP_STRIDE):
        xl = x_ref[pl.ds(l, ncp, stride=CMP_STRIDE), :]
        acc_a += _dot((xl + pos_ref[l:l + 1, :]).astype(BF16), w1_ref[l])
        acc_b += _dot((xl + pos_ref[CMP_STRIDE + l:CMP_STRIDE + l + 1, :]).astype(BF16), w1_ref[CMP_STRIDE + l])
    hidden = acc_a + pltpu.roll(acc_b, ncp - 1, axis=0)
    o_ref[...] = _dot(jax.nn.gelu(hidden).astype(BF16), w2_ref[...])


def _compress(zb, pos_kv, w1, w2, batch, seq):
    ncp = seq // CMP_STRIDE
    hidden_w = w1.shape[-1]
    return pl.pallas_call(
        _cmp_kernel,
        name="compress",
        grid=(2, batch, NSA_KV_HEADS),
        in_specs=[
            pl.BlockSpec((seq, LANE), lambda k, b, g: (b, BT_KC + 2 * k + g)),
            pl.BlockSpec((None, CMP_LEN, HEAD_DIM), lambda k, b, g: (k, 0, 0)),
            pl.BlockSpec((None, CMP_LEN, HEAD_DIM, hidden_w), lambda k, b, g: (k, 0, 0, 0)),
            pl.BlockSpec((None, hidden_w, HEAD_DIM), lambda k, b, g: (k, 0, 0)),
        ],
        out_specs=pl.BlockSpec((None, None, None, ncp, HEAD_DIM), lambda k, b, g: (k, b, g, 0, 0)),
        out_shape=jax.ShapeDtypeStruct((2, batch, NSA_KV_HEADS, ncp, HEAD_DIM), F32),
        compiler_params=pltpu.CompilerParams(
            dimension_semantics=("arbitrary",) * 3, vmem_limit_bytes=VMEM_LIMIT),
    )(zb, pos_kv, w1, w2)


def _masked_exp(s, mask):
    s = s + jnp.where(mask, 0.0, -MASK_BIG)
    e = jnp.exp2(s - jnp.max(s, axis=-1, keepdims=True))
    return e, jnp.sum(e, axis=-1, keepdims=True)


def _attn_kernel(q_ref, kc_ref, vc_ref, ks_ref, vs_ref, kw_ref, vw_ref, gz_ref, ovl_ref, oh_ref, ga_ref, o_ref,
                 *, first_block):
    G, R, tq = NSA_KV_HEADS, GQA_GROUP, Q_BLOCK
    n_keys = ks_ref.shape[0]
    n_sel = ovl_ref.shape[0]
    top_k = min(SEL_TOPK, n_sel)
    ncp = kc_ref.shape[1]
    t0 = (first_block + pl.program_id(1)) * tq
    t_idx = t0 + lax.broadcasted_iota(jnp.int32, (1, tq, 1), 1)
    gsl = lambda g: slice(g * HEAD_DIM, (g + 1) * HEAD_DIM)

    def front(g):
        q = q_ref[:, g * R * HEAD_DIM:(g + 1) * R * HEAD_DIM]
        q4 = jnp.concatenate([q[:, r * HEAD_DIM:(r + 1) * HEAD_DIM] for r in range(R)], axis=0)

        s = _dot_nt(q4, kc_ref[g].astype(BF16)).reshape(R, tq, ncp)
        c_idx = lax.broadcasted_iota(jnp.int32, (1, 1, ncp), 2)
        e, l = _masked_exp(s, (c_idx * CMP_STRIDE + (CMP_LEN - 1)) <= t_idx)
        pc = e * (jnp.where(t_idx >= CMP_LEN - 1, 1.0, 0.0) / l)
        o_cmp = _dot(pc.reshape(R * tq, ncp).astype(BF16), vc_ref[g].astype(BF16)).reshape(R, tq, HEAD_DIM)

        pcs = pc[0]
        for r in range(1, R):
            pcs = pcs + pc[r]
        imp = lax.dot_general(ovl_ref[...], pcs, (((1,), (1,)), ((), ())),
                              preferred_element_type=F32, precision=lax.Precision.HIGHEST)
        blk = lax.broadcasted_iota(jnp.int32, (n_sel, tq), 0)
        tcol = t0 + lax.broadcasted_iota(jnp.int32, (n_sel, tq), 1)
        forced = (blk == lax.shift_right_logical(tcol, int(np.log2(SEL_LEN)))) | (blk == 0)
        v = jnp.where(forced, SEL_FORCE, jnp.where(blk * SEL_LEN <= tcol, imp, -SEL_FORCE))
        sub = lax.broadcasted_iota(jnp.int32, (SUBLANE, tq), 0)
        groups = [v[k:k + SUBLANE, :] for k in range(0, n_sel, SUBLANE)]
        ranks = [jnp.zeros((SUBLANE, tq), F32) for _ in groups]
        for sp in range(n_sel):
            other = v[sp:sp + 1, :]
            for gi, vg in enumerate(groups):
                first = gi * SUBLANE
                if first > sp:
                    beats = other >= vg
                elif first + SUBLANE - 1 <= sp:
                    beats = other > vg
                else:
                    beats = (other > vg) | ((other == vg) & (sub > sp - first))
                ranks[gi] = ranks[gi] + jnp.where(beats, 1.0, 0.0)
        rank = jnp.concatenate(ranks, axis=0)
        unsel = jnp.where(rank < float(top_k), 0.0, -1.0)
        unsel = jnp.concatenate([unsel, jnp.zeros((LANE - n_sel, tq), F32)], axis=0).T.astype(BF16)
        q_aug = jnp.concatenate([q4, jnp.concatenate([unsel] * R, axis=0)], axis=1)
        return q4, q_aug, o_cmp

    fronts = [front(g) for g in range(G)]

    n_full = n_keys - SEL_KT
    kp = n_full + lax.broadcasted_iota(jnp.int32, (1, 1, SEL_KT), 2)
    causal_bias = jnp.where(kp <= t_idx, 0.0, -MASK_BIG)

    def selected(g):
        k_aug = jnp.concatenate([ks_ref[:, gsl(g)], oh_ref[...]], axis=1)
        s = _dot_nt(fronts[g][1], k_aug).reshape(R, tq, n_keys)
        s_last = s[:, :, n_full:] + causal_bias
        m = jnp.max(s_last, axis=-1, keepdims=True)
        if n_full:
            s_full = s[:, :, :n_full]
            m = jnp.maximum(m, jnp.max(s_full, axis=-1, keepdims=True))
        p = jnp.exp2(s_last - m)
        l = jnp.sum(p, axis=-1, keepdims=True)
        acc = _dot(p.reshape(R * tq, SEL_KT).astype(BF16), vs_ref[n_full:n_keys, gsl(g)])
        if n_full:
            p = jnp.exp2(s_full - m)
            l = l + jnp.sum(p, axis=-1, keepdims=True)
            acc = acc + _dot(p.reshape(R * tq, n_full).astype(BF16), vs_ref[0:n_full, gsl(g)])
        return acc.reshape(R, tq, HEAD_DIM) * (1.0 / l)

    wk = min(WINDOW + tq, n_keys)
    w0 = pl.multiple_of(jnp.maximum(t0 - WINDOW, 0), tq)
    kp = w0 + lax.broadcasted_iota(jnp.int32, (1, 1, wk), 2)
    win_mask = (kp <= t_idx) & (kp > t_idx - WINDOW)

    heads = []
    for g in range(G):
        q4, _, o_cmp = fronts[g]
        o_sel = selected(g)

        s = _dot_nt(q4, kw_ref[pl.ds(w0, wk), gsl(g)]).reshape(R, tq, wk)
        e, l = _masked_exp(s, win_mask)
        o_win = _dot(e.reshape(R * tq, wk).astype(BF16), vw_ref[pl.ds(w0, wk), gsl(g)]).reshape(R, tq, HEAD_DIM)
        o_win = o_win * (1.0 / l)

        gz = gz_ref[:, gsl(g)]
        for r in range(R):
            c = r * N_GATES
            heads.append(gz[:, c:c + 1] * o_cmp[r] + gz[:, c + 1:c + 2] * o_sel[r] + gz[:, c + 2:c + 3] * o_win[r])
    o_ref[...] = _rms(jnp.concatenate(heads, axis=1), ga_ref[...]).astype(o_ref.dtype)


def _attention(za, zb, kcv, ovl_t, onehot, g_attn, batch, seq):
    ncp = seq // CMP_STRIDE
    n_sel = seq // SEL_LEN
    G = NSA_KV_HEADS
    blocks = SEL_KT // Q_BLOCK
    za3 = za.reshape(batch, seq, A_WIDTH)
    zb3 = zb.reshape(batch, seq, B_WIDTH)
    outs = []
    for c in range(seq // SEL_KT):
        n_keys = (c + 1) * SEL_KT
        q_row = lambda b, i, c=c: (b, c * blocks + i, 0)
        kv_spec = lambda tile: pl.BlockSpec((None, n_keys, G * LANE), lambda b, i: (b, 0, tile // G))
        outs.append(pl.pallas_call(
            functools.partial(_attn_kernel, first_block=c * blocks),
            name=f"attention_{c}",
            grid=(batch, blocks),
            in_specs=[
                pl.BlockSpec((None, Q_BLOCK, NSA_WIDTH), q_row),
                pl.BlockSpec((None, None, G, ncp, HEAD_DIM), lambda b, i: (0, b, 0, 0, 0)),
                pl.BlockSpec((None, None, G, ncp, HEAD_DIM), lambda b, i: (1, b, 0, 0, 0)),
                kv_spec(AT_KS), kv_spec(AT_VS), kv_spec(AT_KW), kv_spec(AT_VW),
                pl.BlockSpec((None, Q_BLOCK, G * LANE), lambda b, i, c=c: (b, c * blocks + i, BT_GATE // G)),
                pl.BlockSpec((n_sel, ncp), lambda b, i: (0, 0)),
                pl.BlockSpec((n_keys, LANE), lambda b, i: (0, 0)),
                pl.BlockSpec((1, NSA_WIDTH), lambda b, i: (0, 0)),
            ],
            out_specs=pl.BlockSpec((None, Q_BLOCK, NSA_WIDTH), lambda b, i: (b, i, 0)),
            out_shape=jax.ShapeDtypeStruct((batch, SEL_KT, NSA_WIDTH), BF16),
            compiler_params=pltpu.CompilerParams(
                dimension_semantics=("arbitrary",) * 2, vmem_limit_bytes=VMEM_LIMIT),
        )(za3, kcv, kcv, za3, za3, za3, za3, zb3, ovl_t, onehot, g_attn))
    return jnp.concatenate(outs, axis=1).reshape(batch * seq, NSA_WIDTH)


def _lru_kernel(zx_ref, zy_ref, cw_ref, cb_ref, wa_ref, ba_ref, wi_ref, bi_ref, lam_ref, gr_ref,
                o_ref, xpad, a_scr, u_scr, h_scr, carry):
    tt, width = zx_ref.shape
    ti = pl.program_id(1)

    @pl.when(ti == 0)
    def _():
        xpad[0:SUBLANE, :] = jnp.zeros((SUBLANE, width), F32)
        carry[...] = jnp.zeros_like(carry)

    xb = zx_ref[...]
    xpad[SUBLANE:SUBLANE + tt, :] = xb
    xc = cb_ref[...] + cw_ref[CONV_WIDTH - 1:CONV_WIDTH, :] * xb
    for k in range(1, CONV_WIDTH):
        xc = xc + cw_ref[CONV_WIDTH - 1 - k:CONV_WIDTH - k, :] * xpad[SUBLANE - k:SUBLANE - k + tt, :]
    xpad[0:SUBLANE, :] = xb[tt - SUBLANE:tt, :]

    sp = jax.nn.softplus(-lam_ref[...])
    row8 = lax.broadcasted_iota(jnp.int32, (tt, LRU_BLOCK_DIM), 0) % SUBLANE
    for hb in range(LRU_BLOCKS):
        sl = slice(hb * LRU_BLOCK_DIM, (hb + 1) * LRU_BLOCK_DIM)
        xs = xc[:, sl]
        xs16 = xs.astype(BF16)
        r = jax.nn.sigmoid(_dot(xs16, wa_ref[hb]) + ba_ref[:, sl])
        ig = jax.nn.sigmoid(_dot(xs16, wi_ref[hb]) + bi_ref[:, sl])
        log_a = -LRU_C * r * sp[:, sl]
        a = jnp.exp(log_a)
        u = jnp.sqrt(-jnp.tanh(log_a) * (a * a + 1.0)) * (ig * xs)
        for s in (1, 2, 4):
            ok = row8 >= s
            a_sh = pltpu.roll(a, s, axis=0)
            u_sh = pltpu.roll(u, s, axis=0)
            u = jnp.where(ok, u + a * u_sh, u)
            a = jnp.where(ok, a * a_sh, a)
        a_scr[:, sl] = a
        u_scr[:, sl] = u

    def group(gi, c):
        r0 = pl.multiple_of(gi * SUBLANE, SUBLANE)
        h = u_scr[pl.ds(r0, SUBLANE), :] + a_scr[pl.ds(r0, SUBLANE), :] * c
        h_scr[pl.ds(r0, SUBLANE), :] = h
        return jnp.broadcast_to(h[SUBLANE - 1:SUBLANE, :], (SUBLANE, width))

    c = lax.fori_loop(0, tt // SUBLANE, group, carry[...], unroll=4)
    carry[...] = c
    o_ref[...] = _rms(h_scr[...] * jax.nn.gelu(zy_ref[...]), gr_ref[...]).astype(o_ref.dtype)


def _rglru(zb, conv_w, conv_b, w_a, b_a, w_i, b_i, lam, g_rec, batch, seq, tt=256):
    nt = seq // tt
    wt = LRU_WIDTH // LANE
    full = lambda a: pl.BlockSpec(a.shape, lambda b, t: (0,) * a.ndim)
    zspec = lambda tile: pl.BlockSpec((tt, LRU_WIDTH), lambda b, t: (b * nt + t, tile // wt))
    return pl.pallas_call(
        _lru_kernel,
        name="rglru",
        grid=(batch, nt),
        in_specs=[zspec(BT_X), zspec(BT_Y), full(conv_w), full(conv_b), full(w_a), full(b_a),
                  full(w_i), full(b_i), full(lam), full(g_rec)],
        out_specs=pl.BlockSpec((tt, LRU_WIDTH), lambda b, t: (b * nt + t, 0)),
        out_shape=jax.ShapeDtypeStruct((batch * seq, LRU_WIDTH), BF16),
        scratch_shapes=[pltpu.VMEM((tt + SUBLANE, LRU_WIDTH), F32), pltpu.VMEM((tt, LRU_WIDTH), F32),
                        pltpu.VMEM((tt, LRU_WIDTH), F32), pltpu.VMEM((tt, LRU_WIDTH), F32),
                        pltpu.VMEM((SUBLANE, LRU_WIDTH), F32)],
        compiler_params=pltpu.CompilerParams(
            dimension_semantics=("arbitrary", "arbitrary"), vmem_limit_bytes=VMEM_LIMIT),
    )(zb, zb, conv_w, conv_b, w_a, b_a, w_i, b_i, lam, g_rec)


def _out_kernel(oa_ref, or_ref, w_ref, post_ref, h_ref, o_ref):
    for c in range(h_ref.shape[0] // ROW_CHUNK):
        rows = slice(c * ROW_CHUNK, (c + 1) * ROW_CHUNK)
        y = (_dot(oa_ref[rows, :], w_ref[0:NSA_WIDTH, :])
             + _dot(or_ref[rows, :], w_ref[NSA_WIDTH:NSA_WIDTH + LRU_WIDTH, :]))
        o_ref[rows, :] = h_ref[rows, :] + _rms(y, post_ref[...])


def _out_proj(o_attn, o_rec, w_out, post_g, h, tm=512):
    n, d = h.shape
    row = lambda i: (i, 0)
    full = lambda a: pl.BlockSpec(a.shape, lambda i: (0,) * a.ndim)
    return pl.pallas_call(
        _out_kernel,
        name="out_proj",
        grid=(n // tm,),
        in_specs=[pl.BlockSpec((tm, NSA_WIDTH), row), pl.BlockSpec((tm, LRU_WIDTH), row),
                  full(w_out), full(post_g), pl.BlockSpec((tm, d), row)],
        out_specs=pl.BlockSpec((tm, d), row),
        out_shape=jax.ShapeDtypeStruct((n, d), F32),
        compiler_params=pltpu.CompilerParams(
            dimension_semantics=("arbitrary",), vmem_limit_bytes=VMEM_LIMIT),
    )(o_attn, o_rec, w_out, post_g, h)


def _ple_kernel(h_ref, p_ref, pre_ref, wg_ref, wp_ref, post_ref, o_ref):
    for c in range(h_ref.shape[0] // ROW_CHUNK):
        rows = slice(c * ROW_CHUNK, (c + 1) * ROW_CHUNK)
        h = h_ref[rows, :]
        gate = jax.nn.sigmoid(_dot(_rms(h, pre_ref[...]).astype(BF16), wg_ref[...]))
        pp = _dot(p_ref[rows, :].astype(BF16), wp_ref[...])
        o_ref[rows, :] = h + _rms(gate * pp, post_ref[...])


def _ple(h, p, pre_g, w_gate, w_proj, post_g, tm=512):
    n, d = h.shape
    row = lambda i: (i, 0)
    full = lambda a: pl.BlockSpec(a.shape, lambda i: (0,) * a.ndim)
    return pl.pallas_call(
        _ple_kernel,
        name="ple",
        grid=(n // tm,),
        in_specs=[pl.BlockSpec((tm, d), row), pl.BlockSpec((tm, p.shape[1]), row),
                  full(pre_g), full(w_gate), full(w_proj), full(post_g)],
        out_specs=pl.BlockSpec((tm, d), row),
        out_shape=jax.ShapeDtypeStruct((n, d), F32),
        compiler_params=pltpu.CompilerParams(
            dimension_semantics=("arbitrary",), vmem_limit_bytes=VMEM_LIMIT),
    )(h, p, pre_g, w_gate, w_proj, post_g)


def _selection_constants(seq):
    ncp = seq // CMP_STRIDE
    n_cmp = (seq - CMP_LEN) // CMP_STRIDE + 1
    n_sel = seq // SEL_LEN
    c = np.arange(ncp)[None, :]
    s = np.arange(n_sel)[:, None]
    ovl_t = ((c * CMP_STRIDE < s * SEL_LEN + SEL_LEN) & (c * CMP_STRIDE + CMP_LEN - 1 >= s * SEL_LEN) & (c < n_cmp))
    onehot = (np.arange(seq)[:, None] // SEL_LEN) == np.arange(LANE)[None, :]
    return jnp.asarray(ovl_t, F32), jnp.asarray(onehot * MASK_BIG, BF16)


def _pad_cols(a, width):
    return jnp.pad(a, ((0, 0), (0, width - a.shape[1])))


def kernel(x, p, positions, ff1_pre_g, ff1_post_g, ff1_w_gate, ff1_w_up, ff1_w_down, mix_pre_g, mix_post_g, w_in, cmp_pos_k, cmp_pos_v, cmp_k_w1, cmp_k_w2, cmp_v_w1, cmp_v_w2, nsa_gate_b, conv_w, conv_b, rg_w_a, rg_b_a, rg_w_i, rg_b_i, rg_lambda, attn_out_g, rec_out_g, w_out, ff2_pre_g, ff2_post_g, ff2_w_gate, ff2_w_up, ff2_w_down, ple_pre_g, ple_post_g, w_ple_gate, w_ple_proj):
    batch, seq, d = x.shape
    depth = p.shape[0]
    n = batch * seq
    vec = lambda a: a.reshape(1, -1)
    gate_cols = NSA_HEADS * N_GATES
    grp_gates = GQA_GROUP * N_GATES
    o_q, o_kc, o_vc, o_ks, o_vs, o_kw, o_vw = (NSA_WIDTH * 0,) + tuple(NSA_WIDTH + k * KV_WIDTH for k in range(6))
    o_g = NSA_WIDTH + 6 * KV_WIDTH
    o_x = o_g + gate_cols
    o_y = o_x + LRU_WIDTH

    half = jnp.arange(ROPE_HALF, dtype=F32)
    inv_freq = ROPE_THETA ** (-half / ROPE_HALF)
    invf = jnp.concatenate([inv_freq, inv_freq, jnp.zeros((LANE - ROPE_DIM,), F32)]).reshape(1, LANE)
    ovl_t, onehot = _selection_constants(seq)
    pos = positions.reshape(n, 1)

    h = x.reshape(n, d)
    for i in range(depth):
        wi = w_in[i]
        cols = lambda o, w: wi[:, o:o + w]
        w_in_z = jnp.concatenate(
            [cols(o_q, NSA_WIDTH), cols(o_ks, KV_WIDTH), cols(o_kw, KV_WIDTH), cols(o_vs, KV_WIDTH),
             cols(o_vw, KV_WIDTH), cols(o_x, LRU_WIDTH), cols(o_y, LRU_WIDTH), cols(o_kc, KV_WIDTH),
             cols(o_vc, KV_WIDTH)]
            + [_pad_cols(cols(o_g + g * grp_gates, grp_gates), LANE) for g in range(NSA_KV_HEADS)]
            + [jnp.zeros((d, B_WIDTH - (BT_GATE + NSA_KV_HEADS) * LANE), F32)],
            axis=1).astype(BF16)
        gb = nsa_gate_b[i].reshape(NSA_KV_HEADS, grp_gates)
        gate_bias = _pad_cols(jnp.pad(gb, ((0, 0), (0, LANE - grp_gates))).reshape(1, NSA_KV_HEADS * LANE), PROJ_TN)

        h = _ffn(h, vec(ff1_pre_g[i]), ff1_w_gate[i].astype(BF16), ff1_w_up[i].astype(BF16),
                 ff1_w_down[i].astype(BF16), vec(ff1_post_g[i]))
        za, zb = _proj(pos, h, vec(mix_pre_g[i]), w_in_z, gate_bias, invf)
        kcv = _compress(
            zb, jnp.stack([cmp_pos_k[i], cmp_pos_v[i]]),
            jnp.stack([cmp_k_w1[i], cmp_v_w1[i]]).reshape(2, CMP_LEN, HEAD_DIM, -1).astype(BF16),
            jnp.stack([cmp_k_w2[i], cmp_v_w2[i]]).astype(BF16), batch, seq)
        o_attn = _attention(za, zb, kcv, ovl_t, onehot, vec(attn_out_g[i]), batch, seq)
        o_rec = _rglru(zb, conv_w[i], vec(conv_b[i]), rg_w_a[i].astype(BF16), vec(rg_b_a[i]),
                       rg_w_i[i].astype(BF16), vec(rg_b_i[i]), vec(rg_lambda[i]), vec(rec_out_g[i]), batch, seq)
        h = _out_proj(o_attn, o_rec, w_out[i].astype(BF16), vec(mix_post_g[i]), h)
        h = _ffn(h, vec(ff2_pre_g[i]), ff2_w_gate[i].astype(BF16), ff2_w_up[i].astype(BF16),
                 ff2_w_down[i].astype(BF16), vec(ff2_post_g[i]))
        h = _ple(h, p[i].reshape(n, -1), vec(ple_pre_g[i]), w_ple_gate[i].astype(BF16),
                 w_ple_proj[i].astype(BF16), vec(ple_post_g[i]))
    return h.reshape(batch, seq, d)
```

```python
import functools

import numpy as np
import jax
import jax.numpy as jnp
from jax import lax
from jax.experimental import pallas as pl
from jax.experimental.pallas import tpu as pltpu

F32 = jnp.float32
BF16 = jnp.bfloat16

D_MODEL = 2048
NSA_HEADS = 8
NSA_KV_HEADS = 2
GQA_GROUP = NSA_HEADS // NSA_KV_HEADS
HEAD_DIM = 128
NSA_WIDTH = NSA_HEADS * HEAD_DIM
KV_WIDTH = NSA_KV_HEADS * HEAD_DIM
ROPE_DIM = HEAD_DIM // 4
ROPE_HALF = ROPE_DIM // 2
ROPE_THETA = 500000.0
CMP_LEN = 32
CMP_STRIDE = 16
SEL_LEN = 64
SEL_TOPK = 16
WINDOW = 512
Q_BLOCK = 128
N_GATES = 3
LRU_WIDTH = 1024
LRU_BLOCKS = 8
LRU_BLOCK_DIM = LRU_WIDTH // LRU_BLOCKS
CONV_WIDTH = 4
LRU_C = 8.0
RMS_EPS = 1e-6
NEG = -1e30
SEL_FORCE = 1e4

LANE = 128
SUBLANE = 8
VMEM_LIMIT = 62 * 1024 * 1024

AT_Q, AT_KS, AT_KW, AT_VS, AT_VW = 0, 8, 10, 12, 14
A_WIDTH = 16 * LANE
BT_X, BT_Y, BT_KC, BT_VC, BT_GATE = 0, 8, 16, 18, 20
B_WIDTH = 24 * LANE
PROJ_TN = 512
PROJ_A_STEPS = A_WIDTH // PROJ_TN
PROJ_STEPS = (A_WIDTH + B_WIDTH) // PROJ_TN
PROJ_HEADS = PROJ_TN // LANE
ROW_CHUNK = 256
PROJ_CHUNK = ROW_CHUNK
FFN_CHUNK = ROW_CHUNK

Q_SCALE = HEAD_DIM ** -0.5 * 1.4426950408889634
SEL_KT = 512
MASK_BIG = 2.0 ** 100


def _rms(x, g):
    return x * lax.rsqrt(jnp.mean(x * x, axis=-1, keepdims=True) + RMS_EPS) * g


def _dot(a, b):
    return jnp.dot(a, b, preferred_element_type=F32)


def _dot_nt(a, b):
    return lax.dot_general(a, b, (((1,), (1,)), ((), ())), preferred_element_type=F32)


def _ffn_kernel(h_ref, pre_ref, wg_ref, wu_ref, wd_ref, post_ref, o_ref, u_scr, *, n_steps):
    j = pl.program_id(1)

    def step(first, last):
        for c in range(h_ref.shape[0] // FFN_CHUNK):
            rows = slice(c * FFN_CHUNK, (c + 1) * FFN_CHUNK)
            if first:
                u = _rms(h_ref[rows, :], pre_ref[...]).astype(BF16)
                u_scr[rows, :] = u
            else:
                u = u_scr[rows, :]
            g = _dot(u, wg_ref[...])
            up = _dot(u, wu_ref[...])
            acc = _dot(((g * jax.nn.sigmoid(g)) * up).astype(BF16), wd_ref[...])
            if not first:
                acc = o_ref[rows, :] + acc
            if last:
                acc = h_ref[rows, :] + 0.5 * _rms(acc, post_ref[...])
            o_ref[rows, :] = acc

    if n_steps == 1:
        step(True, True)
    else:
        pl.when(j == 0)(lambda: step(True, False))
        pl.when((j > 0) & (j < n_steps - 1))(lambda: step(False, False))
        pl.when(j == n_steps - 1)(lambda: step(False, True))


def _ffn(h, pre_g, wg, wu, wd, post_g, tm=1024, tf=512):
    n, d = h.shape
    dff = wg.shape[1]
    tf = min(tf, dff)
    row = lambda i, j: (i, 0)
    vec = pl.BlockSpec((1, d), lambda i, j: (0, 0))
    return pl.pallas_call(
        functools.partial(_ffn_kernel, n_steps=dff // tf),
        name="ffn",
        grid=(n // tm, dff // tf),
        in_specs=[
            pl.BlockSpec((tm, d), row), vec,
            pl.BlockSpec((d, tf), lambda i, j: (0, j)),
            pl.BlockSpec((d, tf), lambda i, j: (0, j)),
            pl.BlockSpec((tf, d), lambda i, j: (j, 0)),
            vec,
        ],
        out_specs=pl.BlockSpec((tm, d), row),
        out_shape=jax.ShapeDtypeStruct((n, d), F32),
        scratch_shapes=[pltpu.VMEM((tm, d), BF16)],
        compiler_params=pltpu.CompilerParams(
            dimension_semantics=("arbitrary", "arbitrary"), vmem_limit_bytes=VMEM_LIMIT),
    )(h, pre_g, wg, wu, wd, post_g)


def _proj_kernel(pos_ref, h_ref, pre_ref, w_ref, gb_ref, invf_ref, za_ref, zb_ref, u_scr, cos_scr, s1_scr, s2_scr):
    j = pl.program_id(1)

    def prepare(rows):
        u_scr[rows, :] = _rms(h_ref[rows, :], pre_ref[...]).astype(BF16)
        ang = pos_ref[rows, :].astype(F32) * invf_ref[...]
        lane = lax.broadcasted_iota(jnp.int32, ang.shape, 1)
        sin = jnp.sin(ang)
        cos_scr[rows, :] = jnp.cos(ang)
        s1_scr[rows, :] = jnp.where((lane >= ROPE_HALF) & (lane < ROPE_DIM), sin, 0.0)
        s2_scr[rows, :] = jnp.where(lane < ROPE_HALF, -sin, 0.0)

    def head(z, rows, hd, rope, mul=None):
        x = z[:, hd * LANE:(hd + 1) * LANE]
        if rope:
            x = (x * cos_scr[rows, :]
                 + pltpu.roll(x, ROPE_HALF, axis=1) * s1_scr[rows, :]
                 + pltpu.roll(x, LANE - ROPE_HALF, axis=1) * s2_scr[rows, :])
        return x if mul is None else x * mul

    def chunks(first=False):
        for c in range(u_scr.shape[0] // PROJ_CHUNK):
            rows = slice(c * PROJ_CHUNK, (c + 1) * PROJ_CHUNK)
            if first:
                prepare(rows)
            yield rows, _dot(u_scr[rows, :], w_ref[...])

    def store(ref, rope_heads, mul=None, first=False):
        for rows, z in chunks(first):
            for hd in range(PROJ_HEADS):
                ref[rows, hd * LANE:(hd + 1) * LANE] = head(z, rows, hd, hd < rope_heads, mul).astype(ref.dtype)

    @pl.when(j == 0)
    def _():
        store(za_ref, PROJ_HEADS, Q_SCALE, first=True)

    @pl.when((j > 0) & (j < AT_KS // PROJ_HEADS))
    def _():
        store(za_ref, PROJ_HEADS, Q_SCALE)

    @pl.when(j == AT_KS // PROJ_HEADS)
    def _():
        store(za_ref, PROJ_HEADS)

    @pl.when(j == AT_VS // PROJ_HEADS)
    def _():
        store(za_ref, 0)

    @pl.when((j >= PROJ_A_STEPS) & (j < PROJ_A_STEPS + BT_KC // PROJ_HEADS))
    def _():
        store(zb_ref, 0)

    @pl.when(j == PROJ_A_STEPS + BT_KC // PROJ_HEADS)
    def _():
        store(zb_ref, NSA_KV_HEADS)

    @pl.when(j == PROJ_A_STEPS + BT_GATE // PROJ_HEADS)
    def _():
        for rows, z in chunks():
            zb_ref[rows, :] = jax.nn.sigmoid(z + gb_ref[...])


def _proj(pos, h, pre_g, w, gb, invf, tm=1024):
    n, d = h.shape
    return pl.pallas_call(
        _proj_kernel,
        name="mix_proj",
        grid=(n // tm, PROJ_STEPS),
        in_specs=[
            pl.BlockSpec((tm, 1), lambda i, j: (i, 0)),
            pl.BlockSpec((tm, d), lambda i, j: (i, 0)),
            pl.BlockSpec((1, d), lambda i, j: (0, 0)),
            pl.BlockSpec((d, PROJ_TN), lambda i, j: (0, j)),
            pl.BlockSpec((1, PROJ_TN), lambda i, j: (0, 0)),
            pl.BlockSpec((1, LANE), lambda i, j: (0, 0)),
        ],
        out_specs=[pl.BlockSpec((tm, PROJ_TN), lambda i, j: (i, jnp.minimum(j, PROJ_A_STEPS - 1))),
                   pl.BlockSpec((tm, PROJ_TN), lambda i, j: (i, jnp.maximum(j - PROJ_A_STEPS, 0)))],
        out_shape=[jax.ShapeDtypeStruct((n, A_WIDTH), BF16), jax.ShapeDtypeStruct((n, B_WIDTH), F32)],
        scratch_shapes=[pltpu.VMEM((tm, d), BF16)] + [pltpu.VMEM((tm, LANE), F32)] * 3,
        compiler_params=pltpu.CompilerParams(
            dimension_semantics=("arbitrary", "arbitrary"), vmem_limit_bytes=VMEM_LIMIT),
    )(pos, h, pre_g, w, gb, invf)


def _cmp_kernel(x_ref, pos_ref, w1_ref, w2_ref, o_ref):
    ncp = o_ref.shape[0]
    hidden_w = w1_ref.shape[-1]
    acc_a = jnp.zeros((ncp, hidden_w), F32)
    acc_b = jnp.zeros((ncp, hidden_w), F32)
    for l in range(CMP_STRIDE):
        xl = x_ref[pl.ds(l, ncp, stride=CMP_STRIDE), :]
        acc_a += _dot((xl + pos_ref[l:l + 1, :]).astype(BF16), w1_ref[l])
        acc_b += _dot((xl + pos_ref[CMP_STRIDE + l:CMP_STRIDE + l + 1, :]).astype(BF16), w1_ref[CMP_STRIDE + l])
    hidden = acc_a + pltpu.roll(acc_b, ncp - 1, axis=0)
    o_ref[...] = _dot(jax.nn.gelu(hidden).astype(BF16), w2_ref[...])


def _compress(zb, pos_kv, w1, w2, batch, seq):
    ncp = seq // CMP_STRIDE
    hidden_w = w1.shape[-1]
    return pl.pallas_call(
        _cmp_kernel,
        name="compress",
        grid=(2, batch, NSA_KV_HEADS),
        in_specs=[
            pl.BlockSpec((seq, LANE), lambda k, b, g: (b, BT_KC + 2 * k + g)),
            pl.BlockSpec((None, CMP_LEN, HEAD_DIM), lambda k, b, g: (k, 0, 0)),
            pl.BlockSpec((None, CMP_LEN, HEAD_DIM, hidden_w), lambda k, b, g: (k, 0, 0, 0)),
            pl.BlockSpec((None, hidden_w, HEAD_DIM), lambda k, b, g: (k, 0, 0)),
        ],
        out_specs=pl.BlockSpec((None, None, None, ncp, HEAD_DIM), lambda k, b, g: (k, b, g, 0, 0)),
        out_shape=jax.ShapeDtypeStruct((2, batch, NSA_KV_HEADS, ncp, HEAD_DIM), F32),
        compiler_params=pltpu.CompilerParams(
            dimension_semantics=("arbitrary",) * 3, vmem_limit_bytes=VMEM_LIMIT),
    )(zb, pos_kv, w1, w2)


def _masked_exp(s, mask):
    s = s + jnp.where(mask, 0.0, -MASK_BIG)
    e = jnp.exp2(s - jnp.max(s, axis=-1, keepdims=True))
    return e, jnp.sum(e, axis=-1, keepdims=True)


def _attn_kernel(q_ref, kc_ref, vc_ref, ks_ref, vs_ref, kw_ref, vw_ref, gz_ref, ovl_ref, oh_ref, ga_ref, o_ref,
                 *, first_block):
    G, R, tq = NSA_KV_HEADS, GQA_GROUP, Q_BLOCK
    n_keys = ks_ref.shape[0]
    top_k = min(SEL_TOPK, ovl_ref.shape[0])
    n_blk = n_keys // SEL_LEN
    n_cmp = min(kc_ref.shape[1], -(-(n_keys // CMP_STRIDE) // LANE) * LANE)
    t0 = (first_block + pl.program_id(1)) * tq
    t_idx = t0 + lax.broadcasted_iota(jnp.int32, (1, tq, 1), 1)
    gsl = lambda g: slice(g * HEAD_DIM, (g + 1) * HEAD_DIM)

    def front(g):
        q = q_ref[:, g * R * HEAD_DIM:(g + 1) * R * HEAD_DIM]
        q4 = jnp.concatenate([q[:, r * HEAD_DIM:(r + 1) * HEAD_DIM] for r in range(R)], axis=0)

        s = _dot_nt(q4, kc_ref[g, 0:n_cmp, :].astype(BF16)).reshape(R, tq, n_cmp)
        c_idx = lax.broadcasted_iota(jnp.int32, (1, 1, n_cmp), 2)
        e, l = _masked_exp(s, (c_idx * CMP_STRIDE + (CMP_LEN - 1)) <= t_idx)
        pc = e * (jnp.where(t_idx >= CMP_LEN - 1, 1.0, 0.0) / l)
        o_cmp = _dot(pc.reshape(R * tq, n_cmp).astype(BF16), vc_ref[g, 0:n_cmp, :].astype(BF16))
        o_cmp = o_cmp.reshape(R, tq, HEAD_DIM)
        if n_blk <= top_k:
            return q4, jnp.concatenate([q4, jnp.zeros_like(q4)], axis=1), o_cmp

        pcs = pc[0]
        for r in range(1, R):
            pcs = pcs + pc[r]
        hi = pcs.astype(BF16)
        lo = (pcs - hi.astype(F32)).astype(BF16)
        ovl = ovl_ref[0:n_blk, 0:n_cmp]
        imp = _dot_nt(ovl, hi) + _dot_nt(ovl, lo)
        blk = lax.broadcasted_iota(jnp.int32, (n_blk, tq), 0)
        tcol = t0 + lax.broadcasted_iota(jnp.int32, (n_blk, tq), 1)
        forced = (blk == lax.shift_right_logical(tcol, int(np.log2(SEL_LEN)))) | (blk == 0)
        v = jnp.where(forced, SEL_FORCE, jnp.where(blk * SEL_LEN <= tcol, imp, -SEL_FORCE))
        sub = lax.broadcasted_iota(jnp.int32, (SUBLANE, tq), 0)
        groups = [v[k:k + SUBLANE, :] for k in range(0, n_blk, SUBLANE)]
        ranks = [jnp.zeros((SUBLANE, tq), F32) for _ in groups]
        for sp in range(n_blk):
            other = v[sp:sp + 1, :]
            for gi, vg in enumerate(groups):
                first = gi * SUBLANE
                if first > sp:
                    beats = other >= vg
                elif first + SUBLANE - 1 <= sp:
                    beats = other > vg
                else:
                    beats = (other > vg) | ((other == vg) & (sub > sp - first))
                ranks[gi] = ranks[gi] + jnp.where(beats, 1.0, 0.0)
        rank = jnp.concatenate(ranks, axis=0)
        unsel = jnp.where(rank < float(top_k), 0.0, -1.0)
        unsel = jnp.concatenate([unsel, jnp.zeros((LANE - n_blk, tq), F32)], axis=0).T.astype(BF16)
        q_aug = jnp.concatenate([q4, jnp.concatenate([unsel] * R, axis=0)], axis=1)
        return q4, q_aug, o_cmp

    fronts = [front(g) for g in range(G)]

    n_full = n_keys - SEL_KT
    kp = n_full + lax.broadcasted_iota(jnp.int32, (1, 1, SEL_KT), 2)
    causal_bias = jnp.where(kp <= t_idx, 0.0, -MASK_BIG)

    def selected(g):
        k_aug = jnp.concatenate([ks_ref[:, gsl(g)], oh_ref[...]], axis=1)
        s = _dot_nt(fronts[g][1], k_aug).reshape(R, tq, n_keys)
        s_last = s[:, :, n_full:] + causal_bias
        m = jnp.max(s_last, axis=-1, keepdims=True)
        if n_full:
            s_full = s[:, :, :n_full]
            m = jnp.maximum(m, jnp.max(s_full, axis=-1, keepdims=True))
        p = jnp.exp2(s_last - m)
        l = jnp.sum(p, axis=-1, keepdims=True)
        acc = _dot(p.reshape(R * tq, SEL_KT).astype(BF16), vs_ref[n_full:n_keys, gsl(g)])
        if n_full:
            p = jnp.exp2(s_full - m)
            l = l + jnp.sum(p, axis=-1, keepdims=True)
            acc = acc + _dot(p.reshape(R * tq, n_full).astype(BF16), vs_ref[0:n_full, gsl(g)])
        return acc.reshape(R, tq, HEAD_DIM) * (1.0 / l)

    wk = min(WINDOW + tq, n_keys)
    w0 = pl.multiple_of(jnp.maximum(t0 - WINDOW, 0), tq)
    kp = w0 + lax.broadcasted_iota(jnp.int32, (1, 1, wk), 2)
    win_mask = (kp <= t_idx) & (kp > t_idx - WINDOW)

    heads = []
    for g in range(G):
        q4, _, o_cmp = fronts[g]
        o_sel = selected(g)

        s = _dot_nt(q4, kw_ref[pl.ds(w0, wk), gsl(g)]).reshape(R, tq, wk)
        e, l = _masked_exp(s, win_mask)
        o_win = _dot(e.reshape(R * tq, wk).astype(BF16), vw_ref[pl.ds(w0, wk), gsl(g)]).reshape(R, tq, HEAD_DIM)
        o_win = o_win * (1.0 / l)

        gz = gz_ref[:, gsl(g)]
        for r in range(R):
            c = r * N_GATES
            heads.append(gz[:, c:c + 1] * o_cmp[r] + gz[:, c + 1:c + 2] * o_sel[r] + gz[:, c + 2:c + 3] * o_win[r])
    o_ref[...] = _rms(jnp.concatenate(heads, axis=1), ga_ref[...]).astype(o_ref.dtype)


def _attention(za, zb, kcv, ovl_t, onehot, g_attn, batch, seq):
    ncp = seq // CMP_STRIDE
    n_sel = seq // SEL_LEN
    G = NSA_KV_HEADS
    blocks = SEL_KT // Q_BLOCK
    za3 = za.reshape(batch, seq, A_WIDTH)
    zb3 = zb.reshape(batch, seq, B_WIDTH)
    outs = []
    for c in range(seq // SEL_KT):
        n_keys = (c + 1) * SEL_KT
        q_row = lambda b, i, c=c: (b, c * blocks + i, 0)
        kv_spec = lambda tile: pl.BlockSpec((None, n_keys, G * LANE), lambda b, i: (b, 0, tile // G))
        outs.append(pl.pallas_call(
            functools.partial(_attn_kernel, first_block=c * blocks),
            name=f"attention_{c}",
            grid=(batch, blocks),
            in_specs=[
                pl.BlockSpec((None, Q_BLOCK, NSA_WIDTH), q_row),
                pl.BlockSpec((None, None, G, ncp, HEAD_DIM), lambda b, i: (0, b, 0, 0, 0)),
                pl.BlockSpec((None, None, G, ncp, HEAD_DIM), lambda b, i: (1, b, 0, 0, 0)),
                kv_spec(AT_KS), kv_spec(AT_VS), kv_spec(AT_KW), kv_spec(AT_VW),
                pl.BlockSpec((None, Q_BLOCK, G * LANE), lambda b, i, c=c: (b, c * blocks + i, BT_GATE // G)),
                pl.BlockSpec((n_sel, ncp), lambda b, i: (0, 0)),
                pl.BlockSpec((n_keys, LANE), lambda b, i: (0, 0)),
                pl.BlockSpec((1, NSA_WIDTH), lambda b, i: (0, 0)),
            ],
            out_specs=pl.BlockSpec((None, Q_BLOCK, NSA_WIDTH), lambda b, i: (b, i, 0)),
            out_shape=jax.ShapeDtypeStruct((batch, SEL_KT, NSA_WIDTH), BF16),
            compiler_params=pltpu.CompilerParams(
                dimension_semantics=("arbitrary",) * 2, vmem_limit_bytes=VMEM_LIMIT),
        )(za3, kcv, kcv, za3, za3, za3, za3, zb3, ovl_t, onehot, g_attn))
    return jnp.concatenate(outs, axis=1).reshape(batch * seq, NSA_WIDTH)


def _lru_kernel(zx_ref, zy_ref, cw_ref, cb_ref, wa_ref, ba_ref, wi_ref, bi_ref, lam_ref, gr_ref,
                o_ref, xpad, a_scr, u_scr, h_scr, carry):
    tt, width = zx_ref.shape
    ti = pl.program_id(1)

    @pl.when(ti == 0)
    def _():
        xpad[0:SUBLANE, :] = jnp.zeros((SUBLANE, width), F32)
        carry[...] = jnp.zeros_like(carry)

    xb = zx_ref[...]
    xpad[SUBLANE:SUBLANE + tt, :] = xb
    xc = cb_ref[...] + cw_ref[CONV_WIDTH - 1:CONV_WIDTH, :] * xb
    for k in range(1, CONV_WIDTH):
        xc = xc + cw_ref[CONV_WIDTH - 1 - k:CONV_WIDTH - k, :] * xpad[SUBLANE - k:SUBLANE - k + tt, :]
    xpad[0:SUBLANE, :] = xb[tt - SUBLANE:tt, :]

    sp = jax.nn.softplus(-lam_ref[...])
    row8 = lax.broadcasted_iota(jnp.int32, (1, SUBLANE, LRU_BLOCK_DIM), 1)
    for hb in range(LRU_BLOCKS):
        sl = slice(hb * LRU_BLOCK_DIM, (hb + 1) * LRU_BLOCK_DIM)
        xs = xc[:, sl]
        xs16 = xs.astype(BF16)
        r = jax.nn.sigmoid(_dot(xs16, wa_ref[hb]) + ba_ref[:, sl])
        ig = jax.nn.sigmoid(_dot(xs16, wi_ref[hb]) + bi_ref[:, sl])
        log_a = -LRU_C * r * sp[:, sl]
        a = jnp.exp(log_a)
        u = jnp.sqrt(-jnp.tanh(log_a) * (a * a + 1.0)) * (ig * xs)
        a = a.reshape(tt // SUBLANE, SUBLANE, LRU_BLOCK_DIM)
        u = u.reshape(tt // SUBLANE, SUBLANE, LRU_BLOCK_DIM)
        for s in (1, 2, 4):
            ok = row8 >= s
            a_sh = pltpu.roll(a, s, axis=1)
            u_sh = pltpu.roll(u, s, axis=1)
            u = jnp.where(ok, u + a * u_sh, u)
            a = jnp.where(ok, a * a_sh, a)
        a_scr[:, sl] = a.reshape(tt, LRU_BLOCK_DIM)
        u_scr[:, sl] = u.reshape(tt, LRU_BLOCK_DIM)

    def group(gi, c):
        r0 = pl.multiple_of(gi * SUBLANE, SUBLANE)
        h = u_scr[pl.ds(r0, SUBLANE), :] + a_scr[pl.ds(r0, SUBLANE), :] * c
        h_scr[pl.ds(r0, SUBLANE), :] = h
        return jnp.broadcast_to(h[SUBLANE - 1:SUBLANE, :], (SUBLANE, width))

    c = lax.fori_loop(0, tt // SUBLANE, group, carry[...], unroll=4)
    carry[...] = c
    o_ref[...] = _rms(h_scr[...] * jax.nn.gelu(zy_ref[...]), gr_ref[...]).astype(o_ref.dtype)


def _rglru(zb, conv_w, conv_b, w_a, b_a, w_i, b_i, lam, g_rec, batch, seq, tt=256):
    nt = seq // tt
    wt = LRU_WIDTH // LANE
    full = lambda a: pl.BlockSpec(a.shape, lambda b, t: (0,) * a.ndim)
    zspec = lambda tile: pl.BlockSpec((tt, LRU_WIDTH), lambda b, t: (b * nt + t, tile // wt))
    return pl.pallas_call(
        _lru_kernel,
        name="rglru",
        grid=(batch, nt),
        in_specs=[zspec(BT_X), zspec(BT_Y), full(conv_w), full(conv_b), full(w_a), full(b_a),
                  full(w_i), full(b_i), full(lam), full(g_rec)],
        out_specs=pl.BlockSpec((tt, LRU_WIDTH), lambda b, t: (b * nt + t, 0)),
        out_shape=jax.ShapeDtypeStruct((batch * seq, LRU_WIDTH), BF16),
        scratch_shapes=[pltpu.VMEM((tt + SUBLANE, LRU_WIDTH), F32), pltpu.VMEM((tt, LRU_WIDTH), F32),
                        pltpu.VMEM((tt, LRU_WIDTH), F32), pltpu.VMEM((tt, LRU_WIDTH), F32),
                        pltpu.VMEM((SUBLANE, LRU_WIDTH), F32)],
        compiler_params=pltpu.CompilerParams(
            dimension_semantics=("arbitrary", "arbitrary"), vmem_limit_bytes=VMEM_LIMIT),
    )(zb, zb, conv_w, conv_b, w_a, b_a, w_i, b_i, lam, g_rec)


def _out_kernel(oa_ref, or_ref, w_ref, post_ref, h_ref, o_ref):
    for c in range(h_ref.shape[0] // ROW_CHUNK):
        rows = slice(c * ROW_CHUNK, (c + 1) * ROW_CHUNK)
        y = (_dot(oa_ref[rows, :], w_ref[0:NSA_WIDTH, :])
             + _dot(or_ref[rows, :], w_ref[NSA_WIDTH:NSA_WIDTH + LRU_WIDTH, :]))
        o_ref[rows, :] = h_ref[rows, :] + _rms(y, post_ref[...])


def _out_proj(o_attn, o_rec, w_out, post_g, h, tm=512):
    n, d = h.shape
    row = lambda i: (i, 0)
    full = lambda a: pl.BlockSpec(a.shape, lambda i: (0,) * a.ndim)
    return pl.pallas_call(
        _out_kernel,
        name="out_proj",
        grid=(n // tm,),
        in_specs=[pl.BlockSpec((tm, NSA_WIDTH), row), pl.BlockSpec((tm, LRU_WIDTH), row),
                  full(w_out), full(post_g), pl.BlockSpec((tm, d), row)],
        out_specs=pl.BlockSpec((tm, d), row),
        out_shape=jax.ShapeDtypeStruct((n, d), F32),
        compiler_params=pltpu.CompilerParams(
            dimension_semantics=("arbitrary",), vmem_limit_bytes=VMEM_LIMIT),
    )(o_attn, o_rec, w_out, post_g, h)


def _ple_kernel(h_ref, p_ref, pre_ref, wg_ref, wp_ref, post_ref, o_ref):
    for c in range(h_ref.shape[0] // ROW_CHUNK):
        rows = slice(c * ROW_CHUNK, (c + 1) * ROW_CHUNK)
        h = h_ref[rows, :]
        gate = jax.nn.sigmoid(_dot(_rms(h, pre_ref[...]).astype(BF16), wg_ref[...]))
        pp = _dot(p_ref[rows, :].astype(BF16), wp_ref[...])
        o_ref[rows, :] = h + _rms(gate * pp, post_ref[...])


def _ple(h, p, pre_g, w_gate, w_proj, post_g, tm=512):
    n, d = h.shape
    row = lambda i: (i, 0)
    full = lambda a: pl.BlockSpec(a.shape, lambda i: (0,) * a.ndim)
    return pl.pallas_call(
        _ple_kernel,
        name="ple",
        grid=(n // tm,),
        in_specs=[pl.BlockSpec((tm, d), row), pl.BlockSpec((tm, p.shape[1]), row),
                  full(pre_g), full(w_gate), full(w_proj), full(post_g)],
        out_specs=pl.BlockSpec((tm, d), row),
        out_shape=jax.ShapeDtypeStruct((n, d), F32),
        compiler_params=pltpu.CompilerParams(
            dimension_semantics=("arbitrary",), vmem_limit_bytes=VMEM_LIMIT),
    )(h, p, pre_g, w_gate, w_proj, post_g)


def _selection_constants(seq):
    ncp = seq // CMP_STRIDE
    n_cmp = (seq - CMP_LEN) // CMP_STRIDE + 1
    n_sel = seq // SEL_LEN
    c = np.arange(ncp)[None, :]
    s = np.arange(n_sel)[:, None]
    ovl_t = ((c * CMP_STRIDE < s * SEL_LEN + SEL_LEN) & (c * CMP_STRIDE + CMP_LEN - 1 >= s * SEL_LEN) & (c < n_cmp))
    onehot = (np.arange(seq)[:, None] // SEL_LEN) == np.arange(LANE)[None, :]
    return jnp.asarray(ovl_t, BF16), jnp.asarray(onehot * MASK_BIG, BF16)


def _pad_cols(a, width):
    return jnp.pad(a, ((0, 0), (0, width - a.shape[1])))


def kernel(x, p, positions, ff1_pre_g, ff1_post_g, ff1_w_gate, ff1_w_up, ff1_w_down, mix_pre_g, mix_post_g, w_in, cmp_pos_k, cmp_pos_v, cmp_k_w1, cmp_k_w2, cmp_v_w1, cmp_v_w2, nsa_gate_b, conv_w, conv_b, rg_w_a, rg_b_a, rg_w_i, rg_b_i, rg_lambda, attn_out_g, rec_out_g, w_out, ff2_pre_g, ff2_post_g, ff2_w_gate, ff2_w_up, ff2_w_down, ple_pre_g, ple_post_g, w_ple_gate, w_ple_proj):
    batch, seq, d = x.shape
    depth = p.shape[0]
    n = batch * seq
    vec = lambda a: a.reshape(1, -1)
    gate_cols = NSA_HEADS * N_GATES
    grp_gates = GQA_GROUP * N_GATES
    o_q, o_kc, o_vc, o_ks, o_vs, o_kw, o_vw = (NSA_WIDTH * 0,) + tuple(NSA_WIDTH + k * KV_WIDTH for k in range(6))
    o_g = NSA_WIDTH + 6 * KV_WIDTH
    o_x = o_g + gate_cols
    o_y = o_x + LRU_WIDTH

    half = jnp.arange(ROPE_HALF, dtype=F32)
    inv_freq = ROPE_THETA ** (-half / ROPE_HALF)
    invf = jnp.concatenate([inv_freq, inv_freq, jnp.zeros((LANE - ROPE_DIM,), F32)]).reshape(1, LANE)
    ovl_t, onehot = _selection_constants(seq)
    pos = positions.reshape(n, 1)

    h = x.reshape(n, d)
    for i in range(depth):
        wi = w_in[i]
        cols = lambda o, w: wi[:, o:o + w]
        w_in_z = jnp.concatenate(
            [cols(o_q, NSA_WIDTH), cols(o_ks, KV_WIDTH), cols(o_kw, KV_WIDTH), cols(o_vs, KV_WIDTH),
             cols(o_vw, KV_WIDTH), cols(o_x, LRU_WIDTH), cols(o_y, LRU_WIDTH), cols(o_kc, KV_WIDTH),
             cols(o_vc, KV_WIDTH)]
            + [_pad_cols(cols(o_g + g * grp_gates, grp_gates), LANE) for g in range(NSA_KV_HEADS)]
            + [jnp.zeros((d, B_WIDTH - (BT_GATE + NSA_KV_HEADS) * LANE), F32)],
            axis=1).astype(BF16)
        gb = nsa_gate_b[i].reshape(NSA_KV_HEADS, grp_gates)
        gate_bias = _pad_cols(jnp.pad(gb, ((0, 0), (0, LANE - grp_gates))).reshape(1, NSA_KV_HEADS * LANE), PROJ_TN)

        h = _ffn(h, vec(ff1_pre_g[i]), ff1_w_gate[i].astype(BF16), ff1_w_up[i].astype(BF16),
                 ff1_w_down[i].astype(BF16), vec(ff1_post_g[i]))
        za, zb = _proj(pos, h, vec(mix_pre_g[i]), w_in_z, gate_bias, invf)
        kcv = _compress(
            zb, jnp.stack([cmp_pos_k[i], cmp_pos_v[i]]),
            jnp.stack([cmp_k_w1[i], cmp_v_w1[i]]).reshape(2, CMP_LEN, HEAD_DIM, -1).astype(BF16),
            jnp.stack([cmp_k_w2[i], cmp_v_w2[i]]).astype(BF16), batch, seq)
        o_attn = _attention(za, zb, kcv, ovl_t, onehot, vec(attn_out_g[i]), batch, seq)
        o_rec = _rglru(zb, conv_w[i], vec(conv_b[i]), rg_w_a[i].astype(BF16), vec(rg_b_a[i]),
                       rg_w_i[i].astype(BF16), vec(rg_b_i[i]), vec(rg_lambda[i]), vec(rec_out_g[i]), batch, seq)
        h = _out_proj(o_attn, o_rec, w_out[i].astype(BF16), vec(mix_post_g[i]), h)
        h = _ffn(h, vec(ff2_pre_g[i]), ff2_w_gate[i].astype(BF16), ff2_w_up[i].astype(BF16),
                 ff2_w_down[i].astype(BF16), vec(ff2_post_g[i]))
        h = _ple(h, p[i].reshape(n, -1), vec(ple_pre_g[i]), w_ple_gate[i].astype(BF16),
                 w_ple_proj[i].astype(BF16), vec(ple_post_g[i]))
    return h.reshape(batch, seq, d)
```

```python
import functools

import numpy as np
import jax
import jax.numpy as jnp
from jax import lax
from jax.experimental import pallas as pl
from jax.experimental.pallas import tpu as pltpu

F32 = jnp.float32
BF16 = jnp.bfloat16

D_MODEL = 2048
NSA_HEADS = 8
NSA_KV_HEADS = 2
GQA_GROUP = NSA_HEADS // NSA_KV_HEADS
HEAD_DIM = 128
NSA_WIDTH = NSA_HEADS * HEAD_DIM
KV_WIDTH = NSA_KV_HEADS * HEAD_DIM
ROPE_DIM = HEAD_DIM // 4
ROPE_HALF = ROPE_DIM // 2
ROPE_THETA = 500000.0
CMP_LEN = 32
CMP_STRIDE = 16
SEL_LEN = 64
SEL_TOPK = 16
WINDOW = 512
Q_BLOCK = 128
N_GATES = 3
LRU_WIDTH = 1024
LRU_BLOCKS = 8
LRU_BLOCK_DIM = LRU_WIDTH // LRU_BLOCKS
CONV_WIDTH = 4
LRU_C = 8.0
RMS_EPS = 1e-6
NEG = -1e30
SEL_FORCE = 1e4

LANE = 128
SUBLANE = 8
VMEM_LIMIT = 62 * 1024 * 1024

AT_Q, AT_KS, AT_KW, AT_VS, AT_VW = 0, 8, 10, 12, 14
A_WIDTH = 16 * LANE
BT_X, BT_Y, BT_KC, BT_VC, BT_GATE = 0, 8, 16, 18, 20
B_WIDTH = 24 * LANE
PROJ_TN = 512
PROJ_A_STEPS = A_WIDTH // PROJ_TN
PROJ_STEPS = (A_WIDTH + B_WIDTH) // PROJ_TN
PROJ_HEADS = PROJ_TN // LANE
ROW_CHUNK = 256
PROJ_CHUNK = ROW_CHUNK
FFN_CHUNK = ROW_CHUNK

Q_SCALE = HEAD_DIM ** -0.5 * 1.4426950408889634
SEL_KT = 512
MASK_BIG = 2.0 ** 100


def _rms(x, g):
    return x * lax.rsqrt(jnp.mean(x * x, axis=-1, keepdims=True) + RMS_EPS) * g


def _dot(a, b):
    return jnp.dot(a, b, preferred_element_type=F32)


def _dot_nt(a, b):
    return lax.dot_general(a, b, (((1,), (1,)), ((), ())), preferred_element_type=F32)


def _ffn_steps(j, n_steps, h_ref, pre_ref, post_ref, o_ref, u_scr, weights):
    def step(first, last):
        wg, wu, wd = weights()
        for c in range(h_ref.shape[0] // FFN_CHUNK):
            rows = slice(c * FFN_CHUNK, (c + 1) * FFN_CHUNK)
            if first:
                u = _rms(h_ref[rows, :], pre_ref[...]).astype(BF16)
                u_scr[rows, :] = u
            else:
                u = u_scr[rows, :]
            g = _dot(u, wg)
            up = _dot(u, wu)
            acc = _dot(((g * jax.nn.sigmoid(g)) * up).astype(BF16), wd)
            if not first:
                acc = o_ref[rows, :] + acc
            if last:
                acc = h_ref[rows, :] + 0.5 * _rms(acc, post_ref[...])
            o_ref[rows, :] = acc

    if n_steps == 1:
        step(True, True)
    else:
        pl.when(j == 0)(lambda: step(True, False))
        pl.when((j > 0) & (j < n_steps - 1))(lambda: step(False, False))
        pl.when(j == n_steps - 1)(lambda: step(False, True))


def _ffn_head_kernel(h_ref, pre_ref, wg_ref, wu_ref, wd_ref, post_ref, o_ref, wg16_ref, wu16_ref, wd16_ref,
                     u_scr, *, n_steps):
    def weights():
        w16 = []
        for src, dst in ((wg_ref, wg16_ref), (wu_ref, wu16_ref), (wd_ref, wd16_ref)):
            w = src[...].astype(BF16)
            dst[...] = w
            w16.append(w)
        return w16

    _ffn_steps(pl.program_id(0), n_steps, h_ref, pre_ref, post_ref, o_ref, u_scr, weights)


def _ffn_tail_kernel(h_ref, head_ref, pre_ref, wg_ref, wu_ref, wd_ref, post_ref, o_ref, u_scr, *,
                     n_steps, n_copy):
    i, j = pl.program_id(0), pl.program_id(1)
    slab = head_ref.shape[0]

    @pl.when((i == 0) & (j < n_copy))
    def _():
        o_ref[pl.ds(pl.multiple_of(j * slab, slab), slab), :] = head_ref[...]

    @pl.when(i > 0)
    def _():
        _ffn_steps(j, n_steps, h_ref, pre_ref, post_ref, o_ref, u_scr,
                   lambda: (wg_ref[...], wu_ref[...], wd_ref[...]))


def _ffn(h, pre_g, wg, wu, wd, post_g, tm=1024, tf=512, tf_head=256):
    n, d = h.shape
    dff = wg.shape[1]
    tf, tf_head = min(tf, dff), min(tf_head, dff)
    vec1 = pl.BlockSpec((1, d), lambda j: (0, 0))
    head, wg16, wu16, wd16 = pl.pallas_call(
        functools.partial(_ffn_head_kernel, n_steps=dff // tf_head),
        name="ffn_head",
        grid=(dff // tf_head,),
        in_specs=[
            pl.BlockSpec((tm, d), lambda j: (0, 0)), vec1,
            pl.BlockSpec((d, tf_head), lambda j: (0, j)),
            pl.BlockSpec((d, tf_head), lambda j: (0, j)),
            pl.BlockSpec((tf_head, d), lambda j: (j, 0)),
            vec1,
        ],
        out_specs=[pl.BlockSpec((tm, d), lambda j: (0, 0)),
                   pl.BlockSpec((d, tf_head), lambda j: (0, j)),
                   pl.BlockSpec((d, tf_head), lambda j: (0, j)),
                   pl.BlockSpec((tf_head, d), lambda j: (j, 0))],
        out_shape=[jax.ShapeDtypeStruct((tm, d), F32), jax.ShapeDtypeStruct((d, dff), BF16),
                   jax.ShapeDtypeStruct((d, dff), BF16), jax.ShapeDtypeStruct((dff, d), BF16)],
        scratch_shapes=[pltpu.VMEM((tm, d), BF16)],
        compiler_params=pltpu.CompilerParams(
            dimension_semantics=("arbitrary",), vmem_limit_bytes=VMEM_LIMIT),
    )(h, pre_g, wg, wu, wd, post_g)

    n_steps = dff // tf
    n_copy = 1 << (min(n_steps, tm // LANE).bit_length() - 1)
    row = lambda i, j: (i, 0)
    vec = pl.BlockSpec((1, d), lambda i, j: (0, 0))
    wcol = lambda i, j: (0, jnp.where(i > 0, j, 0))
    return pl.pallas_call(
        functools.partial(_ffn_tail_kernel, n_steps=n_steps, n_copy=n_copy),
        name="ffn_tail",
        grid=(n // tm, n_steps),
        in_specs=[
            pl.BlockSpec((tm, d), row),
            pl.BlockSpec((tm // n_copy, d), lambda i, j: (jnp.where(i == 0, jnp.minimum(j, n_copy - 1), n_copy - 1), 0)),
            vec,
            pl.BlockSpec((d, tf), wcol),
            pl.BlockSpec((d, tf), wcol),
            pl.BlockSpec((tf, d), lambda i, j: (jnp.where(i > 0, j, 0), 0)),
            vec,
        ],
        out_specs=pl.BlockSpec((tm, d), row),
        out_shape=jax.ShapeDtypeStruct((n, d), F32),
        scratch_shapes=[pltpu.VMEM((tm, d), BF16)],
        compiler_params=pltpu.CompilerParams(
            dimension_semantics=("arbitrary", "arbitrary"), vmem_limit_bytes=VMEM_LIMIT),
    )(h, head, pre_g, wg16, wu16, wd16, post_g)


def _proj_kernel(pos_ref, h_ref, pre_ref, w_ref, gb_ref, invf_ref, za_ref, zb_ref, u_scr, cos_scr, s1_scr, s2_scr):
    j = pl.program_id(1)

    def prepare(rows):
        u_scr[rows, :] = _rms(h_ref[rows, :], pre_ref[...]).astype(BF16)
        ang = pos_ref[rows, :].astype(F32) * invf_ref[...]
        lane = lax.broadcasted_iota(jnp.int32, ang.shape, 1)
        sin = jnp.sin(ang)
        cos_scr[rows, :] = jnp.cos(ang)
        s1_scr[rows, :] = jnp.where((lane >= ROPE_HALF) & (lane < ROPE_DIM), sin, 0.0)
        s2_scr[rows, :] = jnp.where(lane < ROPE_HALF, -sin, 0.0)

    def head(z, rows, hd, rope, mul=None):
        x = z[:, hd * LANE:(hd + 1) * LANE]
        if rope:
            x = (x * cos_scr[rows, :]
                 + pltpu.roll(x, ROPE_HALF, axis=1) * s1_scr[rows, :]
                 + pltpu.roll(x, LANE - ROPE_HALF, axis=1) * s2_scr[rows, :])
        return x if mul is None else x * mul

    def chunks(first=False):
        for c in range(u_scr.shape[0] // PROJ_CHUNK):
            rows = slice(c * PROJ_CHUNK, (c + 1) * PROJ_CHUNK)
            if first:
                prepare(rows)
            yield rows, _dot(u_scr[rows, :], w_ref[...])

    def store(ref, rope_heads, mul=None, first=False):
        for rows, z in chunks(first):
            for hd in range(PROJ_HEADS):
                ref[rows, hd * LANE:(hd + 1) * LANE] = head(z, rows, hd, hd < rope_heads, mul).astype(ref.dtype)

    @pl.when(j == 0)
    def _():
        store(za_ref, PROJ_HEADS, Q_SCALE, first=True)

    @pl.when((j > 0) & (j < AT_KS // PROJ_HEADS))
    def _():
        store(za_ref, PROJ_HEADS, Q_SCALE)

    @pl.when(j == AT_KS // PROJ_HEADS)
    def _():
        store(za_ref, PROJ_HEADS)

    @pl.when(j == AT_VS // PROJ_HEADS)
    def _():
        store(za_ref, 0)

    @pl.when((j >= PROJ_A_STEPS) & (j < PROJ_A_STEPS + BT_KC // PROJ_HEADS))
    def _():
        store(zb_ref, 0)

    @pl.when(j == PROJ_A_STEPS + BT_KC // PROJ_HEADS)
    def _():
        store(zb_ref, NSA_KV_HEADS)

    @pl.when(j == PROJ_A_STEPS + BT_GATE // PROJ_HEADS)
    def _():
        for rows, z in chunks():
            zb_ref[rows, :] = jax.nn.sigmoid(z + gb_ref[...])


def _proj(pos, h, pre_g, w, gb, invf, tm=1024):
    n, d = h.shape
    return pl.pallas_call(
        _proj_kernel,
        name="mix_proj",
        grid=(n // tm, PROJ_STEPS),
        in_specs=[
            pl.BlockSpec((tm, 1), lambda i, j: (i, 0)),
            pl.BlockSpec((tm, d), lambda i, j: (i, 0)),
            pl.BlockSpec((1, d), lambda i, j: (0, 0)),
            pl.BlockSpec((d, PROJ_TN), lambda i, j: (0, j)),
            pl.BlockSpec((1, PROJ_TN), lambda i, j: (0, 0)),
            pl.BlockSpec((1, LANE), lambda i, j: (0, 0)),
        ],
        out_specs=[pl.BlockSpec((tm, PROJ_TN), lambda i, j: (i, jnp.minimum(j, PROJ_A_STEPS - 1))),
                   pl.BlockSpec((tm, PROJ_TN), lambda i, j: (i, jnp.maximum(j - PROJ_A_STEPS, 0)))],
        out_shape=[jax.ShapeDtypeStruct((n, A_WIDTH), BF16), jax.ShapeDtypeStruct((n, B_WIDTH), F32)],
        scratch_shapes=[pltpu.VMEM((tm, d), BF16)] + [pltpu.VMEM((tm, LANE), F32)] * 3,
        compiler_params=pltpu.CompilerParams(
            dimension_semantics=("arbitrary", "arbitrary"), vmem_limit_bytes=VMEM_LIMIT),
    )(pos, h, pre_g, w, gb, invf)


def _cmp_kernel(x_ref, pos_ref, w1_ref, w2_ref, o_ref):
    ncp = o_ref.shape[0]
    hidden_w = w1_ref.shape[-1]
    acc_a = jnp.zeros((ncp, hidden_w), F32)
    acc_b = jnp.zeros((ncp, hidden_w), F32)
    for l in range(CMP_STRIDE):
        xl = x_ref[pl.ds(l, ncp, stride=CMP_STRIDE), :]
        acc_a += _dot((xl + pos_ref[l:l + 1, :]).astype(BF16), w1_ref[l])
        acc_b += _dot((xl + pos_ref[CMP_STRIDE + l:CMP_STRIDE + l + 1, :]).astype(BF16), w1_ref[CMP_STRIDE + l])
    hidden = acc_a + pltpu.roll(acc_b, ncp - 1, axis=0)
    o_ref[...] = _dot(jax.nn.gelu(hidden).astype(BF16), w2_ref[...])


def _compress(zb, pos_kv, w1, w2, batch, seq):
    ncp = seq // CMP_STRIDE
    hidden_w = w1.shape[-1]
    return pl.pallas_call(
        _cmp_kernel,
        name="compress",
        grid=(2, batch, NSA_KV_HEADS),
        in_specs=[
            pl.BlockSpec((seq, LANE), lambda k, b, g: (b, BT_KC + 2 * k + g)),
            pl.BlockSpec((None, CMP_LEN, HEAD_DIM), lambda k, b, g: (k, 0, 0)),
            pl.BlockSpec((None, CMP_LEN, HEAD_DIM, hidden_w), lambda k, b, g: (k, 0, 0, 0)),
            pl.BlockSpec((None, hidden_w, HEAD_DIM), lambda k, b, g: (k, 0, 0)),
        ],
        out_specs=pl.BlockSpec((None, None, None, ncp, HEAD_DIM), lambda k, b, g: (k, b, g, 0, 0)),
        out_shape=jax.ShapeDtypeStruct((2, batch, NSA_KV_HEADS, ncp, HEAD_DIM), F32),
        compiler_params=pltpu.CompilerParams(
            dimension_semantics=("arbitrary",) * 3, vmem_limit_bytes=VMEM_LIMIT),
    )(zb, pos_kv, w1, w2)


def _masked_exp(s, mask):
    s = s + jnp.where(mask, 0.0, -MASK_BIG)
    e = jnp.exp2(s - jnp.max(s, axis=-1, keepdims=True))
    return e, jnp.sum(e, axis=-1, keepdims=True)


def _attn_kernel(q_ref, kc_ref, vc_ref, ks_ref, vs_ref, kw_ref, vw_ref, gz_ref, ovl_ref, oh_ref, ga_ref, o_ref,
                 *, first_block):
    G, R, tq = NSA_KV_HEADS, GQA_GROUP, Q_BLOCK
    n_keys = ks_ref.shape[0]
    top_k = min(SEL_TOPK, ovl_ref.shape[0])
    n_blk = n_keys // SEL_LEN
    n_cmp = min(kc_ref.shape[1], -(-(n_keys // CMP_STRIDE) // LANE) * LANE)
    t0 = (first_block + pl.program_id(1)) * tq
    t_idx = t0 + lax.broadcasted_iota(jnp.int32, (1, tq, 1), 1)
    gsl = lambda g: slice(g * HEAD_DIM, (g + 1) * HEAD_DIM)

    def front(g):
        q = q_ref[:, g * R * HEAD_DIM:(g + 1) * R * HEAD_DIM]
        q4 = jnp.concatenate([q[:, r * HEAD_DIM:(r + 1) * HEAD_DIM] for r in range(R)], axis=0)

        s = _dot_nt(q4, kc_ref[g, 0:n_cmp, :].astype(BF16)).reshape(R, tq, n_cmp)
        c_idx = lax.broadcasted_iota(jnp.int32, (1, 1, n_cmp), 2)
        e, l = _masked_exp(s, (c_idx * CMP_STRIDE + (CMP_LEN - 1)) <= t_idx)
        pc = e * (jnp.where(t_idx >= CMP_LEN - 1, 1.0, 0.0) / l)
        o_cmp = _dot(pc.reshape(R * tq, n_cmp).astype(BF16), vc_ref[g, 0:n_cmp, :].astype(BF16))
        o_cmp = o_cmp.reshape(R, tq, HEAD_DIM)
        if n_blk <= top_k:
            return q4, jnp.concatenate([q4, jnp.zeros_like(q4)], axis=1), o_cmp

        pcs = pc[0]
        for r in range(1, R):
            pcs = pcs + pc[r]
        hi = pcs.astype(BF16)
        lo = (pcs - hi.astype(F32)).astype(BF16)
        ovl = ovl_ref[0:n_blk, 0:n_cmp]
        imp = _dot_nt(ovl, hi) + _dot_nt(ovl, lo)
        blk = lax.broadcasted_iota(jnp.int32, (n_blk, tq), 0)
        tcol = t0 + lax.broadcasted_iota(jnp.int32, (n_blk, tq), 1)
        forced = (blk == lax.shift_right_logical(tcol, int(np.log2(SEL_LEN)))) | (blk == 0)
        v = jnp.where(forced, SEL_FORCE, jnp.where(blk * SEL_LEN <= tcol, imp, -SEL_FORCE))
        sub = lax.broadcasted_iota(jnp.int32, (SUBLANE, tq), 0)
        groups = [v[k:k + SUBLANE, :] for k in range(0, n_blk, SUBLANE)]
        ranks = [jnp.zeros((SUBLANE, tq), F32) for _ in groups]
        for sp in range(n_blk):
            other = v[sp:sp + 1, :]
            for gi, vg in enumerate(groups):
                first = gi * SUBLANE
                if first > sp:
                    beats = other >= vg
                elif first + SUBLANE - 1 <= sp:
                    beats = other > vg
                else:
                    beats = (other > vg) | ((other == vg) & (sub > sp - first))
                ranks[gi] = ranks[gi] + jnp.where(beats, 1.0, 0.0)
        rank = jnp.concatenate(ranks, axis=0)
        unsel = jnp.where(rank < float(top_k), 0.0, -1.0)
        unsel = jnp.concatenate([unsel, jnp.zeros((LANE - n_blk, tq), F32)], axis=0).T.astype(BF16)
        q_aug = jnp.concatenate([q4, jnp.concatenate([unsel] * R, axis=0)], axis=1)
        return q4, q_aug, o_cmp

    fronts = [front(g) for g in range(G)]

    n_full = n_keys - SEL_KT
    kp = n_full + lax.broadcasted_iota(jnp.int32, (1, 1, SEL_KT), 2)
    causal_bias = jnp.where(kp <= t_idx, 0.0, -MASK_BIG)

    def selected(g):
        k_aug = jnp.concatenate([ks_ref[:, gsl(g)], oh_ref[...]], axis=1)
        s = _dot_nt(fronts[g][1], k_aug).reshape(R, tq, n_keys)
        s_last = s[:, :, n_full:] + causal_bias
        m = jnp.max(s_last, axis=-1, keepdims=True)
        if n_full:
            s_full = s[:, :, :n_full]
            m = jnp.maximum(m, jnp.max(s_full, axis=-1, keepdims=True))
        p = jnp.exp2(s_last - m)
        l = jnp.sum(p, axis=-1, keepdims=True)
        acc = _dot(p.reshape(R * tq, SEL_KT).astype(BF16), vs_ref[n_full:n_keys, gsl(g)])
        if n_full:
            p = jnp.exp2(s_full - m)
            l = l + jnp.sum(p, axis=-1, keepdims=True)
            acc = acc + _dot(p.reshape(R * tq, n_full).astype(BF16), vs_ref[0:n_full, gsl(g)])
        return acc.reshape(R, tq, HEAD_DIM) * (1.0 / l)

    wk = min(WINDOW + tq, n_keys)
    w0 = pl.multiple_of(jnp.maximum(t0 - WINDOW, 0), tq)
    kp = w0 + lax.broadcasted_iota(jnp.int32, (1, 1, wk), 2)
    win_mask = (kp <= t_idx) & (kp > t_idx - WINDOW)

    heads = []
    for g in range(G):
        q4, _, o_cmp = fronts[g]
        o_sel = selected(g)

        s = _dot_nt(q4, kw_ref[pl.ds(w0, wk), gsl(g)]).reshape(R, tq, wk)
        e, l = _masked_exp(s, win_mask)
        o_win = _dot(e.reshape(R * tq, wk).astype(BF16), vw_ref[pl.ds(w0, wk), gsl(g)]).reshape(R, tq, HEAD_DIM)
        o_win = o_win * (1.0 / l)

        gz = gz_ref[:, gsl(g)]
        for r in range(R):
            c = r * N_GATES
            heads.append(gz[:, c:c + 1] * o_cmp[r] + gz[:, c + 1:c + 2] * o_sel[r] + gz[:, c + 2:c + 3] * o_win[r])
    o_ref[...] = _rms(jnp.concatenate(heads, axis=1), ga_ref[...]).astype(o_ref.dtype)


def _attention(za, zb, kcv, ovl_t, onehot, g_attn, batch, seq):
    ncp = seq // CMP_STRIDE
    n_sel = seq // SEL_LEN
    G = NSA_KV_HEADS
    blocks = SEL_KT // Q_BLOCK
    za3 = za.reshape(batch, seq, A_WIDTH)
    zb3 = zb.reshape(batch, seq, B_WIDTH)
    outs = []
    for c in range(seq // SEL_KT):
        n_keys = (c + 1) * SEL_KT
        q_row = lambda b, i, c=c: (b, c * blocks + i, 0)
        kv_spec = lambda tile: pl.BlockSpec((None, n_keys, G * LANE), lambda b, i: (b, 0, tile // G))
        outs.append(pl.pallas_call(
            functools.partial(_attn_kernel, first_block=c * blocks),
            name=f"attention_{c}",
            grid=(batch, blocks),
            in_specs=[
                pl.BlockSpec((None, Q_BLOCK, NSA_WIDTH), q_row),
                pl.BlockSpec((None, None, G, ncp, HEAD_DIM), lambda b, i: (0, b, 0, 0, 0)),
                pl.BlockSpec((None, None, G, ncp, HEAD_DIM), lambda b, i: (1, b, 0, 0, 0)),
                kv_spec(AT_KS), kv_spec(AT_VS), kv_spec(AT_KW), kv_spec(AT_VW),
                pl.BlockSpec((None, Q_BLOCK, G * LANE), lambda b, i, c=c: (b, c * blocks + i, BT_GATE // G)),
                pl.BlockSpec((n_sel, ncp), lambda b, i: (0, 0)),
                pl.BlockSpec((n_keys, LANE), lambda b, i: (0, 0)),
                pl.BlockSpec((1, NSA_WIDTH), lambda b, i: (0, 0)),
            ],
            out_specs=pl.BlockSpec((None, Q_BLOCK, NSA_WIDTH), lambda b, i: (b, i, 0)),
            out_shape=jax.ShapeDtypeStruct((batch, SEL_KT, NSA_WIDTH), BF16),
            compiler_params=pltpu.CompilerParams(
                dimension_semantics=("arbitrary",) * 2, vmem_limit_bytes=VMEM_LIMIT),
        )(za3, kcv, kcv, za3, za3, za3, za3, zb3, ovl_t, onehot, g_attn))
    return jnp.concatenate(outs, axis=1).reshape(batch * seq, NSA_WIDTH)


def _lru_kernel(zx_ref, zy_ref, cw_ref, cb_ref, wa_ref, ba_ref, wi_ref, bi_ref, lam_ref, gr_ref,
                o_ref, xpad, a_scr, u_scr, h_scr, carry):
    tt, width = zx_ref.shape
    ti = pl.program_id(1)

    @pl.when(ti == 0)
    def _():
        xpad[0:SUBLANE, :] = jnp.zeros((SUBLANE, width), F32)
        carry[...] = jnp.zeros_like(carry)

    xb = zx_ref[...]
    xpad[SUBLANE:SUBLANE + tt, :] = xb
    xc = cb_ref[...] + cw_ref[CONV_WIDTH - 1:CONV_WIDTH, :] * xb
    for k in range(1, CONV_WIDTH):
        xc = xc + cw_ref[CONV_WIDTH - 1 - k:CONV_WIDTH - k, :] * xpad[SUBLANE - k:SUBLANE - k + tt, :]
    xpad[0:SUBLANE, :] = xb[tt - SUBLANE:tt, :]

    sp = jax.nn.softplus(-lam_ref[...])
    row8 = lax.broadcasted_iota(jnp.int32, (1, SUBLANE, LRU_BLOCK_DIM), 1)
    for hb in range(LRU_BLOCKS):
        sl = slice(hb * LRU_BLOCK_DIM, (hb + 1) * LRU_BLOCK_DIM)
        xs = xc[:, sl]
        xs16 = xs.astype(BF16)
        r = jax.nn.sigmoid(_dot(xs16, wa_ref[hb]) + ba_ref[:, sl])
        ig = jax.nn.sigmoid(_dot(xs16, wi_ref[hb]) + bi_ref[:, sl])
        log_a = -LRU_C * r * sp[:, sl]
        a = jnp.exp(log_a)
        u = jnp.sqrt(-jnp.tanh(log_a) * (a * a + 1.0)) * (ig * xs)
        a = a.reshape(tt // SUBLANE, SUBLANE, LRU_BLOCK_DIM)
        u = u.reshape(tt // SUBLANE, SUBLANE, LRU_BLOCK_DIM)
        for s in (1, 2, 4):
            ok = row8 >= s
            a_sh = pltpu.roll(a, s, axis=1)
            u_sh = pltpu.roll(u, s, axis=1)
            u = jnp.where(ok, u + a * u_sh, u)
            a = jnp.where(ok, a * a_sh, a)
        a_scr[:, sl] = a.reshape(tt, LRU_BLOCK_DIM)
        u_scr[:, sl] = u.reshape(tt, LRU_BLOCK_DIM)

    def group(gi, c):
        r0 = pl.multiple_of(gi * SUBLANE, SUBLANE)
        h = u_scr[pl.ds(r0, SUBLANE), :] + a_scr[pl.ds(r0, SUBLANE), :] * c
        h_scr[pl.ds(r0, SUBLANE), :] = h
        return jnp.broadcast_to(h[SUBLANE - 1:SUBLANE, :], (SUBLANE, width))

    c = lax.fori_loop(0, tt // SUBLANE, group, carry[...], unroll=4)
    carry[...] = c
    o_ref[...] = _rms(h_scr[...] * jax.nn.gelu(zy_ref[...]), gr_ref[...]).astype(o_ref.dtype)


def _rglru(zb, conv_w, conv_b, w_a, b_a, w_i, b_i, lam, g_rec, batch, seq, tt=256):
    nt = seq // tt
    wt = LRU_WIDTH // LANE
    full = lambda a: pl.BlockSpec(a.shape, lambda b, t: (0,) * a.ndim)
    zspec = lambda tile: pl.BlockSpec((tt, LRU_WIDTH), lambda b, t: (b * nt + t, tile // wt))
    return pl.pallas_call(
        _lru_kernel,
        name="rglru",
        grid=(batch, nt),
        in_specs=[zspec(BT_X), zspec(BT_Y), full(conv_w), full(conv_b), full(w_a), full(b_a),
                  full(w_i), full(b_i), full(lam), full(g_rec)],
        out_specs=pl.BlockSpec((tt, LRU_WIDTH), lambda b, t: (b * nt + t, 0)),
        out_shape=jax.ShapeDtypeStruct((batch * seq, LRU_WIDTH), BF16),
        scratch_shapes=[pltpu.VMEM((tt + SUBLANE, LRU_WIDTH), F32), pltpu.VMEM((tt, LRU_WIDTH), F32),
                        pltpu.VMEM((tt, LRU_WIDTH), F32), pltpu.VMEM((tt, LRU_WIDTH), F32),
                        pltpu.VMEM((SUBLANE, LRU_WIDTH), F32)],
        compiler_params=pltpu.CompilerParams(
            dimension_semantics=("arbitrary", "arbitrary"), vmem_limit_bytes=VMEM_LIMIT),
    )(zb, zb, conv_w, conv_b, w_a, b_a, w_i, b_i, lam, g_rec)


def _out_kernel(oa_ref, or_ref, w_ref, post_ref, h_ref, o_ref):
    for c in range(h_ref.shape[0] // ROW_CHUNK):
        rows = slice(c * ROW_CHUNK, (c + 1) * ROW_CHUNK)
        y = (_dot(oa_ref[rows, :], w_ref[0:NSA_WIDTH, :])
             + _dot(or_ref[rows, :], w_ref[NSA_WIDTH:NSA_WIDTH + LRU_WIDTH, :]))
        o_ref[rows, :] = h_ref[rows, :] + _rms(y, post_ref[...])


def _out_proj(o_attn, o_rec, w_out, post_g, h, tm=512):
    n, d = h.shape
    row = lambda i: (i, 0)
    full = lambda a: pl.BlockSpec(a.shape, lambda i: (0,) * a.ndim)
    return pl.pallas_call(
        _out_kernel,
        name="out_proj",
        grid=(n // tm,),
        in_specs=[pl.BlockSpec((tm, NSA_WIDTH), row), pl.BlockSpec((tm, LRU_WIDTH), row),
                  full(w_out), full(post_g), pl.BlockSpec((tm, d), row)],
        out_specs=pl.BlockSpec((tm, d), row),
        out_shape=jax.ShapeDtypeStruct((n, d), F32),
        compiler_params=pltpu.CompilerParams(
            dimension_semantics=("arbitrary",), vmem_limit_bytes=VMEM_LIMIT),
    )(o_attn, o_rec, w_out, post_g, h)


def _ple_kernel(h_ref, p_ref, pre_ref, wg_ref, wp_ref, post_ref, o_ref):
    for c in range(h_ref.shape[0] // ROW_CHUNK):
        rows = slice(c * ROW_CHUNK, (c + 1) * ROW_CHUNK)
        h = h_ref[rows, :]
        gate = jax.nn.sigmoid(_dot(_rms(h, pre_ref[...]).astype(BF16), wg_ref[...]))
        pp = _dot(p_ref[rows, :].astype(BF16), wp_ref[...])
        o_ref[rows, :] = h + _rms(gate * pp, post_ref[...])


def _ple(h, p, pre_g, w_gate, w_proj, post_g, tm=512):
    n, d = h.shape
    row = lambda i: (i, 0)
    full = lambda a: pl.BlockSpec(a.shape, lambda i: (0,) * a.ndim)
    return pl.pallas_call(
        _ple_kernel,
        name="ple",
        grid=(n // tm,),
        in_specs=[pl.BlockSpec((tm, d), row), pl.BlockSpec((tm, p.shape[1]), row),
                  full(pre_g), full(w_gate), full(w_proj), full(post_g)],
        out_specs=pl.BlockSpec((tm, d), row),
        out_shape=jax.ShapeDtypeStruct((n, d), F32),
        compiler_params=pltpu.CompilerParams(
            dimension_semantics=("arbitrary",), vmem_limit_bytes=VMEM_LIMIT),
    )(h, p, pre_g, w_gate, w_proj, post_g)


def _selection_constants(seq):
    ncp = seq // CMP_STRIDE
    n_cmp = (seq - CMP_LEN) // CMP_STRIDE + 1
    n_sel = seq // SEL_LEN
    c = np.arange(ncp)[None, :]
    s = np.arange(n_sel)[:, None]
    ovl_t = ((c * CMP_STRIDE < s * SEL_LEN + SEL_LEN) & (c * CMP_STRIDE + CMP_LEN - 1 >= s * SEL_LEN) & (c < n_cmp))
    onehot = (np.arange(seq)[:, None] // SEL_LEN) == np.arange(LANE)[None, :]
    return jnp.asarray(ovl_t, BF16), jnp.asarray(onehot * MASK_BIG, BF16)


def _pad_cols(a, width):
    return jnp.pad(a, ((0, 0), (0, width - a.shape[1])))


def kernel(x, p, positions, ff1_pre_g, ff1_post_g, ff1_w_gate, ff1_w_up, ff1_w_down, mix_pre_g, mix_post_g, w_in, cmp_pos_k, cmp_pos_v, cmp_k_w1, cmp_k_w2, cmp_v_w1, cmp_v_w2, nsa_gate_b, conv_w, conv_b, rg_w_a, rg_b_a, rg_w_i, rg_b_i, rg_lambda, attn_out_g, rec_out_g, w_out, ff2_pre_g, ff2_post_g, ff2_w_gate, ff2_w_up, ff2_w_down, ple_pre_g, ple_post_g, w_ple_gate, w_ple_proj):
    batch, seq, d = x.shape
    depth = p.shape[0]
    n = batch * seq
    vec = lambda a: a.reshape(1, -1)
    gate_cols = NSA_HEADS * N_GATES
    grp_gates = GQA_GROUP * N_GATES
    o_q, o_kc, o_vc, o_ks, o_vs, o_kw, o_vw = (NSA_WIDTH * 0,) + tuple(NSA_WIDTH + k * KV_WIDTH for k in range(6))
    o_g = NSA_WIDTH + 6 * KV_WIDTH
    o_x = o_g + gate_cols
    o_y = o_x + LRU_WIDTH

    half = jnp.arange(ROPE_HALF, dtype=F32)
    inv_freq = ROPE_THETA ** (-half / ROPE_HALF)
    invf = jnp.concatenate([inv_freq, inv_freq, jnp.zeros((LANE - ROPE_DIM,), F32)]).reshape(1, LANE)
    ovl_t, onehot = _selection_constants(seq)
    pos = positions.reshape(n, 1)

    h = x.reshape(n, d)
    for i in range(depth):
        wi = w_in[i]
        cols = lambda o, w: wi[:, o:o + w]
        w_in_z = jnp.concatenate(
            [cols(o_q, NSA_WIDTH), cols(o_ks, KV_WIDTH), cols(o_kw, KV_WIDTH), cols(o_vs, KV_WIDTH),
             cols(o_vw, KV_WIDTH), cols(o_x, LRU_WIDTH), cols(o_y, LRU_WIDTH), cols(o_kc, KV_WIDTH),
             cols(o_vc, KV_WIDTH)]
            + [_pad_cols(cols(o_g + g * grp_gates, grp_gates), LANE) for g in range(NSA_KV_HEADS)]
            + [jnp.zeros((d, B_WIDTH - (BT_GATE + NSA_KV_HEADS) * LANE), F32)],
            axis=1).astype(BF16)
        gb = nsa_gate_b[i].reshape(NSA_KV_HEADS, grp_gates)
        gate_bias = _pad_cols(jnp.pad(gb, ((0, 0), (0, LANE - grp_gates))).reshape(1, NSA_KV_HEADS * LANE), PROJ_TN)

        h = _ffn(h, vec(ff1_pre_g[i]), ff1_w_gate[i], ff1_w_up[i], ff1_w_down[i], vec(ff1_post_g[i]))
        za, zb = _proj(pos, h, vec(mix_pre_g[i]), w_in_z, gate_bias, invf)
        kcv = _compress(
            zb, jnp.stack([cmp_pos_k[i], cmp_pos_v[i]]),
            jnp.stack([cmp_k_w1[i], cmp_v_w1[i]]).reshape(2, CMP_LEN, HEAD_DIM, -1).astype(BF16),
            jnp.stack([cmp_k_w2[i], cmp_v_w2[i]]).astype(BF16), batch, seq)
        o_attn = _attention(za, zb, kcv, ovl_t, onehot, vec(attn_out_g[i]), batch, seq)
        o_rec = _rglru(zb, conv_w[i], vec(conv_b[i]), rg_w_a[i].astype(BF16), vec(rg_b_a[i]),
                       rg_w_i[i].astype(BF16), vec(rg_b_i[i]), vec(rg_lambda[i]), vec(rec_out_g[i]), batch, seq)
        h = _out_proj(o_attn, o_rec, w_out[i].astype(BF16), vec(mix_post_g[i]), h)
        h = _ffn(h, vec(ff2_pre_g[i]), ff2_w_gate[i], ff2_w_up[i], ff2_w_down[i], vec(ff2_post_g[i]))
        h = _ple(h, p[i].reshape(n, -1), vec(ple_pre_g[i]), w_ple_gate[i].astype(BF16),
                 w_ple_proj[i].astype(BF16), vec(ple_post_g[i]))
    return h.reshape(batch, seq, d)
```

```python
import functools

import numpy as np
import jax
import jax.numpy as jnp
from jax import lax
from jax.experimental import pallas as pl
from jax.experimental.pallas import tpu as pltpu

F32 = jnp.float32
BF16 = jnp.bfloat16

D_MODEL = 2048
NSA_HEADS = 8
NSA_KV_HEADS = 2
GQA_GROUP = NSA_HEADS // NSA_KV_HEADS
HEAD_DIM = 128
NSA_WIDTH = NSA_HEADS * HEAD_DIM
KV_WIDTH = NSA_KV_HEADS * HEAD_DIM
ROPE_DIM = HEAD_DIM // 4
ROPE_HALF = ROPE_DIM // 2
ROPE_THETA = 500000.0
CMP_LEN = 32
CMP_STRIDE = 16
SEL_LEN = 64
SEL_TOPK = 16
WINDOW = 512
Q_BLOCK = 128
N_GATES = 3
LRU_WIDTH = 1024
LRU_BLOCKS = 8
LRU_BLOCK_DIM = LRU_WIDTH // LRU_BLOCKS
CONV_WIDTH = 4
LRU_C = 8.0
RMS_EPS = 1e-6
NEG = -1e30
SEL_FORCE = 1e4

LANE = 128
SUBLANE = 8
VMEM_LIMIT = 62 * 1024 * 1024

AT_Q, AT_KS, AT_KW, AT_VS, AT_VW = 0, 8, 10, 12, 14
A_WIDTH = 16 * LANE
BT_X, BT_Y, BT_KC, BT_VC, BT_GATE = 0, 8, 16, 18, 20
B_WIDTH = 24 * LANE
PROJ_TN = 512
PROJ_A_STEPS = A_WIDTH // PROJ_TN
PROJ_STEPS = (A_WIDTH + B_WIDTH) // PROJ_TN
PROJ_HEADS = PROJ_TN // LANE
ROW_CHUNK = 256
PROJ_CHUNK = ROW_CHUNK
FFN_CHUNK = ROW_CHUNK

Q_SCALE = HEAD_DIM ** -0.5 * 1.4426950408889634
SEL_KT = 512
MASK_BIG = 2.0 ** 100


def _rms(x, g):
    return x * lax.rsqrt(jnp.mean(x * x, axis=-1, keepdims=True) + RMS_EPS) * g


def _dot(a, b):
    return jnp.dot(a, b, preferred_element_type=F32)


def _dot_nt(a, b):
    return lax.dot_general(a, b, (((1,), (1,)), ((), ())), preferred_element_type=F32)


def _ffn_steps(j, n_steps, h_ref, pre_ref, post_ref, o_ref, u_scr, weights):
    def step(first, last):
        wg, wu, wd = weights()
        for c in range(h_ref.shape[0] // FFN_CHUNK):
            rows = slice(c * FFN_CHUNK, (c + 1) * FFN_CHUNK)
            if first:
                u = _rms(h_ref[rows, :], pre_ref[...]).astype(BF16)
                u_scr[rows, :] = u
            else:
                u = u_scr[rows, :]
            g = _dot(u, wg)
            up = _dot(u, wu)
            acc = _dot(((g * jax.nn.sigmoid(g)) * up).astype(BF16), wd)
            if not first:
                acc = o_ref[rows, :] + acc
            if last:
                acc = h_ref[rows, :] + 0.5 * _rms(acc, post_ref[...])
            o_ref[rows, :] = acc

    if n_steps == 1:
        step(True, True)
    else:
        pl.when(j == 0)(lambda: step(True, False))
        pl.when((j > 0) & (j < n_steps - 1))(lambda: step(False, False))
        pl.when(j == n_steps - 1)(lambda: step(False, True))


def _ffn_head_kernel(h_ref, pre_ref, wg_ref, wu_ref, wd_ref, post_ref, o_ref, wg16_ref, wu16_ref, wd16_ref,
                     u_scr, *, n_steps):
    def weights():
        w16 = []
        for src, dst in ((wg_ref, wg16_ref), (wu_ref, wu16_ref), (wd_ref, wd16_ref)):
            w = src[...].astype(BF16)
            dst[...] = w
            w16.append(w)
        return w16

    _ffn_steps(pl.program_id(0), n_steps, h_ref, pre_ref, post_ref, o_ref, u_scr, weights)


def _ffn_tail_kernel(h_ref, head_ref, pre_ref, wg_ref, wu_ref, wd_ref, post_ref, o_ref, u_scr, *,
                     n_steps, n_copy):
    i, j = pl.program_id(0), pl.program_id(1)
    slab = head_ref.shape[0]

    @pl.when((i == 0) & (j < n_copy))
    def _():
        o_ref[pl.ds(pl.multiple_of(j * slab, slab), slab), :] = head_ref[...]

    @pl.when(i > 0)
    def _():
        _ffn_steps(j, n_steps, h_ref, pre_ref, post_ref, o_ref, u_scr,
                   lambda: (wg_ref[...], wu_ref[...], wd_ref[...]))


def _ffn(h, pre_g, wg, wu, wd, post_g, tm=1024, tf=512, tf_head=256):
    n, d = h.shape
    dff = wg.shape[1]
    tf, tf_head = min(tf, dff), min(tf_head, dff)
    vec1 = pl.BlockSpec((1, d), lambda j: (0, 0))
    head, wg16, wu16, wd16 = pl.pallas_call(
        functools.partial(_ffn_head_kernel, n_steps=dff // tf_head),
        name="ffn_head",
        grid=(dff // tf_head,),
        in_specs=[
            pl.BlockSpec((tm, d), lambda j: (0, 0)), vec1,
            pl.BlockSpec((d, tf_head), lambda j: (0, j)),
            pl.BlockSpec((d, tf_head), lambda j: (0, j)),
            pl.BlockSpec((tf_head, d), lambda j: (j, 0)),
            vec1,
        ],
        out_specs=[pl.BlockSpec((tm, d), lambda j: (0, 0)),
                   pl.BlockSpec((d, tf_head), lambda j: (0, j)),
                   pl.BlockSpec((d, tf_head), lambda j: (0, j)),
                   pl.BlockSpec((tf_head, d), lambda j: (j, 0))],
        out_shape=[jax.ShapeDtypeStruct((tm, d), F32), jax.ShapeDtypeStruct((d, dff), BF16),
                   jax.ShapeDtypeStruct((d, dff), BF16), jax.ShapeDtypeStruct((dff, d), BF16)],
        scratch_shapes=[pltpu.VMEM((tm, d), BF16)],
        compiler_params=pltpu.CompilerParams(
            dimension_semantics=("arbitrary",), vmem_limit_bytes=VMEM_LIMIT),
    )(h, pre_g, wg, wu, wd, post_g)

    n_steps = dff // tf
    n_copy = 1 << (min(n_steps, tm // LANE).bit_length() - 1)
    row = lambda i, j: (i, 0)
    vec = pl.BlockSpec((1, d), lambda i, j: (0, 0))
    wcol = lambda i, j: (0, jnp.where(i > 0, j, 0))
    return pl.pallas_call(
        functools.partial(_ffn_tail_kernel, n_steps=n_steps, n_copy=n_copy),
        name="ffn_tail",
        grid=(n // tm, n_steps),
        in_specs=[
            pl.BlockSpec((tm, d), row),
            pl.BlockSpec((tm // n_copy, d), lambda i, j: (jnp.where(i == 0, jnp.minimum(j, n_copy - 1), n_copy - 1), 0)),
            vec,
            pl.BlockSpec((d, tf), wcol),
            pl.BlockSpec((d, tf), wcol),
            pl.BlockSpec((tf, d), lambda i, j: (jnp.where(i > 0, j, 0), 0)),
            vec,
        ],
        out_specs=pl.BlockSpec((tm, d), row),
        out_shape=jax.ShapeDtypeStruct((n, d), F32),
        scratch_shapes=[pltpu.VMEM((tm, d), BF16)],
        compiler_params=pltpu.CompilerParams(
            dimension_semantics=("arbitrary", "arbitrary"), vmem_limit_bytes=VMEM_LIMIT),
    )(h, head, pre_g, wg16, wu16, wd16, post_g)


def _proj_kernel(pos_ref, h_ref, pre_ref, w_ref, gb_ref, invf_ref, za_ref, zb_ref, u_scr, cos_scr, s1_scr, s2_scr):
    j = pl.program_id(1)

    def prepare(rows):
        u_scr[rows, :] = _rms(h_ref[rows, :], pre_ref[...]).astype(BF16)
        ang = pos_ref[rows, :].astype(F32) * invf_ref[...]
        lane = lax.broadcasted_iota(jnp.int32, ang.shape, 1)
        sin = jnp.sin(ang)
        cos_scr[rows, :] = jnp.cos(ang)
        s1_scr[rows, :] = jnp.where((lane >= ROPE_HALF) & (lane < ROPE_DIM), sin, 0.0)
        s2_scr[rows, :] = jnp.where(lane < ROPE_HALF, -sin, 0.0)

    def head(z, rows, hd, rope, mul=None):
        x = z[:, hd * LANE:(hd + 1) * LANE]
        if rope:
            x = (x * cos_scr[rows, :]
                 + pltpu.roll(x, ROPE_HALF, axis=1) * s1_scr[rows, :]
                 + pltpu.roll(x, LANE - ROPE_HALF, axis=1) * s2_scr[rows, :])
        return x if mul is None else x * mul

    def chunks(first=False):
        for c in range(u_scr.shape[0] // PROJ_CHUNK):
            rows = slice(c * PROJ_CHUNK, (c + 1) * PROJ_CHUNK)
            if first:
                prepare(rows)
            yield rows, _dot(u_scr[rows, :], w_ref[...])

    def store(ref, rope_heads, mul=None, first=False):
        for rows, z in chunks(first):
            for hd in range(PROJ_HEADS):
                ref[rows, hd * LANE:(hd + 1) * LANE] = head(z, rows, hd, hd < rope_heads, mul).astype(ref.dtype)

    @pl.when(j == 0)
    def _():
        store(za_ref, PROJ_HEADS, Q_SCALE, first=True)

    @pl.when((j > 0) & (j < AT_KS // PROJ_HEADS))
    def _():
        store(za_ref, PROJ_HEADS, Q_SCALE)

    @pl.when(j == AT_KS // PROJ_HEADS)
    def _():
        store(za_ref, PROJ_HEADS)

    @pl.when(j == AT_VS // PROJ_HEADS)
    def _():
        store(za_ref, 0)

    @pl.when((j >= PROJ_A_STEPS) & (j < PROJ_A_STEPS + BT_KC // PROJ_HEADS))
    def _():
        store(zb_ref, 0)

    @pl.when(j == PROJ_A_STEPS + BT_KC // PROJ_HEADS)
    def _():
        store(zb_ref, NSA_KV_HEADS)

    @pl.when(j == PROJ_A_STEPS + BT_GATE // PROJ_HEADS)
    def _():
        for rows, z in chunks():
            zb_ref[rows, :] = jax.nn.sigmoid(z + gb_ref[...])


def _proj(pos, h, pre_g, w, gb, invf, tm=1024):
    n, d = h.shape
    return pl.pallas_call(
        _proj_kernel,
        name="mix_proj",
        grid=(n // tm, PROJ_STEPS),
        in_specs=[
            pl.BlockSpec((tm, 1), lambda i, j: (i, 0)),
            pl.BlockSpec((tm, d), lambda i, j: (i, 0)),
            pl.BlockSpec((1, d), lambda i, j: (0, 0)),
            pl.BlockSpec((d, PROJ_TN), lambda i, j: (0, j)),
            pl.BlockSpec((1, PROJ_TN), lambda i, j: (0, 0)),
            pl.BlockSpec((1, LANE), lambda i, j: (0, 0)),
        ],
        out_specs=[pl.BlockSpec((tm, PROJ_TN), lambda i, j: (i, jnp.minimum(j, PROJ_A_STEPS - 1))),
                   pl.BlockSpec((tm, PROJ_TN), lambda i, j: (i, jnp.maximum(j - PROJ_A_STEPS, 0)))],
        out_shape=[jax.ShapeDtypeStruct((n, A_WIDTH), BF16), jax.ShapeDtypeStruct((n, B_WIDTH), F32)],
        scratch_shapes=[pltpu.VMEM((tm, d), BF16)] + [pltpu.VMEM((tm, LANE), F32)] * 3,
        compiler_params=pltpu.CompilerParams(
            dimension_semantics=("arbitrary", "arbitrary"), vmem_limit_bytes=VMEM_LIMIT),
    )(pos, h, pre_g, w, gb, invf)


def _cmp_kernel(x_ref, pos_ref, w1_ref, w2_ref, o_ref):
    ncp = o_ref.shape[0]
    hidden_w = w1_ref.shape[-1]
    acc_a = jnp.zeros((ncp, hidden_w), F32)
    acc_b = jnp.zeros((ncp, hidden_w), F32)
    for l in range(CMP_STRIDE):
        xl = x_ref[pl.ds(l, ncp, stride=CMP_STRIDE), :]
        acc_a += _dot((xl + pos_ref[l:l + 1, :]).astype(BF16), w1_ref[l])
        acc_b += _dot((xl + pos_ref[CMP_STRIDE + l:CMP_STRIDE + l + 1, :]).astype(BF16), w1_ref[CMP_STRIDE + l])
    hidden = acc_a + pltpu.roll(acc_b, ncp - 1, axis=0)
    o_ref[...] = _dot(jax.nn.gelu(hidden).astype(BF16), w2_ref[...])


def _compress(zb, pos_kv, w1, w2, batch, seq):
    ncp = seq // CMP_STRIDE
    hidden_w = w1.shape[-1]
    return pl.pallas_call(
        _cmp_kernel,
        name="compress",
        grid=(2, batch, NSA_KV_HEADS),
        in_specs=[
            pl.BlockSpec((seq, LANE), lambda k, b, g: (b, BT_KC + 2 * k + g)),
            pl.BlockSpec((None, CMP_LEN, HEAD_DIM), lambda k, b, g: (k, 0, 0)),
            pl.BlockSpec((None, CMP_LEN, HEAD_DIM, hidden_w), lambda k, b, g: (k, 0, 0, 0)),
            pl.BlockSpec((None, hidden_w, HEAD_DIM), lambda k, b, g: (k, 0, 0)),
        ],
        out_specs=pl.BlockSpec((None, None, None, ncp, HEAD_DIM), lambda k, b, g: (k, b, g, 0, 0)),
        out_shape=jax.ShapeDtypeStruct((2, batch, NSA_KV_HEADS, ncp, HEAD_DIM), F32),
        compiler_params=pltpu.CompilerParams(
            dimension_semantics=("arbitrary",) * 3, vmem_limit_bytes=VMEM_LIMIT),
    )(zb, pos_kv, w1, w2)


def _masked_exp(s, mask):
    s = s + jnp.where(mask, 0.0, -MASK_BIG)
    e = jnp.exp2(s - jnp.max(s, axis=-1, keepdims=True))
    return e, jnp.sum(e, axis=-1, keepdims=True)


def _attn_kernel(q_ref, kc_ref, vc_ref, ks_ref, vs_ref, kw_ref, vw_ref, gz_ref, ovl_ref, oh_ref, ga_ref, o_ref,
                 *, first_block):
    G, R, tq = NSA_KV_HEADS, GQA_GROUP, Q_BLOCK
    n_keys = ks_ref.shape[0]
    top_k = min(SEL_TOPK, ovl_ref.shape[0])
    n_blk = n_keys // SEL_LEN
    n_cmp = min(kc_ref.shape[1], -(-(n_keys // CMP_STRIDE) // LANE) * LANE)
    t0 = (first_block + pl.program_id(1)) * tq
    t_idx = t0 + lax.broadcasted_iota(jnp.int32, (1, tq, 1), 1)
    gsl = lambda g: slice(g * HEAD_DIM, (g + 1) * HEAD_DIM)

    def stacked_q(g):
        q = q_ref[:, g * R * HEAD_DIM:(g + 1) * R * HEAD_DIM]
        return jnp.concatenate([q[:, r * HEAD_DIM:(r + 1) * HEAD_DIM] for r in range(R)], axis=0)

    q4s = [stacked_q(g) for g in range(G)]

    wk = min(WINDOW + tq, n_keys)
    w0 = pl.multiple_of(jnp.maximum(t0 - WINDOW, 0), tq)
    kp = w0 + lax.broadcasted_iota(jnp.int32, (1, 1, wk), 2)
    win_mask = (kp <= t_idx) & (kp > t_idx - WINDOW)

    ones_col = jnp.ones((n_keys, LANE), BF16)

    def weights(s, m):
        return jnp.exp2((s - m).astype(BF16)).reshape(R * tq, s.shape[-1])

    def pv(p, v, ones):
        return _dot(p, jnp.concatenate([v, ones], axis=1))

    def normalised(acc):
        return (acc[:, :HEAD_DIM] * (1.0 / acc[:, HEAD_DIM:])).reshape(R, tq, HEAD_DIM)

    def window(g):
        s = _dot_nt(q4s[g], kw_ref[pl.ds(w0, wk), gsl(g)]).reshape(R, tq, wk)
        s = s + jnp.where(win_mask, 0.0, -MASK_BIG)
        p = weights(s, jnp.max(s, axis=-1, keepdims=True))
        return normalised(pv(p, vw_ref[pl.ds(w0, wk), gsl(g)], ones_col[0:wk, :]))

    def front(g):
        q4 = q4s[g]

        s = _dot_nt(q4, kc_ref[g, 0:n_cmp, :].astype(BF16)).reshape(R, tq, n_cmp)
        c_idx = lax.broadcasted_iota(jnp.int32, (1, 1, n_cmp), 2)
        e, l = _masked_exp(s, (c_idx * CMP_STRIDE + (CMP_LEN - 1)) <= t_idx)
        pc = e * (jnp.where(t_idx >= CMP_LEN - 1, 1.0, 0.0) / l)
        o_cmp = _dot(pc.reshape(R * tq, n_cmp).astype(BF16), vc_ref[g, 0:n_cmp, :].astype(BF16))
        o_cmp = o_cmp.reshape(R, tq, HEAD_DIM)
        if n_blk <= top_k:
            return q4, jnp.concatenate([q4, jnp.zeros_like(q4)], axis=1), o_cmp

        pcs = pc[0]
        for r in range(1, R):
            pcs = pcs + pc[r]
        hi = pcs.astype(BF16)
        lo = (pcs - hi.astype(F32)).astype(BF16)
        ovl = ovl_ref[0:n_blk, 0:n_cmp]
        imp = _dot_nt(ovl, hi) + _dot_nt(ovl, lo)
        blk = lax.broadcasted_iota(jnp.int32, (n_blk, tq), 0)
        tcol = t0 + lax.broadcasted_iota(jnp.int32, (n_blk, tq), 1)
        forced = (blk == lax.shift_right_logical(tcol, int(np.log2(SEL_LEN)))) | (blk == 0)
        v = jnp.where(forced, SEL_FORCE, jnp.where(blk * SEL_LEN <= tcol, imp, -SEL_FORCE))
        sub = lax.broadcasted_iota(jnp.int32, (SUBLANE, tq), 0)
        groups = [v[k:k + SUBLANE, :] for k in range(0, n_blk, SUBLANE)]
        ranks = [jnp.zeros((SUBLANE, tq), F32) for _ in groups]
        for sp in range(n_blk):
            other = v[sp:sp + 1, :]
            for gi, vg in enumerate(groups):
                first = gi * SUBLANE
                if first > sp:
                    beats = other >= vg
                elif first + SUBLANE - 1 <= sp:
                    beats = other > vg
                else:
                    beats = (other > vg) | ((other == vg) & (sub > sp - first))
                ranks[gi] = ranks[gi] + jnp.where(beats, 1.0, 0.0)
        rank = jnp.concatenate(ranks, axis=0)
        unsel = jnp.where(rank < float(top_k), 0.0, -1.0)
        unsel = jnp.concatenate([unsel, jnp.zeros((LANE - n_blk, tq), F32)], axis=0).T.astype(BF16)
        q_aug = jnp.concatenate([q4, jnp.concatenate([unsel] * R, axis=0)], axis=1)
        return q4, q_aug, o_cmp

    fronts = [front(g) for g in range(G)]

    n_full = n_keys - SEL_KT
    kp = n_full + lax.broadcasted_iota(jnp.int32, (1, 1, SEL_KT), 2)
    causal_bias = jnp.where(kp <= t_idx, 0.0, -MASK_BIG)

    def selected(g):
        k_aug = jnp.concatenate([ks_ref[:, gsl(g)], oh_ref[...]], axis=1)
        s = _dot_nt(fronts[g][1], k_aug).reshape(R, tq, n_keys)
        s_last = s[:, :, n_full:] + causal_bias
        m = jnp.max(s_last, axis=-1, keepdims=True)
        if n_full:
            s_full = s[:, :, :n_full]
            m = jnp.maximum(m, jnp.max(s_full, axis=-1, keepdims=True))
        acc = pv(weights(s_last, m), vs_ref[n_full:n_keys, gsl(g)], ones_col[n_full:n_keys, :])
        if n_full:
            acc = acc + pv(weights(s_full, m), vs_ref[0:n_full, gsl(g)], ones_col[0:n_full, :])
        return normalised(acc)

    o_sels = [selected(g) for g in range(G)]
    o_wins = [window(g) for g in range(G)]
    heads = []
    for g in range(G):
        o_cmp, o_sel, o_win = fronts[g][2], o_sels[g], o_wins[g]
        gz = gz_ref[:, gsl(g)]
        for r in range(R):
            c = r * N_GATES
            heads.append(gz[:, c:c + 1] * o_cmp[r] + gz[:, c + 1:c + 2] * o_sel[r] + gz[:, c + 2:c + 3] * o_win[r])
    o_ref[...] = _rms(jnp.concatenate(heads, axis=1), ga_ref[...]).astype(o_ref.dtype)


def _attention(za, zb, kcv, ovl_t, onehot, g_attn, batch, seq):
    ncp = seq // CMP_STRIDE
    n_sel = seq // SEL_LEN
    G = NSA_KV_HEADS
    blocks = SEL_KT // Q_BLOCK
    za3 = za.reshape(batch, seq, A_WIDTH)
    zb3 = zb.reshape(batch, seq, B_WIDTH)
    outs = []
    for c in range(seq // SEL_KT):
        n_keys = (c + 1) * SEL_KT
        q_row = lambda b, i, c=c: (b, c * blocks + i, 0)
        kv_spec = lambda tile: pl.BlockSpec((None, n_keys, G * LANE), lambda b, i: (b, 0, tile // G))
        outs.append(pl.pallas_call(
            functools.partial(_attn_kernel, first_block=c * blocks),
            name=f"attention_{c}",
            grid=(batch, blocks),
            in_specs=[
                pl.BlockSpec((None, Q_BLOCK, NSA_WIDTH), q_row),
                pl.BlockSpec((None, None, G, ncp, HEAD_DIM), lambda b, i: (0, b, 0, 0, 0)),
                pl.BlockSpec((None, None, G, ncp, HEAD_DIM), lambda b, i: (1, b, 0, 0, 0)),
                kv_spec(AT_KS), kv_spec(AT_VS), kv_spec(AT_KW), kv_spec(AT_VW),
                pl.BlockSpec((None, Q_BLOCK, G * LANE), lambda b, i, c=c: (b, c * blocks + i, BT_GATE // G)),
                pl.BlockSpec((n_sel, ncp), lambda b, i: (0, 0)),
                pl.BlockSpec((n_keys, LANE), lambda b, i: (0, 0)),
                pl.BlockSpec((1, NSA_WIDTH), lambda b, i: (0, 0)),
            ],
            out_specs=pl.BlockSpec((None, Q_BLOCK, NSA_WIDTH), lambda b, i: (b, i, 0)),
            out_shape=jax.ShapeDtypeStruct((batch, SEL_KT, NSA_WIDTH), BF16),
            compiler_params=pltpu.CompilerParams(
                dimension_semantics=("arbitrary",) * 2, vmem_limit_bytes=VMEM_LIMIT),
        )(za3, kcv, kcv, za3, za3, za3, za3, zb3, ovl_t, onehot, g_attn))
    return jnp.concatenate(outs, axis=1).reshape(batch * seq, NSA_WIDTH)


def _lru_kernel(zx_ref, zy_ref, cw_ref, cb_ref, wa_ref, ba_ref, wi_ref, bi_ref, lam_ref, gr_ref,
                o_ref, xpad, a_scr, u_scr, h_scr, carry):
    tt, width = zx_ref.shape
    ti = pl.program_id(1)

    @pl.when(ti == 0)
    def _():
        xpad[0:SUBLANE, :] = jnp.zeros((SUBLANE, width), F32)
        carry[...] = jnp.zeros_like(carry)

    xb = zx_ref[...]
    xpad[SUBLANE:SUBLANE + tt, :] = xb
    xc = cb_ref[...] + cw_ref[CONV_WIDTH - 1:CONV_WIDTH, :] * xb
    for k in range(1, CONV_WIDTH):
        xc = xc + cw_ref[CONV_WIDTH - 1 - k:CONV_WIDTH - k, :] * xpad[SUBLANE - k:SUBLANE - k + tt, :]
    xpad[0:SUBLANE, :] = xb[tt - SUBLANE:tt, :]

    sp = jax.nn.softplus(-lam_ref[...])
    row8 = lax.broadcasted_iota(jnp.int32, (1, SUBLANE, LRU_BLOCK_DIM), 1)
    for hb in range(LRU_BLOCKS):
        sl = slice(hb * LRU_BLOCK_DIM, (hb + 1) * LRU_BLOCK_DIM)
        xs = xc[:, sl]
        xs16 = xs.astype(BF16)
        r = jax.nn.sigmoid(_dot(xs16, wa_ref[hb]) + ba_ref[:, sl])
        ig = jax.nn.sigmoid(_dot(xs16, wi_ref[hb]) + bi_ref[:, sl])
        log_a = -LRU_C * r * sp[:, sl]
        a = jnp.exp(log_a)
        u = jnp.sqrt(-jnp.tanh(log_a) * (a * a + 1.0)) * (ig * xs)
        a = a.reshape(tt // SUBLANE, SUBLANE, LRU_BLOCK_DIM)
        u = u.reshape(tt // SUBLANE, SUBLANE, LRU_BLOCK_DIM)
        for s in (1, 2, 4):
            ok = row8 >= s
            a_sh = pltpu.roll(a, s, axis=1)
            u_sh = pltpu.roll(u, s, axis=1)
            u = jnp.where(ok, u + a * u_sh, u)
            a = jnp.where(ok, a * a_sh, a)
        a_scr[:, sl] = a.reshape(tt, LRU_BLOCK_DIM)
        u_scr[:, sl] = u.reshape(tt, LRU_BLOCK_DIM)

    def group(gi, c):
        r0 = pl.multiple_of(gi * SUBLANE, SUBLANE)
        h = u_scr[pl.ds(r0, SUBLANE), :] + a_scr[pl.ds(r0, SUBLANE), :] * c
        h_scr[pl.ds(r0, SUBLANE), :] = h
        return jnp.broadcast_to(h[SUBLANE - 1:SUBLANE, :], (SUBLANE, width))

    c = lax.fori_loop(0, tt // SUBLANE, group, carry[...], unroll=4)
    carry[...] = c
    o_ref[...] = _rms(h_scr[...] * jax.nn.gelu(zy_ref[...]), gr_ref[...]).astype(o_ref.dtype)


def _rglru(zb, conv_w, conv_b, w_a, b_a, w_i, b_i, lam, g_rec, batch, seq, tt=256):
    nt = seq // tt
    wt = LRU_WIDTH // LANE
    full = lambda a: pl.BlockSpec(a.shape, lambda b, t: (0,) * a.ndim)
    zspec = lambda tile: pl.BlockSpec((tt, LRU_WIDTH), lambda b, t: (b * nt + t, tile // wt))
    return pl.pallas_call(
        _lru_kernel,
        name="rglru",
        grid=(batch, nt),
        in_specs=[zspec(BT_X), zspec(BT_Y), full(conv_w), full(conv_b), full(w_a), full(b_a),
                  full(w_i), full(b_i), full(lam), full(g_rec)],
        out_specs=pl.BlockSpec((tt, LRU_WIDTH), lambda b, t: (b * nt + t, 0)),
        out_shape=jax.ShapeDtypeStruct((batch * seq, LRU_WIDTH), BF16),
        scratch_shapes=[pltpu.VMEM((tt + SUBLANE, LRU_WIDTH), F32), pltpu.VMEM((tt, LRU_WIDTH), F32),
                        pltpu.VMEM((tt, LRU_WIDTH), F32), pltpu.VMEM((tt, LRU_WIDTH), F32),
                        pltpu.VMEM((SUBLANE, LRU_WIDTH), F32)],
        compiler_params=pltpu.CompilerParams(
            dimension_semantics=("arbitrary", "arbitrary"), vmem_limit_bytes=VMEM_LIMIT),
    )(zb, zb, conv_w, conv_b, w_a, b_a, w_i, b_i, lam, g_rec)


def _resident_bf16(w_ref, w16_scr):
    @pl.when(pl.program_id(0) == 0)
    def _():
        w16_scr[...] = w_ref[...].astype(BF16)


def _resident(a):
    return pl.BlockSpec(a.shape, lambda i: (0,) * a.ndim, pipeline_mode=pl.Buffered(1))


def _out_kernel(oa_ref, or_ref, w_ref, post_ref, h_ref, o_ref, w16_scr):
    _resident_bf16(w_ref, w16_scr)
    for c in range(h_ref.shape[0] // ROW_CHUNK):
        rows = slice(c * ROW_CHUNK, (c + 1) * ROW_CHUNK)
        y = (_dot(oa_ref[rows, :], w16_scr[0:NSA_WIDTH, :])
             + _dot(or_ref[rows, :], w16_scr[NSA_WIDTH:NSA_WIDTH + LRU_WIDTH, :]))
        o_ref[rows, :] = h_ref[rows, :] + _rms(y, post_ref[...])


def _out_proj(o_attn, o_rec, w_out, post_g, h, tm=512):
    n, d = h.shape
    row = lambda i: (i, 0)
    return pl.pallas_call(
        _out_kernel,
        name="out_proj",
        grid=(n // tm,),
        in_specs=[pl.BlockSpec((tm, NSA_WIDTH), row), pl.BlockSpec((tm, LRU_WIDTH), row),
                  _resident(w_out), _resident(post_g), pl.BlockSpec((tm, d), row)],
        out_specs=pl.BlockSpec((tm, d), row),
        out_shape=jax.ShapeDtypeStruct((n, d), F32),
        scratch_shapes=[pltpu.VMEM(w_out.shape, BF16)],
        compiler_params=pltpu.CompilerParams(
            dimension_semantics=("arbitrary",), vmem_limit_bytes=VMEM_LIMIT),
    )(o_attn, o_rec, w_out, post_g, h)


def _ple_kernel(h_ref, p_ref, pre_ref, wg_ref, wp_ref, post_ref, o_ref, wg16_scr, wp16_scr):
    _resident_bf16(wg_ref, wg16_scr)
    _resident_bf16(wp_ref, wp16_scr)
    for c in range(h_ref.shape[0] // ROW_CHUNK):
        rows = slice(c * ROW_CHUNK, (c + 1) * ROW_CHUNK)
        h = h_ref[rows, :]
        gate = jax.nn.sigmoid(_dot(_rms(h, pre_ref[...]).astype(BF16), wg16_scr[...]))
        pp = _dot(p_ref[rows, :].astype(BF16), wp16_scr[...])
        o_ref[rows, :] = h + _rms(gate * pp, post_ref[...])


def _ple(h, p, pre_g, w_gate, w_proj, post_g, tm=512):
    n, d = h.shape
    row = lambda i: (i, 0)
    return pl.pallas_call(
        _ple_kernel,
        name="ple",
        grid=(n // tm,),
        in_specs=[pl.BlockSpec((tm, d), row), pl.BlockSpec((tm, p.shape[1]), row),
                  _resident(pre_g), _resident(w_gate), _resident(w_proj), _resident(post_g)],
        out_specs=pl.BlockSpec((tm, d), row),
        out_shape=jax.ShapeDtypeStruct((n, d), F32),
        scratch_shapes=[pltpu.VMEM(w_gate.shape, BF16), pltpu.VMEM(w_proj.shape, BF16)],
        compiler_params=pltpu.CompilerParams(
            dimension_semantics=("arbitrary",), vmem_limit_bytes=VMEM_LIMIT),
    )(h, p, pre_g, w_gate, w_proj, post_g)


def _selection_constants(seq):
    ncp = seq // CMP_STRIDE
    n_cmp = (seq - CMP_LEN) // CMP_STRIDE + 1
    n_sel = seq // SEL_LEN
    c = np.arange(ncp)[None, :]
    s = np.arange(n_sel)[:, None]
    ovl_t = ((c * CMP_STRIDE < s * SEL_LEN + SEL_LEN) & (c * CMP_STRIDE + CMP_LEN - 1 >= s * SEL_LEN) & (c < n_cmp))
    onehot = (np.arange(seq)[:, None] // SEL_LEN) == np.arange(LANE)[None, :]
    return jnp.asarray(ovl_t, BF16), jnp.asarray(onehot * MASK_BIG, BF16)


def _pad_cols(a, width):
    return jnp.pad(a, ((0, 0), (0, width - a.shape[1])))


def kernel(x, p, positions, ff1_pre_g, ff1_post_g, ff1_w_gate, ff1_w_up, ff1_w_down, mix_pre_g, mix_post_g, w_in, cmp_pos_k, cmp_pos_v, cmp_k_w1, cmp_k_w2, cmp_v_w1, cmp_v_w2, nsa_gate_b, conv_w, conv_b, rg_w_a, rg_b_a, rg_w_i, rg_b_i, rg_lambda, attn_out_g, rec_out_g, w_out, ff2_pre_g, ff2_post_g, ff2_w_gate, ff2_w_up, ff2_w_down, ple_pre_g, ple_post_g, w_ple_gate, w_ple_proj):
    batch, seq, d = x.shape
    depth = p.shape[0]
    n = batch * seq
    vec = lambda a: a.reshape(1, -1)
    gate_cols = NSA_HEADS * N_GATES
    grp_gates = GQA_GROUP * N_GATES
    o_q, o_kc, o_vc, o_ks, o_vs, o_kw, o_vw = (NSA_WIDTH * 0,) + tuple(NSA_WIDTH + k * KV_WIDTH for k in range(6))
    o_g = NSA_WIDTH + 6 * KV_WIDTH
    o_x = o_g + gate_cols
    o_y = o_x + LRU_WIDTH

    half = jnp.arange(ROPE_HALF, dtype=F32)
    inv_freq = ROPE_THETA ** (-half / ROPE_HALF)
    invf = jnp.concatenate([inv_freq, inv_freq, jnp.zeros((LANE - ROPE_DIM,), F32)]).reshape(1, LANE)
    ovl_t, onehot = _selection_constants(seq)
    pos = positions.reshape(n, 1)

    h = x.reshape(n, d)
    for i in range(depth):
        wi = w_in[i]
        cols = lambda o, w: wi[:, o:o + w]
        w_in_z = jnp.concatenate(
            [cols(o_q, NSA_WIDTH), cols(o_ks, KV_WIDTH), cols(o_kw, KV_WIDTH), cols(o_vs, KV_WIDTH),
             cols(o_vw, KV_WIDTH), cols(o_x, LRU_WIDTH), cols(o_y, LRU_WIDTH), cols(o_kc, KV_WIDTH),
             cols(o_vc, KV_WIDTH)]
            + [_pad_cols(cols(o_g + g * grp_gates, grp_gates), LANE) for g in range(NSA_KV_HEADS)]
            + [jnp.zeros((d, B_WIDTH - (BT_GATE + NSA_KV_HEADS) * LANE), F32)],
            axis=1).astype(BF16)
        gb = nsa_gate_b[i].reshape(NSA_KV_HEADS, grp_gates)
        gate_bias = _pad_cols(jnp.pad(gb, ((0, 0), (0, LANE - grp_gates))).reshape(1, NSA_KV_HEADS * LANE), PROJ_TN)

        h = _ffn(h, vec(ff1_pre_g[i]), ff1_w_gate[i], ff1_w_up[i], ff1_w_down[i], vec(ff1_post_g[i]))
        za, zb = _proj(pos, h, vec(mix_pre_g[i]), w_in_z, gate_bias, invf)
        kcv = _compress(
            zb, jnp.stack([cmp_pos_k[i], cmp_pos_v[i]]),
            jnp.stack([cmp_k_w1[i], cmp_v_w1[i]]).reshape(2, CMP_LEN, HEAD_DIM, -1).astype(BF16),
            jnp.stack([cmp_k_w2[i], cmp_v_w2[i]]).astype(BF16), batch, seq)
        o_attn = _attention(za, zb, kcv, ovl_t, onehot, vec(attn_out_g[i]), batch, seq)
        o_rec = _rglru(zb, conv_w[i], vec(conv_b[i]), rg_w_a[i].astype(BF16), vec(rg_b_a[i]),
                       rg_w_i[i].astype(BF16), vec(rg_b_i[i]), vec(rg_lambda[i]), vec(rec_out_g[i]), batch, seq)
        h = _out_proj(o_attn, o_rec, w_out[i], vec(mix_post_g[i]), h)
        h = _ffn(h, vec(ff2_pre_g[i]), ff2_w_gate[i], ff2_w_up[i], ff2_w_down[i], vec(ff2_post_g[i]))
        h = _ple(h, p[i].reshape(n, -1), vec(ple_pre_g[i]), w_ple_gate[i], w_ple_proj[i], vec(ple_post_g[i]))
    return h.reshape(batch, seq, d)
```

```python
import functools

import numpy as np
import jax
import jax.numpy as jnp
from jax import lax
from jax.experimental import pallas as pl
from jax.experimental.pallas import tpu as pltpu

F32 = jnp.float32
BF16 = jnp.bfloat16

D_MODEL = 2048
NSA_HEADS = 8
NSA_KV_HEADS = 2
GQA_GROUP = NSA_HEADS // NSA_KV_HEADS
HEAD_DIM = 128
NSA_WIDTH = NSA_HEADS * HEAD_DIM
KV_WIDTH = NSA_KV_HEADS * HEAD_DIM
ROPE_DIM = HEAD_DIM // 4
ROPE_HALF = ROPE_DIM // 2
ROPE_THETA = 500000.0
CMP_LEN = 32
CMP_STRIDE = 16
SEL_LEN = 64
SEL_TOPK = 16
WINDOW = 512
Q_BLOCK = 128
N_GATES = 3
LRU_WIDTH = 1024
LRU_BLOCKS = 8
LRU_BLOCK_DIM = LRU_WIDTH // LRU_BLOCKS
CONV_WIDTH = 4
LRU_C = 8.0
RMS_EPS = 1e-6
NEG = -1e30
SEL_FORCE = 1e4

LANE = 128
SUBLANE = 8
VMEM_LIMIT = 62 * 1024 * 1024

AT_Q, AT_KS, AT_VS, AT_KW, AT_VW = 0, 8, 10, 12, 14
A_WIDTH = 16 * LANE
BT_X, BT_Y, BT_KC, BT_VC, BT_GATE = 0, 8, 16, 18, 20
B_WIDTH = 24 * LANE
PROJ_TN = 512
PROJ_A_STEPS = A_WIDTH // PROJ_TN
PROJ_STEPS = (A_WIDTH + B_WIDTH) // PROJ_TN
PROJ_HEADS = PROJ_TN // LANE
ROW_CHUNK = 256
PROJ_CHUNK = ROW_CHUNK
FFN_CHUNK = ROW_CHUNK

Q_SCALE = HEAD_DIM ** -0.5 * 1.4426950408889634
SEL_KT = 512
MASK_BIG = 2.0 ** 100


def _rms(x, g):
    return x * lax.rsqrt(jnp.mean(x * x, axis=-1, keepdims=True) + RMS_EPS) * g


def _dot(a, b):
    return jnp.dot(a, b, preferred_element_type=F32)


def _dot_nt(a, b):
    return lax.dot_general(a, b, (((1,), (1,)), ((), ())), preferred_element_type=F32)


def _ffn_steps(j, n_steps, h_ref, pre_ref, post_ref, o_ref, u_scr, weights):
    def step(first, last):
        wg, wu, wd = weights()
        for c in range(h_ref.shape[0] // FFN_CHUNK):
            rows = slice(c * FFN_CHUNK, (c + 1) * FFN_CHUNK)
            if first:
                u = _rms(h_ref[rows, :], pre_ref[...]).astype(BF16)
                u_scr[rows, :] = u
            else:
                u = u_scr[rows, :]
            g = _dot(u, wg)
            up = _dot(u, wu)
            acc = _dot(((g * jax.nn.sigmoid(g)) * up).astype(BF16), wd)
            if not first:
                acc = o_ref[rows, :] + acc
            if last:
                acc = h_ref[rows, :] + 0.5 * _rms(acc, post_ref[...])
            o_ref[rows, :] = acc

    if n_steps == 1:
        step(True, True)
    else:
        pl.when(j == 0)(lambda: step(True, False))
        pl.when((j > 0) & (j < n_steps - 1))(lambda: step(False, False))
        pl.when(j == n_steps - 1)(lambda: step(False, True))


def _ffn_head_kernel(h_ref, pre_ref, wg_ref, wu_ref, wd_ref, post_ref, o_ref, wg16_ref, wu16_ref, wd16_ref,
                     u_scr, *, n_steps):
    def weights():
        w16 = []
        for src, dst in ((wg_ref, wg16_ref), (wu_ref, wu16_ref), (wd_ref, wd16_ref)):
            w = src[...].astype(BF16)
            dst[...] = w
            w16.append(w)
        return w16

    _ffn_steps(pl.program_id(0), n_steps, h_ref, pre_ref, post_ref, o_ref, u_scr, weights)


def _ffn_tail_kernel(h_ref, head_ref, pre_ref, wg_ref, wu_ref, wd_ref, post_ref, o_ref, u_scr, *,
                     n_steps, n_copy):
    i, j = pl.program_id(0), pl.program_id(1)
    slab = head_ref.shape[0]

    @pl.when((i == 0) & (j < n_copy))
    def _():
        o_ref[pl.ds(pl.multiple_of(j * slab, slab), slab), :] = head_ref[...]

    @pl.when(i > 0)
    def _():
        _ffn_steps(j, n_steps, h_ref, pre_ref, post_ref, o_ref, u_scr,
                   lambda: (wg_ref[...], wu_ref[...], wd_ref[...]))


def _ffn(h, pre_g, wg, wu, wd, post_g, tm=1024, tf=512, tf_head=256):
    n, d = h.shape
    dff = wg.shape[1]
    tf, tf_head = min(tf, dff), min(tf_head, dff)
    vec1 = pl.BlockSpec((1, d), lambda j: (0, 0))
    head, wg16, wu16, wd16 = pl.pallas_call(
        functools.partial(_ffn_head_kernel, n_steps=dff // tf_head),
        name="ffn_head",
        grid=(dff // tf_head,),
        in_specs=[
            pl.BlockSpec((tm, d), lambda j: (0, 0)), vec1,
            pl.BlockSpec((d, tf_head), lambda j: (0, j)),
            pl.BlockSpec((d, tf_head), lambda j: (0, j)),
            pl.BlockSpec((tf_head, d), lambda j: (j, 0)),
            vec1,
        ],
        out_specs=[pl.BlockSpec((tm, d), lambda j: (0, 0)),
                   pl.BlockSpec((d, tf_head), lambda j: (0, j)),
                   pl.BlockSpec((d, tf_head), lambda j: (0, j)),
                   pl.BlockSpec((tf_head, d), lambda j: (j, 0))],
        out_shape=[jax.ShapeDtypeStruct((tm, d), F32), jax.ShapeDtypeStruct((d, dff), BF16),
                   jax.ShapeDtypeStruct((d, dff), BF16), jax.ShapeDtypeStruct((dff, d), BF16)],
        scratch_shapes=[pltpu.VMEM((tm, d), BF16)],
        compiler_params=pltpu.CompilerParams(
            dimension_semantics=("arbitrary",), vmem_limit_bytes=VMEM_LIMIT),
    )(h, pre_g, wg, wu, wd, post_g)

    n_steps = dff // tf
    n_copy = 1 << (min(n_steps, tm // LANE).bit_length() - 1)
    row = lambda i, j: (i, 0)
    vec = pl.BlockSpec((1, d), lambda i, j: (0, 0))
    wcol = lambda i, j: (0, jnp.where(i > 0, j, 0))
    return pl.pallas_call(
        functools.partial(_ffn_tail_kernel, n_steps=n_steps, n_copy=n_copy),
        name="ffn_tail",
        grid=(n // tm, n_steps),
        in_specs=[
            pl.BlockSpec((tm, d), row),
            pl.BlockSpec((tm // n_copy, d), lambda i, j: (jnp.where(i == 0, jnp.minimum(j, n_copy - 1), n_copy - 1), 0)),
            vec,
            pl.BlockSpec((d, tf), wcol),
            pl.BlockSpec((d, tf), wcol),
            pl.BlockSpec((tf, d), lambda i, j: (jnp.where(i > 0, j, 0), 0)),
            vec,
        ],
        out_specs=pl.BlockSpec((tm, d), row),
        out_shape=jax.ShapeDtypeStruct((n, d), F32),
        scratch_shapes=[pltpu.VMEM((tm, d), BF16)],
        compiler_params=pltpu.CompilerParams(
            dimension_semantics=("arbitrary", "arbitrary"), vmem_limit_bytes=VMEM_LIMIT),
    )(h, head, pre_g, wg16, wu16, wd16, post_g)


def _proj_kernel(pos_ref, h_ref, pre_ref, win_ref, wrest_ref, gb_ref, invf_ref, za_ref, zb_ref,
                 u_scr, cos_scr, s1_scr, s2_scr):
    j = pl.program_id(1)

    def prepare(rows):
        u_scr[rows, :] = _rms(h_ref[rows, :], pre_ref[...]).astype(BF16)
        ang = pos_ref[rows, :].astype(F32) * invf_ref[...]
        lane = lax.broadcasted_iota(jnp.int32, ang.shape, 1)
        sin = jnp.sin(ang)
        cos_scr[rows, :] = jnp.cos(ang)
        s1_scr[rows, :] = jnp.where((lane >= ROPE_HALF) & (lane < ROPE_DIM), sin, 0.0)
        s2_scr[rows, :] = jnp.where(lane < ROPE_HALF, -sin, 0.0)

    def head(z, rows, hd, rope, mul=None):
        x = z[:, hd * LANE:(hd + 1) * LANE]
        if rope:
            x = (x * cos_scr[rows, :]
                 + pltpu.roll(x, ROPE_HALF, axis=1) * s1_scr[rows, :]
                 + pltpu.roll(x, LANE - ROPE_HALF, axis=1) * s2_scr[rows, :])
        return x if mul is None else x * mul

    def chunks(w, first=False):
        for c in range(u_scr.shape[0] // PROJ_CHUNK):
            rows = slice(c * PROJ_CHUNK, (c + 1) * PROJ_CHUNK)
            if first:
                prepare(rows)
            yield rows, _dot(u_scr[rows, :], w)

    def store(ref, rope_heads, w, mul=None, first=False):
        for rows, z in chunks(w, first):
            for hd in range(PROJ_HEADS):
                ref[rows, hd * LANE:(hd + 1) * LANE] = head(z, rows, hd, hd < rope_heads, mul).astype(ref.dtype)

    w_in = lambda: win_ref[...].astype(BF16)

    @pl.when(j == 0)
    def _():
        store(za_ref, PROJ_HEADS, w_in(), Q_SCALE, first=True)

    @pl.when((j > 0) & (j < AT_KS // PROJ_HEADS))
    def _():
        store(za_ref, PROJ_HEADS, w_in(), Q_SCALE)

    @pl.when((j == AT_KS // PROJ_HEADS) | (j == AT_KW // PROJ_HEADS))
    def _():
        store(za_ref, NSA_KV_HEADS, w_in())

    @pl.when((j >= PROJ_A_STEPS) & (j < PROJ_A_STEPS + BT_KC // PROJ_HEADS))
    def _():
        store(zb_ref, 0, wrest_ref[...])

    @pl.when(j == PROJ_A_STEPS + BT_KC // PROJ_HEADS)
    def _():
        store(zb_ref, NSA_KV_HEADS, w_in())

    @pl.when(j == PROJ_A_STEPS + BT_GATE // PROJ_HEADS)
    def _():
        for rows, z in chunks(wrest_ref[...]):
            zb_ref[rows, :] = jax.nn.sigmoid(z + gb_ref[...])


def _proj_w_in_block(j):
    q_steps, sel_step, cmp_step = AT_KS // PROJ_HEADS, AT_KS // PROJ_HEADS, PROJ_A_STEPS + BT_KC // PROJ_HEADS
    return jnp.where(j < q_steps, j,
                     jnp.where(j == sel_step, q_steps + 1,
                               jnp.where(j < cmp_step, q_steps + 2, q_steps)))


def _proj_w_rest_block(j):
    n_lru = BT_KC // PROJ_HEADS
    return jnp.where(j == PROJ_STEPS - 1, n_lru, jnp.clip(j - PROJ_A_STEPS, 0, n_lru - 1))


def _proj(pos, h, pre_g, w_in, w_rest, gb, invf, tm=1024):
    n, d = h.shape
    return pl.pallas_call(
        _proj_kernel,
        name="mix_proj",
        grid=(n // tm, PROJ_STEPS),
        in_specs=[
            pl.BlockSpec((tm, 1), lambda i, j: (i, 0)),
            pl.BlockSpec((tm, d), lambda i, j: (i, 0)),
            pl.BlockSpec((1, d), lambda i, j: (0, 0)),
            pl.BlockSpec((d, PROJ_TN), lambda i, j: (0, _proj_w_in_block(j))),
            pl.BlockSpec((d, PROJ_TN), lambda i, j: (0, _proj_w_rest_block(j))),
            pl.BlockSpec((1, PROJ_TN), lambda i, j: (0, 0)),
            pl.BlockSpec((1, LANE), lambda i, j: (0, 0)),
        ],
        out_specs=[pl.BlockSpec((tm, PROJ_TN), lambda i, j: (i, jnp.minimum(j, PROJ_A_STEPS - 1))),
                   pl.BlockSpec((tm, PROJ_TN), lambda i, j: (i, jnp.maximum(j - PROJ_A_STEPS, 0)))],
        out_shape=[jax.ShapeDtypeStruct((n, A_WIDTH), BF16), jax.ShapeDtypeStruct((n, B_WIDTH), F32)],
        scratch_shapes=[pltpu.VMEM((tm, d), BF16)] + [pltpu.VMEM((tm, LANE), F32)] * 3,
        compiler_params=pltpu.CompilerParams(
            dimension_semantics=("arbitrary", "arbitrary"), vmem_limit_bytes=VMEM_LIMIT),
    )(pos, h, pre_g, w_in, w_rest, gb, invf)


def _cmp_kernel(x_ref, pos_ref, w1_ref, w2_ref, o_ref):
    ncp = o_ref.shape[0]
    hidden_w = w1_ref.shape[-1]
    acc_a = jnp.zeros((ncp, hidden_w), F32)
    acc_b = jnp.zeros((ncp, hidden_w), F32)
    for l in range(CMP_STRIDE):
        xl = x_ref[pl.ds(l, ncp, stride=CMP_STRIDE), :]
        acc_a += _dot((xl + pos_ref[l:l + 1, :]).astype(BF16), w1_ref[l])
        acc_b += _dot((xl + pos_ref[CMP_STRIDE + l:CMP_STRIDE + l + 1, :]).astype(BF16), w1_ref[CMP_STRIDE + l])
    hidden = acc_a + pltpu.roll(acc_b, ncp - 1, axis=0)
    o_ref[...] = _dot(jax.nn.gelu(hidden).astype(BF16), w2_ref[...])


def _compress(zb, pos_kv, w1, w2, batch, seq):
    ncp = seq // CMP_STRIDE
    hidden_w = w1.shape[-1]
    return pl.pallas_call(
        _cmp_kernel,
        name="compress",
        grid=(2, batch, NSA_KV_HEADS),
        in_specs=[
            pl.BlockSpec((seq, LANE), lambda k, b, g: (b, BT_KC + 2 * k + g)),
            pl.BlockSpec((None, CMP_LEN, HEAD_DIM), lambda k, b, g: (k, 0, 0)),
            pl.BlockSpec((None, CMP_LEN, HEAD_DIM, hidden_w), lambda k, b, g: (k, 0, 0, 0)),
            pl.BlockSpec((None, hidden_w, HEAD_DIM), lambda k, b, g: (k, 0, 0)),
        ],
        out_specs=pl.BlockSpec((None, None, None, ncp, HEAD_DIM), lambda k, b, g: (k, b, g, 0, 0)),
        out_shape=jax.ShapeDtypeStruct((2, batch, NSA_KV_HEADS, ncp, HEAD_DIM), F32),
        compiler_params=pltpu.CompilerParams(
            dimension_semantics=("arbitrary",) * 3, vmem_limit_bytes=VMEM_LIMIT),
    )(zb, pos_kv, w1, w2)


def _masked_exp(s, mask):
    s = s + jnp.where(mask, 0.0, -MASK_BIG)
    e = jnp.exp2(s - jnp.max(s, axis=-1, keepdims=True))
    return e, jnp.sum(e, axis=-1, keepdims=True)


def _attn_kernel(q_ref, kc_ref, vc_ref, ks_ref, vs_ref, kw_ref, vw_ref, gz_ref, ovl_ref, oh_ref, ga_ref, o_ref,
                 *, first_block):
    G, R, tq = NSA_KV_HEADS, GQA_GROUP, Q_BLOCK
    n_keys = ks_ref.shape[0]
    top_k = min(SEL_TOPK, ovl_ref.shape[0])
    n_blk = n_keys // SEL_LEN
    n_cmp = min(kc_ref.shape[1], -(-(n_keys // CMP_STRIDE) // LANE) * LANE)
    t0 = (first_block + pl.program_id(1)) * tq
    t_idx = t0 + lax.broadcasted_iota(jnp.int32, (1, tq, 1), 1)
    gsl = lambda g: slice(g * HEAD_DIM, (g + 1) * HEAD_DIM)

    def stacked_q(g):
        q = q_ref[:, g * R * HEAD_DIM:(g + 1) * R * HEAD_DIM]
        return jnp.concatenate([q[:, r * HEAD_DIM:(r + 1) * HEAD_DIM] for r in range(R)], axis=0)

    q4s = [stacked_q(g) for g in range(G)]

    wk = min(WINDOW + tq, n_keys)
    w0 = pl.multiple_of(jnp.maximum(t0 - WINDOW, 0), tq)
    kp = w0 + lax.broadcasted_iota(jnp.int32, (1, 1, wk), 2)
    win_mask = (kp <= t_idx) & (kp > t_idx - WINDOW)

    ones_col = jnp.ones((n_keys, LANE), BF16)

    def weights(s, m):
        return jnp.exp2((s - m).astype(BF16)).reshape(R * tq, s.shape[-1])

    def pv(p, v, ones):
        return _dot(p, jnp.concatenate([v, ones], axis=1))

    def normalised(acc):
        return (acc[:, :HEAD_DIM] * (1.0 / acc[:, HEAD_DIM:])).reshape(R, tq, HEAD_DIM)

    def window(g):
        s = _dot_nt(q4s[g], kw_ref[pl.ds(w0, wk), gsl(g)]).reshape(R, tq, wk)
        s = s + jnp.where(win_mask, 0.0, -MASK_BIG)
        p = weights(s, jnp.max(s, axis=-1, keepdims=True))
        return normalised(pv(p, vw_ref[pl.ds(w0, wk), gsl(g)], ones_col[0:wk, :]))

    def front(g):
        q4 = q4s[g]

        s = _dot_nt(q4, kc_ref[g, 0:n_cmp, :].astype(BF16)).reshape(R, tq, n_cmp)
        c_idx = lax.broadcasted_iota(jnp.int32, (1, 1, n_cmp), 2)
        e, l = _masked_exp(s, (c_idx * CMP_STRIDE + (CMP_LEN - 1)) <= t_idx)
        pc = e * (jnp.where(t_idx >= CMP_LEN - 1, 1.0, 0.0) / l)
        o_cmp = _dot(pc.reshape(R * tq, n_cmp).astype(BF16), vc_ref[g, 0:n_cmp, :].astype(BF16))
        o_cmp = o_cmp.reshape(R, tq, HEAD_DIM)
        if n_blk <= top_k:
            return q4, jnp.concatenate([q4, jnp.zeros_like(q4)], axis=1), o_cmp

        pcs = pc[0]
        for r in range(1, R):
            pcs = pcs + pc[r]
        hi = pcs.astype(BF16)
        lo = (pcs - hi.astype(F32)).astype(BF16)
        ovl = ovl_ref[0:n_blk, 0:n_cmp]
        imp = _dot_nt(ovl, hi) + _dot_nt(ovl, lo)
        blk = lax.broadcasted_iota(jnp.int32, (n_blk, tq), 0)
        tcol = t0 + lax.broadcasted_iota(jnp.int32, (n_blk, tq), 1)
        forced = (blk == lax.shift_right_logical(tcol, int(np.log2(SEL_LEN)))) | (blk == 0)
        v = jnp.where(forced, SEL_FORCE, jnp.where(blk * SEL_LEN <= tcol, imp, -SEL_FORCE))
        sub = lax.broadcasted_iota(jnp.int32, (SUBLANE, tq), 0)
        groups = [v[k:k + SUBLANE, :] for k in range(0, n_blk, SUBLANE)]
        ranks = [jnp.zeros((SUBLANE, tq), F32) for _ in groups]
        for sp in range(n_blk):
            other = v[sp:sp + 1, :]
            for gi, vg in enumerate(groups):
                first = gi * SUBLANE
                if first > sp:
                    beats = other >= vg
                elif first + SUBLANE - 1 <= sp:
                    beats = other > vg
                else:
                    beats = (other > vg) | ((other == vg) & (sub > sp - first))
                ranks[gi] = ranks[gi] + jnp.where(beats, 1.0, 0.0)
        rank = jnp.concatenate(ranks, axis=0)
        unsel = jnp.where(rank < float(top_k), 0.0, -1.0)
        unsel = jnp.concatenate([unsel, jnp.zeros((LANE - n_blk, tq), F32)], axis=0).T.astype(BF16)
        q_aug = jnp.concatenate([q4, jnp.concatenate([unsel] * R, axis=0)], axis=1)
        return q4, q_aug, o_cmp

    fronts = [front(g) for g in range(G)]

    n_full = n_keys - SEL_KT
    kp = n_full + lax.broadcasted_iota(jnp.int32, (1, 1, SEL_KT), 2)
    causal_bias = jnp.where(kp <= t_idx, 0.0, -MASK_BIG)

    def selected(g):
        k_aug = jnp.concatenate([ks_ref[:, gsl(g)], oh_ref[...]], axis=1)
        s = _dot_nt(fronts[g][1], k_aug).reshape(R, tq, n_keys)
        s_last = s[:, :, n_full:] + causal_bias
        m = jnp.max(s_last, axis=-1, keepdims=True)
        if n_full:
            s_full = s[:, :, :n_full]
            m = jnp.maximum(m, jnp.max(s_full, axis=-1, keepdims=True))
        acc = pv(weights(s_last, m), vs_ref[n_full:n_keys, gsl(g)], ones_col[n_full:n_keys, :])
        if n_full:
            acc = acc + pv(weights(s_full, m), vs_ref[0:n_full, gsl(g)], ones_col[0:n_full, :])
        return normalised(acc)

    o_sels = [selected(g) for g in range(G)]
    o_wins = [window(g) for g in range(G)]
    heads = []
    for g in range(G):
        o_cmp, o_sel, o_win = fronts[g][2], o_sels[g], o_wins[g]
        gz = gz_ref[:, gsl(g)]
        for r in range(R):
            c = r * N_GATES
            heads.append(gz[:, c:c + 1] * o_cmp[r] + gz[:, c + 1:c + 2] * o_sel[r] + gz[:, c + 2:c + 3] * o_win[r])
    o_ref[...] = _rms(jnp.concatenate(heads, axis=1), ga_ref[...]).astype(o_ref.dtype)


def _attention(za, zb, kcv, ovl_t, onehot, g_attn, batch, seq):
    ncp = seq // CMP_STRIDE
    n_sel = seq // SEL_LEN
    G = NSA_KV_HEADS
    blocks = SEL_KT // Q_BLOCK
    za3 = za.reshape(batch, seq, A_WIDTH)
    zb3 = zb.reshape(batch, seq, B_WIDTH)
    outs = []
    for c in range(seq // SEL_KT):
        n_keys = (c + 1) * SEL_KT
        q_row = lambda b, i, c=c: (b, c * blocks + i, 0)
        kv_spec = lambda tile: pl.BlockSpec((None, n_keys, G * LANE), lambda b, i: (b, 0, tile // G))
        outs.append(pl.pallas_call(
            functools.partial(_attn_kernel, first_block=c * blocks),
            name=f"attention_{c}",
            grid=(batch, blocks),
            in_specs=[
                pl.BlockSpec((None, Q_BLOCK, NSA_WIDTH), q_row),
                pl.BlockSpec((None, None, G, ncp, HEAD_DIM), lambda b, i: (0, b, 0, 0, 0)),
                pl.BlockSpec((None, None, G, ncp, HEAD_DIM), lambda b, i: (1, b, 0, 0, 0)),
                kv_spec(AT_KS), kv_spec(AT_VS), kv_spec(AT_KW), kv_spec(AT_VW),
                pl.BlockSpec((None, Q_BLOCK, G * LANE), lambda b, i, c=c: (b, c * blocks + i, BT_GATE // G)),
                pl.BlockSpec((n_sel, ncp), lambda b, i: (0, 0)),
                pl.BlockSpec((n_keys, LANE), lambda b, i: (0, 0)),
                pl.BlockSpec((1, NSA_WIDTH), lambda b, i: (0, 0)),
            ],
            out_specs=pl.BlockSpec((None, Q_BLOCK, NSA_WIDTH), lambda b, i: (b, i, 0)),
            out_shape=jax.ShapeDtypeStruct((batch, SEL_KT, NSA_WIDTH), BF16),
            compiler_params=pltpu.CompilerParams(
                dimension_semantics=("arbitrary",) * 2, vmem_limit_bytes=VMEM_LIMIT),
        )(za3, kcv, kcv, za3, za3, za3, za3, zb3, ovl_t, onehot, g_attn))
    return jnp.concatenate(outs, axis=1).reshape(batch * seq, NSA_WIDTH)


def _lru_kernel(zx_ref, zy_ref, cw_ref, cb_ref, wa_ref, ba_ref, wi_ref, bi_ref, lam_ref, gr_ref,
                o_ref, xpad, a_scr, u_scr, h_scr, carry):
    tt, width = zx_ref.shape
    ti = pl.program_id(1)

    @pl.when(ti == 0)
    def _():
        xpad[0:SUBLANE, :] = jnp.zeros((SUBLANE, width), F32)
        carry[...] = jnp.zeros_like(carry)

    xb = zx_ref[...]
    xpad[SUBLANE:SUBLANE + tt, :] = xb
    xc = cb_ref[...] + cw_ref[CONV_WIDTH - 1:CONV_WIDTH, :] * xb
    for k in range(1, CONV_WIDTH):
        xc = xc + cw_ref[CONV_WIDTH - 1 - k:CONV_WIDTH - k, :] * xpad[SUBLANE - k:SUBLANE - k + tt, :]
    xpad[0:SUBLANE, :] = xb[tt - SUBLANE:tt, :]

    sp = jax.nn.softplus(-lam_ref[...])
    row8 = lax.broadcasted_iota(jnp.int32, (1, SUBLANE, LRU_BLOCK_DIM), 1)
    for hb in range(LRU_BLOCKS):
        sl = slice(hb * LRU_BLOCK_DIM, (hb + 1) * LRU_BLOCK_DIM)
        xs = xc[:, sl]
        xs16 = xs.astype(BF16)
        r = jax.nn.sigmoid(_dot(xs16, wa_ref[hb]) + ba_ref[:, sl])
        ig = jax.nn.sigmoid(_dot(xs16, wi_ref[hb]) + bi_ref[:, sl])
        log_a = -LRU_C * r * sp[:, sl]
        a = jnp.exp(log_a)
        u = jnp.sqrt(-jnp.tanh(log_a) * (a * a + 1.0)) * (ig * xs)
        a = a.reshape(tt // SUBLANE, SUBLANE, LRU_BLOCK_DIM)
        u = u.reshape(tt // SUBLANE, SUBLANE, LRU_BLOCK_DIM)
        for s in (1, 2, 4):
            ok = row8 >= s
            a_sh = pltpu.roll(a, s, axis=1)
            u_sh = pltpu.roll(u, s, axis=1)
            u = jnp.where(ok, u + a * u_sh, u)
            a = jnp.where(ok, a * a_sh, a)
        a_scr[:, sl] = a.reshape(tt, LRU_BLOCK_DIM)
        u_scr[:, sl] = u.reshape(tt, LRU_BLOCK_DIM)

    def group(gi, c):
        r0 = pl.multiple_of(gi * SUBLANE, SUBLANE)
        h = u_scr[pl.ds(r0, SUBLANE), :] + a_scr[pl.ds(r0, SUBLANE), :] * c
        h_scr[pl.ds(r0, SUBLANE), :] = h
        return jnp.broadcast_to(h[SUBLANE - 1:SUBLANE, :], (SUBLANE, width))

    c = lax.fori_loop(0, tt // SUBLANE, group, carry[...], unroll=4)
    carry[...] = c
    o_ref[...] = _rms(h_scr[...] * jax.nn.gelu(zy_ref[...]), gr_ref[...]).astype(o_ref.dtype)


def _rglru(zb, conv_w, conv_b, w_a, b_a, w_i, b_i, lam, g_rec, batch, seq, tt=256):
    nt = seq // tt
    wt = LRU_WIDTH // LANE
    full = lambda a: pl.BlockSpec(a.shape, lambda b, t: (0,) * a.ndim)
    zspec = lambda tile: pl.BlockSpec((tt, LRU_WIDTH), lambda b, t: (b * nt + t, tile // wt))
    return pl.pallas_call(
        _lru_kernel,
        name="rglru",
        grid=(batch, nt),
        in_specs=[zspec(BT_X), zspec(BT_Y), full(conv_w), full(conv_b), full(w_a), full(b_a),
                  full(w_i), full(b_i), full(lam), full(g_rec)],
        out_specs=pl.BlockSpec((tt, LRU_WIDTH), lambda b, t: (b * nt + t, 0)),
        out_shape=jax.ShapeDtypeStruct((batch * seq, LRU_WIDTH), BF16),
        scratch_shapes=[pltpu.VMEM((tt + SUBLANE, LRU_WIDTH), F32), pltpu.VMEM((tt, LRU_WIDTH), F32),
                        pltpu.VMEM((tt, LRU_WIDTH), F32), pltpu.VMEM((tt, LRU_WIDTH), F32),
                        pltpu.VMEM((SUBLANE, LRU_WIDTH), F32)],
        compiler_params=pltpu.CompilerParams(
            dimension_semantics=("arbitrary", "arbitrary"), vmem_limit_bytes=VMEM_LIMIT),
    )(zb, zb, conv_w, conv_b, w_a, b_a, w_i, b_i, lam, g_rec)


def _resident_bf16(w_ref, w16_scr):
    @pl.when(pl.program_id(0) == 0)
    def _():
        w16_scr[...] = w_ref[...].astype(BF16)


def _resident(a):
    return pl.BlockSpec(a.shape, lambda i: (0,) * a.ndim, pipeline_mode=pl.Buffered(1))


def _out_kernel(oa_ref, or_ref, w_ref, post_ref, h_ref, o_ref, w16_scr):
    _resident_bf16(w_ref, w16_scr)
    for c in range(h_ref.shape[0] // ROW_CHUNK):
        rows = slice(c * ROW_CHUNK, (c + 1) * ROW_CHUNK)
        y = (_dot(oa_ref[rows, :], w16_scr[0:NSA_WIDTH, :])
             + _dot(or_ref[rows, :], w16_scr[NSA_WIDTH:NSA_WIDTH + LRU_WIDTH, :]))
        o_ref[rows, :] = h_ref[rows, :] + _rms(y, post_ref[...])


def _out_proj(o_attn, o_rec, w_out, post_g, h, tm=512):
    n, d = h.shape
    row = lambda i: (i, 0)
    return pl.pallas_call(
        _out_kernel,
        name="out_proj",
        grid=(n // tm,),
        in_specs=[pl.BlockSpec((tm, NSA_WIDTH), row), pl.BlockSpec((tm, LRU_WIDTH), row),
                  _resident(w_out), _resident(post_g), pl.BlockSpec((tm, d), row)],
        out_specs=pl.BlockSpec((tm, d), row),
        out_shape=jax.ShapeDtypeStruct((n, d), F32),
        scratch_shapes=[pltpu.VMEM(w_out.shape, BF16)],
        compiler_params=pltpu.CompilerParams(
            dimension_semantics=("arbitrary",), vmem_limit_bytes=VMEM_LIMIT),
    )(o_attn, o_rec, w_out, post_g, h)


def _ple_kernel(h_ref, p_ref, pre_ref, wg_ref, wp_ref, post_ref, o_ref, wg16_scr, wp16_scr):
    _resident_bf16(wg_ref, wg16_scr)
    _resident_bf16(wp_ref, wp16_scr)
    for c in range(h_ref.shape[0] // ROW_CHUNK):
        rows = slice(c * ROW_CHUNK, (c + 1) * ROW_CHUNK)
        h = h_ref[rows, :]
        gate = jax.nn.sigmoid(_dot(_rms(h, pre_ref[...]).astype(BF16), wg16_scr[...]))
        pp = _dot(p_ref[rows, :].astype(BF16), wp16_scr[...])
        o_ref[rows, :] = h + _rms(gate * pp, post_ref[...])


def _ple(h, p, pre_g, w_gate, w_proj, post_g, tm=512):
    n, d = h.shape
    row = lambda i: (i, 0)
    return pl.pallas_call(
        _ple_kernel,
        name="ple",
        grid=(n // tm,),
        in_specs=[pl.BlockSpec((tm, d), row), pl.BlockSpec((tm, p.shape[1]), row),
                  _resident(pre_g), _resident(w_gate), _resident(w_proj), _resident(post_g)],
        out_specs=pl.BlockSpec((tm, d), row),
        out_shape=jax.ShapeDtypeStruct((n, d), F32),
        scratch_shapes=[pltpu.VMEM(w_gate.shape, BF16), pltpu.VMEM(w_proj.shape, BF16)],
        compiler_params=pltpu.CompilerParams(
            dimension_semantics=("arbitrary",), vmem_limit_bytes=VMEM_LIMIT),
    )(h, p, pre_g, w_gate, w_proj, post_g)


def _selection_constants(seq):
    ncp = seq // CMP_STRIDE
    n_cmp = (seq - CMP_LEN) // CMP_STRIDE + 1
    n_sel = seq // SEL_LEN
    c = np.arange(ncp)[None, :]
    s = np.arange(n_sel)[:, None]
    ovl_t = ((c * CMP_STRIDE < s * SEL_LEN + SEL_LEN) & (c * CMP_STRIDE + CMP_LEN - 1 >= s * SEL_LEN) & (c < n_cmp))
    onehot = (np.arange(seq)[:, None] // SEL_LEN) == np.arange(LANE)[None, :]
    return jnp.asarray(ovl_t, BF16), jnp.asarray(onehot * MASK_BIG, BF16)


def _pad_cols(a, width):
    return jnp.pad(a, ((0, 0), (0, width - a.shape[1])))


def kernel(x, p, positions, ff1_pre_g, ff1_post_g, ff1_w_gate, ff1_w_up, ff1_w_down, mix_pre_g, mix_post_g, w_in, cmp_pos_k, cmp_pos_v, cmp_k_w1, cmp_k_w2, cmp_v_w1, cmp_v_w2, nsa_gate_b, conv_w, conv_b, rg_w_a, rg_b_a, rg_w_i, rg_b_i, rg_lambda, attn_out_g, rec_out_g, w_out, ff2_pre_g, ff2_post_g, ff2_w_gate, ff2_w_up, ff2_w_down, ple_pre_g, ple_post_g, w_ple_gate, w_ple_proj):
    batch, seq, d = x.shape
    depth = p.shape[0]
    n = batch * seq
    vec = lambda a: a.reshape(1, -1)
    gate_cols = NSA_HEADS * N_GATES
    grp_gates = GQA_GROUP * N_GATES
    o_q, o_kc, o_vc, o_ks, o_vs, o_kw, o_vw = (NSA_WIDTH * 0,) + tuple(NSA_WIDTH + k * KV_WIDTH for k in range(6))
    o_g = NSA_WIDTH + 6 * KV_WIDTH
    o_x = o_g + gate_cols
    o_y = o_x + LRU_WIDTH

    half = jnp.arange(ROPE_HALF, dtype=F32)
    inv_freq = ROPE_THETA ** (-half / ROPE_HALF)
    invf = jnp.concatenate([inv_freq, inv_freq, jnp.zeros((LANE - ROPE_DIM,), F32)]).reshape(1, LANE)
    ovl_t, onehot = _selection_constants(seq)
    pos = positions.reshape(n, 1)

    h = x.reshape(n, d)
    for i in range(depth):
        wi = w_in[i]
        cols = lambda o, w: wi[:, o:o + w]
        w_rest = jnp.concatenate(
            [cols(o_x, LRU_WIDTH), cols(o_y, LRU_WIDTH)]
            + [_pad_cols(cols(o_g + g * grp_gates, grp_gates), LANE) for g in range(NSA_KV_HEADS)]
            + [jnp.zeros((d, B_WIDTH - (BT_GATE + NSA_KV_HEADS) * LANE), F32)],
            axis=1).astype(BF16)
        gb = nsa_gate_b[i].reshape(NSA_KV_HEADS, grp_gates)
        gate_bias = _pad_cols(jnp.pad(gb, ((0, 0), (0, LANE - grp_gates))).reshape(1, NSA_KV_HEADS * LANE), PROJ_TN)

        h = _ffn(h, vec(ff1_pre_g[i]), ff1_w_gate[i], ff1_w_up[i], ff1_w_down[i], vec(ff1_post_g[i]))
        za, zb = _proj(pos, h, vec(mix_pre_g[i]), wi, w_rest, gate_bias, invf)
        kcv = _compress(
            zb, jnp.stack([cmp_pos_k[i], cmp_pos_v[i]]),
            jnp.stack([cmp_k_w1[i], cmp_v_w1[i]]).reshape(2, CMP_LEN, HEAD_DIM, -1).astype(BF16),
            jnp.stack([cmp_k_w2[i], cmp_v_w2[i]]).astype(BF16), batch, seq)
        o_attn = _attention(za, zb, kcv, ovl_t, onehot, vec(attn_out_g[i]), batch, seq)
        o_rec = _rglru(zb, conv_w[i], vec(conv_b[i]), rg_w_a[i].astype(BF16), vec(rg_b_a[i]),
                       rg_w_i[i].astype(BF16), vec(rg_b_i[i]), vec(rg_lambda[i]), vec(rec_out_g[i]), batch, seq)
        h = _out_proj(o_attn, o_rec, w_out[i], vec(mix_post_g[i]), h)
        h = _ffn(h, vec(ff2_pre_g[i]), ff2_w_gate[i], ff2_w_up[i], ff2_w_down[i], vec(ff2_post_g[i]))
        h = _ple(h, p[i].reshape(n, -1), vec(ple_pre_g[i]), w_ple_gate[i], w_ple_proj[i], vec(ple_post_g[i]))
    return h.reshape(batch, seq, d)
```

```python
import functools

import numpy as np
import jax
import jax.numpy as jnp
from jax import lax
from jax.experimental import pallas as pl
from jax.experimental.pallas import tpu as pltpu

F32 = jnp.float32
BF16 = jnp.bfloat16

D_MODEL = 2048
NSA_HEADS = 8
NSA_KV_HEADS = 2
GQA_GROUP = NSA_HEADS // NSA_KV_HEADS
HEAD_DIM = 128
NSA_WIDTH = NSA_HEADS * HEAD_DIM
KV_WIDTH = NSA_KV_HEADS * HEAD_DIM
ROPE_DIM = HEAD_DIM // 4
ROPE_HALF = ROPE_DIM // 2
ROPE_THETA = 500000.0
CMP_LEN = 32
CMP_STRIDE = 16
SEL_LEN = 64
SEL_TOPK = 16
WINDOW = 512
Q_BLOCK = 128
N_GATES = 3
LRU_WIDTH = 1024
LRU_BLOCKS = 8
LRU_BLOCK_DIM = LRU_WIDTH // LRU_BLOCKS
CONV_WIDTH = 4
LRU_C = 8.0
RMS_EPS = 1e-6
NEG = -1e30
SEL_FORCE = 1e4

LANE = 128
SUBLANE = 8
VMEM_LIMIT = 62 * 1024 * 1024

AT_Q, AT_KS, AT_KW, AT_VS, AT_VW = 0, 8, 10, 12, 14
A_WIDTH = 16 * LANE
BT_X, BT_Y, BT_KC, BT_VC, BT_GATE = 0, 8, 16, 18, 20
B_WIDTH = 24 * LANE
PROJ_TN = 512
PROJ_A_STEPS = A_WIDTH // PROJ_TN
PROJ_STEPS = (A_WIDTH + B_WIDTH) // PROJ_TN
PROJ_HEADS = PROJ_TN // LANE
ROW_CHUNK = 256
PROJ_CHUNK = ROW_CHUNK
ROPE_PACK = LANE // ROPE_DIM
FFN_CHUNK = ROW_CHUNK

Q_SCALE = HEAD_DIM ** -0.5 * 1.4426950408889634
SEL_KT = 512
MASK_BIG = 2.0 ** 100


def _rms(x, g):
    return x * lax.rsqrt(jnp.mean(x * x, axis=-1, keepdims=True) + RMS_EPS) * g


def _dot(a, b):
    return jnp.dot(a, b, preferred_element_type=F32)


def _dot_nt(a, b):
    return lax.dot_general(a, b, (((1,), (1,)), ((), ())), preferred_element_type=F32)


def _ffn_steps(j, n_steps, h_ref, pre_ref, post_ref, o_ref, u_scr, weights):
    def step(first, last):
        wg, wu, wd = weights()
        for c in range(h_ref.shape[0] // FFN_CHUNK):
            rows = slice(c * FFN_CHUNK, (c + 1) * FFN_CHUNK)
            if first:
                u = _rms(h_ref[rows, :], pre_ref[...]).astype(BF16)
                u_scr[rows, :] = u
            else:
                u = u_scr[rows, :]
            g = _dot(u, wg)
            up = _dot(u, wu)
            acc = _dot(((g * jax.nn.sigmoid(g)) * up).astype(BF16), wd)
            if not first:
                acc = o_ref[rows, :] + acc
            if last:
                acc = h_ref[rows, :] + 0.5 * _rms(acc, post_ref[...])
            o_ref[rows, :] = acc

    if n_steps == 1:
        step(True, True)
    else:
        pl.when(j == 0)(lambda: step(True, False))
        pl.when((j > 0) & (j < n_steps - 1))(lambda: step(False, False))
        pl.when(j == n_steps - 1)(lambda: step(False, True))


def _ffn_head_kernel(h_ref, pre_ref, wg_ref, wu_ref, wd_ref, post_ref, o_ref, wg16_ref, wu16_ref, wd16_ref,
                     u_scr, *, n_steps):
    def weights():
        w16 = []
        for src, dst in ((wg_ref, wg16_ref), (wu_ref, wu16_ref), (wd_ref, wd16_ref)):
            w = src[...].astype(BF16)
            dst[...] = w
            w16.append(w)
        return w16

    _ffn_steps(pl.program_id(0), n_steps, h_ref, pre_ref, post_ref, o_ref, u_scr, weights)


def _ffn_tail_kernel(h_ref, head_ref, pre_ref, wg_ref, wu_ref, wd_ref, post_ref, o_ref, u_scr, *,
                     n_steps, n_copy):
    i, j = pl.program_id(0), pl.program_id(1)
    slab = head_ref.shape[0]

    @pl.when((i == 0) & (j < n_copy))
    def _():
        o_ref[pl.ds(pl.multiple_of(j * slab, slab), slab), :] = head_ref[...]

    @pl.when(i > 0)
    def _():
        _ffn_steps(j, n_steps, h_ref, pre_ref, post_ref, o_ref, u_scr,
                   lambda: (wg_ref[...], wu_ref[...], wd_ref[...]))


def _ffn(h, pre_g, wg, wu, wd, post_g, tm=1024, tf=512, tf_head=256):
    n, d = h.shape
    dff = wg.shape[1]
    tf, tf_head = min(tf, dff), min(tf_head, dff)
    vec1 = pl.BlockSpec((1, d), lambda j: (0, 0))
    head, wg16, wu16, wd16 = pl.pallas_call(
        functools.partial(_ffn_head_kernel, n_steps=dff // tf_head),
        name="ffn_head",
        grid=(dff // tf_head,),
        in_specs=[
            pl.BlockSpec((tm, d), lambda j: (0, 0)), vec1,
            pl.BlockSpec((d, tf_head), lambda j: (0, j)),
            pl.BlockSpec((d, tf_head), lambda j: (0, j)),
            pl.BlockSpec((tf_head, d), lambda j: (j, 0)),
            vec1,
        ],
        out_specs=[pl.BlockSpec((tm, d), lambda j: (0, 0)),
                   pl.BlockSpec((d, tf_head), lambda j: (0, j)),
                   pl.BlockSpec((d, tf_head), lambda j: (0, j)),
                   pl.BlockSpec((tf_head, d), lambda j: (j, 0))],
        out_shape=[jax.ShapeDtypeStruct((tm, d), F32), jax.ShapeDtypeStruct((d, dff), BF16),
                   jax.ShapeDtypeStruct((d, dff), BF16), jax.ShapeDtypeStruct((dff, d), BF16)],
        scratch_shapes=[pltpu.VMEM((tm, d), BF16)],
        compiler_params=pltpu.CompilerParams(
            dimension_semantics=("arbitrary",), vmem_limit_bytes=VMEM_LIMIT),
    )(h, pre_g, wg, wu, wd, post_g)

    n_steps = dff // tf
    n_copy = 1 << (min(n_steps, tm // LANE).bit_length() - 1)
    row = lambda i, j: (i, 0)
    vec = pl.BlockSpec((1, d), lambda i, j: (0, 0))
    wcol = lambda i, j: (0, jnp.where(i > 0, j, 0))
    return pl.pallas_call(
        functools.partial(_ffn_tail_kernel, n_steps=n_steps, n_copy=n_copy),
        name="ffn_tail",
        grid=(n // tm, n_steps),
        in_specs=[
            pl.BlockSpec((tm, d), row),
            pl.BlockSpec((tm // n_copy, d), lambda i, j: (jnp.where(i == 0, jnp.minimum(j, n_copy - 1), n_copy - 1), 0)),
            vec,
            pl.BlockSpec((d, tf), wcol),
            pl.BlockSpec((d, tf), wcol),
            pl.BlockSpec((tf, d), lambda i, j: (jnp.where(i > 0, j, 0), 0)),
            vec,
        ],
        out_specs=pl.BlockSpec((tm, d), row),
        out_shape=jax.ShapeDtypeStruct((n, d), F32),
        scratch_shapes=[pltpu.VMEM((tm, d), BF16)],
        compiler_params=pltpu.CompilerParams(
            dimension_semantics=("arbitrary", "arbitrary"), vmem_limit_bytes=VMEM_LIMIT),
    )(h, head, pre_g, wg16, wu16, wd16, post_g)


def _proj_kernel(pos_ref, h_ref, pre_ref, w_ref, gb_ref, invf_ref, za_ref, zb_ref, u_scr, cos_scr, s1_scr, s2_scr):
    j = pl.program_id(1)

    def prepare(rows):
        u_scr[rows, :] = _rms(h_ref[rows, :], pre_ref[...]).astype(BF16)
        sub = PROJ_CHUNK // ROPE_PACK
        packed = slice(rows.start // ROPE_PACK, rows.start // ROPE_PACK + sub)
        ang = pos_ref[packed, :].astype(F32) * invf_ref[...]
        cos, sin = jnp.cos(ang), jnp.sin(ang)
        lane = lax.broadcasted_iota(jnp.int32, ang.shape, 1)
        for k in range(ROPE_PACK):
            grp = slice(rows.start + k * sub, rows.start + (k + 1) * sub)
            c, s = (cos, sin) if k == 0 else (pltpu.roll(cos, LANE - k * ROPE_DIM, axis=1),
                                              pltpu.roll(sin, LANE - k * ROPE_DIM, axis=1))
            cos_scr[grp, :] = jnp.where(lane < ROPE_DIM, c, 1.0)
            s1_scr[grp, :] = jnp.where((lane >= ROPE_HALF) & (lane < ROPE_DIM), s, 0.0)
            s2_scr[grp, :] = jnp.where(lane < ROPE_HALF, -s, 0.0)

    def head(z, rows, hd, rope, mul=None):
        x = z[:, hd * LANE:(hd + 1) * LANE]
        if rope:
            x = (x * cos_scr[rows, :]
                 + pltpu.roll(x, ROPE_HALF, axis=1) * s1_scr[rows, :]
                 + pltpu.roll(x, LANE - ROPE_HALF, axis=1) * s2_scr[rows, :])
        return x if mul is None else x * mul

    def chunks(first=False):
        for c in range(u_scr.shape[0] // PROJ_CHUNK):
            rows = slice(c * PROJ_CHUNK, (c + 1) * PROJ_CHUNK)
            if first:
                prepare(rows)
            yield rows, _dot(u_scr[rows, :], w_ref[...])

    def store(ref, rope_heads, mul=None, first=False):
        for rows, z in chunks(first):
            for hd in range(PROJ_HEADS):
                ref[rows, hd * LANE:(hd + 1) * LANE] = head(z, rows, hd, hd < rope_heads, mul).astype(ref.dtype)

    @pl.when(j == 0)
    def _():
        store(za_ref, PROJ_HEADS, Q_SCALE, first=True)

    @pl.when((j > 0) & (j < AT_KS // PROJ_HEADS))
    def _():
        store(za_ref, PROJ_HEADS, Q_SCALE)

    @pl.when(j == AT_KS // PROJ_HEADS)
    def _():
        store(za_ref, PROJ_HEADS)

    @pl.when(j == AT_VS // PROJ_HEADS)
    def _():
        store(za_ref, 0)

    @pl.when((j >= PROJ_A_STEPS) & (j < PROJ_A_STEPS + BT_KC // PROJ_HEADS))
    def _():
        store(zb_ref, 0)

    @pl.when(j == PROJ_A_STEPS + BT_KC // PROJ_HEADS)
    def _():
        store(zb_ref, NSA_KV_HEADS)

    @pl.when(j == PROJ_A_STEPS + BT_GATE // PROJ_HEADS)
    def _():
        for rows, z in chunks():
            zb_ref[rows, :] = jax.nn.sigmoid(z + gb_ref[...])


def _proj(pos, h, pre_g, w, gb, invf, tm=1024):
    n, d = h.shape
    return pl.pallas_call(
        _proj_kernel,
        name="mix_proj",
        grid=(n // tm, PROJ_STEPS),
        in_specs=[
            pl.BlockSpec((tm // ROPE_PACK, LANE), lambda i, j: (i, 0)),
            pl.BlockSpec((tm, d), lambda i, j: (i, 0)),
            pl.BlockSpec((1, d), lambda i, j: (0, 0)),
            pl.BlockSpec((d, PROJ_TN), lambda i, j: (0, j)),
            pl.BlockSpec((1, PROJ_TN), lambda i, j: (0, 0)),
            pl.BlockSpec((1, LANE), lambda i, j: (0, 0)),
        ],
        out_specs=[pl.BlockSpec((tm, PROJ_TN), lambda i, j: (i, jnp.minimum(j, PROJ_A_STEPS - 1))),
                   pl.BlockSpec((tm, PROJ_TN), lambda i, j: (i, jnp.maximum(j - PROJ_A_STEPS, 0)))],
        out_shape=[jax.ShapeDtypeStruct((n, A_WIDTH), BF16), jax.ShapeDtypeStruct((n, B_WIDTH), F32)],
        scratch_shapes=[pltpu.VMEM((tm, d), BF16)] + [pltpu.VMEM((tm, LANE), F32)] * 3,
        compiler_params=pltpu.CompilerParams(
            dimension_semantics=("arbitrary", "arbitrary"), vmem_limit_bytes=VMEM_LIMIT),
    )(pos, h, pre_g, w, gb, invf)


def _cmp_kernel(x_ref, pos_ref, w1_ref, w2_ref, o_ref):
    ncp = o_ref.shape[0]
    hidden_w = w1_ref.shape[-1]
    acc_a = jnp.zeros((ncp, hidden_w), F32)
    acc_b = jnp.zeros((ncp, hidden_w), F32)
    for l in range(CMP_STRIDE):
        xl = x_ref[pl.ds(l, ncp, stride=CMP_STRIDE), :]
        acc_a += _dot((xl + pos_ref[l:l + 1, :]).astype(BF16), w1_ref[l])
        acc_b += _dot((xl + pos_ref[CMP_STRIDE + l:CMP_STRIDE + l + 1, :]).astype(BF16), w1_ref[CMP_STRIDE + l])
    hidden = acc_a + pltpu.roll(acc_b, ncp - 1, axis=0)
    o_ref[...] = _dot(jax.nn.gelu(hidden).astype(BF16), w2_ref[...])


def _compress(zb, pos_kv, w1, w2, batch, seq):
    ncp = seq // CMP_STRIDE
    hidden_w = w1.shape[-1]
    return pl.pallas_call(
        _cmp_kernel,
        name="compress",
        grid=(2, batch, NSA_KV_HEADS),
        in_specs=[
            pl.BlockSpec((seq, LANE), lambda k, b, g: (b, BT_KC + 2 * k + g)),
            pl.BlockSpec((None, CMP_LEN, HEAD_DIM), lambda k, b, g: (k, 0, 0)),
            pl.BlockSpec((None, CMP_LEN, HEAD_DIM, hidden_w), lambda k, b, g: (k, 0, 0, 0)),
            pl.BlockSpec((None, hidden_w, HEAD_DIM), lambda k, b, g: (k, 0, 0)),
        ],
        out_specs=pl.BlockSpec((None, None, None, ncp, HEAD_DIM), lambda k, b, g: (k, b, g, 0, 0)),
        out_shape=jax.ShapeDtypeStruct((2, batch, NSA_KV_HEADS, ncp, HEAD_DIM), F32),
        compiler_params=pltpu.CompilerParams(
            dimension_semantics=("arbitrary",) * 3, vmem_limit_bytes=VMEM_LIMIT),
    )(zb, pos_kv, w1, w2)


def _masked_exp(s, mask):
    s = s + jnp.where(mask, 0.0, -MASK_BIG)
    e = jnp.exp2(s - jnp.max(s, axis=-1, keepdims=True))
    return e, jnp.sum(e, axis=-1, keepdims=True)


def _attn_kernel(q_ref, kc_ref, vc_ref, ks_ref, vs_ref, kw_ref, vw_ref, gz_ref, ovl_ref, oh_ref, ga_ref, prev_ref,
                 o_ref, *, first_block):
    del prev_ref
    G, R, tq = NSA_KV_HEADS, GQA_GROUP, Q_BLOCK
    n_keys = ks_ref.shape[0]
    top_k = min(SEL_TOPK, ovl_ref.shape[0])
    n_blk = n_keys // SEL_LEN
    n_cmp = min(kc_ref.shape[1], -(-(n_keys // CMP_STRIDE) // LANE) * LANE)
    t0 = (first_block + pl.program_id(1)) * tq
    t_idx = t0 + lax.broadcasted_iota(jnp.int32, (1, tq, 1), 1)
    gsl = lambda g: slice(g * HEAD_DIM, (g + 1) * HEAD_DIM)

    def stacked_q(g):
        q = q_ref[:, g * R * HEAD_DIM:(g + 1) * R * HEAD_DIM]
        return jnp.concatenate([q[:, r * HEAD_DIM:(r + 1) * HEAD_DIM] for r in range(R)], axis=0)

    q4s = [stacked_q(g) for g in range(G)]

    wk = min(WINDOW + tq, n_keys)
    w0 = pl.multiple_of(jnp.maximum(t0 - WINDOW, 0), tq)
    kp = w0 + lax.broadcasted_iota(jnp.int32, (1, 1, wk), 2)
    win_mask = (kp <= t_idx) & (kp > t_idx - WINDOW)

    ones_col = jnp.ones((n_keys, LANE), BF16)

    def weights(s, m):
        return jnp.exp2((s - m).astype(BF16)).reshape(R * tq, s.shape[-1])

    def pv(p, v, ones):
        return _dot(p, jnp.concatenate([v, ones], axis=1))

    def normalised(acc):
        return (acc[:, :HEAD_DIM] * (1.0 / acc[:, HEAD_DIM:])).reshape(R, tq, HEAD_DIM)

    def window(g):
        s = _dot_nt(q4s[g], kw_ref[pl.ds(w0, wk), gsl(g)]).reshape(R, tq, wk)
        s = s + jnp.where(win_mask, 0.0, -MASK_BIG)
        p = weights(s, jnp.max(s, axis=-1, keepdims=True))
        return normalised(pv(p, vw_ref[pl.ds(w0, wk), gsl(g)], ones_col[0:wk, :]))

    def front(g):
        q4 = q4s[g]

        s = _dot_nt(q4, kc_ref[g, 0:n_cmp, :].astype(BF16)).reshape(R, tq, n_cmp)
        c_idx = lax.broadcasted_iota(jnp.int32, (1, 1, n_cmp), 2)
        e, l = _masked_exp(s, (c_idx * CMP_STRIDE + (CMP_LEN - 1)) <= t_idx)
        pc = e * (jnp.where(t_idx >= CMP_LEN - 1, 1.0, 0.0) / l)
        o_cmp = _dot(pc.reshape(R * tq, n_cmp).astype(BF16), vc_ref[g, 0:n_cmp, :].astype(BF16))
        o_cmp = o_cmp.reshape(R, tq, HEAD_DIM)
        if n_blk <= top_k:
            return q4, jnp.concatenate([q4, jnp.zeros_like(q4)], axis=1), o_cmp

        pcs = pc[0]
        for r in range(1, R):
            pcs = pcs + pc[r]
        hi = pcs.astype(BF16)
        lo = (pcs - hi.astype(F32)).astype(BF16)
        ovl = ovl_ref[0:n_blk, 0:n_cmp]
        imp = _dot_nt(ovl, hi) + _dot_nt(ovl, lo)
        blk = lax.broadcasted_iota(jnp.int32, (n_blk, tq), 0)
        tcol = t0 + lax.broadcasted_iota(jnp.int32, (n_blk, tq), 1)
        forced = (blk == lax.shift_right_logical(tcol, int(np.log2(SEL_LEN)))) | (blk == 0)
        v = jnp.where(forced, SEL_FORCE, jnp.where(blk * SEL_LEN <= tcol, imp, -SEL_FORCE))
        sub = lax.broadcasted_iota(jnp.int32, (SUBLANE, tq), 0)
        groups = [v[k:k + SUBLANE, :] for k in range(0, n_blk, SUBLANE)]
        ranks = [jnp.zeros((SUBLANE, tq), F32) for _ in groups]
        for sp in range(n_blk):
            other = v[sp:sp + 1, :]
            for gi, vg in enumerate(groups):
                first = gi * SUBLANE
                if first > sp:
                    beats = other >= vg
                elif first + SUBLANE - 1 <= sp:
                    beats = other > vg
                else:
                    beats = (other > vg) | ((other == vg) & (sub > sp - first))
                ranks[gi] = ranks[gi] + jnp.where(beats, 1.0, 0.0)
        rank = jnp.concatenate(ranks, axis=0)
        unsel = jnp.where(rank < float(top_k), 0.0, -1.0)
        unsel = jnp.concatenate([unsel, jnp.zeros((LANE - n_blk, tq), F32)], axis=0).T.astype(BF16)
        q_aug = jnp.concatenate([q4, jnp.concatenate([unsel] * R, axis=0)], axis=1)
        return q4, q_aug, o_cmp

    fronts = [front(g) for g in range(G)]

    n_full = n_keys - SEL_KT
    kp = n_full + lax.broadcasted_iota(jnp.int32, (1, 1, SEL_KT), 2)
    causal_bias = jnp.where(kp <= t_idx, 0.0, -MASK_BIG)

    def selected(g):
        k_aug = jnp.concatenate([ks_ref[:, gsl(g)], oh_ref[...]], axis=1)
        s = _dot_nt(fronts[g][1], k_aug).reshape(R, tq, n_keys)
        s_last = s[:, :, n_full:] + causal_bias
        m = jnp.max(s_last, axis=-1, keepdims=True)
        if n_full:
            s_full = s[:, :, :n_full]
            m = jnp.maximum(m, jnp.max(s_full, axis=-1, keepdims=True))
        acc = pv(weights(s_last, m), vs_ref[n_full:n_keys, gsl(g)], ones_col[n_full:n_keys, :])
        if n_full:
            acc = acc + pv(weights(s_full, m), vs_ref[0:n_full, gsl(g)], ones_col[0:n_full, :])
        return normalised(acc)

    o_sels = [selected(g) for g in range(G)]
    o_wins = [window(g) for g in range(G)]
    heads = []
    for g in range(G):
        o_cmp, o_sel, o_win = fronts[g][2], o_sels[g], o_wins[g]
        gz = gz_ref[:, gsl(g)]
        for r in range(R):
            c = r * N_GATES
            heads.append(gz[:, c:c + 1] * o_cmp[r] + gz[:, c + 1:c + 2] * o_sel[r] + gz[:, c + 2:c + 3] * o_win[r])
    o_ref[...] = _rms(jnp.concatenate(heads, axis=1), ga_ref[...]).astype(o_ref.dtype)


def _attention(za, zb, kcv, ovl_t, onehot, g_attn, batch, seq):
    ncp = seq // CMP_STRIDE
    n_sel = seq // SEL_LEN
    G = NSA_KV_HEADS
    blocks = SEL_KT // Q_BLOCK
    za3 = za.reshape(batch, seq, A_WIDTH)
    zb3 = zb.reshape(batch, seq, B_WIDTH)
    out = jnp.zeros((batch, seq, NSA_WIDTH), BF16)
    for c in range(seq // SEL_KT):
        n_keys = (c + 1) * SEL_KT
        q_row = lambda b, i, c=c: (b, c * blocks + i, 0)
        kv_spec = lambda tile: pl.BlockSpec((None, n_keys, G * LANE), lambda b, i: (b, 0, tile // G))
        operands = (za3, kcv, kcv, za3, za3, za3, za3, zb3, ovl_t, onehot, g_attn, out)
        out = pl.pallas_call(
            functools.partial(_attn_kernel, first_block=c * blocks),
            name=f"attention_{c}",
            grid=(batch, blocks),
            in_specs=[
                pl.BlockSpec((None, Q_BLOCK, NSA_WIDTH), q_row),
                pl.BlockSpec((None, None, G, ncp, HEAD_DIM), lambda b, i: (0, b, 0, 0, 0)),
                pl.BlockSpec((None, None, G, ncp, HEAD_DIM), lambda b, i: (1, b, 0, 0, 0)),
                kv_spec(AT_KS), kv_spec(AT_VS), kv_spec(AT_KW), kv_spec(AT_VW),
                pl.BlockSpec((None, Q_BLOCK, G * LANE), lambda b, i, c=c: (b, c * blocks + i, BT_GATE // G)),
                pl.BlockSpec((n_sel, ncp), lambda b, i: (0, 0)),
                pl.BlockSpec((n_keys, LANE), lambda b, i: (0, 0)),
                pl.BlockSpec((1, NSA_WIDTH), lambda b, i: (0, 0)),
                pl.BlockSpec(memory_space=pl.ANY),
            ],
            out_specs=pl.BlockSpec((None, Q_BLOCK, NSA_WIDTH), q_row),
            out_shape=jax.ShapeDtypeStruct((batch, seq, NSA_WIDTH), BF16),
            input_output_aliases={len(operands) - 1: 0},
            compiler_params=pltpu.CompilerParams(
                dimension_semantics=("arbitrary",) * 2, vmem_limit_bytes=VMEM_LIMIT),
        )(*operands)
    return out.reshape(batch * seq, NSA_WIDTH)


def _lru_kernel(zx_ref, zy_ref, cw_ref, cb_ref, wa_ref, ba_ref, wi_ref, bi_ref, lam_ref, gr_ref,
                o_ref, xpad, a_scr, u_scr, h_scr, carry):
    tt, width = zx_ref.shape
    ti = pl.program_id(1)

    @pl.when(ti == 0)
    def _():
        xpad[0:SUBLANE, :] = jnp.zeros((SUBLANE, width), F32)
        carry[...] = jnp.zeros_like(carry)

    xb = zx_ref[...]
    xpad[SUBLANE:SUBLANE + tt, :] = xb
    xc = cb_ref[...] + cw_ref[CONV_WIDTH - 1:CONV_WIDTH, :] * xb
    for k in range(1, CONV_WIDTH):
        xc = xc + cw_ref[CONV_WIDTH - 1 - k:CONV_WIDTH - k, :] * xpad[SUBLANE - k:SUBLANE - k + tt, :]
    xpad[0:SUBLANE, :] = xb[tt - SUBLANE:tt, :]

    sp = jax.nn.softplus(-lam_ref[...])
    row8 = lax.broadcasted_iota(jnp.int32, (1, SUBLANE, LRU_BLOCK_DIM), 1)
    for hb in range(LRU_BLOCKS):
        sl = slice(hb * LRU_BLOCK_DIM, (hb + 1) * LRU_BLOCK_DIM)
        xs = xc[:, sl]
        xs16 = xs.astype(BF16)
        r = jax.nn.sigmoid(_dot(xs16, wa_ref[hb]) + ba_ref[:, sl])
        ig = jax.nn.sigmoid(_dot(xs16, wi_ref[hb]) + bi_ref[:, sl])
        log_a = -LRU_C * r * sp[:, sl]
        a = jnp.exp(log_a)
        u = jnp.sqrt(-jnp.tanh(log_a) * (a * a + 1.0)) * (ig * xs)
        a = a.reshape(tt // SUBLANE, SUBLANE, LRU_BLOCK_DIM)
        u = u.reshape(tt // SUBLANE, SUBLANE, LRU_BLOCK_DIM)
        for s in (1, 2, 4):
            ok = row8 >= s
            a_sh = pltpu.roll(a, s, axis=1)
            u_sh = pltpu.roll(u, s, axis=1)
            u = jnp.where(ok, u + a * u_sh, u)
            a = jnp.where(ok, a * a_sh, a)
        a_scr[:, sl] = a.reshape(tt, LRU_BLOCK_DIM)
        u_scr[:, sl] = u.reshape(tt, LRU_BLOCK_DIM)

    def group(gi, c):
        r0 = pl.multiple_of(gi * SUBLANE, SUBLANE)
        h = u_scr[pl.ds(r0, SUBLANE), :] + a_scr[pl.ds(r0, SUBLANE), :] * c
        h_scr[pl.ds(r0, SUBLANE), :] = h
        return jnp.broadcast_to(h[SUBLANE - 1:SUBLANE, :], (SUBLANE, width))

    c = lax.fori_loop(0, tt // SUBLANE, group, carry[...], unroll=4)
    carry[...] = c
    o_ref[...] = _rms(h_scr[...] * jax.nn.gelu(zy_ref[...]), gr_ref[...]).astype(o_ref.dtype)


def _rglru(zb, conv_w, conv_b, w_a, b_a, w_i, b_i, lam, g_rec, batch, seq, tt=256):
    nt = seq // tt
    wt = LRU_WIDTH // LANE
    full = lambda a: pl.BlockSpec(a.shape, lambda b, t: (0,) * a.ndim)
    zspec = lambda tile: pl.BlockSpec((tt, LRU_WIDTH), lambda b, t: (b * nt + t, tile // wt))
    return pl.pallas_call(
        _lru_kernel,
        name="rglru",
        grid=(batch, nt),
        in_specs=[zspec(BT_X), zspec(BT_Y), full(conv_w), full(conv_b), full(w_a), full(b_a),
                  full(w_i), full(b_i), full(lam), full(g_rec)],
        out_specs=pl.BlockSpec((tt, LRU_WIDTH), lambda b, t: (b * nt + t, 0)),
        out_shape=jax.ShapeDtypeStruct((batch * seq, LRU_WIDTH), BF16),
        scratch_shapes=[pltpu.VMEM((tt + SUBLANE, LRU_WIDTH), F32), pltpu.VMEM((tt, LRU_WIDTH), F32),
                        pltpu.VMEM((tt, LRU_WIDTH), F32), pltpu.VMEM((tt, LRU_WIDTH), F32),
                        pltpu.VMEM((SUBLANE, LRU_WIDTH), F32)],
        compiler_params=pltpu.CompilerParams(
            dimension_semantics=("arbitrary", "arbitrary"), vmem_limit_bytes=VMEM_LIMIT),
    )(zb, zb, conv_w, conv_b, w_a, b_a, w_i, b_i, lam, g_rec)


def _resident_bf16(w_ref, w16_scr):
    @pl.when(pl.program_id(0) == 0)
    def _():
        w16_scr[...] = w_ref[...].astype(BF16)


def _resident(a):
    return pl.BlockSpec(a.shape, lambda i: (0,) * a.ndim, pipeline_mode=pl.Buffered(1))


def _out_kernel(oa_ref, or_ref, w_ref, post_ref, h_ref, o_ref, w16_scr):
    _resident_bf16(w_ref, w16_scr)
    for c in range(h_ref.shape[0] // ROW_CHUNK):
        rows = slice(c * ROW_CHUNK, (c + 1) * ROW_CHUNK)
        y = (_dot(oa_ref[rows, :], w16_scr[0:NSA_WIDTH, :])
             + _dot(or_ref[rows, :], w16_scr[NSA_WIDTH:NSA_WIDTH + LRU_WIDTH, :]))
        o_ref[rows, :] = h_ref[rows, :] + _rms(y, post_ref[...])


def _out_proj(o_attn, o_rec, w_out, post_g, h, tm=512):
    n, d = h.shape
    row = lambda i: (i, 0)
    return pl.pallas_call(
        _out_kernel,
        name="out_proj",
        grid=(n // tm,),
        in_specs=[pl.BlockSpec((tm, NSA_WIDTH), row), pl.BlockSpec((tm, LRU_WIDTH), row),
                  _resident(w_out), _resident(post_g), pl.BlockSpec((tm, d), row)],
        out_specs=pl.BlockSpec((tm, d), row),
        out_shape=jax.ShapeDtypeStruct((n, d), F32),
        scratch_shapes=[pltpu.VMEM(w_out.shape, BF16)],
        compiler_params=pltpu.CompilerParams(
            dimension_semantics=("arbitrary",), vmem_limit_bytes=VMEM_LIMIT),
    )(o_attn, o_rec, w_out, post_g, h)


def _ple_kernel(h_ref, p_ref, pre_ref, wg_ref, wp_ref, post_ref, o_ref, wg16_scr, wp16_scr):
    _resident_bf16(wg_ref, wg16_scr)
    _resident_bf16(wp_ref, wp16_scr)
    for c in range(h_ref.shape[0] // ROW_CHUNK):
        rows = slice(c * ROW_CHUNK, (c + 1) * ROW_CHUNK)
        h = h_ref[rows, :]
        gate = jax.nn.sigmoid(_dot(_rms(h, pre_ref[...]).astype(BF16), wg16_scr[...]))
        pp = _dot(p_ref[rows, :].astype(BF16), wp16_scr[...])
        o_ref[rows, :] = h + _rms(gate * pp, post_ref[...])


def _ple(h, p, pre_g, w_gate, w_proj, post_g, tm=512):
    n, d = h.shape
    row = lambda i: (i, 0)
    return pl.pallas_call(
        _ple_kernel,
        name="ple",
        grid=(n // tm,),
        in_specs=[pl.BlockSpec((tm, d), row), pl.BlockSpec((tm, p.shape[1]), row),
                  _resident(pre_g), _resident(w_gate), _resident(w_proj), _resident(post_g)],
        out_specs=pl.BlockSpec((tm, d), row),
        out_shape=jax.ShapeDtypeStruct((n, d), F32),
        scratch_shapes=[pltpu.VMEM(w_gate.shape, BF16), pltpu.VMEM(w_proj.shape, BF16)],
        compiler_params=pltpu.CompilerParams(
            dimension_semantics=("arbitrary",), vmem_limit_bytes=VMEM_LIMIT),
    )(h, p, pre_g, w_gate, w_proj, post_g)


def _selection_constants(seq):
    ncp = seq // CMP_STRIDE
    n_cmp = (seq - CMP_LEN) // CMP_STRIDE + 1
    n_sel = seq // SEL_LEN
    c = np.arange(ncp)[None, :]
    s = np.arange(n_sel)[:, None]
    ovl_t = ((c * CMP_STRIDE < s * SEL_LEN + SEL_LEN) & (c * CMP_STRIDE + CMP_LEN - 1 >= s * SEL_LEN) & (c < n_cmp))
    onehot = (np.arange(seq)[:, None] // SEL_LEN) == np.arange(LANE)[None, :]
    return jnp.asarray(ovl_t, BF16), jnp.asarray(onehot * MASK_BIG, BF16)


def _pad_cols(a, width):
    return jnp.pad(a, ((0, 0), (0, width - a.shape[1])))


def kernel(x, p, positions, ff1_pre_g, ff1_post_g, ff1_w_gate, ff1_w_up, ff1_w_down, mix_pre_g, mix_post_g, w_in, cmp_pos_k, cmp_pos_v, cmp_k_w1, cmp_k_w2, cmp_v_w1, cmp_v_w2, nsa_gate_b, conv_w, conv_b, rg_w_a, rg_b_a, rg_w_i, rg_b_i, rg_lambda, attn_out_g, rec_out_g, w_out, ff2_pre_g, ff2_post_g, ff2_w_gate, ff2_w_up, ff2_w_down, ple_pre_g, ple_post_g, w_ple_gate, w_ple_proj):
    batch, seq, d = x.shape
    depth = p.shape[0]
    n = batch * seq
    vec = lambda a: a.reshape(1, -1)
    gate_cols = NSA_HEADS * N_GATES
    grp_gates = GQA_GROUP * N_GATES
    o_q, o_kc, o_vc, o_ks, o_vs, o_kw, o_vw = (NSA_WIDTH * 0,) + tuple(NSA_WIDTH + k * KV_WIDTH for k in range(6))
    o_g = NSA_WIDTH + 6 * KV_WIDTH
    o_x = o_g + gate_cols
    o_y = o_x + LRU_WIDTH

    half = jnp.arange(ROPE_HALF, dtype=F32)
    inv_freq = ROPE_THETA ** (-half / ROPE_HALF)
    invf = jnp.tile(jnp.concatenate([inv_freq, inv_freq]), ROPE_PACK).reshape(1, LANE)
    ovl_t, onehot = _selection_constants(seq)
    sub = PROJ_CHUNK // ROPE_PACK
    pos = positions.reshape(n // PROJ_CHUNK, ROPE_PACK, sub).transpose(0, 2, 1).reshape(n // ROPE_PACK, ROPE_PACK)
    pos = jnp.repeat(pos, ROPE_DIM, axis=1)

    h = x.reshape(n, d)
    for i in range(depth):
        wi = w_in[i]
        cols = lambda o, w: wi[:, o:o + w]
        w_in_z = jnp.concatenate(
            [cols(o_q, NSA_WIDTH), cols(o_ks, KV_WIDTH), cols(o_kw, KV_WIDTH), cols(o_vs, KV_WIDTH),
             cols(o_vw, KV_WIDTH), cols(o_x, LRU_WIDTH), cols(o_y, LRU_WIDTH), cols(o_kc, KV_WIDTH),
             cols(o_vc, KV_WIDTH)]
            + [_pad_cols(cols(o_g + g * grp_gates, grp_gates), LANE) for g in range(NSA_KV_HEADS)]
            + [jnp.zeros((d, B_WIDTH - (BT_GATE + NSA_KV_HEADS) * LANE), F32)],
            axis=1).astype(BF16)
        gb = nsa_gate_b[i].reshape(NSA_KV_HEADS, grp_gates)
        gate_bias = _pad_cols(jnp.pad(gb, ((0, 0), (0, LANE - grp_gates))).reshape(1, NSA_KV_HEADS * LANE), PROJ_TN)

        h = _ffn(h, vec(ff1_pre_g[i]), ff1_w_gate[i], ff1_w_up[i], ff1_w_down[i], vec(ff1_post_g[i]))
        za, zb = _proj(pos, h, vec(mix_pre_g[i]), w_in_z, gate_bias, invf)
        kcv = _compress(
            zb, jnp.stack([cmp_pos_k[i], cmp_pos_v[i]]),
            jnp.stack([cmp_k_w1[i], cmp_v_w1[i]]).reshape(2, CMP_LEN, HEAD_DIM, -1).astype(BF16),
            jnp.stack([cmp_k_w2[i], cmp_v_w2[i]]).astype(BF16), batch, seq)
        o_attn = _attention(za, zb, kcv, ovl_t, onehot, vec(attn_out_g[i]), batch, seq)
        o_rec = _rglru(zb, conv_w[i], vec(conv_b[i]), rg_w_a[i].astype(BF16), vec(rg_b_a[i]),
                       rg_w_i[i].astype(BF16), vec(rg_b_i[i]), vec(rg_lambda[i]), vec(rec_out_g[i]), batch, seq)
        h = _out_proj(o_attn, o_rec, w_out[i], vec(mix_post_g[i]), h)
        h = _ffn(h, vec(ff2_pre_g[i]), ff2_w_gate[i], ff2_w_up[i], ff2_w_down[i], vec(ff2_post_g[i]))
        h = _ple(h, p[i].reshape(n, -1), vec(ple_pre_g[i]), w_ple_gate[i], w_ple_proj[i], vec(ple_post_g[i]))
    return h.reshape(batch, seq, d)
```

```python
import functools

import numpy as np
import jax
import jax.numpy as jnp
from jax import lax
from jax.experimental import pallas as pl
from jax.experimental.pallas import tpu as pltpu

F32 = jnp.float32
BF16 = jnp.bfloat16

D_MODEL = 2048
NSA_HEADS = 8
NSA_KV_HEADS = 2
GQA_GROUP = NSA_HEADS // NSA_KV_HEADS
HEAD_DIM = 128
NSA_WIDTH = NSA_HEADS * HEAD_DIM
KV_WIDTH = NSA_KV_HEADS * HEAD_DIM
ROPE_DIM = HEAD_DIM // 4
ROPE_HALF = ROPE_DIM // 2
ROPE_THETA = 500000.0
CMP_LEN = 32
CMP_STRIDE = 16
SEL_LEN = 64
SEL_TOPK = 16
WINDOW = 512
Q_BLOCK = 128
N_GATES = 3
LRU_WIDTH = 1024
LRU_BLOCKS = 8
LRU_BLOCK_DIM = LRU_WIDTH // LRU_BLOCKS
CONV_WIDTH = 4
LRU_C = 8.0
RMS_EPS = 1e-6
NEG = -1e30
SEL_FORCE = 1e4

LANE = 128
SUBLANE = 8
VMEM_LIMIT = 62 * 1024 * 1024

AT_Q, AT_KS, AT_VS, AT_KW, AT_VW = 0, 8, 10, 12, 14
A_WIDTH = 16 * LANE
BT_X, BT_Y, BT_KC, BT_VC, BT_GATE = 0, 8, 16, 18, 20
B_WIDTH = 24 * LANE
PROJ_TN = 512
PROJ_A_STEPS = A_WIDTH // PROJ_TN
PROJ_STEPS = (A_WIDTH + B_WIDTH) // PROJ_TN
PROJ_HEADS = PROJ_TN // LANE
ROW_CHUNK = 256
PROJ_CHUNK = ROW_CHUNK
ROPE_PACK = LANE // ROPE_DIM
FFN_CHUNK = ROW_CHUNK

Q_SCALE = HEAD_DIM ** -0.5 * 1.4426950408889634
SEL_KT = 512
MASK_BIG = 2.0 ** 100


def _rms(x, g):
    return x * lax.rsqrt(jnp.mean(x * x, axis=-1, keepdims=True) + RMS_EPS) * g


def _dot(a, b):
    return jnp.dot(a, b, preferred_element_type=F32)


def _dot_nt(a, b):
    return lax.dot_general(a, b, (((1,), (1,)), ((), ())), preferred_element_type=F32)


def _ffn_steps(j, n_steps, h_ref, pre_ref, post_ref, o_ref, u_scr, weights):
    def step(first, last):
        wg, wu, wd = weights()
        for c in range(h_ref.shape[0] // FFN_CHUNK):
            rows = slice(c * FFN_CHUNK, (c + 1) * FFN_CHUNK)
            if first:
                u = _rms(h_ref[rows, :], pre_ref[...]).astype(BF16)
                u_scr[rows, :] = u
            else:
                u = u_scr[rows, :]
            g = _dot(u, wg)
            up = _dot(u, wu)
            acc = _dot(((g * jax.nn.sigmoid(g)) * up).astype(BF16), wd)
            if not first:
                acc = o_ref[rows, :] + acc
            if last:
                acc = h_ref[rows, :] + 0.5 * _rms(acc, post_ref[...])
            o_ref[rows, :] = acc

    if n_steps == 1:
        step(True, True)
    else:
        pl.when(j == 0)(lambda: step(True, False))
        pl.when((j > 0) & (j < n_steps - 1))(lambda: step(False, False))
        pl.when(j == n_steps - 1)(lambda: step(False, True))


def _ffn_head_kernel(h_ref, pre_ref, wg_ref, wu_ref, wd_ref, post_ref, o_ref, wg16_ref, wu16_ref, wd16_ref,
                     u_scr, *, n_steps):
    def weights():
        w16 = []
        for src, dst in ((wg_ref, wg16_ref), (wu_ref, wu16_ref), (wd_ref, wd16_ref)):
            w = src[...].astype(BF16)
            dst[...] = w
            w16.append(w)
        return w16

    _ffn_steps(pl.program_id(0), n_steps, h_ref, pre_ref, post_ref, o_ref, u_scr, weights)


def _ffn_tail_kernel(h_ref, head_ref, pre_ref, wg_ref, wu_ref, wd_ref, post_ref, o_ref, u_scr, *,
                     n_steps, n_copy):
    i, j = pl.program_id(0), pl.program_id(1)
    slab = head_ref.shape[0]

    @pl.when((i == 0) & (j < n_copy))
    def _():
        o_ref[pl.ds(pl.multiple_of(j * slab, slab), slab), :] = head_ref[...]

    @pl.when(i > 0)
    def _():
        _ffn_steps(j, n_steps, h_ref, pre_ref, post_ref, o_ref, u_scr,
                   lambda: (wg_ref[...], wu_ref[...], wd_ref[...]))


def _ffn(h, pre_g, wg, wu, wd, post_g, tm=1024, tf=512, tf_head=256):
    n, d = h.shape
    dff = wg.shape[1]
    tf, tf_head = min(tf, dff), min(tf_head, dff)
    vec1 = pl.BlockSpec((1, d), lambda j: (0, 0))
    head, wg16, wu16, wd16 = pl.pallas_call(
        functools.partial(_ffn_head_kernel, n_steps=dff // tf_head),
        name="ffn_head",
        grid=(dff // tf_head,),
        in_specs=[
            pl.BlockSpec((tm, d), lambda j: (0, 0)), vec1,
            pl.BlockSpec((d, tf_head), lambda j: (0, j)),
            pl.BlockSpec((d, tf_head), lambda j: (0, j)),
            pl.BlockSpec((tf_head, d), lambda j: (j, 0)),
            vec1,
        ],
        out_specs=[pl.BlockSpec((tm, d), lambda j: (0, 0)),
                   pl.BlockSpec((d, tf_head), lambda j: (0, j)),
                   pl.BlockSpec((d, tf_head), lambda j: (0, j)),
                   pl.BlockSpec((tf_head, d), lambda j: (j, 0))],
        out_shape=[jax.ShapeDtypeStruct((tm, d), F32), jax.ShapeDtypeStruct((d, dff), BF16),
                   jax.ShapeDtypeStruct((d, dff), BF16), jax.ShapeDtypeStruct((dff, d), BF16)],
        scratch_shapes=[pltpu.VMEM((tm, d), BF16)],
        compiler_params=pltpu.CompilerParams(
            dimension_semantics=("arbitrary",), vmem_limit_bytes=VMEM_LIMIT),
    )(h, pre_g, wg, wu, wd, post_g)

    n_steps = dff // tf
    n_copy = 1 << (min(n_steps, tm // LANE).bit_length() - 1)
    row = lambda i, j: (i, 0)
    vec = pl.BlockSpec((1, d), lambda i, j: (0, 0))
    wcol = lambda i, j: (0, jnp.where(i > 0, j, 0))
    return pl.pallas_call(
        functools.partial(_ffn_tail_kernel, n_steps=n_steps, n_copy=n_copy),
        name="ffn_tail",
        grid=(n // tm, n_steps),
        in_specs=[
            pl.BlockSpec((tm, d), row),
            pl.BlockSpec((tm // n_copy, d), lambda i, j: (jnp.where(i == 0, jnp.minimum(j, n_copy - 1), n_copy - 1), 0)),
            vec,
            pl.BlockSpec((d, tf), wcol),
            pl.BlockSpec((d, tf), wcol),
            pl.BlockSpec((tf, d), lambda i, j: (jnp.where(i > 0, j, 0), 0)),
            vec,
        ],
        out_specs=pl.BlockSpec((tm, d), row),
        out_shape=jax.ShapeDtypeStruct((n, d), F32),
        scratch_shapes=[pltpu.VMEM((tm, d), BF16)],
        compiler_params=pltpu.CompilerParams(
            dimension_semantics=("arbitrary", "arbitrary"), vmem_limit_bytes=VMEM_LIMIT),
    )(h, head, pre_g, wg16, wu16, wd16, post_g)


def _proj_kernel(pos_ref, h_ref, pre_ref, win_ref, wrest_ref, gb_ref, invf_ref, za_ref, zb_ref,
                 u_scr, cos_scr, s1_scr, s2_scr):
    j = pl.program_id(1)

    def prepare(rows):
        u_scr[rows, :] = _rms(h_ref[rows, :], pre_ref[...]).astype(BF16)
        sub = PROJ_CHUNK // ROPE_PACK
        packed = slice(rows.start // ROPE_PACK, rows.start // ROPE_PACK + sub)
        ang = pos_ref[packed, :].astype(F32) * invf_ref[...]
        cos, sin = jnp.cos(ang), jnp.sin(ang)
        lane = lax.broadcasted_iota(jnp.int32, ang.shape, 1)
        for k in range(ROPE_PACK):
            grp = slice(rows.start + k * sub, rows.start + (k + 1) * sub)
            c, s = (cos, sin) if k == 0 else (pltpu.roll(cos, LANE - k * ROPE_DIM, axis=1),
                                              pltpu.roll(sin, LANE - k * ROPE_DIM, axis=1))
            cos_scr[grp, :] = jnp.where(lane < ROPE_DIM, c, 1.0)
            s1_scr[grp, :] = jnp.where((lane >= ROPE_HALF) & (lane < ROPE_DIM), s, 0.0)
            s2_scr[grp, :] = jnp.where(lane < ROPE_HALF, -s, 0.0)

    def head(z, rows, hd, rope, mul=None):
        x = z[:, hd * LANE:(hd + 1) * LANE]
        if rope:
            x = (x * cos_scr[rows, :]
                 + pltpu.roll(x, ROPE_HALF, axis=1) * s1_scr[rows, :]
                 + pltpu.roll(x, LANE - ROPE_HALF, axis=1) * s2_scr[rows, :])
        return x if mul is None else x * mul

    def chunks(w_ref, first=False):
        for c in range(u_scr.shape[0] // PROJ_CHUNK):
            rows = slice(c * PROJ_CHUNK, (c + 1) * PROJ_CHUNK)
            if first:
                prepare(rows)
            yield rows, _dot(u_scr[rows, :], w_ref[...])

    def store(ref, rope_heads, w_ref, mul=None, first=False):
        for rows, z in chunks(w_ref, first):
            for hd in range(PROJ_HEADS):
                ref[rows, hd * LANE:(hd + 1) * LANE] = head(z, rows, hd, hd < rope_heads, mul).astype(ref.dtype)


    @pl.when(j == 0)
    def _():
        store(za_ref, PROJ_HEADS, win_ref, Q_SCALE, first=True)

    @pl.when((j > 0) & (j < AT_KS // PROJ_HEADS))
    def _():
        store(za_ref, PROJ_HEADS, win_ref, Q_SCALE)

    @pl.when((j == AT_KS // PROJ_HEADS) | (j == AT_KW // PROJ_HEADS))
    def _():
        store(za_ref, NSA_KV_HEADS, win_ref)

    @pl.when((j >= PROJ_A_STEPS) & (j < PROJ_A_STEPS + BT_KC // PROJ_HEADS))
    def _():
        store(zb_ref, 0, wrest_ref)

    @pl.when(j == PROJ_A_STEPS + BT_KC // PROJ_HEADS)
    def _():
        store(zb_ref, NSA_KV_HEADS, win_ref)

    @pl.when(j == PROJ_A_STEPS + BT_GATE // PROJ_HEADS)
    def _():
        for rows, z in chunks(wrest_ref):
            zb_ref[rows, :] = jax.nn.sigmoid(z + gb_ref[...])


def _proj_w_in_block(j):
    q_steps, sel_step, cmp_step = AT_KS // PROJ_HEADS, AT_KS // PROJ_HEADS, PROJ_A_STEPS + BT_KC // PROJ_HEADS
    return jnp.where(j < q_steps, j,
                     jnp.where(j == sel_step, q_steps + 1,
                               jnp.where(j < cmp_step, q_steps + 2, q_steps)))


def _proj_w_rest_block(j):
    n_lru = BT_KC // PROJ_HEADS
    return jnp.where(j == PROJ_STEPS - 1, n_lru, jnp.clip(j - PROJ_A_STEPS, 0, n_lru - 1))


def _proj(pos, h, pre_g, w_in, w_rest, gb, invf, tm=1024):
    n, d = h.shape
    return pl.pallas_call(
        _proj_kernel,
        name="mix_proj",
        grid=(n // tm, PROJ_STEPS),
        in_specs=[
            pl.BlockSpec((tm // ROPE_PACK, LANE), lambda i, j: (i, 0)),
            pl.BlockSpec((tm, d), lambda i, j: (i, 0)),
            pl.BlockSpec((1, d), lambda i, j: (0, 0)),
            pl.BlockSpec((d, PROJ_TN), lambda i, j: (0, _proj_w_in_block(j))),
            pl.BlockSpec((d, PROJ_TN), lambda i, j: (0, _proj_w_rest_block(j))),
            pl.BlockSpec((1, PROJ_TN), lambda i, j: (0, 0)),
            pl.BlockSpec((1, LANE), lambda i, j: (0, 0)),
        ],
        out_specs=[pl.BlockSpec((tm, PROJ_TN), lambda i, j: (i, jnp.minimum(j, PROJ_A_STEPS - 1))),
                   pl.BlockSpec((tm, PROJ_TN), lambda i, j: (i, jnp.maximum(j - PROJ_A_STEPS, 0)))],
        out_shape=[jax.ShapeDtypeStruct((n, A_WIDTH), BF16), jax.ShapeDtypeStruct((n, B_WIDTH), F32)],
        scratch_shapes=[pltpu.VMEM((tm, d), BF16)] + [pltpu.VMEM((tm, LANE), F32)] * 3,
        compiler_params=pltpu.CompilerParams(
            dimension_semantics=("arbitrary", "arbitrary"), vmem_limit_bytes=VMEM_LIMIT),
    )(pos, h, pre_g, w_in, w_rest, gb, invf)


def _cmp_kernel(x_ref, pos_ref, w1_ref, w2_ref, o_ref):
    ncp = o_ref.shape[0]
    hidden_w = w1_ref.shape[-1]
    acc_a = jnp.zeros((ncp, hidden_w), F32)
    acc_b = jnp.zeros((ncp, hidden_w), F32)
    for l in range(CMP_STRIDE):
        xl = x_ref[pl.ds(l, ncp, stride=CMP_STRIDE), :]
        acc_a += _dot((xl + pos_ref[l:l + 1, :]).astype(BF16), w1_ref[l])
        acc_b += _dot((xl + pos_ref[CMP_STRIDE + l:CMP_STRIDE + l + 1, :]).astype(BF16), w1_ref[CMP_STRIDE + l])
    hidden = acc_a + pltpu.roll(acc_b, ncp - 1, axis=0)
    o_ref[...] = _dot(jax.nn.gelu(hidden).astype(BF16), w2_ref[...])


def _compress(zb, pos_kv, w1, w2, batch, seq):
    ncp = seq // CMP_STRIDE
    hidden_w = w1.shape[-1]
    return pl.pallas_call(
        _cmp_kernel,
        name="compress",
        grid=(2, batch, NSA_KV_HEADS),
        in_specs=[
            pl.BlockSpec((seq, LANE), lambda k, b, g: (b, BT_KC + 2 * k + g)),
            pl.BlockSpec((None, CMP_LEN, HEAD_DIM), lambda k, b, g: (k, 0, 0)),
            pl.BlockSpec((None, CMP_LEN, HEAD_DIM, hidden_w), lambda k, b, g: (k, 0, 0, 0)),
            pl.BlockSpec((None, hidden_w, HEAD_DIM), lambda k, b, g: (k, 0, 0)),
        ],
        out_specs=pl.BlockSpec((None, None, None, ncp, HEAD_DIM), lambda k, b, g: (k, b, g, 0, 0)),
        out_shape=jax.ShapeDtypeStruct((2, batch, NSA_KV_HEADS, ncp, HEAD_DIM), F32),
        compiler_params=pltpu.CompilerParams(
            dimension_semantics=("arbitrary",) * 3, vmem_limit_bytes=VMEM_LIMIT),
    )(zb, pos_kv, w1, w2)


def _masked_exp(s, mask):
    s = s + jnp.where(mask, 0.0, -MASK_BIG)
    e = jnp.exp2(s - jnp.max(s, axis=-1, keepdims=True))
    return e, jnp.sum(e, axis=-1, keepdims=True)


def _attn_kernel(q_ref, kc_ref, vc_ref, ks_ref, vs_ref, kw_ref, vw_ref, gz_ref, ovl_ref, oh_ref, ga_ref, prev_ref,
                 o_ref, *, first_block):
    del prev_ref
    G, R, tq = NSA_KV_HEADS, GQA_GROUP, Q_BLOCK
    n_keys = ks_ref.shape[0]
    top_k = min(SEL_TOPK, ovl_ref.shape[0])
    n_blk = n_keys // SEL_LEN
    n_cmp = min(kc_ref.shape[1], -(-(n_keys // CMP_STRIDE) // LANE) * LANE)
    t0 = (first_block + pl.program_id(1)) * tq
    t_idx = t0 + lax.broadcasted_iota(jnp.int32, (1, tq, 1), 1)
    gsl = lambda g: slice(g * HEAD_DIM, (g + 1) * HEAD_DIM)

    def stacked_q(g):
        q = q_ref[:, g * R * HEAD_DIM:(g + 1) * R * HEAD_DIM]
        return jnp.concatenate([q[:, r * HEAD_DIM:(r + 1) * HEAD_DIM] for r in range(R)], axis=0)

    q4s = [stacked_q(g) for g in range(G)]

    wk = min(WINDOW + tq, n_keys)
    w0 = pl.multiple_of(jnp.maximum(t0 - WINDOW, 0), tq)
    kp = w0 + lax.broadcasted_iota(jnp.int32, (1, 1, wk), 2)
    win_mask = (kp <= t_idx) & (kp > t_idx - WINDOW)

    ones_col = jnp.ones((n_keys, LANE), BF16)

    def weights(s, m):
        return jnp.exp2((s - m).astype(BF16)).reshape(R * tq, s.shape[-1])

    def pv(p, v, ones):
        return _dot(p, jnp.concatenate([v, ones], axis=1))

    def normalised(acc):
        return (acc[:, :HEAD_DIM] * (1.0 / acc[:, HEAD_DIM:])).reshape(R, tq, HEAD_DIM)

    def window(g):
        s = _dot_nt(q4s[g], kw_ref[pl.ds(w0, wk), gsl(g)]).reshape(R, tq, wk)
        s = s + jnp.where(win_mask, 0.0, -MASK_BIG)
        p = weights(s, jnp.max(s, axis=-1, keepdims=True))
        return normalised(pv(p, vw_ref[pl.ds(w0, wk), gsl(g)], ones_col[0:wk, :]))

    def front(g):
        q4 = q4s[g]

        s = _dot_nt(q4, kc_ref[g, 0:n_cmp, :].astype(BF16)).reshape(R, tq, n_cmp)
        c_idx = lax.broadcasted_iota(jnp.int32, (1, 1, n_cmp), 2)
        e, l = _masked_exp(s, (c_idx * CMP_STRIDE + (CMP_LEN - 1)) <= t_idx)
        pc = e * (jnp.where(t_idx >= CMP_LEN - 1, 1.0, 0.0) / l)
        o_cmp = _dot(pc.reshape(R * tq, n_cmp).astype(BF16), vc_ref[g, 0:n_cmp, :].astype(BF16))
        o_cmp = o_cmp.reshape(R, tq, HEAD_DIM)
        if n_blk <= top_k:
            return q4, jnp.concatenate([q4, jnp.zeros_like(q4)], axis=1), o_cmp

        pcs = pc[0]
        for r in range(1, R):
            pcs = pcs + pc[r]
        hi = pcs.astype(BF16)
        lo = (pcs - hi.astype(F32)).astype(BF16)
        ovl = ovl_ref[0:n_blk, 0:n_cmp]
        imp = _dot_nt(ovl, hi) + _dot_nt(ovl, lo)
        blk = lax.broadcasted_iota(jnp.int32, (n_blk, tq), 0)
        tcol = t0 + lax.broadcasted_iota(jnp.int32, (n_blk, tq), 1)
        forced = (blk == lax.shift_right_logical(tcol, int(np.log2(SEL_LEN)))) | (blk == 0)
        v = jnp.where(forced, SEL_FORCE, jnp.where(blk * SEL_LEN <= tcol, imp, -SEL_FORCE))
        sub = lax.broadcasted_iota(jnp.int32, (SUBLANE, tq), 0)
        groups = [v[k:k + SUBLANE, :] for k in range(0, n_blk, SUBLANE)]
        ranks = [jnp.zeros((SUBLANE, tq), F32) for _ in groups]
        for sp in range(n_blk):
            other = v[sp:sp + 1, :]
            for gi, vg in enumerate(groups):
                first = gi * SUBLANE
                if first > sp:
                    beats = other >= vg
                elif first + SUBLANE - 1 <= sp:
                    beats = other > vg
                else:
                    beats = (other > vg) | ((other == vg) & (sub > sp - first))
                ranks[gi] = ranks[gi] + jnp.where(beats, 1.0, 0.0)
        rank = jnp.concatenate(ranks, axis=0)
        unsel = jnp.where(rank < float(top_k), 0.0, -1.0)
        unsel = jnp.concatenate([unsel, jnp.zeros((LANE - n_blk, tq), F32)], axis=0).T.astype(BF16)
        q_aug = jnp.concatenate([q4, jnp.concatenate([unsel] * R, axis=0)], axis=1)
        return q4, q_aug, o_cmp

    fronts = [front(g) for g in range(G)]

    n_full = n_keys - SEL_KT
    kp = n_full + lax.broadcasted_iota(jnp.int32, (1, 1, SEL_KT), 2)
    causal_bias = jnp.where(kp <= t_idx, 0.0, -MASK_BIG)

    def selected(g):
        k_aug = jnp.concatenate([ks_ref[:, gsl(g)], oh_ref[...]], axis=1)
        s = _dot_nt(fronts[g][1], k_aug).reshape(R, tq, n_keys)
        s_last = s[:, :, n_full:] + causal_bias
        m = jnp.max(s_last, axis=-1, keepdims=True)
        if n_full:
            s_full = s[:, :, :n_full]
            m = jnp.maximum(m, jnp.max(s_full, axis=-1, keepdims=True))
        acc = pv(weights(s_last, m), vs_ref[n_full:n_keys, gsl(g)], ones_col[n_full:n_keys, :])
        if n_full:
            acc = acc + pv(weights(s_full, m), vs_ref[0:n_full, gsl(g)], ones_col[0:n_full, :])
        return normalised(acc)

    o_sels = [selected(g) for g in range(G)]
    o_wins = [window(g) for g in range(G)]
    heads = []
    for g in range(G):
        o_cmp, o_sel, o_win = fronts[g][2], o_sels[g], o_wins[g]
        gz = gz_ref[:, gsl(g)]
        for r in range(R):
            c = r * N_GATES
            heads.append(gz[:, c:c + 1] * o_cmp[r] + gz[:, c + 1:c + 2] * o_sel[r] + gz[:, c + 2:c + 3] * o_win[r])
    o_ref[...] = _rms(jnp.concatenate(heads, axis=1), ga_ref[...]).astype(o_ref.dtype)


def _attention(za, zb, kcv, ovl_t, onehot, g_attn, batch, seq):
    ncp = seq // CMP_STRIDE
    n_sel = seq // SEL_LEN
    G = NSA_KV_HEADS
    blocks = SEL_KT // Q_BLOCK
    za3 = za.reshape(batch, seq, A_WIDTH)
    zb3 = zb.reshape(batch, seq, B_WIDTH)
    out = jnp.zeros((batch, seq, NSA_WIDTH), BF16)
    for c in range(seq // SEL_KT):
        n_keys = (c + 1) * SEL_KT
        q_row = lambda b, i, c=c: (b, c * blocks + i, 0)
        kv_spec = lambda tile: pl.BlockSpec((None, n_keys, G * LANE), lambda b, i: (b, 0, tile // G))
        operands = (za3, kcv, kcv, za3, za3, za3, za3, zb3, ovl_t, onehot, g_attn, out)
        out = pl.pallas_call(
            functools.partial(_attn_kernel, first_block=c * blocks),
            name=f"attention_{c}",
            grid=(batch, blocks),
            in_specs=[
                pl.BlockSpec((None, Q_BLOCK, NSA_WIDTH), q_row),
                pl.BlockSpec((None, None, G, ncp, HEAD_DIM), lambda b, i: (0, b, 0, 0, 0)),
                pl.BlockSpec((None, None, G, ncp, HEAD_DIM), lambda b, i: (1, b, 0, 0, 0)),
                kv_spec(AT_KS), kv_spec(AT_VS), kv_spec(AT_KW), kv_spec(AT_VW),
                pl.BlockSpec((None, Q_BLOCK, G * LANE), lambda b, i, c=c: (b, c * blocks + i, BT_GATE // G)),
                pl.BlockSpec((n_sel, ncp), lambda b, i: (0, 0)),
                pl.BlockSpec((n_keys, LANE), lambda b, i: (0, 0)),
                pl.BlockSpec((1, NSA_WIDTH), lambda b, i: (0, 0)),
                pl.BlockSpec(memory_space=pl.ANY),
            ],
            out_specs=pl.BlockSpec((None, Q_BLOCK, NSA_WIDTH), q_row),
            out_shape=jax.ShapeDtypeStruct((batch, seq, NSA_WIDTH), BF16),
            input_output_aliases={len(operands) - 1: 0},
            compiler_params=pltpu.CompilerParams(
                dimension_semantics=("arbitrary",) * 2, vmem_limit_bytes=VMEM_LIMIT),
        )(*operands)
    return out.reshape(batch * seq, NSA_WIDTH)


def _lru_kernel(zx_ref, zy_ref, cw_ref, cb_ref, wa_ref, ba_ref, wi_ref, bi_ref, lam_ref, gr_ref,
                o_ref, xpad, a_scr, u_scr, h_scr, carry):
    tt, width = zx_ref.shape
    ti = pl.program_id(1)

    @pl.when(ti == 0)
    def _():
        xpad[0:SUBLANE, :] = jnp.zeros((SUBLANE, width), F32)
        carry[...] = jnp.zeros_like(carry)

    xb = zx_ref[...]
    xpad[SUBLANE:SUBLANE + tt, :] = xb
    xc = cb_ref[...] + cw_ref[CONV_WIDTH - 1:CONV_WIDTH, :] * xb
    for k in range(1, CONV_WIDTH):
        xc = xc + cw_ref[CONV_WIDTH - 1 - k:CONV_WIDTH - k, :] * xpad[SUBLANE - k:SUBLANE - k + tt, :]
    xpad[0:SUBLANE, :] = xb[tt - SUBLANE:tt, :]

    sp = jax.nn.softplus(-lam_ref[...])
    row8 = lax.broadcasted_iota(jnp.int32, (1, SUBLANE, LRU_BLOCK_DIM), 1)
    for hb in range(LRU_BLOCKS):
        sl = slice(hb * LRU_BLOCK_DIM, (hb + 1) * LRU_BLOCK_DIM)
        xs = xc[:, sl]
        xs16 = xs.astype(BF16)
        r = jax.nn.sigmoid(_dot(xs16, wa_ref[hb]) + ba_ref[:, sl])
        ig = jax.nn.sigmoid(_dot(xs16, wi_ref[hb]) + bi_ref[:, sl])
        log_a = -LRU_C * r * sp[:, sl]
        a = jnp.exp(log_a)
        u = jnp.sqrt(-jnp.tanh(log_a) * (a * a + 1.0)) * (ig * xs)
        a = a.reshape(tt // SUBLANE, SUBLANE, LRU_BLOCK_DIM)
        u = u.reshape(tt // SUBLANE, SUBLANE, LRU_BLOCK_DIM)
        for s in (1, 2, 4):
            ok = row8 >= s
            a_sh = pltpu.roll(a, s, axis=1)
            u_sh = pltpu.roll(u, s, axis=1)
            u = jnp.where(ok, u + a * u_sh, u)
            a = jnp.where(ok, a * a_sh, a)
        a_scr[:, sl] = a.reshape(tt, LRU_BLOCK_DIM)
        u_scr[:, sl] = u.reshape(tt, LRU_BLOCK_DIM)

    def group(gi, c):
        r0 = pl.multiple_of(gi * SUBLANE, SUBLANE)
        h = u_scr[pl.ds(r0, SUBLANE), :] + a_scr[pl.ds(r0, SUBLANE), :] * c
        h_scr[pl.ds(r0, SUBLANE), :] = h
        return jnp.broadcast_to(h[SUBLANE - 1:SUBLANE, :], (SUBLANE, width))

    c = lax.fori_loop(0, tt // SUBLANE, group, carry[...], unroll=4)
    carry[...] = c
    o_ref[...] = _rms(h_scr[...] * jax.nn.gelu(zy_ref[...]), gr_ref[...]).astype(o_ref.dtype)


def _rglru(zb, conv_w, conv_b, w_a, b_a, w_i, b_i, lam, g_rec, batch, seq, tt=256):
    nt = seq // tt
    wt = LRU_WIDTH // LANE
    full = lambda a: pl.BlockSpec(a.shape, lambda b, t: (0,) * a.ndim)
    zspec = lambda tile: pl.BlockSpec((tt, LRU_WIDTH), lambda b, t: (b * nt + t, tile // wt))
    return pl.pallas_call(
        _lru_kernel,
        name="rglru",
        grid=(batch, nt),
        in_specs=[zspec(BT_X), zspec(BT_Y), full(conv_w), full(conv_b), full(w_a), full(b_a),
                  full(w_i), full(b_i), full(lam), full(g_rec)],
        out_specs=pl.BlockSpec((tt, LRU_WIDTH), lambda b, t: (b * nt + t, 0)),
        out_shape=jax.ShapeDtypeStruct((batch * seq, LRU_WIDTH), BF16),
        scratch_shapes=[pltpu.VMEM((tt + SUBLANE, LRU_WIDTH), F32), pltpu.VMEM((tt, LRU_WIDTH), F32),
                        pltpu.VMEM((tt, LRU_WIDTH), F32), pltpu.VMEM((tt, LRU_WIDTH), F32),
                        pltpu.VMEM((SUBLANE, LRU_WIDTH), F32)],
        compiler_params=pltpu.CompilerParams(
            dimension_semantics=("arbitrary", "arbitrary"), vmem_limit_bytes=VMEM_LIMIT),
    )(zb, zb, conv_w, conv_b, w_a, b_a, w_i, b_i, lam, g_rec)


def _resident_bf16(w_ref, w16_scr):
    @pl.when(pl.program_id(0) == 0)
    def _():
        w16_scr[...] = w_ref[...].astype(BF16)


def _resident(a):
    return pl.BlockSpec(a.shape, lambda i: (0,) * a.ndim, pipeline_mode=pl.Buffered(1))


def _out_kernel(oa_ref, or_ref, w_ref, post_ref, h_ref, o_ref, w16_scr):
    _resident_bf16(w_ref, w16_scr)
    for c in range(h_ref.shape[0] // ROW_CHUNK):
        rows = slice(c * ROW_CHUNK, (c + 1) * ROW_CHUNK)
        y = (_dot(oa_ref[rows, :], w16_scr[0:NSA_WIDTH, :])
             + _dot(or_ref[rows, :], w16_scr[NSA_WIDTH:NSA_WIDTH + LRU_WIDTH, :]))
        o_ref[rows, :] = h_ref[rows, :] + _rms(y, post_ref[...])


def _out_proj(o_attn, o_rec, w_out, post_g, h, tm=512):
    n, d = h.shape
    row = lambda i: (i, 0)
    return pl.pallas_call(
        _out_kernel,
        name="out_proj",
        grid=(n // tm,),
        in_specs=[pl.BlockSpec((tm, NSA_WIDTH), row), pl.BlockSpec((tm, LRU_WIDTH), row),
                  _resident(w_out), _resident(post_g), pl.BlockSpec((tm, d), row)],
        out_specs=pl.BlockSpec((tm, d), row),
        out_shape=jax.ShapeDtypeStruct((n, d), F32),
        scratch_shapes=[pltpu.VMEM(w_out.shape, BF16)],
        compiler_params=pltpu.CompilerParams(
            dimension_semantics=("arbitrary",), vmem_limit_bytes=VMEM_LIMIT),
    )(o_attn, o_rec, w_out, post_g, h)


def _ple_kernel(h_ref, p_ref, pre_ref, wg_ref, wp_ref, post_ref, o_ref, wg16_scr, wp16_scr):
    _resident_bf16(wg_ref, wg16_scr)
    _resident_bf16(wp_ref, wp16_scr)
    for c in range(h_ref.shape[0] // ROW_CHUNK):
        rows = slice(c * ROW_CHUNK, (c + 1) * ROW_CHUNK)
        h = h_ref[rows, :]
        gate = jax.nn.sigmoid(_dot(_rms(h, pre_ref[...]).astype(BF16), wg16_scr[...]))
        pp = _dot(p_ref[rows, :].astype(BF16), wp16_scr[...])
        o_ref[rows, :] = h + _rms(gate * pp, post_ref[...])


def _ple(h, p, pre_g, w_gate, w_proj, post_g, tm=512):
    n, d = h.shape
    row = lambda i: (i, 0)
    return pl.pallas_call(
        _ple_kernel,
        name="ple",
        grid=(n // tm,),
        in_specs=[pl.BlockSpec((tm, d), row), pl.BlockSpec((tm, p.shape[1]), row),
                  _resident(pre_g), _resident(w_gate), _resident(w_proj), _resident(post_g)],
        out_specs=pl.BlockSpec((tm, d), row),
        out_shape=jax.ShapeDtypeStruct((n, d), F32),
        scratch_shapes=[pltpu.VMEM(w_gate.shape, BF16), pltpu.VMEM(w_proj.shape, BF16)],
        compiler_params=pltpu.CompilerParams(
            dimension_semantics=("arbitrary",), vmem_limit_bytes=VMEM_LIMIT),
    )(h, p, pre_g, w_gate, w_proj, post_g)


def _selection_constants(seq):
    ncp = seq // CMP_STRIDE
    n_cmp = (seq - CMP_LEN) // CMP_STRIDE + 1
    n_sel = seq // SEL_LEN
    c = np.arange(ncp)[None, :]
    s = np.arange(n_sel)[:, None]
    ovl_t = ((c * CMP_STRIDE < s * SEL_LEN + SEL_LEN) & (c * CMP_STRIDE + CMP_LEN - 1 >= s * SEL_LEN) & (c < n_cmp))
    onehot = (np.arange(seq)[:, None] // SEL_LEN) == np.arange(LANE)[None, :]
    return jnp.asarray(ovl_t, BF16), jnp.asarray(onehot * MASK_BIG, BF16)


def _pad_cols(a, width):
    return jnp.pad(a, ((0, 0), (0, width - a.shape[1])))


def kernel(x, p, positions, ff1_pre_g, ff1_post_g, ff1_w_gate, ff1_w_up, ff1_w_down, mix_pre_g, mix_post_g, w_in, cmp_pos_k, cmp_pos_v, cmp_k_w1, cmp_k_w2, cmp_v_w1, cmp_v_w2, nsa_gate_b, conv_w, conv_b, rg_w_a, rg_b_a, rg_w_i, rg_b_i, rg_lambda, attn_out_g, rec_out_g, w_out, ff2_pre_g, ff2_post_g, ff2_w_gate, ff2_w_up, ff2_w_down, ple_pre_g, ple_post_g, w_ple_gate, w_ple_proj):
    batch, seq, d = x.shape
    depth = p.shape[0]
    n = batch * seq
    vec = lambda a: a.reshape(1, -1)
    gate_cols = NSA_HEADS * N_GATES
    grp_gates = GQA_GROUP * N_GATES
    o_q, o_kc, o_vc, o_ks, o_vs, o_kw, o_vw = (NSA_WIDTH * 0,) + tuple(NSA_WIDTH + k * KV_WIDTH for k in range(6))
    o_g = NSA_WIDTH + 6 * KV_WIDTH
    o_x = o_g + gate_cols
    o_y = o_x + LRU_WIDTH

    half = jnp.arange(ROPE_HALF, dtype=F32)
    inv_freq = ROPE_THETA ** (-half / ROPE_HALF)
    invf = jnp.tile(jnp.concatenate([inv_freq, inv_freq]), ROPE_PACK).reshape(1, LANE)
    ovl_t, onehot = _selection_constants(seq)
    sub = PROJ_CHUNK // ROPE_PACK
    pos = positions.reshape(n // PROJ_CHUNK, ROPE_PACK, sub).transpose(0, 2, 1).reshape(n // ROPE_PACK, ROPE_PACK)
    pos = jnp.repeat(pos, ROPE_DIM, axis=1)

    h = x.reshape(n, d)
    for i in range(depth):
        wi = w_in[i].astype(BF16)
        cols = lambda o, w: wi[:, o:o + w]
        w_rest = jnp.concatenate(
            [cols(o_x, LRU_WIDTH), cols(o_y, LRU_WIDTH)]
            + [_pad_cols(cols(o_g + g * grp_gates, grp_gates), LANE) for g in range(NSA_KV_HEADS)]
            + [jnp.zeros((d, B_WIDTH - (BT_GATE + NSA_KV_HEADS) * LANE), BF16)],
            axis=1)
        gb = nsa_gate_b[i].reshape(NSA_KV_HEADS, grp_gates)
        gate_bias = _pad_cols(jnp.pad(gb, ((0, 0), (0, LANE - grp_gates))).reshape(1, NSA_KV_HEADS * LANE), PROJ_TN)

        h = _ffn(h, vec(ff1_pre_g[i]), ff1_w_gate[i], ff1_w_up[i], ff1_w_down[i], vec(ff1_post_g[i]))
        za, zb = _proj(pos, h, vec(mix_pre_g[i]), wi, w_rest, gate_bias, invf)
        kcv = _compress(
            zb, jnp.stack([cmp_pos_k[i], cmp_pos_v[i]]),
            jnp.stack([cmp_k_w1[i], cmp_v_w1[i]]).reshape(2, CMP_LEN, HEAD_DIM, -1).astype(BF16),
            jnp.stack([cmp_k_w2[i], cmp_v_w2[i]]).astype(BF16), batch, seq)
        o_attn = _attention(za, zb, kcv, ovl_t, onehot, vec(attn_out_g[i]), batch, seq)
        o_rec = _rglru(zb, conv_w[i], vec(conv_b[i]), rg_w_a[i].astype(BF16), vec(rg_b_a[i]),
                       rg_w_i[i].astype(BF16), vec(rg_b_i[i]), vec(rg_lambda[i]), vec(rec_out_g[i]), batch, seq)
        h = _out_proj(o_attn, o_rec, w_out[i], vec(mix_post_g[i]), h)
        h = _ffn(h, vec(ff2_pre_g[i]), ff2_w_gate[i], ff2_w_up[i], ff2_w_down[i], vec(ff2_post_g[i]))
        h = _ple(h, p[i].reshape(n, -1), vec(ple_pre_g[i]), w_ple_gate[i], w_ple_proj[i], vec(ple_post_g[i]))
    return h.reshape(batch, seq, d)
```

```python
import functools

import numpy as np
import jax
import jax.numpy as jnp
from jax import lax
from jax.experimental import pallas as pl
from jax.experimental.pallas import tpu as pltpu

F32 = jnp.float32
BF16 = jnp.bfloat16

D_MODEL = 2048
NSA_HEADS = 8
NSA_KV_HEADS = 2
GQA_GROUP = NSA_HEADS // NSA_KV_HEADS
HEAD_DIM = 128
NSA_WIDTH = NSA_HEADS * HEAD_DIM
KV_WIDTH = NSA_KV_HEADS * HEAD_DIM
ROPE_DIM = HEAD_DIM // 4
ROPE_HALF = ROPE_DIM // 2
ROPE_THETA = 500000.0
CMP_LEN = 32
CMP_STRIDE = 16
SEL_LEN = 64
SEL_TOPK = 16
WINDOW = 512
Q_BLOCK = 128
N_GATES = 3
LRU_WIDTH = 1024
LRU_BLOCKS = 8
LRU_BLOCK_DIM = LRU_WIDTH // LRU_BLOCKS
CONV_WIDTH = 4
LRU_C = 8.0
RMS_EPS = 1e-6
NEG = -1e30
SEL_FORCE = 1e4

LANE = 128
SUBLANE = 8
VMEM_LIMIT = 62 * 1024 * 1024

AT_Q, AT_KS, AT_KW, AT_VS, AT_VW = 0, 8, 10, 12, 14
A_WIDTH = 16 * LANE
BT_X, BT_Y, BT_KC, BT_VC, BT_GATE = 0, 8, 16, 18, 20
B_WIDTH = 24 * LANE
PROJ_TN = 512
PROJ_A_STEPS = A_WIDTH // PROJ_TN
PROJ_STEPS = (A_WIDTH + B_WIDTH) // PROJ_TN
PROJ_HEADS = PROJ_TN // LANE
ROW_CHUNK = 256
PROJ_CHUNK = ROW_CHUNK
ROPE_PACK = LANE // ROPE_DIM
FFN_CHUNK = 2 * ROW_CHUNK

Q_SCALE = HEAD_DIM ** -0.5 * 1.4426950408889634
SEL_KT = 512
MASK_BIG = 2.0 ** 100


def _rms(x, g):
    return x * lax.rsqrt(jnp.mean(x * x, axis=-1, keepdims=True) + RMS_EPS) * g


def _dot(a, b):
    return jnp.dot(a, b, preferred_element_type=F32)


def _dot_nt(a, b):
    return lax.dot_general(a, b, (((1,), (1,)), ((), ())), preferred_element_type=F32)


def _ffn_steps(j, n_steps, h_ref, pre_ref, post_ref, o_ref, u_scr, weights):
    def step(first, last):
        wg, wu, wd = weights()
        for c in range(h_ref.shape[0] // FFN_CHUNK):
            rows = slice(c * FFN_CHUNK, (c + 1) * FFN_CHUNK)
            if first:
                u = _rms(h_ref[rows, :], pre_ref[...]).astype(BF16)
                u_scr[rows, :] = u
            else:
                u = u_scr[rows, :]
            g = _dot(u, wg)
            up = _dot(u, wu)
            acc = _dot(((g * jax.nn.sigmoid(g)) * up).astype(BF16), wd)
            if not first:
                acc = o_ref[rows, :] + acc
            if last:
                acc = h_ref[rows, :] + 0.5 * _rms(acc, post_ref[...])
            o_ref[rows, :] = acc

    if n_steps == 1:
        step(True, True)
    else:
        pl.when(j == 0)(lambda: step(True, False))
        pl.when((j > 0) & (j < n_steps - 1))(lambda: step(False, False))
        pl.when(j == n_steps - 1)(lambda: step(False, True))


def _ffn_head_kernel(h_ref, pre_ref, wg_ref, wu_ref, wd_ref, post_ref, o_ref, wg16_ref, wu16_ref, wd16_ref,
                     u_scr, *, n_steps):
    def weights():
        w16 = []
        for src, dst in ((wg_ref, wg16_ref), (wu_ref, wu16_ref), (wd_ref, wd16_ref)):
            w = src[...].astype(BF16)
            dst[...] = w
            w16.append(w)
        return w16

    _ffn_steps(pl.program_id(0), n_steps, h_ref, pre_ref, post_ref, o_ref, u_scr, weights)


def _ffn_tail_kernel(h_ref, head_ref, pre_ref, wg_ref, wu_ref, wd_ref, post_ref, o_ref, u_scr, *,
                     n_steps, n_copy):
    i, j = pl.program_id(0), pl.program_id(1)
    slab = head_ref.shape[0]

    @pl.when((i == 0) & (j < n_copy))
    def _():
        o_ref[pl.ds(pl.multiple_of(j * slab, slab), slab), :] = head_ref[...]

    @pl.when(i > 0)
    def _():
        _ffn_steps(j, n_steps, h_ref, pre_ref, post_ref, o_ref, u_scr,
                   lambda: (wg_ref[...], wu_ref[...], wd_ref[...]))


def _ffn(h, pre_g, wg, wu, wd, post_g, tm=1024, tf=512, tf_head=256):
    n, d = h.shape
    dff = wg.shape[1]
    tf, tf_head = min(tf, dff), min(tf_head, dff)
    vec1 = pl.BlockSpec((1, d), lambda j: (0, 0))
    head, wg16, wu16, wd16 = pl.pallas_call(
        functools.partial(_ffn_head_kernel, n_steps=dff // tf_head),
        name="ffn_head",
        grid=(dff // tf_head,),
        in_specs=[
            pl.BlockSpec((tm, d), lambda j: (0, 0)), vec1,
            pl.BlockSpec((d, tf_head), lambda j: (0, j)),
            pl.BlockSpec((d, tf_head), lambda j: (0, j)),
            pl.BlockSpec((tf_head, d), lambda j: (j, 0)),
            vec1,
        ],
        out_specs=[pl.BlockSpec((tm, d), lambda j: (0, 0)),
                   pl.BlockSpec((d, tf_head), lambda j: (0, j)),
                   pl.BlockSpec((d, tf_head), lambda j: (0, j)),
                   pl.BlockSpec((tf_head, d), lambda j: (j, 0))],
        out_shape=[jax.ShapeDtypeStruct((tm, d), F32), jax.ShapeDtypeStruct((d, dff), BF16),
                   jax.ShapeDtypeStruct((d, dff), BF16), jax.ShapeDtypeStruct((dff, d), BF16)],
        scratch_shapes=[pltpu.VMEM((tm, d), BF16)],
        compiler_params=pltpu.CompilerParams(
            dimension_semantics=("arbitrary",), vmem_limit_bytes=VMEM_LIMIT),
    )(h, pre_g, wg, wu, wd, post_g)

    n_steps = dff // tf
    n_copy = 1 << (min(n_steps, tm // LANE).bit_length() - 1)
    row = lambda i, j: (i, 0)
    vec = pl.BlockSpec((1, d), lambda i, j: (0, 0))
    wcol = lambda i, j: (0, jnp.where(i > 0, j, 0))
    return pl.pallas_call(
        functools.partial(_ffn_tail_kernel, n_steps=n_steps, n_copy=n_copy),
        name="ffn_tail",
        grid=(n // tm, n_steps),
        in_specs=[
            pl.BlockSpec((tm, d), row),
            pl.BlockSpec((tm // n_copy, d), lambda i, j: (jnp.where(i == 0, jnp.minimum(j, n_copy - 1), n_copy - 1), 0)),
            vec,
            pl.BlockSpec((d, tf), wcol),
            pl.BlockSpec((d, tf), wcol),
            pl.BlockSpec((tf, d), lambda i, j: (jnp.where(i > 0, j, 0), 0)),
            vec,
        ],
        out_specs=pl.BlockSpec((tm, d), row),
        out_shape=jax.ShapeDtypeStruct((n, d), F32),
        scratch_shapes=[pltpu.VMEM((tm, d), BF16)],
        compiler_params=pltpu.CompilerParams(
            dimension_semantics=("arbitrary", "arbitrary"), vmem_limit_bytes=VMEM_LIMIT),
    )(h, head, pre_g, wg16, wu16, wd16, post_g)


def _proj_kernel(pos_ref, h_ref, pre_ref, w_ref, gb_ref, invf_ref, za_ref, zb_ref, u_scr, cos_scr, s1_scr, s2_scr):
    j = pl.program_id(1)

    def prepare(rows):
        u_scr[rows, :] = _rms(h_ref[rows, :], pre_ref[...]).astype(BF16)
        sub = PROJ_CHUNK // ROPE_PACK
        packed = slice(rows.start // ROPE_PACK, rows.start // ROPE_PACK + sub)
        ang = pos_ref[packed, :].astype(F32) * invf_ref[...]
        cos, sin = jnp.cos(ang), jnp.sin(ang)
        lane = lax.broadcasted_iota(jnp.int32, ang.shape, 1)
        for k in range(ROPE_PACK):
            grp = slice(rows.start + k * sub, rows.start + (k + 1) * sub)
            c, s = (cos, sin) if k == 0 else (pltpu.roll(cos, LANE - k * ROPE_DIM, axis=1),
                                              pltpu.roll(sin, LANE - k * ROPE_DIM, axis=1))
            cos_scr[grp, :] = jnp.where(lane < ROPE_DIM, c, 1.0)
            s1_scr[grp, :] = jnp.where((lane >= ROPE_HALF) & (lane < ROPE_DIM), s, 0.0)
            s2_scr[grp, :] = jnp.where(lane < ROPE_HALF, -s, 0.0)

    def head(z, rows, hd, rope, mul=None):
        x = z[:, hd * LANE:(hd + 1) * LANE]
        if rope:
            x = (x * cos_scr[rows, :]
                 + pltpu.roll(x, ROPE_HALF, axis=1) * s1_scr[rows, :]
                 + pltpu.roll(x, LANE - ROPE_HALF, axis=1) * s2_scr[rows, :])
        return x if mul is None else x * mul

    def chunks(first=False):
        for c in range(u_scr.shape[0] // PROJ_CHUNK):
            rows = slice(c * PROJ_CHUNK, (c + 1) * PROJ_CHUNK)
            if first:
                prepare(rows)
            yield rows, _dot(u_scr[rows, :], w_ref[...])

    def store(ref, rope_heads, mul=None, first=False):
        for rows, z in chunks(first):
            for hd in range(PROJ_HEADS):
                ref[rows, hd * LANE:(hd + 1) * LANE] = head(z, rows, hd, hd < rope_heads, mul).astype(ref.dtype)

    @pl.when(j == 0)
    def _():
        store(za_ref, PROJ_HEADS, Q_SCALE, first=True)

    @pl.when((j > 0) & (j < AT_KS // PROJ_HEADS))
    def _():
        store(za_ref, PROJ_HEADS, Q_SCALE)

    @pl.when(j == AT_KS // PROJ_HEADS)
    def _():
        store(za_ref, PROJ_HEADS)

    @pl.when(j == AT_VS // PROJ_HEADS)
    def _():
        store(za_ref, 0)

    @pl.when((j >= PROJ_A_STEPS) & (j < PROJ_A_STEPS + BT_KC // PROJ_HEADS))
    def _():
        store(zb_ref, 0)

    @pl.when(j == PROJ_A_STEPS + BT_KC // PROJ_HEADS)
    def _():
        store(zb_ref, NSA_KV_HEADS)

    @pl.when(j == PROJ_A_STEPS + BT_GATE // PROJ_HEADS)
    def _():
        for rows, z in chunks():
            zb_ref[rows, :] = jax.nn.sigmoid(z + gb_ref[...])


def _proj(pos, h, pre_g, w, gb, invf, tm=1024):
    n, d = h.shape
    return pl.pallas_call(
        _proj_kernel,
        name="mix_proj",
        grid=(n // tm, PROJ_STEPS),
        in_specs=[
            pl.BlockSpec((tm // ROPE_PACK, LANE), lambda i, j: (i, 0)),
            pl.BlockSpec((tm, d), lambda i, j: (i, 0)),
            pl.BlockSpec((1, d), lambda i, j: (0, 0)),
            pl.BlockSpec((d, PROJ_TN), lambda i, j: (0, j)),
            pl.BlockSpec((1, PROJ_TN), lambda i, j: (0, 0)),
            pl.BlockSpec((1, LANE), lambda i, j: (0, 0)),
        ],
        out_specs=[pl.BlockSpec((tm, PROJ_TN), lambda i, j: (i, jnp.minimum(j, PROJ_A_STEPS - 1))),
                   pl.BlockSpec((tm, PROJ_TN), lambda i, j: (i, jnp.maximum(j - PROJ_A_STEPS, 0)))],
        out_shape=[jax.ShapeDtypeStruct((n, A_WIDTH), BF16), jax.ShapeDtypeStruct((n, B_WIDTH), F32)],
        scratch_shapes=[pltpu.VMEM((tm, d), BF16)] + [pltpu.VMEM((tm, LANE), F32)] * 3,
        compiler_params=pltpu.CompilerParams(
            dimension_semantics=("arbitrary", "arbitrary"), vmem_limit_bytes=VMEM_LIMIT),
    )(pos, h, pre_g, w, gb, invf)


def _cmp_kernel(x_ref, pos_ref, w1_ref, w2_ref, o_ref):
    ncp = o_ref.shape[0]
    hidden_w = w1_ref.shape[-1]
    acc_a = jnp.zeros((ncp, hidden_w), F32)
    acc_b = jnp.zeros((ncp, hidden_w), F32)
    for l in range(CMP_STRIDE):
        xl = x_ref[pl.ds(l, ncp, stride=CMP_STRIDE), :]
        acc_a += _dot((xl + pos_ref[l:l + 1, :]).astype(BF16), w1_ref[l])
        acc_b += _dot((xl + pos_ref[CMP_STRIDE + l:CMP_STRIDE + l + 1, :]).astype(BF16), w1_ref[CMP_STRIDE + l])
    hidden = acc_a + pltpu.roll(acc_b, ncp - 1, axis=0)
    o_ref[...] = _dot(jax.nn.gelu(hidden).astype(BF16), w2_ref[...])


def _compress(zb, pos_kv, w1, w2, batch, seq):
    ncp = seq // CMP_STRIDE
    hidden_w = w1.shape[-1]
    return pl.pallas_call(
        _cmp_kernel,
        name="compress",
        grid=(2, batch, NSA_KV_HEADS),
        in_specs=[
            pl.BlockSpec((seq, LANE), lambda k, b, g: (b, BT_KC + 2 * k + g)),
            pl.BlockSpec((None, CMP_LEN, HEAD_DIM), lambda k, b, g: (k, 0, 0)),
            pl.BlockSpec((None, CMP_LEN, HEAD_DIM, hidden_w), lambda k, b, g: (k, 0, 0, 0)),
            pl.BlockSpec((None, hidden_w, HEAD_DIM), lambda k, b, g: (k, 0, 0)),
        ],
        out_specs=pl.BlockSpec((None, None, None, ncp, HEAD_DIM), lambda k, b, g: (k, b, g, 0, 0)),
        out_shape=jax.ShapeDtypeStruct((2, batch, NSA_KV_HEADS, ncp, HEAD_DIM), F32),
        compiler_params=pltpu.CompilerParams(
            dimension_semantics=("arbitrary",) * 3, vmem_limit_bytes=VMEM_LIMIT),
    )(zb, pos_kv, w1, w2)


def _masked_exp(s, mask):
    s = s + jnp.where(mask, 0.0, -MASK_BIG)
    e = jnp.exp2(s - jnp.max(s, axis=-1, keepdims=True))
    return e, jnp.sum(e, axis=-1, keepdims=True)


def _attn_kernel(q_ref, kc_ref, vc_ref, ks_ref, vs_ref, kw_ref, vw_ref, gz_ref, ovl_ref, oh_ref, ga_ref, prev_ref,
                 o_ref, *, first_block):
    del prev_ref
    G, R, tq = NSA_KV_HEADS, GQA_GROUP, Q_BLOCK
    n_keys = ks_ref.shape[0]
    top_k = min(SEL_TOPK, ovl_ref.shape[0])
    n_blk = n_keys // SEL_LEN
    n_cmp = min(kc_ref.shape[1], -(-(n_keys // CMP_STRIDE) // LANE) * LANE)
    t0 = (first_block + pl.program_id(1)) * tq
    t_idx = t0 + lax.broadcasted_iota(jnp.int32, (1, tq, 1), 1)
    gsl = lambda g: slice(g * HEAD_DIM, (g + 1) * HEAD_DIM)

    def stacked_q(g):
        q = q_ref[:, g * R * HEAD_DIM:(g + 1) * R * HEAD_DIM]
        return jnp.concatenate([q[:, r * HEAD_DIM:(r + 1) * HEAD_DIM] for r in range(R)], axis=0)

    q4s = [stacked_q(g) for g in range(G)]

    wk = min(WINDOW + tq, n_keys)
    w0 = pl.multiple_of(jnp.maximum(t0 - WINDOW, 0), tq)
    kp = w0 + lax.broadcasted_iota(jnp.int32, (1, 1, wk), 2)
    win_mask = (kp <= t_idx) & (kp > t_idx - WINDOW)

    ones_col = jnp.ones((n_keys, LANE), BF16)

    def weights(s, m):
        return jnp.exp2((s - m).astype(BF16)).reshape(R * tq, s.shape[-1])

    def pv(p, v, ones):
        return _dot(p, jnp.concatenate([v, ones], axis=1))

    def normalised(acc):
        return (acc[:, :HEAD_DIM] * (1.0 / acc[:, HEAD_DIM:])).reshape(R, tq, HEAD_DIM)

    def window(g):
        s = _dot_nt(q4s[g], kw_ref[pl.ds(w0, wk), gsl(g)]).reshape(R, tq, wk)
        s = s + jnp.where(win_mask, 0.0, -MASK_BIG)
        p = weights(s, jnp.max(s, axis=-1, keepdims=True))
        return normalised(pv(p, vw_ref[pl.ds(w0, wk), gsl(g)], ones_col[0:wk, :]))

    def front(g):
        q4 = q4s[g]

        s = _dot_nt(q4, kc_ref[g, 0:n_cmp, :].astype(BF16)).reshape(R, tq, n_cmp)
        c_idx = lax.broadcasted_iota(jnp.int32, (1, 1, n_cmp), 2)
        e, l = _masked_exp(s, (c_idx * CMP_STRIDE + (CMP_LEN - 1)) <= t_idx)
        pc = e * (jnp.where(t_idx >= CMP_LEN - 1, 1.0, 0.0) / l)
        o_cmp = _dot(pc.reshape(R * tq, n_cmp).astype(BF16), vc_ref[g, 0:n_cmp, :].astype(BF16))
        o_cmp = o_cmp.reshape(R, tq, HEAD_DIM)
        if n_blk <= top_k:
            return q4, jnp.concatenate([q4, jnp.zeros_like(q4)], axis=1), o_cmp

        pcs = pc[0]
        for r in range(1, R):
            pcs = pcs + pc[r]
        hi = pcs.astype(BF16)
        lo = (pcs - hi.astype(F32)).astype(BF16)
        ovl = ovl_ref[0:n_blk, 0:n_cmp]
        imp = _dot_nt(ovl, hi) + _dot_nt(ovl, lo)
        blk = lax.broadcasted_iota(jnp.int32, (n_blk, tq), 0)
        tcol = t0 + lax.broadcasted_iota(jnp.int32, (n_blk, tq), 1)
        forced = (blk == lax.shift_right_logical(tcol, int(np.log2(SEL_LEN)))) | (blk == 0)
        v = jnp.where(forced, SEL_FORCE, jnp.where(blk * SEL_LEN <= tcol, imp, -SEL_FORCE))
        sub = lax.broadcasted_iota(jnp.int32, (SUBLANE, tq), 0)
        groups = [v[k:k + SUBLANE, :] for k in range(0, n_blk, SUBLANE)]
        ranks = [jnp.zeros((SUBLANE, tq), F32) for _ in groups]
        for sp in range(n_blk):
            other = v[sp:sp + 1, :]
            for gi, vg in enumerate(groups):
                first = gi * SUBLANE
                if first > sp:
                    beats = other >= vg
                elif first + SUBLANE - 1 <= sp:
                    beats = other > vg
                else:
                    beats = (other > vg) | ((other == vg) & (sub > sp - first))
                ranks[gi] = ranks[gi] + jnp.where(beats, 1.0, 0.0)
        rank = jnp.concatenate(ranks, axis=0)
        unsel = jnp.where(rank < float(top_k), 0.0, -1.0)
        unsel = jnp.concatenate([unsel, jnp.zeros((LANE - n_blk, tq), F32)], axis=0).T.astype(BF16)
        q_aug = jnp.concatenate([q4, jnp.concatenate([unsel] * R, axis=0)], axis=1)
        return q4, q_aug, o_cmp

    fronts = [front(g) for g in range(G)]

    n_full = n_keys - SEL_KT
    kp = n_full + lax.broadcasted_iota(jnp.int32, (1, 1, SEL_KT), 2)
    causal_bias = jnp.where(kp <= t_idx, 0.0, -MASK_BIG)

    def selected(g):
        k_aug = jnp.concatenate([ks_ref[:, gsl(g)], oh_ref[...]], axis=1)
        s = _dot_nt(fronts[g][1], k_aug).reshape(R, tq, n_keys)
        s_last = s[:, :, n_full:] + causal_bias
        m = jnp.max(s_last, axis=-1, keepdims=True)
        if n_full:
            s_full = s[:, :, :n_full]
            m = jnp.maximum(m, jnp.max(s_full, axis=-1, keepdims=True))
        acc = pv(weights(s_last, m), vs_ref[n_full:n_keys, gsl(g)], ones_col[n_full:n_keys, :])
        if n_full:
            acc = acc + pv(weights(s_full, m), vs_ref[0:n_full, gsl(g)], ones_col[0:n_full, :])
        return normalised(acc)

    o_sels = [selected(g) for g in range(G)]
    o_wins = [window(g) for g in range(G)]
    heads = []
    for g in range(G):
        o_cmp, o_sel, o_win = fronts[g][2], o_sels[g], o_wins[g]
        gz = gz_ref[:, gsl(g)]
        for r in range(R):
            c = r * N_GATES
            heads.append(gz[:, c:c + 1] * o_cmp[r] + gz[:, c + 1:c + 2] * o_sel[r] + gz[:, c + 2:c + 3] * o_win[r])
    o_ref[...] = _rms(jnp.concatenate(heads, axis=1), ga_ref[...]).astype(o_ref.dtype)


def _attention(za, zb, kcv, ovl_t, onehot, g_attn, batch, seq):
    ncp = seq // CMP_STRIDE
    n_sel = seq // SEL_LEN
    G = NSA_KV_HEADS
    blocks = SEL_KT // Q_BLOCK
    za3 = za.reshape(batch, seq, A_WIDTH)
    zb3 = zb.reshape(batch, seq, B_WIDTH)
    out = jnp.zeros((batch, seq, NSA_WIDTH), BF16)
    for c in range(seq // SEL_KT):
        n_keys = (c + 1) * SEL_KT
        q_row = lambda b, i, c=c: (b, c * blocks + i, 0)
        kv_spec = lambda tile: pl.BlockSpec((None, n_keys, G * LANE), lambda b, i: (b, 0, tile // G))
        operands = (za3, kcv, kcv, za3, za3, za3, za3, zb3, ovl_t, onehot, g_attn, out)
        out = pl.pallas_call(
            functools.partial(_attn_kernel, first_block=c * blocks),
            name=f"attention_{c}",
            grid=(batch, blocks),
            in_specs=[
                pl.BlockSpec((None, Q_BLOCK, NSA_WIDTH), q_row),
                pl.BlockSpec((None, None, G, ncp, HEAD_DIM), lambda b, i: (0, b, 0, 0, 0)),
                pl.BlockSpec((None, None, G, ncp, HEAD_DIM), lambda b, i: (1, b, 0, 0, 0)),
                kv_spec(AT_KS), kv_spec(AT_VS), kv_spec(AT_KW), kv_spec(AT_VW),
                pl.BlockSpec((None, Q_BLOCK, G * LANE), lambda b, i, c=c: (b, c * blocks + i, BT_GATE // G)),
                pl.BlockSpec((n_sel, ncp), lambda b, i: (0, 0)),
                pl.BlockSpec((n_keys, LANE), lambda b, i: (0, 0)),
                pl.BlockSpec((1, NSA_WIDTH), lambda b, i: (0, 0)),
                pl.BlockSpec(memory_space=pl.ANY),
            ],
            out_specs=pl.BlockSpec((None, Q_BLOCK, NSA_WIDTH), q_row),
            out_shape=jax.ShapeDtypeStruct((batch, seq, NSA_WIDTH), BF16),
            input_output_aliases={len(operands) - 1: 0},
            compiler_params=pltpu.CompilerParams(
                dimension_semantics=("arbitrary",) * 2, vmem_limit_bytes=VMEM_LIMIT),
        )(*operands)
    return out.reshape(batch * seq, NSA_WIDTH)


def _lru_kernel(zx_ref, zy_ref, cw_ref, cb_ref, wa_ref, ba_ref, wi_ref, bi_ref, lam_ref, gr_ref,
                o_ref, xpad, a_scr, u_scr, h_scr, carry):
    tt, width = zx_ref.shape
    ti = pl.program_id(1)

    @pl.when(ti == 0)
    def _():
        xpad[0:SUBLANE, :] = jnp.zeros((SUBLANE, width), F32)
        carry[...] = jnp.zeros_like(carry)

    xb = zx_ref[...]
    xpad[SUBLANE:SUBLANE + tt, :] = xb
    xc = cb_ref[...] + cw_ref[CONV_WIDTH - 1:CONV_WIDTH, :] * xb
    for k in range(1, CONV_WIDTH):
        xc = xc + cw_ref[CONV_WIDTH - 1 - k:CONV_WIDTH - k, :] * xpad[SUBLANE - k:SUBLANE - k + tt, :]
    xpad[0:SUBLANE, :] = xb[tt - SUBLANE:tt, :]

    sp = jax.nn.softplus(-lam_ref[...])
    row8 = lax.broadcasted_iota(jnp.int32, (1, SUBLANE, LRU_BLOCK_DIM), 1)
    for hb in range(LRU_BLOCKS):
        sl = slice(hb * LRU_BLOCK_DIM, (hb + 1) * LRU_BLOCK_DIM)
        xs = xc[:, sl]
        xs16 = xs.astype(BF16)
        r = jax.nn.sigmoid(_dot(xs16, wa_ref[hb]) + ba_ref[:, sl])
        ig = jax.nn.sigmoid(_dot(xs16, wi_ref[hb]) + bi_ref[:, sl])
        log_a = -LRU_C * r * sp[:, sl]
        a = jnp.exp(log_a)
        u = jnp.sqrt(-jnp.tanh(log_a) * (a * a + 1.0)) * (ig * xs)
        a = a.reshape(tt // SUBLANE, SUBLANE, LRU_BLOCK_DIM)
        u = u.reshape(tt // SUBLANE, SUBLANE, LRU_BLOCK_DIM)
        for s in (1, 2, 4):
            ok = row8 >= s
            a_sh = pltpu.roll(a, s, axis=1)
            u_sh = pltpu.roll(u, s, axis=1)
            u = jnp.where(ok, u + a * u_sh, u)
            a = jnp.where(ok, a * a_sh, a)
        a_scr[:, sl] = a.reshape(tt, LRU_BLOCK_DIM)
        u_scr[:, sl] = u.reshape(tt, LRU_BLOCK_DIM)

    def group(gi, c):
        r0 = pl.multiple_of(gi * SUBLANE, SUBLANE)
        h = u_scr[pl.ds(r0, SUBLANE), :] + a_scr[pl.ds(r0, SUBLANE), :] * c
        h_scr[pl.ds(r0, SUBLANE), :] = h
        return jnp.broadcast_to(h[SUBLANE - 1:SUBLANE, :], (SUBLANE, width))

    c = lax.fori_loop(0, tt // SUBLANE, group, carry[...], unroll=4)
    carry[...] = c
    o_ref[...] = _rms(h_scr[...] * jax.nn.gelu(zy_ref[...]), gr_ref[...]).astype(o_ref.dtype)


def _rglru(zb, conv_w, conv_b, w_a, b_a, w_i, b_i, lam, g_rec, batch, seq, tt=256):
    nt = seq // tt
    wt = LRU_WIDTH // LANE
    full = lambda a: pl.BlockSpec(a.shape, lambda b, t: (0,) * a.ndim)
    zspec = lambda tile: pl.BlockSpec((tt, LRU_WIDTH), lambda b, t: (b * nt + t, tile // wt))
    return pl.pallas_call(
        _lru_kernel,
        name="rglru",
        grid=(batch, nt),
        in_specs=[zspec(BT_X), zspec(BT_Y), full(conv_w), full(conv_b), full(w_a), full(b_a),
                  full(w_i), full(b_i), full(lam), full(g_rec)],
        out_specs=pl.BlockSpec((tt, LRU_WIDTH), lambda b, t: (b * nt + t, 0)),
        out_shape=jax.ShapeDtypeStruct((batch * seq, LRU_WIDTH), BF16),
        scratch_shapes=[pltpu.VMEM((tt + SUBLANE, LRU_WIDTH), F32), pltpu.VMEM((tt, LRU_WIDTH), F32),
                        pltpu.VMEM((tt, LRU_WIDTH), F32), pltpu.VMEM((tt, LRU_WIDTH), F32),
                        pltpu.VMEM((SUBLANE, LRU_WIDTH), F32)],
        compiler_params=pltpu.CompilerParams(
            dimension_semantics=("arbitrary", "arbitrary"), vmem_limit_bytes=VMEM_LIMIT),
    )(zb, zb, conv_w, conv_b, w_a, b_a, w_i, b_i, lam, g_rec)


def _resident_bf16(w_ref, w16_scr):
    @pl.when(pl.program_id(0) == 0)
    def _():
        w16_scr[...] = w_ref[...].astype(BF16)


def _resident(a):
    return pl.BlockSpec(a.shape, lambda i: (0,) * a.ndim, pipeline_mode=pl.Buffered(1))


def _out_kernel(oa_ref, or_ref, w_ref, post_ref, h_ref, o_ref, w16_scr):
    _resident_bf16(w_ref, w16_scr)
    for c in range(h_ref.shape[0] // ROW_CHUNK):
        rows = slice(c * ROW_CHUNK, (c + 1) * ROW_CHUNK)
        y = (_dot(oa_ref[rows, :], w16_scr[0:NSA_WIDTH, :])
             + _dot(or_ref[rows, :], w16_scr[NSA_WIDTH:NSA_WIDTH + LRU_WIDTH, :]))
        o_ref[rows, :] = h_ref[rows, :] + _rms(y, post_ref[...])


def _out_proj(o_attn, o_rec, w_out, post_g, h, tm=512):
    n, d = h.shape
    row = lambda i: (i, 0)
    return pl.pallas_call(
        _out_kernel,
        name="out_proj",
        grid=(n // tm,),
        in_specs=[pl.BlockSpec((tm, NSA_WIDTH), row), pl.BlockSpec((tm, LRU_WIDTH), row),
                  _resident(w_out), _resident(post_g), pl.BlockSpec((tm, d), row)],
        out_specs=pl.BlockSpec((tm, d), row),
        out_shape=jax.ShapeDtypeStruct((n, d), F32),
        scratch_shapes=[pltpu.VMEM(w_out.shape, BF16)],
        compiler_params=pltpu.CompilerParams(
            dimension_semantics=("arbitrary",), vmem_limit_bytes=VMEM_LIMIT),
    )(o_attn, o_rec, w_out, post_g, h)


def _ple_kernel(h_ref, p_ref, pre_ref, wg_ref, wp_ref, post_ref, o_ref, wg16_scr, wp16_scr):
    _resident_bf16(wg_ref, wg16_scr)
    _resident_bf16(wp_ref, wp16_scr)
    for c in range(h_ref.shape[0] // ROW_CHUNK):
        rows = slice(c * ROW_CHUNK, (c + 1) * ROW_CHUNK)
        h = h_ref[rows, :]
        gate = jax.nn.sigmoid(_dot(_rms(h, pre_ref[...]).astype(BF16), wg16_scr[...]))
        pp = _dot(p_ref[rows, :].astype(BF16), wp16_scr[...])
        o_ref[rows, :] = h + _rms(gate * pp, post_ref[...])


def _ple(h, p, pre_g, w_gate, w_proj, post_g, tm=512):
    n, d = h.shape
    row = lambda i: (i, 0)
    return pl.pallas_call(
        _ple_kernel,
        name="ple",
        grid=(n // tm,),
        in_specs=[pl.BlockSpec((tm, d), row), pl.BlockSpec((tm, p.shape[1]), row),
                  _resident(pre_g), _resident(w_gate), _resident(w_proj), _resident(post_g)],
        out_specs=pl.BlockSpec((tm, d), row),
        out_shape=jax.ShapeDtypeStruct((n, d), F32),
        scratch_shapes=[pltpu.VMEM(w_gate.shape, BF16), pltpu.VMEM(w_proj.shape, BF16)],
        compiler_params=pltpu.CompilerParams(
            dimension_semantics=("arbitrary",), vmem_limit_bytes=VMEM_LIMIT),
    )(h, p, pre_g, w_gate, w_proj, post_g)


def _selection_constants(seq):
    ncp = seq // CMP_STRIDE
    n_cmp = (seq - CMP_LEN) // CMP_STRIDE + 1
    n_sel = seq // SEL_LEN
    c = np.arange(ncp)[None, :]
    s = np.arange(n_sel)[:, None]
    ovl_t = ((c * CMP_STRIDE < s * SEL_LEN + SEL_LEN) & (c * CMP_STRIDE + CMP_LEN - 1 >= s * SEL_LEN) & (c < n_cmp))
    onehot = (np.arange(seq)[:, None] // SEL_LEN) == np.arange(LANE)[None, :]
    return jnp.asarray(ovl_t, BF16), jnp.asarray(onehot * MASK_BIG, BF16)


def _pad_cols(a, width):
    return jnp.pad(a, ((0, 0), (0, width - a.shape[1])))


def kernel(x, p, positions, ff1_pre_g, ff1_post_g, ff1_w_gate, ff1_w_up, ff1_w_down, mix_pre_g, mix_post_g, w_in, cmp_pos_k, cmp_pos_v, cmp_k_w1, cmp_k_w2, cmp_v_w1, cmp_v_w2, nsa_gate_b, conv_w, conv_b, rg_w_a, rg_b_a, rg_w_i, rg_b_i, rg_lambda, attn_out_g, rec_out_g, w_out, ff2_pre_g, ff2_post_g, ff2_w_gate, ff2_w_up, ff2_w_down, ple_pre_g, ple_post_g, w_ple_gate, w_ple_proj):
    batch, seq, d = x.shape
    depth = p.shape[0]
    n = batch * seq
    vec = lambda a: a.reshape(1, -1)
    gate_cols = NSA_HEADS * N_GATES
    grp_gates = GQA_GROUP * N_GATES
    o_q, o_kc, o_vc, o_ks, o_vs, o_kw, o_vw = (NSA_WIDTH * 0,) + tuple(NSA_WIDTH + k * KV_WIDTH for k in range(6))
    o_g = NSA_WIDTH + 6 * KV_WIDTH
    o_x = o_g + gate_cols
    o_y = o_x + LRU_WIDTH

    half = jnp.arange(ROPE_HALF, dtype=F32)
    inv_freq = ROPE_THETA ** (-half / ROPE_HALF)
    invf = jnp.tile(jnp.concatenate([inv_freq, inv_freq]), ROPE_PACK).reshape(1, LANE)
    ovl_t, onehot = _selection_constants(seq)
    sub = PROJ_CHUNK // ROPE_PACK
    pos = positions.reshape(n // PROJ_CHUNK, ROPE_PACK, sub).transpose(0, 2, 1).reshape(n // ROPE_PACK, ROPE_PACK)
    pos = jnp.repeat(pos, ROPE_DIM, axis=1)

    h = x.reshape(n, d)
    for i in range(depth):
        wi = w_in[i]
        cols = lambda o, w: wi[:, o:o + w]
        w_in_z = jnp.concatenate(
            [cols(o_q, NSA_WIDTH), cols(o_ks, KV_WIDTH), cols(o_kw, KV_WIDTH), cols(o_vs, KV_WIDTH),
             cols(o_vw, KV_WIDTH), cols(o_x, LRU_WIDTH), cols(o_y, LRU_WIDTH), cols(o_kc, KV_WIDTH),
             cols(o_vc, KV_WIDTH)]
            + [_pad_cols(cols(o_g + g * grp_gates, grp_gates), LANE) for g in range(NSA_KV_HEADS)]
            + [jnp.zeros((d, B_WIDTH - (BT_GATE + NSA_KV_HEADS) * LANE), F32)],
            axis=1).astype(BF16)
        gb = nsa_gate_b[i].reshape(NSA_KV_HEADS, grp_gates)
        gate_bias = _pad_cols(jnp.pad(gb, ((0, 0), (0, LANE - grp_gates))).reshape(1, NSA_KV_HEADS * LANE), PROJ_TN)

        h = _ffn(h, vec(ff1_pre_g[i]), ff1_w_gate[i], ff1_w_up[i], ff1_w_down[i], vec(ff1_post_g[i]))
        za, zb = _proj(pos, h, vec(mix_pre_g[i]), w_in_z, gate_bias, invf)
        kcv = _compress(
            zb, jnp.stack([cmp_pos_k[i], cmp_pos_v[i]]),
            jnp.stack([cmp_k_w1[i], cmp_v_w1[i]]).reshape(2, CMP_LEN, HEAD_DIM, -1).astype(BF16),
            jnp.stack([cmp_k_w2[i], cmp_v_w2[i]]).astype(BF16), batch, seq)
        o_attn = _attention(za, zb, kcv, ovl_t, onehot, vec(attn_out_g[i]), batch, seq)
        o_rec = _rglru(zb, conv_w[i], vec(conv_b[i]), rg_w_a[i].astype(BF16), vec(rg_b_a[i]),
                       rg_w_i[i].astype(BF16), vec(rg_b_i[i]), vec(rg_lambda[i]), vec(rec_out_g[i]), batch, seq)
        h = _out_proj(o_attn, o_rec, w_out[i], vec(mix_post_g[i]), h)
        h = _ffn(h, vec(ff2_pre_g[i]), ff2_w_gate[i], ff2_w_up[i], ff2_w_down[i], vec(ff2_post_g[i]))
        h = _ple(h, p[i].reshape(n, -1), vec(ple_pre_g[i]), w_ple_gate[i], w_ple_proj[i], vec(ple_post_g[i]))
    return h.reshape(batch, seq, d)
```

```python
import functools

import numpy as np
import jax
import jax.numpy as jnp
from jax import lax
from jax.experimental import pallas as pl
from jax.experimental.pallas import tpu as pltpu

F32 = jnp.float32
BF16 = jnp.bfloat16

D_MODEL = 2048
NSA_HEADS = 8
NSA_KV_HEADS = 2
GQA_GROUP = NSA_HEADS // NSA_KV_HEADS
HEAD_DIM = 128
NSA_WIDTH = NSA_HEADS * HEAD_DIM
KV_WIDTH = NSA_KV_HEADS * HEAD_DIM
ROPE_DIM = HEAD_DIM // 4
ROPE_HALF = ROPE_DIM // 2
ROPE_THETA = 500000.0
CMP_LEN = 32
CMP_STRIDE = 16
SEL_LEN = 64
SEL_TOPK = 16
WINDOW = 512
Q_BLOCK = 128
N_GATES = 3
LRU_WIDTH = 1024
LRU_BLOCKS = 8
LRU_BLOCK_DIM = LRU_WIDTH // LRU_BLOCKS
CONV_WIDTH = 4
LRU_C = 8.0
RMS_EPS = 1e-6
NEG = -1e30
SEL_FORCE = 1e4

LANE = 128
SUBLANE = 8
VMEM_LIMIT = 62 * 1024 * 1024

AT_Q, AT_KS, AT_KW, AT_VS, AT_VW = 0, 8, 10, 12, 14
A_WIDTH = 16 * LANE
BT_X, BT_Y, BT_KC, BT_VC, BT_GATE = 0, 8, 16, 18, 20
B_WIDTH = 24 * LANE
PROJ_TN = 512
PROJ_A_STEPS = A_WIDTH // PROJ_TN
PROJ_STEPS = (A_WIDTH + B_WIDTH) // PROJ_TN
PROJ_HEADS = PROJ_TN // LANE
ROW_CHUNK = 256
PROJ_CHUNK = ROW_CHUNK
ROPE_PACK = LANE // ROPE_DIM
FFN_CHUNK = 4 * ROW_CHUNK
OUT_CHUNK = 2 * ROW_CHUNK

Q_SCALE = HEAD_DIM ** -0.5 * 1.4426950408889634
SEL_KT = 512
MASK_BIG = 2.0 ** 100


def _rms(x, g):
    return x * lax.rsqrt(jnp.mean(x * x, axis=-1, keepdims=True) + RMS_EPS) * g


def _dot(a, b):
    return jnp.dot(a, b, preferred_element_type=F32)


def _dot_nt(a, b):
    return lax.dot_general(a, b, (((1,), (1,)), ((), ())), preferred_element_type=F32)


def _ffn_steps(j, n_steps, h_ref, pre_ref, post_ref, o_ref, u_scr, weights):
    def step(first, last):
        wg, wu, wd = weights()
        for c in range(h_ref.shape[0] // FFN_CHUNK):
            rows = slice(c * FFN_CHUNK, (c + 1) * FFN_CHUNK)
            if first:
                u = _rms(h_ref[rows, :], pre_ref[...]).astype(BF16)
                u_scr[rows, :] = u
            else:
                u = u_scr[rows, :]
            g = _dot(u, wg)
            up = _dot(u, wu)
            acc = _dot(((g * jax.nn.sigmoid(g)) * up).astype(BF16), wd)
            if not first:
                acc = o_ref[rows, :] + acc
            if last:
                acc = h_ref[rows, :] + 0.5 * _rms(acc, post_ref[...])
            o_ref[rows, :] = acc

    if n_steps == 1:
        step(True, True)
    else:
        pl.when(j == 0)(lambda: step(True, False))
        pl.when((j > 0) & (j < n_steps - 1))(lambda: step(False, False))
        pl.when(j == n_steps - 1)(lambda: step(False, True))


def _ffn_head_kernel(h_ref, pre_ref, wg_ref, wu_ref, wd_ref, post_ref, o_ref, wg16_ref, wu16_ref, wd16_ref,
                     u_scr, *, n_steps):
    def weights():
        w16 = []
        for src, dst in ((wg_ref, wg16_ref), (wu_ref, wu16_ref), (wd_ref, wd16_ref)):
            w = src[...].astype(BF16)
            dst[...] = w
            w16.append(w)
        return w16

    _ffn_steps(pl.program_id(0), n_steps, h_ref, pre_ref, post_ref, o_ref, u_scr, weights)


def _ffn_tail_kernel(h_ref, head_ref, pre_ref, wg_ref, wu_ref, wd_ref, post_ref, o_ref, u_scr, *,
                     n_steps, n_copy):
    i, j = pl.program_id(0), pl.program_id(1)
    slab = head_ref.shape[0]

    @pl.when((i == 0) & (j < n_copy))
    def _():
        o_ref[pl.ds(pl.multiple_of(j * slab, slab), slab), :] = head_ref[...]

    @pl.when(i > 0)
    def _():
        _ffn_steps(j, n_steps, h_ref, pre_ref, post_ref, o_ref, u_scr,
                   lambda: (wg_ref[...], wu_ref[...], wd_ref[...]))


def _ffn(h, pre_g, wg, wu, wd, post_g, tm=1024, tf=512, tf_head=256):
    n, d = h.shape
    dff = wg.shape[1]
    tf, tf_head = min(tf, dff), min(tf_head, dff)
    vec1 = pl.BlockSpec((1, d), lambda j: (0, 0))
    head, wg16, wu16, wd16 = pl.pallas_call(
        functools.partial(_ffn_head_kernel, n_steps=dff // tf_head),
        name="ffn_head",
        grid=(dff // tf_head,),
        in_specs=[
            pl.BlockSpec((tm, d), lambda j: (0, 0)), vec1,
            pl.BlockSpec((d, tf_head), lambda j: (0, j)),
            pl.BlockSpec((d, tf_head), lambda j: (0, j)),
            pl.BlockSpec((tf_head, d), lambda j: (j, 0)),
            vec1,
        ],
        out_specs=[pl.BlockSpec((tm, d), lambda j: (0, 0)),
                   pl.BlockSpec((d, tf_head), lambda j: (0, j)),
                   pl.BlockSpec((d, tf_head), lambda j: (0, j)),
                   pl.BlockSpec((tf_head, d), lambda j: (j, 0))],
        out_shape=[jax.ShapeDtypeStruct((tm, d), F32), jax.ShapeDtypeStruct((d, dff), BF16),
                   jax.ShapeDtypeStruct((d, dff), BF16), jax.ShapeDtypeStruct((dff, d), BF16)],
        scratch_shapes=[pltpu.VMEM((tm, d), BF16)],
        compiler_params=pltpu.CompilerParams(
            dimension_semantics=("arbitrary",), vmem_limit_bytes=VMEM_LIMIT),
    )(h, pre_g, wg, wu, wd, post_g)

    n_steps = dff // tf
    n_copy = 1 << (min(n_steps, tm // LANE).bit_length() - 1)
    row = lambda i, j: (i, 0)
    vec = pl.BlockSpec((1, d), lambda i, j: (0, 0))
    wcol = lambda i, j: (0, jnp.where(i > 0, j, 0))
    return pl.pallas_call(
        functools.partial(_ffn_tail_kernel, n_steps=n_steps, n_copy=n_copy),
        name="ffn_tail",
        grid=(n // tm, n_steps),
        in_specs=[
            pl.BlockSpec((tm, d), row),
            pl.BlockSpec((tm // n_copy, d), lambda i, j: (jnp.where(i == 0, jnp.minimum(j, n_copy - 1), n_copy - 1), 0)),
            vec,
            pl.BlockSpec((d, tf), wcol),
            pl.BlockSpec((d, tf), wcol),
            pl.BlockSpec((tf, d), lambda i, j: (jnp.where(i > 0, j, 0), 0)),
            vec,
        ],
        out_specs=pl.BlockSpec((tm, d), row),
        out_shape=jax.ShapeDtypeStruct((n, d), F32),
        scratch_shapes=[pltpu.VMEM((tm, d), BF16)],
        compiler_params=pltpu.CompilerParams(
            dimension_semantics=("arbitrary", "arbitrary"), vmem_limit_bytes=VMEM_LIMIT),
    )(h, head, pre_g, wg16, wu16, wd16, post_g)


def _proj_kernel(pos_ref, h_ref, pre_ref, w_ref, gb_ref, invf_ref, za_ref, zb_ref, u_scr, cos_scr, s1_scr, s2_scr):
    j = pl.program_id(1)

    def prepare(rows):
        u_scr[rows, :] = _rms(h_ref[rows, :], pre_ref[...]).astype(BF16)
        sub = PROJ_CHUNK // ROPE_PACK
        packed = slice(rows.start // ROPE_PACK, rows.start // ROPE_PACK + sub)
        ang = pos_ref[packed, :].astype(F32) * invf_ref[...]
        cos, sin = jnp.cos(ang), jnp.sin(ang)
        lane = lax.broadcasted_iota(jnp.int32, ang.shape, 1)
        for k in range(ROPE_PACK):
            grp = slice(rows.start + k * sub, rows.start + (k + 1) * sub)
            c, s = (cos, sin) if k == 0 else (pltpu.roll(cos, LANE - k * ROPE_DIM, axis=1),
                                              pltpu.roll(sin, LANE - k * ROPE_DIM, axis=1))
            cos_scr[grp, :] = jnp.where(lane < ROPE_DIM, c, 1.0)
            s1_scr[grp, :] = jnp.where((lane >= ROPE_HALF) & (lane < ROPE_DIM), s, 0.0)
            s2_scr[grp, :] = jnp.where(lane < ROPE_HALF, -s, 0.0)

    def head(z, rows, hd, rope, mul=None):
        x = z[:, hd * LANE:(hd + 1) * LANE]
        if rope:
            x = (x * cos_scr[rows, :]
                 + pltpu.roll(x, ROPE_HALF, axis=1) * s1_scr[rows, :]
                 + pltpu.roll(x, LANE - ROPE_HALF, axis=1) * s2_scr[rows, :])
        return x if mul is None else x * mul

    def chunks(first=False):
        for c in range(u_scr.shape[0] // PROJ_CHUNK):
            rows = slice(c * PROJ_CHUNK, (c + 1) * PROJ_CHUNK)
            if first:
                prepare(rows)
            yield rows, _dot(u_scr[rows, :], w_ref[...])

    def store(ref, rope_heads, mul=None, first=False):
        for rows, z in chunks(first):
            for hd in range(PROJ_HEADS):
                ref[rows, hd * LANE:(hd + 1) * LANE] = head(z, rows, hd, hd < rope_heads, mul).astype(ref.dtype)

    @pl.when(j == 0)
    def _():
        store(za_ref, PROJ_HEADS, Q_SCALE, first=True)

    @pl.when((j > 0) & (j < AT_KS // PROJ_HEADS))
    def _():
        store(za_ref, PROJ_HEADS, Q_SCALE)

    @pl.when(j == AT_KS // PROJ_HEADS)
    def _():
        store(za_ref, PROJ_HEADS)

    @pl.when(j == AT_VS // PROJ_HEADS)
    def _():
        store(za_ref, 0)

    @pl.when((j >= PROJ_A_STEPS) & (j < PROJ_A_STEPS + BT_KC // PROJ_HEADS))
    def _():
        store(zb_ref, 0)

    @pl.when(j == PROJ_A_STEPS + BT_KC // PROJ_HEADS)
    def _():
        store(zb_ref, NSA_KV_HEADS)

    @pl.when(j == PROJ_A_STEPS + BT_GATE // PROJ_HEADS)
    def _():
        for rows, z in chunks():
            zb_ref[rows, :] = jax.nn.sigmoid(z + gb_ref[...])


def _proj(pos, h, pre_g, w, gb, invf, tm=1024):
    n, d = h.shape
    return pl.pallas_call(
        _proj_kernel,
        name="mix_proj",
        grid=(n // tm, PROJ_STEPS),
        in_specs=[
            pl.BlockSpec((tm // ROPE_PACK, LANE), lambda i, j: (i, 0)),
            pl.BlockSpec((tm, d), lambda i, j: (i, 0)),
            pl.BlockSpec((1, d), lambda i, j: (0, 0)),
            pl.BlockSpec((d, PROJ_TN), lambda i, j: (0, j)),
            pl.BlockSpec((1, PROJ_TN), lambda i, j: (0, 0)),
            pl.BlockSpec((1, LANE), lambda i, j: (0, 0)),
        ],
        out_specs=[pl.BlockSpec((tm, PROJ_TN), lambda i, j: (i, jnp.minimum(j, PROJ_A_STEPS - 1))),
                   pl.BlockSpec((tm, PROJ_TN), lambda i, j: (i, jnp.maximum(j - PROJ_A_STEPS, 0)))],
        out_shape=[jax.ShapeDtypeStruct((n, A_WIDTH), BF16), jax.ShapeDtypeStruct((n, B_WIDTH), F32)],
        scratch_shapes=[pltpu.VMEM((tm, d), BF16)] + [pltpu.VMEM((tm, LANE), F32)] * 3,
        compiler_params=pltpu.CompilerParams(
            dimension_semantics=("arbitrary", "arbitrary"), vmem_limit_bytes=VMEM_LIMIT),
    )(pos, h, pre_g, w, gb, invf)


def _cmp_kernel(x_ref, pos_ref, w1_ref, w2_ref, o_ref):
    ncp = o_ref.shape[0]
    hidden_w = w1_ref.shape[-1]
    acc_a = jnp.zeros((ncp, hidden_w), F32)
    acc_b = jnp.zeros((ncp, hidden_w), F32)
    for l in range(CMP_STRIDE):
        xl = x_ref[pl.ds(l, ncp, stride=CMP_STRIDE), :]
        acc_a += _dot((xl + pos_ref[l:l + 1, :]).astype(BF16), w1_ref[l])
        acc_b += _dot((xl + pos_ref[CMP_STRIDE + l:CMP_STRIDE + l + 1, :]).astype(BF16), w1_ref[CMP_STRIDE + l])
    hidden = acc_a + pltpu.roll(acc_b, ncp - 1, axis=0)
    o_ref[...] = _dot(jax.nn.gelu(hidden).astype(BF16), w2_ref[...])


def _compress(zb, pos_kv, w1, w2, batch, seq):
    ncp = seq // CMP_STRIDE
    hidden_w = w1.shape[-1]
    return pl.pallas_call(
        _cmp_kernel,
        name="compress",
        grid=(2, batch, NSA_KV_HEADS),
        in_specs=[
            pl.BlockSpec((seq, LANE), lambda k, b, g: (b, BT_KC + 2 * k + g)),
            pl.BlockSpec((None, CMP_LEN, HEAD_DIM), lambda k, b, g: (k, 0, 0)),
            pl.BlockSpec((None, CMP_LEN, HEAD_DIM, hidden_w), lambda k, b, g: (k, 0, 0, 0)),
            pl.BlockSpec((None, hidden_w, HEAD_DIM), lambda k, b, g: (k, 0, 0)),
        ],
        out_specs=pl.BlockSpec((None, None, None, ncp, HEAD_DIM), lambda k, b, g: (k, b, g, 0, 0)),
        out_shape=jax.ShapeDtypeStruct((2, batch, NSA_KV_HEADS, ncp, HEAD_DIM), F32),
        compiler_params=pltpu.CompilerParams(
            dimension_semantics=("arbitrary",) * 3, vmem_limit_bytes=VMEM_LIMIT),
    )(zb, pos_kv, w1, w2)


def _masked_exp(s, mask):
    s = s + jnp.where(mask, 0.0, -MASK_BIG)
    e = jnp.exp2(s - jnp.max(s, axis=-1, keepdims=True))
    return e, jnp.sum(e, axis=-1, keepdims=True)


def _attn_kernel(q_ref, kc_ref, vc_ref, ks_ref, vs_ref, kw_ref, vw_ref, gz_ref, ovl_ref, oh_ref, ga_ref, prev_ref,
                 o_ref, *, first_block):
    del prev_ref
    G, R, tq = NSA_KV_HEADS, GQA_GROUP, Q_BLOCK
    n_keys = ks_ref.shape[0]
    top_k = min(SEL_TOPK, ovl_ref.shape[0])
    n_blk = n_keys // SEL_LEN
    n_cmp = min(kc_ref.shape[1], -(-(n_keys // CMP_STRIDE) // LANE) * LANE)
    t0 = (first_block + pl.program_id(1)) * tq
    t_idx = t0 + lax.broadcasted_iota(jnp.int32, (1, tq, 1), 1)
    gsl = lambda g: slice(g * HEAD_DIM, (g + 1) * HEAD_DIM)

    def stacked_q(g):
        q = q_ref[:, g * R * HEAD_DIM:(g + 1) * R * HEAD_DIM]
        return jnp.concatenate([q[:, r * HEAD_DIM:(r + 1) * HEAD_DIM] for r in range(R)], axis=0)

    q4s = [stacked_q(g) for g in range(G)]

    wk = min(WINDOW + tq, n_keys)
    w0 = pl.multiple_of(jnp.maximum(t0 - WINDOW, 0), tq)
    kp = w0 + lax.broadcasted_iota(jnp.int32, (1, 1, wk), 2)
    win_mask = (kp <= t_idx) & (kp > t_idx - WINDOW)

    ones_col = jnp.ones((n_keys, LANE), BF16)

    def weights(s, m):
        return jnp.exp2((s - m).astype(BF16)).reshape(R * tq, s.shape[-1])

    def pv(p, v, ones):
        return _dot(p, jnp.concatenate([v, ones], axis=1))

    def normalised(acc):
        return (acc[:, :HEAD_DIM] * (1.0 / acc[:, HEAD_DIM:])).reshape(R, tq, HEAD_DIM)

    def window(g):
        s = _dot_nt(q4s[g], kw_ref[pl.ds(w0, wk), gsl(g)]).reshape(R, tq, wk)
        s = s + jnp.where(win_mask, 0.0, -MASK_BIG)
        p = weights(s, jnp.max(s, axis=-1, keepdims=True))
        return normalised(pv(p, vw_ref[pl.ds(w0, wk), gsl(g)], ones_col[0:wk, :]))

    def front(g):
        q4 = q4s[g]

        s = _dot_nt(q4, kc_ref[g, 0:n_cmp, :].astype(BF16)).reshape(R, tq, n_cmp)
        c_idx = lax.broadcasted_iota(jnp.int32, (1, 1, n_cmp), 2)
        e, l = _masked_exp(s, (c_idx * CMP_STRIDE + (CMP_LEN - 1)) <= t_idx)
        pc = e * (jnp.where(t_idx >= CMP_LEN - 1, 1.0, 0.0) / l)
        o_cmp = _dot(pc.reshape(R * tq, n_cmp).astype(BF16), vc_ref[g, 0:n_cmp, :].astype(BF16))
        o_cmp = o_cmp.reshape(R, tq, HEAD_DIM)
        if n_blk <= top_k:
            return q4, jnp.concatenate([q4, jnp.zeros_like(q4)], axis=1), o_cmp

        pcs = pc[0]
        for r in range(1, R):
            pcs = pcs + pc[r]
        hi = pcs.astype(BF16)
        lo = (pcs - hi.astype(F32)).astype(BF16)
        ovl = ovl_ref[0:n_blk, 0:n_cmp]
        imp = _dot_nt(ovl, hi) + _dot_nt(ovl, lo)
        blk = lax.broadcasted_iota(jnp.int32, (n_blk, tq), 0)
        tcol = t0 + lax.broadcasted_iota(jnp.int32, (n_blk, tq), 1)
        forced = (blk == lax.shift_right_logical(tcol, int(np.log2(SEL_LEN)))) | (blk == 0)
        v = jnp.where(forced, SEL_FORCE, jnp.where(blk * SEL_LEN <= tcol, imp, -SEL_FORCE))
        sub = lax.broadcasted_iota(jnp.int32, (SUBLANE, tq), 0)
        groups = [v[k:k + SUBLANE, :] for k in range(0, n_blk, SUBLANE)]
        ranks = [jnp.zeros((SUBLANE, tq), F32) for _ in groups]
        for sp in range(n_blk):
            other = v[sp:sp + 1, :]
            for gi, vg in enumerate(groups):
                first = gi * SUBLANE
                if first > sp:
                    beats = other >= vg
                elif first + SUBLANE - 1 <= sp:
                    beats = other > vg
                else:
                    beats = (other > vg) | ((other == vg) & (sub > sp - first))
                ranks[gi] = ranks[gi] + jnp.where(beats, 1.0, 0.0)
        rank = jnp.concatenate(ranks, axis=0)
        unsel = jnp.where(rank < float(top_k), 0.0, -1.0)
        unsel = jnp.concatenate([unsel, jnp.zeros((LANE - n_blk, tq), F32)], axis=0).T.astype(BF16)
        q_aug = jnp.concatenate([q4, jnp.concatenate([unsel] * R, axis=0)], axis=1)
        return q4, q_aug, o_cmp

    fronts = [front(g) for g in range(G)]

    n_full = n_keys - SEL_KT
    kp = n_full + lax.broadcasted_iota(jnp.int32, (1, 1, SEL_KT), 2)
    causal_bias = jnp.where(kp <= t_idx, 0.0, -MASK_BIG)

    def selected(g):
        k_aug = jnp.concatenate([ks_ref[:, gsl(g)], oh_ref[...]], axis=1)
        s = _dot_nt(fronts[g][1], k_aug).reshape(R, tq, n_keys)
        s_last = s[:, :, n_full:] + causal_bias
        m = jnp.max(s_last, axis=-1, keepdims=True)
        if n_full:
            s_full = s[:, :, :n_full]
            m = jnp.maximum(m, jnp.max(s_full, axis=-1, keepdims=True))
        acc = pv(weights(s_last, m), vs_ref[n_full:n_keys, gsl(g)], ones_col[n_full:n_keys, :])
        if n_full:
            acc = acc + pv(weights(s_full, m), vs_ref[0:n_full, gsl(g)], ones_col[0:n_full, :])
        return normalised(acc)

    o_sels = [selected(g) for g in range(G)]
    o_wins = [window(g) for g in range(G)]
    heads = []
    for g in range(G):
        o_cmp, o_sel, o_win = fronts[g][2], o_sels[g], o_wins[g]
        gz = gz_ref[:, gsl(g)]
        for r in range(R):
            c = r * N_GATES
            heads.append(gz[:, c:c + 1] * o_cmp[r] + gz[:, c + 1:c + 2] * o_sel[r] + gz[:, c + 2:c + 3] * o_win[r])
    o_ref[...] = _rms(jnp.concatenate(heads, axis=1), ga_ref[...]).astype(o_ref.dtype)


def _attention(za, zb, kcv, ovl_t, onehot, g_attn, batch, seq):
    ncp = seq // CMP_STRIDE
    n_sel = seq // SEL_LEN
    G = NSA_KV_HEADS
    blocks = SEL_KT // Q_BLOCK
    za3 = za.reshape(batch, seq, A_WIDTH)
    zb3 = zb.reshape(batch, seq, B_WIDTH)
    out = jnp.zeros((batch, seq, NSA_WIDTH), BF16)
    for c in range(seq // SEL_KT):
        n_keys = (c + 1) * SEL_KT
        q_row = lambda b, i, c=c: (b, c * blocks + i, 0)
        kv_spec = lambda tile: pl.BlockSpec((None, n_keys, G * LANE), lambda b, i: (b, 0, tile // G))
        operands = (za3, kcv, kcv, za3, za3, za3, za3, zb3, ovl_t, onehot, g_attn, out)
        out = pl.pallas_call(
            functools.partial(_attn_kernel, first_block=c * blocks),
            name=f"attention_{c}",
            grid=(batch, blocks),
            in_specs=[
                pl.BlockSpec((None, Q_BLOCK, NSA_WIDTH), q_row),
                pl.BlockSpec((None, None, G, ncp, HEAD_DIM), lambda b, i: (0, b, 0, 0, 0)),
                pl.BlockSpec((None, None, G, ncp, HEAD_DIM), lambda b, i: (1, b, 0, 0, 0)),
                kv_spec(AT_KS), kv_spec(AT_VS), kv_spec(AT_KW), kv_spec(AT_VW),
                pl.BlockSpec((None, Q_BLOCK, G * LANE), lambda b, i, c=c: (b, c * blocks + i, BT_GATE // G)),
                pl.BlockSpec((n_sel, ncp), lambda b, i: (0, 0)),
                pl.BlockSpec((n_keys, LANE), lambda b, i: (0, 0)),
                pl.BlockSpec((1, NSA_WIDTH), lambda b, i: (0, 0)),
                pl.BlockSpec(memory_space=pl.ANY),
            ],
            out_specs=pl.BlockSpec((None, Q_BLOCK, NSA_WIDTH), q_row),
            out_shape=jax.ShapeDtypeStruct((batch, seq, NSA_WIDTH), BF16),
            input_output_aliases={len(operands) - 1: 0},
            compiler_params=pltpu.CompilerParams(
                dimension_semantics=("arbitrary",) * 2, vmem_limit_bytes=VMEM_LIMIT),
        )(*operands)
    return out.reshape(batch * seq, NSA_WIDTH)


def _lru_kernel(zx_ref, zy_ref, cw_ref, cb_ref, wa_ref, ba_ref, wi_ref, bi_ref, lam_ref, gr_ref,
                o_ref, xpad, a_scr, u_scr, h_scr, carry):
    tt, width = zx_ref.shape
    ti = pl.program_id(1)

    @pl.when(ti == 0)
    def _():
        xpad[0:SUBLANE, :] = jnp.zeros((SUBLANE, width), F32)
        carry[...] = jnp.zeros_like(carry)

    xb = zx_ref[...]
    xpad[SUBLANE:SUBLANE + tt, :] = xb
    xc = cb_ref[...] + cw_ref[CONV_WIDTH - 1:CONV_WIDTH, :] * xb
    for k in range(1, CONV_WIDTH):
        xc = xc + cw_ref[CONV_WIDTH - 1 - k:CONV_WIDTH - k, :] * xpad[SUBLANE - k:SUBLANE - k + tt, :]
    xpad[0:SUBLANE, :] = xb[tt - SUBLANE:tt, :]

    sp = jax.nn.softplus(-lam_ref[...])
    row8 = lax.broadcasted_iota(jnp.int32, (1, SUBLANE, LRU_BLOCK_DIM), 1)
    for hb in range(LRU_BLOCKS):
        sl = slice(hb * LRU_BLOCK_DIM, (hb + 1) * LRU_BLOCK_DIM)
        xs = xc[:, sl]
        xs16 = xs.astype(BF16)
        r = jax.nn.sigmoid(_dot(xs16, wa_ref[hb]) + ba_ref[:, sl])
        ig = jax.nn.sigmoid(_dot(xs16, wi_ref[hb]) + bi_ref[:, sl])
        log_a = -LRU_C * r * sp[:, sl]
        a = jnp.exp(log_a)
        u = jnp.sqrt(-jnp.tanh(log_a) * (a * a + 1.0)) * (ig * xs)
        a = a.reshape(tt // SUBLANE, SUBLANE, LRU_BLOCK_DIM)
        u = u.reshape(tt // SUBLANE, SUBLANE, LRU_BLOCK_DIM)
        for s in (1, 2, 4):
            ok = row8 >= s
            a_sh = pltpu.roll(a, s, axis=1)
            u_sh = pltpu.roll(u, s, axis=1)
            u = jnp.where(ok, u + a * u_sh, u)
            a = jnp.where(ok, a * a_sh, a)
        a_scr[:, sl] = a.reshape(tt, LRU_BLOCK_DIM)
        u_scr[:, sl] = u.reshape(tt, LRU_BLOCK_DIM)

    def group(gi, c):
        r0 = pl.multiple_of(gi * SUBLANE, SUBLANE)
        h = u_scr[pl.ds(r0, SUBLANE), :] + a_scr[pl.ds(r0, SUBLANE), :] * c
        h_scr[pl.ds(r0, SUBLANE), :] = h
        return jnp.broadcast_to(h[SUBLANE - 1:SUBLANE, :], (SUBLANE, width))

    c = lax.fori_loop(0, tt // SUBLANE, group, carry[...], unroll=4)
    carry[...] = c
    o_ref[...] = _rms(h_scr[...] * jax.nn.gelu(zy_ref[...]), gr_ref[...]).astype(o_ref.dtype)


def _rglru(zb, conv_w, conv_b, w_a, b_a, w_i, b_i, lam, g_rec, batch, seq, tt=256):
    nt = seq // tt
    wt = LRU_WIDTH // LANE
    full = lambda a: pl.BlockSpec(a.shape, lambda b, t: (0,) * a.ndim)
    zspec = lambda tile: pl.BlockSpec((tt, LRU_WIDTH), lambda b, t: (b * nt + t, tile // wt))
    return pl.pallas_call(
        _lru_kernel,
        name="rglru",
        grid=(batch, nt),
        in_specs=[zspec(BT_X), zspec(BT_Y), full(conv_w), full(conv_b), full(w_a), full(b_a),
                  full(w_i), full(b_i), full(lam), full(g_rec)],
        out_specs=pl.BlockSpec((tt, LRU_WIDTH), lambda b, t: (b * nt + t, 0)),
        out_shape=jax.ShapeDtypeStruct((batch * seq, LRU_WIDTH), BF16),
        scratch_shapes=[pltpu.VMEM((tt + SUBLANE, LRU_WIDTH), F32), pltpu.VMEM((tt, LRU_WIDTH), F32),
                        pltpu.VMEM((tt, LRU_WIDTH), F32), pltpu.VMEM((tt, LRU_WIDTH), F32),
                        pltpu.VMEM((SUBLANE, LRU_WIDTH), F32)],
        compiler_params=pltpu.CompilerParams(
            dimension_semantics=("arbitrary", "arbitrary"), vmem_limit_bytes=VMEM_LIMIT),
    )(zb, zb, conv_w, conv_b, w_a, b_a, w_i, b_i, lam, g_rec)


def _resident_bf16(w_ref, w16_scr):
    @pl.when(pl.program_id(0) == 0)
    def _():
        w16_scr[...] = w_ref[...].astype(BF16)


def _resident(a):
    return pl.BlockSpec(a.shape, lambda i: (0,) * a.ndim, pipeline_mode=pl.Buffered(1))


def _out_kernel(oa_ref, or_ref, w_ref, post_ref, h_ref, o_ref, w16_scr):
    _resident_bf16(w_ref, w16_scr)
    for c in range(h_ref.shape[0] // OUT_CHUNK):
        rows = slice(c * OUT_CHUNK, (c + 1) * OUT_CHUNK)
        y = (_dot(oa_ref[rows, :], w16_scr[0:NSA_WIDTH, :])
             + _dot(or_ref[rows, :], w16_scr[NSA_WIDTH:NSA_WIDTH + LRU_WIDTH, :]))
        o_ref[rows, :] = h_ref[rows, :] + _rms(y, post_ref[...])


def _out_proj(o_attn, o_rec, w_out, post_g, h, tm=512):
    n, d = h.shape
    row = lambda i: (i, 0)
    return pl.pallas_call(
        _out_kernel,
        name="out_proj",
        grid=(n // tm,),
        in_specs=[pl.BlockSpec((tm, NSA_WIDTH), row), pl.BlockSpec((tm, LRU_WIDTH), row),
                  _resident(w_out), _resident(post_g), pl.BlockSpec((tm, d), row)],
        out_specs=pl.BlockSpec((tm, d), row),
        out_shape=jax.ShapeDtypeStruct((n, d), F32),
        scratch_shapes=[pltpu.VMEM(w_out.shape, BF16)],
        compiler_params=pltpu.CompilerParams(
            dimension_semantics=("arbitrary",), vmem_limit_bytes=VMEM_LIMIT),
    )(o_attn, o_rec, w_out, post_g, h)


def _ple_kernel(h_ref, p_ref, pre_ref, wg_ref, wp_ref, post_ref, o_ref, wg16_scr, wp16_scr):
    _resident_bf16(wg_ref, wg16_scr)
    _resident_bf16(wp_ref, wp16_scr)
    for c in range(h_ref.shape[0] // OUT_CHUNK):
        rows = slice(c * OUT_CHUNK, (c + 1) * OUT_CHUNK)
        h = h_ref[rows, :]
        gate = jax.nn.sigmoid(_dot(_rms(h, pre_ref[...]).astype(BF16), wg16_scr[...]))
        pp = _dot(p_ref[rows, :].astype(BF16), wp16_scr[...])
        o_ref[rows, :] = h + _rms(gate * pp, post_ref[...])


def _ple(h, p, pre_g, w_gate, w_proj, post_g, tm=512):
    n, d = h.shape
    row = lambda i: (i, 0)
    return pl.pallas_call(
        _ple_kernel,
        name="ple",
        grid=(n // tm,),
        in_specs=[pl.BlockSpec((tm, d), row), pl.BlockSpec((tm, p.shape[1]), row),
                  _resident(pre_g), _resident(w_gate), _resident(w_proj), _resident(post_g)],
        out_specs=pl.BlockSpec((tm, d), row),
        out_shape=jax.ShapeDtypeStruct((n, d), F32),
        scratch_shapes=[pltpu.VMEM(w_gate.shape, BF16), pltpu.VMEM(w_proj.shape, BF16)],
        compiler_params=pltpu.CompilerParams(
            dimension_semantics=("arbitrary",), vmem_limit_bytes=VMEM_LIMIT),
    )(h, p, pre_g, w_gate, w_proj, post_g)


def _selection_constants(seq):
    ncp = seq // CMP_STRIDE
    n_cmp = (seq - CMP_LEN) // CMP_STRIDE + 1
    n_sel = seq // SEL_LEN
    c = np.arange(ncp)[None, :]
    s = np.arange(n_sel)[:, None]
    ovl_t = ((c * CMP_STRIDE < s * SEL_LEN + SEL_LEN) & (c * CMP_STRIDE + CMP_LEN - 1 >= s * SEL_LEN) & (c < n_cmp))
    onehot = (np.arange(seq)[:, None] // SEL_LEN) == np.arange(LANE)[None, :]
    return jnp.asarray(ovl_t, BF16), jnp.asarray(onehot * MASK_BIG, BF16)


def _pad_cols(a, width):
    return jnp.pad(a, ((0, 0), (0, width - a.shape[1])))


def kernel(x, p, positions, ff1_pre_g, ff1_post_g, ff1_w_gate, ff1_w_up, ff1_w_down, mix_pre_g, mix_post_g, w_in, cmp_pos_k, cmp_pos_v, cmp_k_w1, cmp_k_w2, cmp_v_w1, cmp_v_w2, nsa_gate_b, conv_w, conv_b, rg_w_a, rg_b_a, rg_w_i, rg_b_i, rg_lambda, attn_out_g, rec_out_g, w_out, ff2_pre_g, ff2_post_g, ff2_w_gate, ff2_w_up, ff2_w_down, ple_pre_g, ple_post_g, w_ple_gate, w_ple_proj):
    batch, seq, d = x.shape
    depth = p.shape[0]
    n = batch * seq
    vec = lambda a: a.reshape(1, -1)
    gate_cols = NSA_HEADS * N_GATES
    grp_gates = GQA_GROUP * N_GATES
    o_q, o_kc, o_vc, o_ks, o_vs, o_kw, o_vw = (NSA_WIDTH * 0,) + tuple(NSA_WIDTH + k * KV_WIDTH for k in range(6))
    o_g = NSA_WIDTH + 6 * KV_WIDTH
    o_x = o_g + gate_cols
    o_y = o_x + LRU_WIDTH

    half = jnp.arange(ROPE_HALF, dtype=F32)
    inv_freq = ROPE_THETA ** (-half / ROPE_HALF)
    invf = jnp.tile(jnp.concatenate([inv_freq, inv_freq]), ROPE_PACK).reshape(1, LANE)
    ovl_t, onehot = _selection_constants(seq)
    sub = PROJ_CHUNK // ROPE_PACK
    pos = positions.reshape(n // PROJ_CHUNK, ROPE_PACK, sub).transpose(0, 2, 1).reshape(n // ROPE_PACK, ROPE_PACK)
    pos = jnp.repeat(pos, ROPE_DIM, axis=1)

    h = x.reshape(n, d)
    for i in range(depth):
        wi = w_in[i]
        cols = lambda o, w: wi[:, o:o + w]
        w_in_z = jnp.concatenate(
            [cols(o_q, NSA_WIDTH), cols(o_ks, KV_WIDTH), cols(o_kw, KV_WIDTH), cols(o_vs, KV_WIDTH),
             cols(o_vw, KV_WIDTH), cols(o_x, LRU_WIDTH), cols(o_y, LRU_WIDTH), cols(o_kc, KV_WIDTH),
             cols(o_vc, KV_WIDTH)]
            + [_pad_cols(cols(o_g + g * grp_gates, grp_gates), LANE) for g in range(NSA_KV_HEADS)]
            + [jnp.zeros((d, B_WIDTH - (BT_GATE + NSA_KV_HEADS) * LANE), F32)],
            axis=1).astype(BF16)
        gb = nsa_gate_b[i].reshape(NSA_KV_HEADS, grp_gates)
        gate_bias = _pad_cols(jnp.pad(gb, ((0, 0), (0, LANE - grp_gates))).reshape(1, NSA_KV_HEADS * LANE), PROJ_TN)

        h = _ffn(h, vec(ff1_pre_g[i]), ff1_w_gate[i], ff1_w_up[i], ff1_w_down[i], vec(ff1_post_g[i]))
        za, zb = _proj(pos, h, vec(mix_pre_g[i]), w_in_z, gate_bias, invf)
        kcv = _compress(
            zb, jnp.stack([cmp_pos_k[i], cmp_pos_v[i]]),
            jnp.stack([cmp_k_w1[i], cmp_v_w1[i]]).reshape(2, CMP_LEN, HEAD_DIM, -1).astype(BF16),
            jnp.stack([cmp_k_w2[i], cmp_v_w2[i]]).astype(BF16), batch, seq)
        o_attn = _attention(za, zb, kcv, ovl_t, onehot, vec(attn_out_g[i]), batch, seq)
        o_rec = _rglru(zb, conv_w[i], vec(conv_b[i]), rg_w_a[i].astype(BF16), vec(rg_b_a[i]),
                       rg_w_i[i].astype(BF16), vec(rg_b_i[i]), vec(rg_lambda[i]), vec(rec_out_g[i]), batch, seq)
        h = _out_proj(o_attn, o_rec, w_out[i], vec(mix_post_g[i]), h)
        h = _ffn(h, vec(ff2_pre_g[i]), ff2_w_gate[i], ff2_w_up[i], ff2_w_down[i], vec(ff2_post_g[i]))
        h = _ple(h, p[i].reshape(n, -1), vec(ple_pre_g[i]), w_ple_gate[i], w_ple_proj[i], vec(ple_post_g[i]))
    return h.reshape(batch, seq, d)
```

```python
import functools

import numpy as np
import jax
import jax.numpy as jnp
from jax import lax
from jax.experimental import pallas as pl
from jax.experimental.pallas import tpu as pltpu

F32 = jnp.float32
BF16 = jnp.bfloat16

D_MODEL = 2048
NSA_HEADS = 8
NSA_KV_HEADS = 2
GQA_GROUP = NSA_HEADS // NSA_KV_HEADS
HEAD_DIM = 128
NSA_WIDTH = NSA_HEADS * HEAD_DIM
KV_WIDTH = NSA_KV_HEADS * HEAD_DIM
ROPE_DIM = HEAD_DIM // 4
ROPE_HALF = ROPE_DIM // 2
ROPE_THETA = 500000.0
CMP_LEN = 32
CMP_STRIDE = 16
SEL_LEN = 64
SEL_TOPK = 16
WINDOW = 512
Q_BLOCK = 128
N_GATES = 3
LRU_WIDTH = 1024
LRU_BLOCKS = 8
LRU_BLOCK_DIM = LRU_WIDTH // LRU_BLOCKS
CONV_WIDTH = 4
LRU_C = 8.0
RMS_EPS = 1e-6
NEG = -1e30
SEL_FORCE = 1e4

LANE = 128
SUBLANE = 8
VMEM_LIMIT = 62 * 1024 * 1024

AT_Q, AT_KS, AT_KW, AT_VS, AT_VW = 0, 8, 10, 12, 14
A_WIDTH = 16 * LANE
BT_X, BT_Y, BT_KC, BT_VC, BT_GATE = 0, 8, 16, 18, 20
B_WIDTH = 24 * LANE
PROJ_TN = 512
PROJ_A_STEPS = A_WIDTH // PROJ_TN
PROJ_STEPS = (A_WIDTH + B_WIDTH) // PROJ_TN
PROJ_HEADS = PROJ_TN // LANE
ROW_CHUNK = 256
PROJ_CHUNK = 2 * ROW_CHUNK
ROPE_PACK = LANE // ROPE_DIM
FFN_CHUNK = 4 * ROW_CHUNK
OUT_CHUNK = 2 * ROW_CHUNK

Q_SCALE = HEAD_DIM ** -0.5 * 1.4426950408889634
SEL_KT = 512
MASK_BIG = 2.0 ** 100


def _rms(x, g):
    return x * lax.rsqrt(jnp.mean(x * x, axis=-1, keepdims=True) + RMS_EPS) * g


def _dot(a, b):
    return jnp.dot(a, b, preferred_element_type=F32)


def _dot_nt(a, b):
    return lax.dot_general(a, b, (((1,), (1,)), ((), ())), preferred_element_type=F32)


def _ffn_steps(j, n_steps, h_ref, pre_ref, post_ref, o_ref, u_scr, weights):
    def step(first, last):
        wg, wu, wd = weights()
        for c in range(h_ref.shape[0] // FFN_CHUNK):
            rows = slice(c * FFN_CHUNK, (c + 1) * FFN_CHUNK)
            if first:
                u = _rms(h_ref[rows, :], pre_ref[...]).astype(BF16)
                u_scr[rows, :] = u
            else:
                u = u_scr[rows, :]
            g = _dot(u, wg)
            up = _dot(u, wu)
            acc = _dot(((g * jax.nn.sigmoid(g)) * up).astype(BF16), wd)
            if not first:
                acc = o_ref[rows, :] + acc
            if last:
                acc = h_ref[rows, :] + 0.5 * _rms(acc, post_ref[...])
            o_ref[rows, :] = acc

    if n_steps == 1:
        step(True, True)
    else:
        pl.when(j == 0)(lambda: step(True, False))
        pl.when((j > 0) & (j < n_steps - 1))(lambda: step(False, False))
        pl.when(j == n_steps - 1)(lambda: step(False, True))


def _ffn_head_kernel(h_ref, pre_ref, wg_ref, wu_ref, wd_ref, post_ref, o_ref, wg16_ref, wu16_ref, wd16_ref,
                     u_scr, *, n_steps):
    def weights():
        w16 = []
        for src, dst in ((wg_ref, wg16_ref), (wu_ref, wu16_ref), (wd_ref, wd16_ref)):
            w = src[...].astype(BF16)
            dst[...] = w
            w16.append(w)
        return w16

    _ffn_steps(pl.program_id(0), n_steps, h_ref, pre_ref, post_ref, o_ref, u_scr, weights)


def _ffn_tail_kernel(h_ref, head_ref, pre_ref, wg_ref, wu_ref, wd_ref, post_ref, o_ref, u_scr, *,
                     n_steps, n_copy):
    i, j = pl.program_id(0), pl.program_id(1)
    slab = head_ref.shape[0]

    @pl.when((i == 0) & (j < n_copy))
    def _():
        o_ref[pl.ds(pl.multiple_of(j * slab, slab), slab), :] = head_ref[...]

    @pl.when(i > 0)
    def _():
        _ffn_steps(j, n_steps, h_ref, pre_ref, post_ref, o_ref, u_scr,
                   lambda: (wg_ref[...], wu_ref[...], wd_ref[...]))


def _ffn(h, pre_g, wg, wu, wd, post_g, tm=1024, tf=512, tf_head=256):
    n, d = h.shape
    dff = wg.shape[1]
    tf, tf_head = min(tf, dff), min(tf_head, dff)
    vec1 = pl.BlockSpec((1, d), lambda j: (0, 0))
    head, wg16, wu16, wd16 = pl.pallas_call(
        functools.partial(_ffn_head_kernel, n_steps=dff // tf_head),
        name="ffn_head",
        grid=(dff // tf_head,),
        in_specs=[
            pl.BlockSpec((tm, d), lambda j: (0, 0)), vec1,
            pl.BlockSpec((d, tf_head), lambda j: (0, j)),
            pl.BlockSpec((d, tf_head), lambda j: (0, j)),
            pl.BlockSpec((tf_head, d), lambda j: (j, 0)),
            vec1,
        ],
        out_specs=[pl.BlockSpec((tm, d), lambda j: (0, 0)),
                   pl.BlockSpec((d, tf_head), lambda j: (0, j)),
                   pl.BlockSpec((d, tf_head), lambda j: (0, j)),
                   pl.BlockSpec((tf_head, d), lambda j: (j, 0))],
        out_shape=[jax.ShapeDtypeStruct((tm, d), F32), jax.ShapeDtypeStruct((d, dff), BF16),
                   jax.ShapeDtypeStruct((d, dff), BF16), jax.ShapeDtypeStruct((dff, d), BF16)],
        scratch_shapes=[pltpu.VMEM((tm, d), BF16)],
        compiler_params=pltpu.CompilerParams(
            dimension_semantics=("arbitrary",), vmem_limit_bytes=VMEM_LIMIT),
    )(h, pre_g, wg, wu, wd, post_g)

    n_steps = dff // tf
    n_copy = 1 << (min(n_steps, tm // LANE).bit_length() - 1)
    row = lambda i, j: (i, 0)
    vec = pl.BlockSpec((1, d), lambda i, j: (0, 0))
    wcol = lambda i, j: (0, jnp.where(i > 0, j, 0))
    return pl.pallas_call(
        functools.partial(_ffn_tail_kernel, n_steps=n_steps, n_copy=n_copy),
        name="ffn_tail",
        grid=(n // tm, n_steps),
        in_specs=[
            pl.BlockSpec((tm, d), row),
            pl.BlockSpec((tm // n_copy, d), lambda i, j: (jnp.where(i == 0, jnp.minimum(j, n_copy - 1), n_copy - 1), 0)),
            vec,
            pl.BlockSpec((d, tf), wcol),
            pl.BlockSpec((d, tf), wcol),
            pl.BlockSpec((tf, d), lambda i, j: (jnp.where(i > 0, j, 0), 0)),
            vec,
        ],
        out_specs=pl.BlockSpec((tm, d), row),
        out_shape=jax.ShapeDtypeStruct((n, d), F32),
        scratch_shapes=[pltpu.VMEM((tm, d), BF16)],
        compiler_params=pltpu.CompilerParams(
            dimension_semantics=("arbitrary", "arbitrary"), vmem_limit_bytes=VMEM_LIMIT),
    )(h, head, pre_g, wg16, wu16, wd16, post_g)


def _proj_kernel(pos_ref, h_ref, pre_ref, w_ref, gb_ref, invf_ref, za_ref, zb_ref, u_scr, cos_scr, s1_scr, s2_scr):
    j = pl.program_id(1)

    def prepare(rows):
        u_scr[rows, :] = _rms(h_ref[rows, :], pre_ref[...]).astype(BF16)
        sub = PROJ_CHUNK // ROPE_PACK
        packed = slice(rows.start // ROPE_PACK, rows.start // ROPE_PACK + sub)
        ang = pos_ref[packed, :].astype(F32) * invf_ref[...]
        cos, sin = jnp.cos(ang), jnp.sin(ang)
        lane = lax.broadcasted_iota(jnp.int32, ang.shape, 1)
        for k in range(ROPE_PACK):
            grp = slice(rows.start + k * sub, rows.start + (k + 1) * sub)
            c, s = (cos, sin) if k == 0 else (pltpu.roll(cos, LANE - k * ROPE_DIM, axis=1),
                                              pltpu.roll(sin, LANE - k * ROPE_DIM, axis=1))
            cos_scr[grp, :] = jnp.where(lane < ROPE_DIM, c, 1.0)
            s1_scr[grp, :] = jnp.where((lane >= ROPE_HALF) & (lane < ROPE_DIM), s, 0.0)
            s2_scr[grp, :] = jnp.where(lane < ROPE_HALF, -s, 0.0)

    def head(z, rows, hd, rope, mul=None):
        x = z[:, hd * LANE:(hd + 1) * LANE]
        if rope:
            x = (x * cos_scr[rows, :]
                 + pltpu.roll(x, ROPE_HALF, axis=1) * s1_scr[rows, :]
                 + pltpu.roll(x, LANE - ROPE_HALF, axis=1) * s2_scr[rows, :])
        return x if mul is None else x * mul

    def chunks(first=False):
        for c in range(u_scr.shape[0] // PROJ_CHUNK):
            rows = slice(c * PROJ_CHUNK, (c + 1) * PROJ_CHUNK)
            if first:
                prepare(rows)
            yield rows, _dot(u_scr[rows, :], w_ref[...])

    def store(ref, rope_heads, mul=None, first=False):
        for rows, z in chunks(first):
            for hd in range(PROJ_HEADS):
                ref[rows, hd * LANE:(hd + 1) * LANE] = head(z, rows, hd, hd < rope_heads, mul).astype(ref.dtype)

    @pl.when(j == 0)
    def _():
        store(za_ref, PROJ_HEADS, Q_SCALE, first=True)

    @pl.when((j > 0) & (j < AT_KS // PROJ_HEADS))
    def _():
        store(za_ref, PROJ_HEADS, Q_SCALE)

    @pl.when(j == AT_KS // PROJ_HEADS)
    def _():
        store(za_ref, PROJ_HEADS)

    @pl.when(j == AT_VS // PROJ_HEADS)
    def _():
        store(za_ref, 0)

    @pl.when((j >= PROJ_A_STEPS) & (j < PROJ_A_STEPS + BT_KC // PROJ_HEADS))
    def _():
        store(zb_ref, 0)

    @pl.when(j == PROJ_A_STEPS + BT_KC // PROJ_HEADS)
    def _():
        store(zb_ref, NSA_KV_HEADS)

    @pl.when(j == PROJ_A_STEPS + BT_GATE // PROJ_HEADS)
    def _():
        for rows, z in chunks():
            zb_ref[rows, :] = jax.nn.sigmoid(z + gb_ref[...])


def _proj(pos, h, pre_g, w, gb, invf, tm=1024):
    n, d = h.shape
    return pl.pallas_call(
        _proj_kernel,
        name="mix_proj",
        grid=(n // tm, PROJ_STEPS),
        in_specs=[
            pl.BlockSpec((tm // ROPE_PACK, LANE), lambda i, j: (i, 0)),
            pl.BlockSpec((tm, d), lambda i, j: (i, 0)),
            pl.BlockSpec((1, d), lambda i, j: (0, 0)),
            pl.BlockSpec((d, PROJ_TN), lambda i, j: (0, j)),
            pl.BlockSpec((1, PROJ_TN), lambda i, j: (0, 0)),
            pl.BlockSpec((1, LANE), lambda i, j: (0, 0)),
        ],
        out_specs=[pl.BlockSpec((tm, PROJ_TN), lambda i, j: (i, jnp.minimum(j, PROJ_A_STEPS - 1))),
                   pl.BlockSpec((tm, PROJ_TN), lambda i, j: (i, jnp.maximum(j - PROJ_A_STEPS, 0)))],
        out_shape=[jax.ShapeDtypeStruct((n, A_WIDTH), BF16), jax.ShapeDtypeStruct((n, B_WIDTH), F32)],
        scratch_shapes=[pltpu.VMEM((tm, d), BF16)] + [pltpu.VMEM((tm, LANE), F32)] * 3,
        compiler_params=pltpu.CompilerParams(
            dimension_semantics=("arbitrary", "arbitrary"), vmem_limit_bytes=VMEM_LIMIT),
    )(pos, h, pre_g, w, gb, invf)


def _cmp_kernel(x_ref, pos_ref, w1_ref, w2_ref, o_ref):
    ncp = o_ref.shape[0]
    xs = [x_ref[pl.ds(l, ncp, stride=CMP_STRIDE), :] for l in range(CMP_STRIDE)]
    half = lambda h: jnp.concatenate(
        [(x + pos_ref[h * CMP_STRIDE + l:h * CMP_STRIDE + l + 1, :]).astype(BF16) for l, x in enumerate(xs)], axis=1)
    acc_a = _dot(half(0), w1_ref[0])
    acc_b = _dot(half(1), w1_ref[1])
    hidden = acc_a + pltpu.roll(acc_b, ncp - 1, axis=0)
    o_ref[...] = _dot(jax.nn.gelu(hidden).astype(BF16), w2_ref[...])


def _compress(zb, pos_kv, w1, w2, batch, seq):
    ncp = seq // CMP_STRIDE
    hidden_w = w1.shape[-1]
    return pl.pallas_call(
        _cmp_kernel,
        name="compress",
        grid=(2, batch, NSA_KV_HEADS),
        in_specs=[
            pl.BlockSpec((seq, LANE), lambda k, b, g: (b, BT_KC + 2 * k + g)),
            pl.BlockSpec((None, CMP_LEN, HEAD_DIM), lambda k, b, g: (k, 0, 0)),
            pl.BlockSpec((None, CMP_LEN // CMP_STRIDE, CMP_STRIDE * HEAD_DIM, hidden_w), lambda k, b, g: (k, 0, 0, 0)),
            pl.BlockSpec((None, hidden_w, HEAD_DIM), lambda k, b, g: (k, 0, 0)),
        ],
        out_specs=pl.BlockSpec((None, None, None, ncp, HEAD_DIM), lambda k, b, g: (k, b, g, 0, 0)),
        out_shape=jax.ShapeDtypeStruct((2, batch, NSA_KV_HEADS, ncp, HEAD_DIM), F32),
        compiler_params=pltpu.CompilerParams(
            dimension_semantics=("arbitrary",) * 3, vmem_limit_bytes=VMEM_LIMIT),
    )(zb, pos_kv, w1, w2)


def _masked_exp(s, mask):
    s = s + jnp.where(mask, 0.0, -MASK_BIG)
    e = jnp.exp2(s - jnp.max(s, axis=-1, keepdims=True))
    return e, jnp.sum(e, axis=-1, keepdims=True)


def _attn_kernel(q_ref, kc_ref, vc_ref, ks_ref, vs_ref, kw_ref, vw_ref, gz_ref, ovl_ref, oh_ref, ga_ref, prev_ref,
                 o_ref, *, first_block):
    del prev_ref
    G, R, tq = NSA_KV_HEADS, GQA_GROUP, Q_BLOCK
    n_keys = ks_ref.shape[0]
    top_k = min(SEL_TOPK, ovl_ref.shape[0])
    n_blk = n_keys // SEL_LEN
    n_cmp = min(kc_ref.shape[1], -(-(n_keys // CMP_STRIDE) // LANE) * LANE)
    t0 = (first_block + pl.program_id(1)) * tq
    t_idx = t0 + lax.broadcasted_iota(jnp.int32, (1, tq, 1), 1)
    gsl = lambda g: slice(g * HEAD_DIM, (g + 1) * HEAD_DIM)

    def stacked_q(g):
        q = q_ref[:, g * R * HEAD_DIM:(g + 1) * R * HEAD_DIM]
        return jnp.concatenate([q[:, r * HEAD_DIM:(r + 1) * HEAD_DIM] for r in range(R)], axis=0)

    q4s = [stacked_q(g) for g in range(G)]

    wk = min(WINDOW + tq, n_keys)
    w0 = pl.multiple_of(jnp.maximum(t0 - WINDOW, 0), tq)
    kp = w0 + lax.broadcasted_iota(jnp.int32, (1, 1, wk), 2)
    win_mask = (kp <= t_idx) & (kp > t_idx - WINDOW)

    ones_col = jnp.ones((n_keys, LANE), BF16)

    def weights(s, m):
        return jnp.exp2((s - m).astype(BF16)).reshape(R * tq, s.shape[-1])

    def pv(p, v, ones):
        return _dot(p, jnp.concatenate([v, ones], axis=1))

    def normalised(acc):
        return (acc[:, :HEAD_DIM] * (1.0 / acc[:, HEAD_DIM:])).reshape(R, tq, HEAD_DIM)

    def window(g):
        s = _dot_nt(q4s[g], kw_ref[pl.ds(w0, wk), gsl(g)]).reshape(R, tq, wk)
        s = s + jnp.where(win_mask, 0.0, -MASK_BIG)
        p = weights(s, jnp.max(s, axis=-1, keepdims=True))
        return normalised(pv(p, vw_ref[pl.ds(w0, wk), gsl(g)], ones_col[0:wk, :]))

    def front(g):
        q4 = q4s[g]

        s = _dot_nt(q4, kc_ref[g, 0:n_cmp, :].astype(BF16)).reshape(R, tq, n_cmp)
        c_idx = lax.broadcasted_iota(jnp.int32, (1, 1, n_cmp), 2)
        e, l = _masked_exp(s, (c_idx * CMP_STRIDE + (CMP_LEN - 1)) <= t_idx)
        pc = e * (jnp.where(t_idx >= CMP_LEN - 1, 1.0, 0.0) / l)
        o_cmp = _dot(pc.reshape(R * tq, n_cmp).astype(BF16), vc_ref[g, 0:n_cmp, :].astype(BF16))
        o_cmp = o_cmp.reshape(R, tq, HEAD_DIM)
        if n_blk <= top_k:
            return q4, jnp.concatenate([q4, jnp.zeros_like(q4)], axis=1), o_cmp

        pcs = pc[0]
        for r in range(1, R):
            pcs = pcs + pc[r]
        hi = pcs.astype(BF16)
        lo = (pcs - hi.astype(F32)).astype(BF16)
        ovl = ovl_ref[0:n_blk, 0:n_cmp]
        imp = _dot_nt(ovl, hi) + _dot_nt(ovl, lo)
        blk = lax.broadcasted_iota(jnp.int32, (n_blk, tq), 0)
        tcol = t0 + lax.broadcasted_iota(jnp.int32, (n_blk, tq), 1)
        forced = (blk == lax.shift_right_logical(tcol, int(np.log2(SEL_LEN)))) | (blk == 0)
        v = jnp.where(forced, SEL_FORCE, jnp.where(blk * SEL_LEN <= tcol, imp, -SEL_FORCE))
        sub = lax.broadcasted_iota(jnp.int32, (SUBLANE, tq), 0)
        groups = [v[k:k + SUBLANE, :] for k in range(0, n_blk, SUBLANE)]
        ranks = [jnp.zeros((SUBLANE, tq), F32) for _ in groups]
        for sp in range(n_blk):
            other = v[sp:sp + 1, :]
            for gi, vg in enumerate(groups):
                first = gi * SUBLANE
                if first > sp:
                    beats = other >= vg
                elif first + SUBLANE - 1 <= sp:
                    beats = other > vg
                else:
                    beats = (other > vg) | ((other == vg) & (sub > sp - first))
                ranks[gi] = ranks[gi] + jnp.where(beats, 1.0, 0.0)
        rank = jnp.concatenate(ranks, axis=0)
        unsel = jnp.where(rank < float(top_k), 0.0, -1.0)
        unsel = jnp.concatenate([unsel, jnp.zeros((LANE - n_blk, tq), F32)], axis=0).T.astype(BF16)
        q_aug = jnp.concatenate([q4, jnp.concatenate([unsel] * R, axis=0)], axis=1)
        return q4, q_aug, o_cmp

    fronts = [front(g) for g in range(G)]

    n_full = n_keys - SEL_KT
    kp = n_full + lax.broadcasted_iota(jnp.int32, (1, 1, SEL_KT), 2)
    causal_bias = jnp.where(kp <= t_idx, 0.0, -MASK_BIG)

    def selected(g):
        k_aug = jnp.concatenate([ks_ref[:, gsl(g)], oh_ref[...]], axis=1)
        s = _dot_nt(fronts[g][1], k_aug).reshape(R, tq, n_keys)
        s_last = s[:, :, n_full:] + causal_bias
        m = jnp.max(s_last, axis=-1, keepdims=True)
        if n_full:
            s_full = s[:, :, :n_full]
            m = jnp.maximum(m, jnp.max(s_full, axis=-1, keepdims=True))
        acc = pv(weights(s_last, m), vs_ref[n_full:n_keys, gsl(g)], ones_col[n_full:n_keys, :])
        if n_full:
            acc = acc + pv(weights(s_full, m), vs_ref[0:n_full, gsl(g)], ones_col[0:n_full, :])
        return normalised(acc)

    o_sels = [selected(g) for g in range(G)]
    o_wins = [window(g) for g in range(G)]
    heads = []
    for g in range(G):
        o_cmp, o_sel, o_win = fronts[g][2], o_sels[g], o_wins[g]
        gz = gz_ref[:, gsl(g)]
        for r in range(R):
            c = r * N_GATES
            heads.append(gz[:, c:c + 1] * o_cmp[r] + gz[:, c + 1:c + 2] * o_sel[r] + gz[:, c + 2:c + 3] * o_win[r])
    o_ref[...] = _rms(jnp.concatenate(heads, axis=1), ga_ref[...]).astype(o_ref.dtype)


def _attention(za, zb, kcv, ovl_t, onehot, g_attn, batch, seq):
    ncp = seq // CMP_STRIDE
    n_sel = seq // SEL_LEN
    G = NSA_KV_HEADS
    blocks = SEL_KT // Q_BLOCK
    za3 = za.reshape(batch, seq, A_WIDTH)
    zb3 = zb.reshape(batch, seq, B_WIDTH)
    out = jnp.zeros((batch, seq, NSA_WIDTH), BF16)
    for c in range(seq // SEL_KT):
        n_keys = (c + 1) * SEL_KT
        q_row = lambda b, i, c=c: (b, c * blocks + i, 0)
        kv_spec = lambda tile: pl.BlockSpec((None, n_keys, G * LANE), lambda b, i: (b, 0, tile // G))
        operands = (za3, kcv, kcv, za3, za3, za3, za3, zb3, ovl_t, onehot, g_attn, out)
        out = pl.pallas_call(
            functools.partial(_attn_kernel, first_block=c * blocks),
            name=f"attention_{c}",
            grid=(batch, blocks),
            in_specs=[
                pl.BlockSpec((None, Q_BLOCK, NSA_WIDTH), q_row),
                pl.BlockSpec((None, None, G, ncp, HEAD_DIM), lambda b, i: (0, b, 0, 0, 0)),
                pl.BlockSpec((None, None, G, ncp, HEAD_DIM), lambda b, i: (1, b, 0, 0, 0)),
                kv_spec(AT_KS), kv_spec(AT_VS), kv_spec(AT_KW), kv_spec(AT_VW),
                pl.BlockSpec((None, Q_BLOCK, G * LANE), lambda b, i, c=c: (b, c * blocks + i, BT_GATE // G)),
                pl.BlockSpec((n_sel, ncp), lambda b, i: (0, 0)),
                pl.BlockSpec((n_keys, LANE), lambda b, i: (0, 0)),
                pl.BlockSpec((1, NSA_WIDTH), lambda b, i: (0, 0)),
                pl.BlockSpec(memory_space=pl.ANY),
            ],
            out_specs=pl.BlockSpec((None, Q_BLOCK, NSA_WIDTH), q_row),
            out_shape=jax.ShapeDtypeStruct((batch, seq, NSA_WIDTH), BF16),
            input_output_aliases={len(operands) - 1: 0},
            compiler_params=pltpu.CompilerParams(
                dimension_semantics=("arbitrary",) * 2, vmem_limit_bytes=VMEM_LIMIT),
        )(*operands)
    return out.reshape(batch * seq, NSA_WIDTH)


def _lru_kernel(zx_ref, zy_ref, cw_ref, cb_ref, wa_ref, ba_ref, wi_ref, bi_ref, lam_ref, gr_ref,
                o_ref, xpad, a_scr, u_scr, h_scr, carry):
    tt, width = zx_ref.shape
    ti = pl.program_id(1)

    @pl.when(ti == 0)
    def _():
        xpad[0:SUBLANE, :] = jnp.zeros((SUBLANE, width), F32)
        carry[...] = jnp.zeros_like(carry)

    xb = zx_ref[...]
    xpad[SUBLANE:SUBLANE + tt, :] = xb
    xc = cb_ref[...] + cw_ref[CONV_WIDTH - 1:CONV_WIDTH, :] * xb
    for k in range(1, CONV_WIDTH):
        xc = xc + cw_ref[CONV_WIDTH - 1 - k:CONV_WIDTH - k, :] * xpad[SUBLANE - k:SUBLANE - k + tt, :]
    xpad[0:SUBLANE, :] = xb[tt - SUBLANE:tt, :]

    sp = jax.nn.softplus(-lam_ref[...])
    row8 = lax.broadcasted_iota(jnp.int32, (1, SUBLANE, LRU_BLOCK_DIM), 1)
    for hb in range(LRU_BLOCKS):
        sl = slice(hb * LRU_BLOCK_DIM, (hb + 1) * LRU_BLOCK_DIM)
        xs = xc[:, sl]
        xs16 = xs.astype(BF16)
        r = jax.nn.sigmoid(_dot(xs16, wa_ref[hb]) + ba_ref[:, sl])
        ig = jax.nn.sigmoid(_dot(xs16, wi_ref[hb]) + bi_ref[:, sl])
        log_a = -LRU_C * r * sp[:, sl]
        a = jnp.exp(log_a)
        u = jnp.sqrt(-jnp.tanh(log_a) * (a * a + 1.0)) * (ig * xs)
        a = a.reshape(tt // SUBLANE, SUBLANE, LRU_BLOCK_DIM)
        u = u.reshape(tt // SUBLANE, SUBLANE, LRU_BLOCK_DIM)
        for s in (1, 2, 4):
            ok = row8 >= s
            a_sh = pltpu.roll(a, s, axis=1)
            u_sh = pltpu.roll(u, s, axis=1)
            u = jnp.where(ok, u + a * u_sh, u)
            a = jnp.where(ok, a * a_sh, a)
        a_scr[:, sl] = a.reshape(tt, LRU_BLOCK_DIM)
        u_scr[:, sl] = u.reshape(tt, LRU_BLOCK_DIM)

    def group(gi, c):
        r0 = pl.multiple_of(gi * SUBLANE, SUBLANE)
        h = u_scr[pl.ds(r0, SUBLANE), :] + a_scr[pl.ds(r0, SUBLANE), :] * c
        h_scr[pl.ds(r0, SUBLANE), :] = h
        return jnp.broadcast_to(h[SUBLANE - 1:SUBLANE, :], (SUBLANE, width))

    c = lax.fori_loop(0, tt // SUBLANE, group, carry[...], unroll=4)
    carry[...] = c
    o_ref[...] = _rms(h_scr[...] * jax.nn.gelu(zy_ref[...]), gr_ref[...]).astype(o_ref.dtype)


def _rglru(zb, conv_w, conv_b, w_a, b_a, w_i, b_i, lam, g_rec, batch, seq, tt=256):
    nt = seq // tt
    wt = LRU_WIDTH // LANE
    full = lambda a: pl.BlockSpec(a.shape, lambda b, t: (0,) * a.ndim)
    zspec = lambda tile: pl.BlockSpec((tt, LRU_WIDTH), lambda b, t: (b * nt + t, tile // wt))
    return pl.pallas_call(
        _lru_kernel,
        name="rglru",
        grid=(batch, nt),
        in_specs=[zspec(BT_X), zspec(BT_Y), full(conv_w), full(conv_b), full(w_a), full(b_a),
                  full(w_i), full(b_i), full(lam), full(g_rec)],
        out_specs=pl.BlockSpec((tt, LRU_WIDTH), lambda b, t: (b * nt + t, 0)),
        out_shape=jax.ShapeDtypeStruct((batch * seq, LRU_WIDTH), BF16),
        scratch_shapes=[pltpu.VMEM((tt + SUBLANE, LRU_WIDTH), F32), pltpu.VMEM((tt, LRU_WIDTH), F32),
                        pltpu.VMEM((tt, LRU_WIDTH), F32), pltpu.VMEM((tt, LRU_WIDTH), F32),
                        pltpu.VMEM((SUBLANE, LRU_WIDTH), F32)],
        compiler_params=pltpu.CompilerParams(
            dimension_semantics=("arbitrary", "arbitrary"), vmem_limit_bytes=VMEM_LIMIT),
    )(zb, zb, conv_w, conv_b, w_a, b_a, w_i, b_i, lam, g_rec)


def _resident_bf16(w_ref, w16_scr):
    @pl.when(pl.program_id(0) == 0)
    def _():
        w16_scr[...] = w_ref[...].astype(BF16)


def _resident(a):
    return pl.BlockSpec(a.shape, lambda i: (0,) * a.ndim, pipeline_mode=pl.Buffered(1))


def _out_kernel(oa_ref, or_ref, w_ref, post_ref, h_ref, o_ref, w16_scr):
    _resident_bf16(w_ref, w16_scr)
    for c in range(h_ref.shape[0] // OUT_CHUNK):
        rows = slice(c * OUT_CHUNK, (c + 1) * OUT_CHUNK)
        y = (_dot(oa_ref[rows, :], w16_scr[0:NSA_WIDTH, :])
             + _dot(or_ref[rows, :], w16_scr[NSA_WIDTH:NSA_WIDTH + LRU_WIDTH, :]))
        o_ref[rows, :] = h_ref[rows, :] + _rms(y, post_ref[...])


def _out_proj(o_attn, o_rec, w_out, post_g, h, tm=512):
    n, d = h.shape
    row = lambda i: (i, 0)
    return pl.pallas_call(
        _out_kernel,
        name="out_proj",
        grid=(n // tm,),
        in_specs=[pl.BlockSpec((tm, NSA_WIDTH), row), pl.BlockSpec((tm, LRU_WIDTH), row),
                  _resident(w_out), _resident(post_g), pl.BlockSpec((tm, d), row)],
        out_specs=pl.BlockSpec((tm, d), row),
        out_shape=jax.ShapeDtypeStruct((n, d), F32),
        scratch_shapes=[pltpu.VMEM(w_out.shape, BF16)],
        compiler_params=pltpu.CompilerParams(
            dimension_semantics=("arbitrary",), vmem_limit_bytes=VMEM_LIMIT),
    )(o_attn, o_rec, w_out, post_g, h)


def _ple_kernel(h_ref, p_ref, pre_ref, wg_ref, wp_ref, post_ref, o_ref, wg16_scr, wp16_scr):
    _resident_bf16(wg_ref, wg16_scr)
    _resident_bf16(wp_ref, wp16_scr)
    for c in range(h_ref.shape[0] // OUT_CHUNK):
        rows = slice(c * OUT_CHUNK, (c + 1) * OUT_CHUNK)
        h = h_ref[rows, :]
        gate = jax.nn.sigmoid(_dot(_rms(h, pre_ref[...]).astype(BF16), wg16_scr[...]))
        pp = _dot(p_ref[rows, :].astype(BF16), wp16_scr[...])
        o_ref[rows, :] = h + _rms(gate * pp, post_ref[...])


def _ple(h, p, pre_g, w_gate, w_proj, post_g, tm=512):
    n, d = h.shape
    row = lambda i: (i, 0)
    return pl.pallas_call(
        _ple_kernel,
        name="ple",
        grid=(n // tm,),
        in_specs=[pl.BlockSpec((tm, d), row), pl.BlockSpec((tm, p.shape[1]), row),
                  _resident(pre_g), _resident(w_gate), _resident(w_proj), _resident(post_g)],
        out_specs=pl.BlockSpec((tm, d), row),
        out_shape=jax.ShapeDtypeStruct((n, d), F32),
        scratch_shapes=[pltpu.VMEM(w_gate.shape, BF16), pltpu.VMEM(w_proj.shape, BF16)],
        compiler_params=pltpu.CompilerParams(
            dimension_semantics=("arbitrary",), vmem_limit_bytes=VMEM_LIMIT),
    )(h, p, pre_g, w_gate, w_proj, post_g)


def _selection_constants(seq):
    ncp = seq // CMP_STRIDE
    n_cmp = (seq - CMP_LEN) // CMP_STRIDE + 1
    n_sel = seq // SEL_LEN
    c = np.arange(ncp)[None, :]
    s = np.arange(n_sel)[:, None]
    ovl_t = ((c * CMP_STRIDE < s * SEL_LEN + SEL_LEN) & (c * CMP_STRIDE + CMP_LEN - 1 >= s * SEL_LEN) & (c < n_cmp))
    onehot = (np.arange(seq)[:, None] // SEL_LEN) == np.arange(LANE)[None, :]
    return jnp.asarray(ovl_t, BF16), jnp.asarray(onehot * MASK_BIG, BF16)


def _pad_cols(a, width):
    return jnp.pad(a, ((0, 0), (0, width - a.shape[1])))


def kernel(x, p, positions, ff1_pre_g, ff1_post_g, ff1_w_gate, ff1_w_up, ff1_w_down, mix_pre_g, mix_post_g, w_in, cmp_pos_k, cmp_pos_v, cmp_k_w1, cmp_k_w2, cmp_v_w1, cmp_v_w2, nsa_gate_b, conv_w, conv_b, rg_w_a, rg_b_a, rg_w_i, rg_b_i, rg_lambda, attn_out_g, rec_out_g, w_out, ff2_pre_g, ff2_post_g, ff2_w_gate, ff2_w_up, ff2_w_down, ple_pre_g, ple_post_g, w_ple_gate, w_ple_proj):
    batch, seq, d = x.shape
    depth = p.shape[0]
    n = batch * seq
    vec = lambda a: a.reshape(1, -1)
    gate_cols = NSA_HEADS * N_GATES
    grp_gates = GQA_GROUP * N_GATES
    o_q, o_kc, o_vc, o_ks, o_vs, o_kw, o_vw = (NSA_WIDTH * 0,) + tuple(NSA_WIDTH + k * KV_WIDTH for k in range(6))
    o_g = NSA_WIDTH + 6 * KV_WIDTH
    o_x = o_g + gate_cols
    o_y = o_x + LRU_WIDTH

    half = jnp.arange(ROPE_HALF, dtype=F32)
    inv_freq = ROPE_THETA ** (-half / ROPE_HALF)
    invf = jnp.tile(jnp.concatenate([inv_freq, inv_freq]), ROPE_PACK).reshape(1, LANE)
    ovl_t, onehot = _selection_constants(seq)
    sub = PROJ_CHUNK // ROPE_PACK
    pos = positions.reshape(n // PROJ_CHUNK, ROPE_PACK, sub).transpose(0, 2, 1).reshape(n // ROPE_PACK, ROPE_PACK)
    pos = jnp.repeat(pos, ROPE_DIM, axis=1)

    h = x.reshape(n, d)
    for i in range(depth):
        wi = w_in[i]
        cols = lambda o, w: wi[:, o:o + w]
        w_in_z = jnp.concatenate(
            [cols(o_q, NSA_WIDTH), cols(o_ks, KV_WIDTH), cols(o_kw, KV_WIDTH), cols(o_vs, KV_WIDTH),
             cols(o_vw, KV_WIDTH), cols(o_x, LRU_WIDTH), cols(o_y, LRU_WIDTH), cols(o_kc, KV_WIDTH),
             cols(o_vc, KV_WIDTH)]
            + [_pad_cols(cols(o_g + g * grp_gates, grp_gates), LANE) for g in range(NSA_KV_HEADS)]
            + [jnp.zeros((d, B_WIDTH - (BT_GATE + NSA_KV_HEADS) * LANE), F32)],
            axis=1).astype(BF16)
        gb = nsa_gate_b[i].reshape(NSA_KV_HEADS, grp_gates)
        gate_bias = _pad_cols(jnp.pad(gb, ((0, 0), (0, LANE - grp_gates))).reshape(1, NSA_KV_HEADS * LANE), PROJ_TN)

        h = _ffn(h, vec(ff1_pre_g[i]), ff1_w_gate[i], ff1_w_up[i], ff1_w_down[i], vec(ff1_post_g[i]))
        za, zb = _proj(pos, h, vec(mix_pre_g[i]), w_in_z, gate_bias, invf)
        kcv = _compress(
            zb, jnp.stack([cmp_pos_k[i], cmp_pos_v[i]]),
            jnp.stack([cmp_k_w1[i], cmp_v_w1[i]]).reshape(
                2, CMP_LEN // CMP_STRIDE, CMP_STRIDE * HEAD_DIM, -1).astype(BF16),
            jnp.stack([cmp_k_w2[i], cmp_v_w2[i]]).astype(BF16), batch, seq)
        o_attn = _attention(za, zb, kcv, ovl_t, onehot, vec(attn_out_g[i]), batch, seq)
        o_rec = _rglru(zb, conv_w[i], vec(conv_b[i]), rg_w_a[i].astype(BF16), vec(rg_b_a[i]),
                       rg_w_i[i].astype(BF16), vec(rg_b_i[i]), vec(rg_lambda[i]), vec(rec_out_g[i]), batch, seq)
        h = _out_proj(o_attn, o_rec, w_out[i], vec(mix_post_g[i]), h)
        h = _ffn(h, vec(ff2_pre_g[i]), ff2_w_gate[i], ff2_w_up[i], ff2_w_down[i], vec(ff2_post_g[i]))
        h = _ple(h, p[i].reshape(n, -1), vec(ple_pre_g[i]), w_ple_gate[i], w_ple_proj[i], vec(ple_post_g[i]))
    return h.reshape(batch, seq, d)
```

```python
import functools

import numpy as np
import jax
import jax.numpy as jnp
from jax import lax
from jax.experimental import pallas as pl
from jax.experimental.pallas import tpu as pltpu

F32 = jnp.float32
BF16 = jnp.bfloat16

D_MODEL = 2048
NSA_HEADS = 8
NSA_KV_HEADS = 2
GQA_GROUP = NSA_HEADS // NSA_KV_HEADS
HEAD_DIM = 128
NSA_WIDTH = NSA_HEADS * HEAD_DIM
KV_WIDTH = NSA_KV_HEADS * HEAD_DIM
ROPE_DIM = HEAD_DIM // 4
ROPE_HALF = ROPE_DIM // 2
ROPE_THETA = 500000.0
CMP_LEN = 32
CMP_STRIDE = 16
SEL_LEN = 64
SEL_TOPK = 16
WINDOW = 512
Q_BLOCK = 128
N_GATES = 3
LRU_WIDTH = 1024
LRU_BLOCKS = 8
LRU_BLOCK_DIM = LRU_WIDTH // LRU_BLOCKS
CONV_WIDTH = 4
LRU_C = 8.0
RMS_EPS = 1e-6
NEG = -1e30
SEL_FORCE = 1e4

LANE = 128
SUBLANE = 8
VMEM_LIMIT = 62 * 1024 * 1024

AT_Q, AT_KS, AT_KW, AT_VS, AT_VW = 0, 8, 10, 12, 14
A_WIDTH = 16 * LANE
BT_X, BT_Y, BT_KC, BT_VC, BT_GATE = 0, 8, 16, 18, 20
B_WIDTH = 24 * LANE
PROJ_TN = 512
PROJ_A_STEPS = A_WIDTH // PROJ_TN
PROJ_STEPS = (A_WIDTH + B_WIDTH) // PROJ_TN
PROJ_HEADS = PROJ_TN // LANE
ROW_CHUNK = 256
PROJ_CHUNK = ROW_CHUNK // 2
ROPE_PACK = LANE // ROPE_DIM
FFN_CHUNK = 4 * ROW_CHUNK
OUT_CHUNK = 2 * ROW_CHUNK

Q_SCALE = HEAD_DIM ** -0.5 * 1.4426950408889634
SEL_KT = 512
MASK_BIG = 2.0 ** 100


def _rms(x, g):
    return x * lax.rsqrt(jnp.mean(x * x, axis=-1, keepdims=True) + RMS_EPS) * g


def _dot(a, b):
    return jnp.dot(a, b, preferred_element_type=F32)


def _dot_nt(a, b):
    return lax.dot_general(a, b, (((1,), (1,)), ((), ())), preferred_element_type=F32)


def _ffn_steps(j, n_steps, h_ref, pre_ref, post_ref, o_ref, u_scr, weights):
    def step(first, last):
        wg, wu, wd = weights()
        for c in range(h_ref.shape[0] // FFN_CHUNK):
            rows = slice(c * FFN_CHUNK, (c + 1) * FFN_CHUNK)
            if first:
                u = _rms(h_ref[rows, :], pre_ref[...]).astype(BF16)
                u_scr[rows, :] = u
            else:
                u = u_scr[rows, :]
            g = _dot(u, wg)
            up = _dot(u, wu)
            acc = _dot(((g * jax.nn.sigmoid(g)) * up).astype(BF16), wd)
            if not first:
                acc = o_ref[rows, :] + acc
            if last:
                acc = h_ref[rows, :] + 0.5 * _rms(acc, post_ref[...])
            o_ref[rows, :] = acc

    if n_steps == 1:
        step(True, True)
    else:
        pl.when(j == 0)(lambda: step(True, False))
        pl.when((j > 0) & (j < n_steps - 1))(lambda: step(False, False))
        pl.when(j == n_steps - 1)(lambda: step(False, True))


def _ffn_head_kernel(h_ref, pre_ref, wg_ref, wu_ref, wd_ref, post_ref, o_ref, wg16_ref, wu16_ref, wd16_ref,
                     u_scr, *, n_steps):
    def weights():
        w16 = []
        for src, dst in ((wg_ref, wg16_ref), (wu_ref, wu16_ref), (wd_ref, wd16_ref)):
            w = src[...].astype(BF16)
            dst[...] = w
            w16.append(w)
        return w16

    _ffn_steps(pl.program_id(0), n_steps, h_ref, pre_ref, post_ref, o_ref, u_scr, weights)


def _ffn_tail_kernel(h_ref, head_ref, pre_ref, wg_ref, wu_ref, wd_ref, post_ref, o_ref, u_scr, *,
                     n_steps, n_copy):
    i, j = pl.program_id(0), pl.program_id(1)
    slab = head_ref.shape[0]

    @pl.when((i == 0) & (j < n_copy))
    def _():
        o_ref[pl.ds(pl.multiple_of(j * slab, slab), slab), :] = head_ref[...]

    @pl.when(i > 0)
    def _():
        _ffn_steps(j, n_steps, h_ref, pre_ref, post_ref, o_ref, u_scr,
                   lambda: (wg_ref[...], wu_ref[...], wd_ref[...]))


def _ffn(h, pre_g, wg, wu, wd, post_g, tm=1024, tf=512, tf_head=256):
    n, d = h.shape
    dff = wg.shape[1]
    tf, tf_head = min(tf, dff), min(tf_head, dff)
    vec1 = pl.BlockSpec((1, d), lambda j: (0, 0))
    head, wg16, wu16, wd16 = pl.pallas_call(
        functools.partial(_ffn_head_kernel, n_steps=dff // tf_head),
        name="ffn_head",
        grid=(dff // tf_head,),
        in_specs=[
            pl.BlockSpec((tm, d), lambda j: (0, 0)), vec1,
            pl.BlockSpec((d, tf_head), lambda j: (0, j)),
            pl.BlockSpec((d, tf_head), lambda j: (0, j)),
            pl.BlockSpec((tf_head, d), lambda j: (j, 0)),
            vec1,
        ],
        out_specs=[pl.BlockSpec((tm, d), lambda j: (0, 0)),
                   pl.BlockSpec((d, tf_head), lambda j: (0, j)),
                   pl.BlockSpec((d, tf_head), lambda j: (0, j)),
                   pl.BlockSpec((tf_head, d), lambda j: (j, 0))],
        out_shape=[jax.ShapeDtypeStruct((tm, d), F32), jax.ShapeDtypeStruct((d, dff), BF16),
                   jax.ShapeDtypeStruct((d, dff), BF16), jax.ShapeDtypeStruct((dff, d), BF16)],
        scratch_shapes=[pltpu.VMEM((tm, d), BF16)],
        compiler_params=pltpu.CompilerParams(
            dimension_semantics=("arbitrary",), vmem_limit_bytes=VMEM_LIMIT),
    )(h, pre_g, wg, wu, wd, post_g)

    n_steps = dff // tf
    n_copy = 1 << (min(n_steps, tm // LANE).bit_length() - 1)
    row = lambda i, j: (i, 0)
    vec = pl.BlockSpec((1, d), lambda i, j: (0, 0))
    wcol = lambda i, j: (0, jnp.where(i > 0, j, 0))
    return pl.pallas_call(
        functools.partial(_ffn_tail_kernel, n_steps=n_steps, n_copy=n_copy),
        name="ffn_tail",
        grid=(n // tm, n_steps),
        in_specs=[
            pl.BlockSpec((tm, d), row),
            pl.BlockSpec((tm // n_copy, d), lambda i, j: (jnp.where(i == 0, jnp.minimum(j, n_copy - 1), n_copy - 1), 0)),
            vec,
            pl.BlockSpec((d, tf), wcol),
            pl.BlockSpec((d, tf), wcol),
            pl.BlockSpec((tf, d), lambda i, j: (jnp.where(i > 0, j, 0), 0)),
            vec,
        ],
        out_specs=pl.BlockSpec((tm, d), row),
        out_shape=jax.ShapeDtypeStruct((n, d), F32),
        scratch_shapes=[pltpu.VMEM((tm, d), BF16)],
        compiler_params=pltpu.CompilerParams(
            dimension_semantics=("arbitrary", "arbitrary"), vmem_limit_bytes=VMEM_LIMIT),
    )(h, head, pre_g, wg16, wu16, wd16, post_g)


def _proj_kernel(pos_ref, h_ref, pre_ref, w_ref, gb_ref, invf_ref, za_ref, zb_ref, u_scr, cos_scr, s1_scr, s2_scr):
    j = pl.program_id(1)

    def prepare(rows):
        u_scr[rows, :] = _rms(h_ref[rows, :], pre_ref[...]).astype(BF16)
        sub = PROJ_CHUNK // ROPE_PACK
        packed = slice(rows.start // ROPE_PACK, rows.start // ROPE_PACK + sub)
        ang = pos_ref[packed, :].astype(F32) * invf_ref[...]
        cos, sin = jnp.cos(ang), jnp.sin(ang)
        lane = lax.broadcasted_iota(jnp.int32, ang.shape, 1)
        for k in range(ROPE_PACK):
            grp = slice(rows.start + k * sub, rows.start + (k + 1) * sub)
            c, s = (cos, sin) if k == 0 else (pltpu.roll(cos, LANE - k * ROPE_DIM, axis=1),
                                              pltpu.roll(sin, LANE - k * ROPE_DIM, axis=1))
            cos_scr[grp, :] = jnp.where(lane < ROPE_DIM, c, 1.0)
            s1_scr[grp, :] = jnp.where((lane >= ROPE_HALF) & (lane < ROPE_DIM), s, 0.0)
            s2_scr[grp, :] = jnp.where(lane < ROPE_HALF, -s, 0.0)

    def head(z, rows, hd, rope, mul=None):
        x = z[:, hd * LANE:(hd + 1) * LANE]
        if rope:
            x = (x * cos_scr[rows, :]
                 + pltpu.roll(x, ROPE_HALF, axis=1) * s1_scr[rows, :]
                 + pltpu.roll(x, LANE - ROPE_HALF, axis=1) * s2_scr[rows, :])
        return x if mul is None else x * mul

    def chunks(first=False):
        for c in range(u_scr.shape[0] // PROJ_CHUNK):
            rows = slice(c * PROJ_CHUNK, (c + 1) * PROJ_CHUNK)
            if first:
                prepare(rows)
            yield rows, _dot(u_scr[rows, :], w_ref[...])

    def store(ref, rope_heads, mul=None, first=False):
        for rows, z in chunks(first):
            for hd in range(PROJ_HEADS):
                ref[rows, hd * LANE:(hd + 1) * LANE] = head(z, rows, hd, hd < rope_heads, mul).astype(ref.dtype)

    @pl.when(j == 0)
    def _():
        store(za_ref, PROJ_HEADS, Q_SCALE, first=True)

    @pl.when((j > 0) & (j < AT_KS // PROJ_HEADS))
    def _():
        store(za_ref, PROJ_HEADS, Q_SCALE)

    @pl.when(j == AT_KS // PROJ_HEADS)
    def _():
        store(za_ref, PROJ_HEADS)

    @pl.when(j == AT_VS // PROJ_HEADS)
    def _():
        store(za_ref, 0)

    @pl.when((j >= PROJ_A_STEPS) & (j < PROJ_A_STEPS + BT_KC // PROJ_HEADS))
    def _():
        store(zb_ref, 0)

    @pl.when(j == PROJ_A_STEPS + BT_KC // PROJ_HEADS)
    def _():
        store(zb_ref, NSA_KV_HEADS)

    @pl.when(j == PROJ_A_STEPS + BT_GATE // PROJ_HEADS)
    def _():
        for rows, z in chunks():
            zb_ref[rows, :] = jax.nn.sigmoid(z + gb_ref[...])


def _proj(pos, h, pre_g, w, gb, invf, tm=1024):
    n, d = h.shape
    return pl.pallas_call(
        _proj_kernel,
        name="mix_proj",
        grid=(n // tm, PROJ_STEPS),
        in_specs=[
            pl.BlockSpec((tm // ROPE_PACK, LANE), lambda i, j: (i, 0)),
            pl.BlockSpec((tm, d), lambda i, j: (i, 0)),
            pl.BlockSpec((1, d), lambda i, j: (0, 0)),
            pl.BlockSpec((d, PROJ_TN), lambda i, j: (0, j)),
            pl.BlockSpec((1, PROJ_TN), lambda i, j: (0, 0)),
            pl.BlockSpec((1, LANE), lambda i, j: (0, 0)),
        ],
        out_specs=[pl.BlockSpec((tm, PROJ_TN), lambda i, j: (i, jnp.minimum(j, PROJ_A_STEPS - 1))),
                   pl.BlockSpec((tm, PROJ_TN), lambda i, j: (i, jnp.maximum(j - PROJ_A_STEPS, 0)))],
        out_shape=[jax.ShapeDtypeStruct((n, A_WIDTH), BF16), jax.ShapeDtypeStruct((n, B_WIDTH), F32)],
        scratch_shapes=[pltpu.VMEM((tm, d), BF16)] + [pltpu.VMEM((tm, LANE), F32)] * 3,
        compiler_params=pltpu.CompilerParams(
            dimension_semantics=("arbitrary", "arbitrary"), vmem_limit_bytes=VMEM_LIMIT),
    )(pos, h, pre_g, w, gb, invf)


def _cmp_kernel(x_ref, pos_ref, w1_ref, w2_ref, o_ref):
    ncp = o_ref.shape[0]
    xs = [x_ref[pl.ds(l, ncp, stride=CMP_STRIDE), :] for l in range(CMP_STRIDE)]
    half = lambda h: jnp.concatenate(
        [(x + pos_ref[h * CMP_STRIDE + l:h * CMP_STRIDE + l + 1, :]).astype(BF16) for l, x in enumerate(xs)], axis=1)
    acc_a = _dot(half(0), w1_ref[0])
    acc_b = _dot(half(1), w1_ref[1])
    hidden = acc_a + pltpu.roll(acc_b, ncp - 1, axis=0)
    o_ref[...] = _dot(jax.nn.gelu(hidden).astype(BF16), w2_ref[...])


def _compress(zb, pos_kv, w1, w2, batch, seq):
    ncp = seq // CMP_STRIDE
    hidden_w = w1.shape[-1]
    return pl.pallas_call(
        _cmp_kernel,
        name="compress",
        grid=(2, batch, NSA_KV_HEADS),
        in_specs=[
            pl.BlockSpec((seq, LANE), lambda k, b, g: (b, BT_KC + 2 * k + g)),
            pl.BlockSpec((None, CMP_LEN, HEAD_DIM), lambda k, b, g: (k, 0, 0)),
            pl.BlockSpec((None, CMP_LEN // CMP_STRIDE, CMP_STRIDE * HEAD_DIM, hidden_w), lambda k, b, g: (k, 0, 0, 0)),
            pl.BlockSpec((None, hidden_w, HEAD_DIM), lambda k, b, g: (k, 0, 0)),
        ],
        out_specs=pl.BlockSpec((None, None, None, ncp, HEAD_DIM), lambda k, b, g: (k, b, g, 0, 0)),
        out_shape=jax.ShapeDtypeStruct((2, batch, NSA_KV_HEADS, ncp, HEAD_DIM), F32),
        compiler_params=pltpu.CompilerParams(
            dimension_semantics=("arbitrary",) * 3, vmem_limit_bytes=VMEM_LIMIT),
    )(zb, pos_kv, w1, w2)


def _masked_exp(s, mask):
    s = s + jnp.where(mask, 0.0, -MASK_BIG)
    e = jnp.exp2(s - jnp.max(s, axis=-1, keepdims=True))
    return e, jnp.sum(e, axis=-1, keepdims=True)


def _attn_kernel(q_ref, kc_ref, vc_ref, ks_ref, vs_ref, kw_ref, vw_ref, gz_ref, ovl_ref, oh_ref, ga_ref, prev_ref,
                 o_ref, *, first_block):
    del prev_ref
    G, R, tq = NSA_KV_HEADS, GQA_GROUP, Q_BLOCK
    n_keys = ks_ref.shape[0]
    top_k = min(SEL_TOPK, ovl_ref.shape[0])
    n_blk = n_keys // SEL_LEN
    n_cmp = min(kc_ref.shape[1], -(-(n_keys // CMP_STRIDE) // LANE) * LANE)
    t0 = (first_block + pl.program_id(1)) * tq
    t_idx = t0 + lax.broadcasted_iota(jnp.int32, (1, tq, 1), 1)
    gsl = lambda g: slice(g * HEAD_DIM, (g + 1) * HEAD_DIM)

    def stacked_q(g):
        q = q_ref[:, g * R * HEAD_DIM:(g + 1) * R * HEAD_DIM]
        return jnp.concatenate([q[:, r * HEAD_DIM:(r + 1) * HEAD_DIM] for r in range(R)], axis=0)

    q4s = [stacked_q(g) for g in range(G)]

    wk = min(WINDOW + tq, n_keys)
    w0 = pl.multiple_of(jnp.maximum(t0 - WINDOW, 0), tq)
    kp = w0 + lax.broadcasted_iota(jnp.int32, (1, 1, wk), 2)
    win_mask = (kp <= t_idx) & (kp > t_idx - WINDOW)

    ones_col = jnp.ones((n_keys, LANE), BF16)

    def weights(s, m):
        return jnp.exp2((s - m).astype(BF16)).reshape(R * tq, s.shape[-1])

    def pv(p, v, ones):
        return _dot(p, jnp.concatenate([v, ones], axis=1))

    def normalised(acc):
        return (acc[:, :HEAD_DIM] * (1.0 / acc[:, HEAD_DIM:])).reshape(R, tq, HEAD_DIM)

    def window(g):
        s = _dot_nt(q4s[g], kw_ref[pl.ds(w0, wk), gsl(g)]).reshape(R, tq, wk)
        s = s + jnp.where(win_mask, 0.0, -MASK_BIG)
        p = weights(s, jnp.max(s, axis=-1, keepdims=True))
        return normalised(pv(p, vw_ref[pl.ds(w0, wk), gsl(g)], ones_col[0:wk, :]))

    def front(g):
        q4 = q4s[g]

        s = _dot_nt(q4, kc_ref[g, 0:n_cmp, :].astype(BF16)).reshape(R, tq, n_cmp)
        c_idx = lax.broadcasted_iota(jnp.int32, (1, 1, n_cmp), 2)
        e, l = _masked_exp(s, (c_idx * CMP_STRIDE + (CMP_LEN - 1)) <= t_idx)
        pc = e * (jnp.where(t_idx >= CMP_LEN - 1, 1.0, 0.0) / l)
        o_cmp = _dot(pc.reshape(R * tq, n_cmp).astype(BF16), vc_ref[g, 0:n_cmp, :].astype(BF16))
        o_cmp = o_cmp.reshape(R, tq, HEAD_DIM)
        if n_blk <= top_k:
            return q4, jnp.concatenate([q4, jnp.zeros_like(q4)], axis=1), o_cmp

        pcs = pc[0]
        for r in range(1, R):
            pcs = pcs + pc[r]
        hi = pcs.astype(BF16)
        lo = (pcs - hi.astype(F32)).astype(BF16)
        ovl = ovl_ref[0:n_blk, 0:n_cmp]
        imp = _dot_nt(ovl, hi) + _dot_nt(ovl, lo)
        blk = lax.broadcasted_iota(jnp.int32, (n_blk, tq), 0)
        tcol = t0 + lax.broadcasted_iota(jnp.int32, (n_blk, tq), 1)
        forced = (blk == lax.shift_right_logical(tcol, int(np.log2(SEL_LEN)))) | (blk == 0)
        v = jnp.where(forced, SEL_FORCE, jnp.where(blk * SEL_LEN <= tcol, imp, -SEL_FORCE))
        sub = lax.broadcasted_iota(jnp.int32, (SUBLANE, tq), 0)
        groups = [v[k:k + SUBLANE, :] for k in range(0, n_blk, SUBLANE)]
        ranks = [jnp.zeros((SUBLANE, tq), F32) for _ in groups]
        for sp in range(n_blk):
            other = v[sp:sp + 1, :]
            for gi, vg in enumerate(groups):
                first = gi * SUBLANE
                if first > sp:
                    beats = other >= vg
                elif first + SUBLANE - 1 <= sp:
                    beats = other > vg
                else:
                    beats = (other > vg) | ((other == vg) & (sub > sp - first))
                ranks[gi] = ranks[gi] + jnp.where(beats, 1.0, 0.0)
        rank = jnp.concatenate(ranks, axis=0)
        unsel = jnp.where(rank < float(top_k), 0.0, -1.0)
        unsel = jnp.concatenate([unsel, jnp.zeros((LANE - n_blk, tq), F32)], axis=0).T.astype(BF16)
        q_aug = jnp.concatenate([q4, jnp.concatenate([unsel] * R, axis=0)], axis=1)
        return q4, q_aug, o_cmp

    fronts = [front(g) for g in range(G)]

    n_full = n_keys - SEL_KT
    kp = n_full + lax.broadcasted_iota(jnp.int32, (1, 1, SEL_KT), 2)
    causal_bias = jnp.where(kp <= t_idx, 0.0, -MASK_BIG)

    def selected(g):
        k_aug = jnp.concatenate([ks_ref[:, gsl(g)], oh_ref[...]], axis=1)
        s = _dot_nt(fronts[g][1], k_aug).reshape(R, tq, n_keys)
        s_last = s[:, :, n_full:] + causal_bias
        m = jnp.max(s_last, axis=-1, keepdims=True)
        if n_full:
            s_full = s[:, :, :n_full]
            m = jnp.maximum(m, jnp.max(s_full, axis=-1, keepdims=True))
        acc = pv(weights(s_last, m), vs_ref[n_full:n_keys, gsl(g)], ones_col[n_full:n_keys, :])
        if n_full:
            acc = acc + pv(weights(s_full, m), vs_ref[0:n_full, gsl(g)], ones_col[0:n_full, :])
        return normalised(acc)

    o_sels = [selected(g) for g in range(G)]
    o_wins = [window(g) for g in range(G)]
    heads = []
    for g in range(G):
        o_cmp, o_sel, o_win = fronts[g][2], o_sels[g], o_wins[g]
        gz = gz_ref[:, gsl(g)]
        for r in range(R):
            c = r * N_GATES
            heads.append(gz[:, c:c + 1] * o_cmp[r] + gz[:, c + 1:c + 2] * o_sel[r] + gz[:, c + 2:c + 3] * o_win[r])
    o_ref[...] = _rms(jnp.concatenate(heads, axis=1), ga_ref[...]).astype(o_ref.dtype)


def _attention(za, zb, kcv, ovl_t, onehot, g_attn, batch, seq):
    ncp = seq // CMP_STRIDE
    n_sel = seq // SEL_LEN
    G = NSA_KV_HEADS
    blocks = SEL_KT // Q_BLOCK
    za3 = za.reshape(batch, seq, A_WIDTH)
    zb3 = zb.reshape(batch, seq, B_WIDTH)
    out = jnp.zeros((batch, seq, NSA_WIDTH), BF16)
    for c in range(seq // SEL_KT):
        n_keys = (c + 1) * SEL_KT
        q_row = lambda b, i, c=c: (b, c * blocks + i, 0)
        kv_spec = lambda tile: pl.BlockSpec((None, n_keys, G * LANE), lambda b, i: (b, 0, tile // G))
        operands = (za3, kcv, kcv, za3, za3, za3, za3, zb3, ovl_t, onehot, g_attn, out)
        out = pl.pallas_call(
            functools.partial(_attn_kernel, first_block=c * blocks),
            name=f"attention_{c}",
            grid=(batch, blocks),
            in_specs=[
                pl.BlockSpec((None, Q_BLOCK, NSA_WIDTH), q_row),
                pl.BlockSpec((None, None, G, ncp, HEAD_DIM), lambda b, i: (0, b, 0, 0, 0)),
                pl.BlockSpec((None, None, G, ncp, HEAD_DIM), lambda b, i: (1, b, 0, 0, 0)),
                kv_spec(AT_KS), kv_spec(AT_VS), kv_spec(AT_KW), kv_spec(AT_VW),
                pl.BlockSpec((None, Q_BLOCK, G * LANE), lambda b, i, c=c: (b, c * blocks + i, BT_GATE // G)),
                pl.BlockSpec((n_sel, ncp), lambda b, i: (0, 0)),
                pl.BlockSpec((n_keys, LANE), lambda b, i: (0, 0)),
                pl.BlockSpec((1, NSA_WIDTH), lambda b, i: (0, 0)),
                pl.BlockSpec(memory_space=pl.ANY),
            ],
            out_specs=pl.BlockSpec((None, Q_BLOCK, NSA_WIDTH), q_row),
            out_shape=jax.ShapeDtypeStruct((batch, seq, NSA_WIDTH), BF16),
            input_output_aliases={len(operands) - 1: 0},
            compiler_params=pltpu.CompilerParams(
                dimension_semantics=("arbitrary",) * 2, vmem_limit_bytes=VMEM_LIMIT),
        )(*operands)
    return out.reshape(batch * seq, NSA_WIDTH)


def _lru_kernel(zx_ref, zy_ref, cw_ref, cb_ref, wa_ref, ba_ref, wi_ref, bi_ref, lam_ref, gr_ref,
                o_ref, xpad, a_scr, u_scr, h_scr, carry):
    tt, width = zx_ref.shape
    ti = pl.program_id(1)

    @pl.when(ti == 0)
    def _():
        xpad[0:SUBLANE, :] = jnp.zeros((SUBLANE, width), F32)
        carry[...] = jnp.zeros_like(carry)

    xb = zx_ref[...]
    xpad[SUBLANE:SUBLANE + tt, :] = xb
    xc = cb_ref[...] + cw_ref[CONV_WIDTH - 1:CONV_WIDTH, :] * xb
    for k in range(1, CONV_WIDTH):
        xc = xc + cw_ref[CONV_WIDTH - 1 - k:CONV_WIDTH - k, :] * xpad[SUBLANE - k:SUBLANE - k + tt, :]
    xpad[0:SUBLANE, :] = xb[tt - SUBLANE:tt, :]

    sp = jax.nn.softplus(-lam_ref[...])
    row8 = lax.broadcasted_iota(jnp.int32, (1, SUBLANE, LRU_BLOCK_DIM), 1)
    for hb in range(LRU_BLOCKS):
        sl = slice(hb * LRU_BLOCK_DIM, (hb + 1) * LRU_BLOCK_DIM)
        xs = xc[:, sl]
        xs16 = xs.astype(BF16)
        r = jax.nn.sigmoid(_dot(xs16, wa_ref[hb]) + ba_ref[:, sl])
        ig = jax.nn.sigmoid(_dot(xs16, wi_ref[hb]) + bi_ref[:, sl])
        log_a = -LRU_C * r * sp[:, sl]
        a = jnp.exp(log_a)
        u = jnp.sqrt(-jnp.tanh(log_a) * (a * a + 1.0)) * (ig * xs)
        a = a.reshape(tt // SUBLANE, SUBLANE, LRU_BLOCK_DIM)
        u = u.reshape(tt // SUBLANE, SUBLANE, LRU_BLOCK_DIM)
        for s in (1, 2, 4):
            ok = row8 >= s
            a_sh = pltpu.roll(a, s, axis=1)
            u_sh = pltpu.roll(u, s, axis=1)
            u = jnp.where(ok, u + a * u_sh, u)
            a = jnp.where(ok, a * a_sh, a)
        a_scr[:, sl] = a.reshape(tt, LRU_BLOCK_DIM)
        u_scr[:, sl] = u.reshape(tt, LRU_BLOCK_DIM)

    def group(gi, c):
        r0 = pl.multiple_of(gi * SUBLANE, SUBLANE)
        h = u_scr[pl.ds(r0, SUBLANE), :] + a_scr[pl.ds(r0, SUBLANE), :] * c
        h_scr[pl.ds(r0, SUBLANE), :] = h
        return jnp.broadcast_to(h[SUBLANE - 1:SUBLANE, :], (SUBLANE, width))

    c = lax.fori_loop(0, tt // SUBLANE, group, carry[...], unroll=4)
    carry[...] = c
    o_ref[...] = _rms(h_scr[...] * jax.nn.gelu(zy_ref[...]), gr_ref[...]).astype(o_ref.dtype)


def _rglru(zb, conv_w, conv_b, w_a, b_a, w_i, b_i, lam, g_rec, batch, seq, tt=512):
    nt = seq // tt
    wt = LRU_WIDTH // LANE
    full = lambda a: pl.BlockSpec(a.shape, lambda b, t: (0,) * a.ndim)
    zspec = lambda tile: pl.BlockSpec((tt, LRU_WIDTH), lambda b, t: (b * nt + t, tile // wt))
    return pl.pallas_call(
        _lru_kernel,
        name="rglru",
        grid=(batch, nt),
        in_specs=[zspec(BT_X), zspec(BT_Y), full(conv_w), full(conv_b), full(w_a), full(b_a),
                  full(w_i), full(b_i), full(lam), full(g_rec)],
        out_specs=pl.BlockSpec((tt, LRU_WIDTH), lambda b, t: (b * nt + t, 0)),
        out_shape=jax.ShapeDtypeStruct((batch * seq, LRU_WIDTH), BF16),
        scratch_shapes=[pltpu.VMEM((tt + SUBLANE, LRU_WIDTH), F32), pltpu.VMEM((tt, LRU_WIDTH), F32),
                        pltpu.VMEM((tt, LRU_WIDTH), F32), pltpu.VMEM((tt, LRU_WIDTH), F32),
                        pltpu.VMEM((SUBLANE, LRU_WIDTH), F32)],
        compiler_params=pltpu.CompilerParams(
            dimension_semantics=("arbitrary", "arbitrary"), vmem_limit_bytes=VMEM_LIMIT),
    )(zb, zb, conv_w, conv_b, w_a, b_a, w_i, b_i, lam, g_rec)


def _resident_bf16(w_ref, w16_scr):
    @pl.when(pl.program_id(0) == 0)
    def _():
        w16_scr[...] = w_ref[...].astype(BF16)


def _resident(a):
    return pl.BlockSpec(a.shape, lambda i: (0,) * a.ndim, pipeline_mode=pl.Buffered(1))


def _out_kernel(oa_ref, or_ref, w_ref, post_ref, h_ref, o_ref, w16_scr):
    _resident_bf16(w_ref, w16_scr)
    for c in range(h_ref.shape[0] // OUT_CHUNK):
        rows = slice(c * OUT_CHUNK, (c + 1) * OUT_CHUNK)
        y = (_dot(oa_ref[rows, :], w16_scr[0:NSA_WIDTH, :])
             + _dot(or_ref[rows, :], w16_scr[NSA_WIDTH:NSA_WIDTH + LRU_WIDTH, :]))
        o_ref[rows, :] = h_ref[rows, :] + _rms(y, post_ref[...])


def _out_proj(o_attn, o_rec, w_out, post_g, h, tm=512):
    n, d = h.shape
    row = lambda i: (i, 0)
    return pl.pallas_call(
        _out_kernel,
        name="out_proj",
        grid=(n // tm,),
        in_specs=[pl.BlockSpec((tm, NSA_WIDTH), row), pl.BlockSpec((tm, LRU_WIDTH), row),
                  _resident(w_out), _resident(post_g), pl.BlockSpec((tm, d), row)],
        out_specs=pl.BlockSpec((tm, d), row),
        out_shape=jax.ShapeDtypeStruct((n, d), F32),
        scratch_shapes=[pltpu.VMEM(w_out.shape, BF16)],
        compiler_params=pltpu.CompilerParams(
            dimension_semantics=("arbitrary",), vmem_limit_bytes=VMEM_LIMIT),
    )(o_attn, o_rec, w_out, post_g, h)


def _ple_kernel(h_ref, p_ref, pre_ref, wg_ref, wp_ref, post_ref, o_ref, wg16_scr, wp16_scr):
    _resident_bf16(wg_ref, wg16_scr)
    _resident_bf16(wp_ref, wp16_scr)
    for c in range(h_ref.shape[0] // OUT_CHUNK):
        rows = slice(c * OUT_CHUNK, (c + 1) * OUT_CHUNK)
        h = h_ref[rows, :]
        gate = jax.nn.sigmoid(_dot(_rms(h, pre_ref[...]).astype(BF16), wg16_scr[...]))
        pp = _dot(p_ref[rows, :].astype(BF16), wp16_scr[...])
        o_ref[rows, :] = h + _rms(gate * pp, post_ref[...])


def _ple(h, p, pre_g, w_gate, w_proj, post_g, tm=512):
    n, d = h.shape
    row = lambda i: (i, 0)
    return pl.pallas_call(
        _ple_kernel,
        name="ple",
        grid=(n // tm,),
        in_specs=[pl.BlockSpec((tm, d), row), pl.BlockSpec((tm, p.shape[1]), row),
                  _resident(pre_g), _resident(w_gate), _resident(w_proj), _resident(post_g)],
        out_specs=pl.BlockSpec((tm, d), row),
        out_shape=jax.ShapeDtypeStruct((n, d), F32),
        scratch_shapes=[pltpu.VMEM(w_gate.shape, BF16), pltpu.VMEM(w_proj.shape, BF16)],
        compiler_params=pltpu.CompilerParams(
            dimension_semantics=("arbitrary",), vmem_limit_bytes=VMEM_LIMIT),
    )(h, p, pre_g, w_gate, w_proj, post_g)


def _selection_constants(seq):
    ncp = seq // CMP_STRIDE
    n_cmp = (seq - CMP_LEN) // CMP_STRIDE + 1
    n_sel = seq // SEL_LEN
    c = np.arange(ncp)[None, :]
    s = np.arange(n_sel)[:, None]
    ovl_t = ((c * CMP_STRIDE < s * SEL_LEN + SEL_LEN) & (c * CMP_STRIDE + CMP_LEN - 1 >= s * SEL_LEN) & (c < n_cmp))
    onehot = (np.arange(seq)[:, None] // SEL_LEN) == np.arange(LANE)[None, :]
    return jnp.asarray(ovl_t, BF16), jnp.asarray(onehot * MASK_BIG, BF16)


def _pad_cols(a, width):
    return jnp.pad(a, ((0, 0), (0, width - a.shape[1])))


def kernel(x, p, positions, ff1_pre_g, ff1_post_g, ff1_w_gate, ff1_w_up, ff1_w_down, mix_pre_g, mix_post_g, w_in, cmp_pos_k, cmp_pos_v, cmp_k_w1, cmp_k_w2, cmp_v_w1, cmp_v_w2, nsa_gate_b, conv_w, conv_b, rg_w_a, rg_b_a, rg_w_i, rg_b_i, rg_lambda, attn_out_g, rec_out_g, w_out, ff2_pre_g, ff2_post_g, ff2_w_gate, ff2_w_up, ff2_w_down, ple_pre_g, ple_post_g, w_ple_gate, w_ple_proj):
    batch, seq, d = x.shape
    depth = p.shape[0]
    n = batch * seq
    vec = lambda a: a.reshape(1, -1)
    gate_cols = NSA_HEADS * N_GATES
    grp_gates = GQA_GROUP * N_GATES
    o_q, o_kc, o_vc, o_ks, o_vs, o_kw, o_vw = (NSA_WIDTH * 0,) + tuple(NSA_WIDTH + k * KV_WIDTH for k in range(6))
    o_g = NSA_WIDTH + 6 * KV_WIDTH
    o_x = o_g + gate_cols
    o_y = o_x + LRU_WIDTH

    half = jnp.arange(ROPE_HALF, dtype=F32)
    inv_freq = ROPE_THETA ** (-half / ROPE_HALF)
    invf = jnp.tile(jnp.concatenate([inv_freq, inv_freq]), ROPE_PACK).reshape(1, LANE)
    ovl_t, onehot = _selection_constants(seq)
    sub = PROJ_CHUNK // ROPE_PACK
    pos = positions.reshape(n // PROJ_CHUNK, ROPE_PACK, sub).transpose(0, 2, 1).reshape(n // ROPE_PACK, ROPE_PACK)
    pos = jnp.repeat(pos, ROPE_DIM, axis=1)

    h = x.reshape(n, d)
    for i in range(depth):
        wi = w_in[i]
        cols = lambda o, w: wi[:, o:o + w]
        w_in_z = jnp.concatenate(
            [cols(o_q, NSA_WIDTH), cols(o_ks, KV_WIDTH), cols(o_kw, KV_WIDTH), cols(o_vs, KV_WIDTH),
             cols(o_vw, KV_WIDTH), cols(o_x, LRU_WIDTH), cols(o_y, LRU_WIDTH), cols(o_kc, KV_WIDTH),
             cols(o_vc, KV_WIDTH)]
            + [_pad_cols(cols(o_g + g * grp_gates, grp_gates), LANE) for g in range(NSA_KV_HEADS)]
            + [jnp.zeros((d, B_WIDTH - (BT_GATE + NSA_KV_HEADS) * LANE), F32)],
            axis=1).astype(BF16)
        gb = nsa_gate_b[i].reshape(NSA_KV_HEADS, grp_gates)
        gate_bias = _pad_cols(jnp.pad(gb, ((0, 0), (0, LANE - grp_gates))).reshape(1, NSA_KV_HEADS * LANE), PROJ_TN)

        h = _ffn(h, vec(ff1_pre_g[i]), ff1_w_gate[i], ff1_w_up[i], ff1_w_down[i], vec(ff1_post_g[i]))
        za, zb = _proj(pos, h, vec(mix_pre_g[i]), w_in_z, gate_bias, invf)
        kcv = _compress(
            zb, jnp.stack([cmp_pos_k[i], cmp_pos_v[i]]),
            jnp.stack([cmp_k_w1[i], cmp_v_w1[i]]).reshape(
                2, CMP_LEN // CMP_STRIDE, CMP_STRIDE * HEAD_DIM, -1).astype(BF16),
            jnp.stack([cmp_k_w2[i], cmp_v_w2[i]]).astype(BF16), batch, seq)
        o_attn = _attention(za, zb, kcv, ovl_t, onehot, vec(attn_out_g[i]), batch, seq)
        o_rec = _rglru(zb, conv_w[i], vec(conv_b[i]), rg_w_a[i].astype(BF16), vec(rg_b_a[i]),
                       rg_w_i[i].astype(BF16), vec(rg_b_i[i]), vec(rg_lambda[i]), vec(rec_out_g[i]), batch, seq)
        h = _out_proj(o_attn, o_rec, w_out[i], vec(mix_post_g[i]), h)
        h = _ffn(h, vec(ff2_pre_g[i]), ff2_w_gate[i], ff2_w_up[i], ff2_w_down[i], vec(ff2_post_g[i]))
        h = _ple(h, p[i].reshape(n, -1), vec(ple_pre_g[i]), w_ple_gate[i], w_ple_proj[i], vec(ple_post_g[i]))
    return h.reshape(batch, seq, d)
```

```python
import functools

import numpy as np
import jax
import jax.numpy as jnp
from jax import lax
from jax.experimental import pallas as pl
from jax.experimental.pallas import tpu as pltpu

F32 = jnp.float32
BF16 = jnp.bfloat16

D_MODEL = 2048
NSA_HEADS = 8
NSA_KV_HEADS = 2
GQA_GROUP = NSA_HEADS // NSA_KV_HEADS
HEAD_DIM = 128
NSA_WIDTH = NSA_HEADS * HEAD_DIM
KV_WIDTH = NSA_KV_HEADS * HEAD_DIM
ROPE_DIM = HEAD_DIM // 4
ROPE_HALF = ROPE_DIM // 2
ROPE_THETA = 500000.0
CMP_LEN = 32
CMP_STRIDE = 16
SEL_LEN = 64
SEL_TOPK = 16
WINDOW = 512
Q_BLOCK = 128
N_GATES = 3
LRU_WIDTH = 1024
LRU_BLOCKS = 8
LRU_BLOCK_DIM = LRU_WIDTH // LRU_BLOCKS
CONV_WIDTH = 4
LRU_C = 8.0
RMS_EPS = 1e-6
NEG = -1e30
SEL_FORCE = 1e4

LANE = 128
SUBLANE = 8
VMEM_LIMIT = 62 * 1024 * 1024

AT_Q, AT_KS, AT_KW, AT_VS, AT_VW = 0, 8, 10, 12, 14
A_WIDTH = 16 * LANE
BT_X, BT_Y, BT_KC, BT_VC, BT_GATE = 0, 8, 16, 18, 20
B_WIDTH = 24 * LANE
PROJ_TN = 512
PROJ_A_STEPS = A_WIDTH // PROJ_TN
PROJ_STEPS = (A_WIDTH + B_WIDTH) // PROJ_TN
PROJ_HEADS = PROJ_TN // LANE
ROW_CHUNK = 256
PROJ_CHUNK = ROW_CHUNK
ROPE_PACK = LANE // ROPE_DIM
FFN_CHUNK = 4 * ROW_CHUNK
OUT_CHUNK = 2 * ROW_CHUNK

Q_SCALE = HEAD_DIM ** -0.5 * 1.4426950408889634
SEL_KT = 512
MASK_BIG = 2.0 ** 100


def _rms(x, g):
    return x * lax.rsqrt(jnp.mean(x * x, axis=-1, keepdims=True) + RMS_EPS) * g


def _dot(a, b):
    return jnp.dot(a, b, preferred_element_type=F32)


def _dot_nt(a, b):
    return lax.dot_general(a, b, (((1,), (1,)), ((), ())), preferred_element_type=F32)


def _ffn_steps(j, n_steps, h_ref, pre_ref, post_ref, o_ref, u_scr, weights):
    def step(first, last):
        wg, wu, wd = weights()
        for c in range(h_ref.shape[0] // FFN_CHUNK):
            rows = slice(c * FFN_CHUNK, (c + 1) * FFN_CHUNK)
            if first:
                u = _rms(h_ref[rows, :], pre_ref[...]).astype(BF16)
                u_scr[rows, :] = u
            else:
                u = u_scr[rows, :]
            g = _dot(u, wg)
            up = _dot(u, wu)
            acc = _dot(((g * jax.nn.sigmoid(g)) * up).astype(BF16), wd)
            if not first:
                acc = o_ref[rows, :] + acc
            if last:
                acc = h_ref[rows, :] + 0.5 * _rms(acc, post_ref[...])
            o_ref[rows, :] = acc

    if n_steps == 1:
        step(True, True)
    else:
        pl.when(j == 0)(lambda: step(True, False))
        pl.when((j > 0) & (j < n_steps - 1))(lambda: step(False, False))
        pl.when(j == n_steps - 1)(lambda: step(False, True))


def _ffn_head_kernel(h_ref, pre_ref, wg_ref, wu_ref, wd_ref, post_ref, o_ref, wg16_ref, wu16_ref, wd16_ref,
                     u_scr, *, n_steps):
    def weights():
        w16 = []
        for src, dst in ((wg_ref, wg16_ref), (wu_ref, wu16_ref), (wd_ref, wd16_ref)):
            w = src[...].astype(BF16)
            dst[...] = w
            w16.append(w)
        return w16

    _ffn_steps(pl.program_id(0), n_steps, h_ref, pre_ref, post_ref, o_ref, u_scr, weights)


def _ffn_tail_kernel(h_ref, head_ref, pre_ref, wg_ref, wu_ref, wd_ref, post_ref, o_ref, u_scr, *,
                     n_steps, n_copy):
    i, j = pl.program_id(0), pl.program_id(1)
    slab = head_ref.shape[0]

    @pl.when((i == 0) & (j < n_copy))
    def _():
        o_ref[pl.ds(pl.multiple_of(j * slab, slab), slab), :] = head_ref[...]

    @pl.when(i > 0)
    def _():
        _ffn_steps(j, n_steps, h_ref, pre_ref, post_ref, o_ref, u_scr,
                   lambda: (wg_ref[...], wu_ref[...], wd_ref[...]))


def _ffn(h, pre_g, wg, wu, wd, post_g, tm=1024, tf=512, tf_head=256):
    n, d = h.shape
    dff = wg.shape[1]
    tf, tf_head = min(tf, dff), min(tf_head, dff)
    vec1 = pl.BlockSpec((1, d), lambda j: (0, 0))
    head, wg16, wu16, wd16 = pl.pallas_call(
        functools.partial(_ffn_head_kernel, n_steps=dff // tf_head),
        name="ffn_head",
        grid=(dff // tf_head,),
        in_specs=[
            pl.BlockSpec((tm, d), lambda j: (0, 0)), vec1,
            pl.BlockSpec((d, tf_head), lambda j: (0, j)),
            pl.BlockSpec((d, tf_head), lambda j: (0, j)),
            pl.BlockSpec((tf_head, d), lambda j: (j, 0)),
            vec1,
        ],
        out_specs=[pl.BlockSpec((tm, d), lambda j: (0, 0)),
                   pl.BlockSpec((d, tf_head), lambda j: (0, j)),
                   pl.BlockSpec((d, tf_head), lambda j: (0, j)),
                   pl.BlockSpec((tf_head, d), lambda j: (j, 0))],
        out_shape=[jax.ShapeDtypeStruct((tm, d), F32), jax.ShapeDtypeStruct((d, dff), BF16),
                   jax.ShapeDtypeStruct((d, dff), BF16), jax.ShapeDtypeStruct((dff, d), BF16)],
        scratch_shapes=[pltpu.VMEM((tm, d), BF16)],
        compiler_params=pltpu.CompilerParams(
            dimension_semantics=("arbitrary",), vmem_limit_bytes=VMEM_LIMIT),
    )(h, pre_g, wg, wu, wd, post_g)

    n_steps = dff // tf
    n_copy = 1 << (min(n_steps, tm // LANE).bit_length() - 1)
    row = lambda i, j: (i, 0)
    vec = pl.BlockSpec((1, d), lambda i, j: (0, 0))
    wcol = lambda i, j: (0, jnp.where(i > 0, j, 0))
    return pl.pallas_call(
        functools.partial(_ffn_tail_kernel, n_steps=n_steps, n_copy=n_copy),
        name="ffn_tail",
        grid=(n // tm, n_steps),
        in_specs=[
            pl.BlockSpec((tm, d), row),
            pl.BlockSpec((tm // n_copy, d), lambda i, j: (jnp.where(i == 0, jnp.minimum(j, n_copy - 1), n_copy - 1), 0)),
            vec,
            pl.BlockSpec((d, tf), wcol),
            pl.BlockSpec((d, tf), wcol),
            pl.BlockSpec((tf, d), lambda i, j: (jnp.where(i > 0, j, 0), 0)),
            vec,
        ],
        out_specs=pl.BlockSpec((tm, d), row),
        out_shape=jax.ShapeDtypeStruct((n, d), F32),
        scratch_shapes=[pltpu.VMEM((tm, d), BF16)],
        compiler_params=pltpu.CompilerParams(
            dimension_semantics=("arbitrary", "arbitrary"), vmem_limit_bytes=VMEM_LIMIT),
    )(h, head, pre_g, wg16, wu16, wd16, post_g)


def _proj_kernel(pos_ref, h_ref, pre_ref, w_ref, gb_ref, invf_ref, za_ref, zb_ref, u_scr, cos_scr, s1_scr, s2_scr):
    j = pl.program_id(1)

    def prepare(rows):
        u_scr[rows, :] = _rms(h_ref[rows, :], pre_ref[...]).astype(BF16)
        sub = PROJ_CHUNK // ROPE_PACK
        packed = slice(rows.start // ROPE_PACK, rows.start // ROPE_PACK + sub)
        ang = pos_ref[packed, :].astype(F32) * invf_ref[...]
        cos, sin = jnp.cos(ang), jnp.sin(ang)
        lane = lax.broadcasted_iota(jnp.int32, ang.shape, 1)
        for k in range(ROPE_PACK):
            grp = slice(rows.start + k * sub, rows.start + (k + 1) * sub)
            c, s = (cos, sin) if k == 0 else (pltpu.roll(cos, LANE - k * ROPE_DIM, axis=1),
                                              pltpu.roll(sin, LANE - k * ROPE_DIM, axis=1))
            cos_scr[grp, :] = jnp.where(lane < ROPE_DIM, c, 1.0)
            s1_scr[grp, :] = jnp.where((lane >= ROPE_HALF) & (lane < ROPE_DIM), s, 0.0)
            s2_scr[grp, :] = jnp.where(lane < ROPE_HALF, -s, 0.0)

    def head(z, rows, hd, rope, mul=None):
        x = z[:, hd * LANE:(hd + 1) * LANE]
        if rope:
            x = (x * cos_scr[rows, :]
                 + pltpu.roll(x, ROPE_HALF, axis=1) * s1_scr[rows, :]
                 + pltpu.roll(x, LANE - ROPE_HALF, axis=1) * s2_scr[rows, :])
        return x if mul is None else x * mul

    def chunks(first=False):
        for c in range(u_scr.shape[0] // PROJ_CHUNK):
            rows = slice(c * PROJ_CHUNK, (c + 1) * PROJ_CHUNK)
            if first:
                prepare(rows)
            yield rows, _dot(u_scr[rows, :], w_ref[...])

    def store(ref, rope_heads, mul=None, first=False):
        for rows, z in chunks(first):
            for hd in range(PROJ_HEADS):
                ref[rows, hd * LANE:(hd + 1) * LANE] = head(z, rows, hd, hd < rope_heads, mul).astype(ref.dtype)

    @pl.when(j == 0)
    def _():
        store(za_ref, PROJ_HEADS, Q_SCALE, first=True)

    @pl.when((j > 0) & (j < AT_KS // PROJ_HEADS))
    def _():
        store(za_ref, PROJ_HEADS, Q_SCALE)

    @pl.when(j == AT_KS // PROJ_HEADS)
    def _():
        store(za_ref, PROJ_HEADS)

    @pl.when(j == AT_VS // PROJ_HEADS)
    def _():
        store(za_ref, 0)

    @pl.when((j >= PROJ_A_STEPS) & (j < PROJ_A_STEPS + BT_KC // PROJ_HEADS))
    def _():
        store(zb_ref, 0)

    @pl.when(j == PROJ_A_STEPS + BT_KC // PROJ_HEADS)
    def _():
        store(zb_ref, NSA_KV_HEADS)

    @pl.when(j == PROJ_A_STEPS + BT_GATE // PROJ_HEADS)
    def _():
        for rows, z in chunks():
            zb_ref[rows, :] = jax.nn.sigmoid(z + gb_ref[...])


def _proj(pos, h, pre_g, w, gb, invf, tm=1024):
    n, d = h.shape
    return pl.pallas_call(
        _proj_kernel,
        name="mix_proj",
        grid=(n // tm, PROJ_STEPS),
        in_specs=[
            pl.BlockSpec((tm // ROPE_PACK, LANE), lambda i, j: (i, 0)),
            pl.BlockSpec((tm, d), lambda i, j: (i, 0)),
            pl.BlockSpec((1, d), lambda i, j: (0, 0)),
            pl.BlockSpec((d, PROJ_TN), lambda i, j: (0, j)),
            pl.BlockSpec((1, PROJ_TN), lambda i, j: (0, 0)),
            pl.BlockSpec((1, LANE), lambda i, j: (0, 0)),
        ],
        out_specs=[pl.BlockSpec((tm, PROJ_TN), lambda i, j: (i, jnp.minimum(j, PROJ_A_STEPS - 1))),
                   pl.BlockSpec((tm, PROJ_TN), lambda i, j: (i, jnp.maximum(j - PROJ_A_STEPS, 0)))],
        out_shape=[jax.ShapeDtypeStruct((n, A_WIDTH), BF16), jax.ShapeDtypeStruct((n, B_WIDTH), F32)],
        scratch_shapes=[pltpu.VMEM((tm, d), BF16)] + [pltpu.VMEM((tm, LANE), F32)] * 3,
        compiler_params=pltpu.CompilerParams(
            dimension_semantics=("arbitrary", "arbitrary"), vmem_limit_bytes=VMEM_LIMIT),
    )(pos, h, pre_g, w, gb, invf)


def _cmp_kernel(x_ref, pos_ref, w1_ref, w2_ref, o_ref):
    ncp = o_ref.shape[0]
    xs = [x_ref[pl.ds(l, ncp, stride=CMP_STRIDE), :] for l in range(CMP_STRIDE)]
    half = lambda h: jnp.concatenate(
        [(x + pos_ref[h * CMP_STRIDE + l:h * CMP_STRIDE + l + 1, :]).astype(BF16) for l, x in enumerate(xs)], axis=1)
    acc_a = _dot(half(0), w1_ref[0])
    acc_b = _dot(half(1), w1_ref[1])
    hidden = acc_a + pltpu.roll(acc_b, ncp - 1, axis=0)
    o_ref[...] = _dot(jax.nn.gelu(hidden).astype(BF16), w2_ref[...])


def _compress(zb, pos_kv, w1, w2, batch, seq):
    ncp = seq // CMP_STRIDE
    hidden_w = w1.shape[-1]
    return pl.pallas_call(
        _cmp_kernel,
        name="compress",
        grid=(2, batch, NSA_KV_HEADS),
        in_specs=[
            pl.BlockSpec((seq, LANE), lambda k, b, g: (b, BT_KC + 2 * k + g)),
            pl.BlockSpec((None, CMP_LEN, HEAD_DIM), lambda k, b, g: (k, 0, 0)),
            pl.BlockSpec((None, CMP_LEN // CMP_STRIDE, CMP_STRIDE * HEAD_DIM, hidden_w), lambda k, b, g: (k, 0, 0, 0)),
            pl.BlockSpec((None, hidden_w, HEAD_DIM), lambda k, b, g: (k, 0, 0)),
        ],
        out_specs=pl.BlockSpec((None, None, None, ncp, HEAD_DIM), lambda k, b, g: (k, b, g, 0, 0)),
        out_shape=jax.ShapeDtypeStruct((2, batch, NSA_KV_HEADS, ncp, HEAD_DIM), F32),
        compiler_params=pltpu.CompilerParams(
            dimension_semantics=("arbitrary",) * 3, vmem_limit_bytes=VMEM_LIMIT),
    )(zb, pos_kv, w1, w2)


def _masked_exp(s, mask):
    s = s + jnp.where(mask, 0.0, -MASK_BIG)
    e = jnp.exp2(s - jnp.max(s, axis=-1, keepdims=True))
    return e, jnp.sum(e, axis=-1, keepdims=True)


def _attn_kernel(q_ref, kc_ref, vc_ref, ks_ref, vs_ref, kw_ref, vw_ref, gz_ref, ovl_ref, oh_ref, ga_ref, prev_ref,
                 o_ref, *, first_block):
    del prev_ref
    G, R, tq = NSA_KV_HEADS, GQA_GROUP, Q_BLOCK
    n_keys = ks_ref.shape[0]
    top_k = min(SEL_TOPK, ovl_ref.shape[0])
    n_blk = n_keys // SEL_LEN
    n_cmp = min(kc_ref.shape[1], -(-(n_keys // CMP_STRIDE) // LANE) * LANE)
    t0 = (first_block + pl.program_id(1)) * tq
    t_idx = t0 + lax.broadcasted_iota(jnp.int32, (1, tq, 1), 1)
    gsl = lambda g: slice(g * HEAD_DIM, (g + 1) * HEAD_DIM)

    def stacked_q(g):
        q = q_ref[:, g * R * HEAD_DIM:(g + 1) * R * HEAD_DIM]
        return jnp.concatenate([q[:, r * HEAD_DIM:(r + 1) * HEAD_DIM] for r in range(R)], axis=0)

    q4s = [stacked_q(g) for g in range(G)]

    wk = min(WINDOW + tq, n_keys)
    w0 = pl.multiple_of(jnp.maximum(t0 - WINDOW, 0), tq)
    kp = w0 + lax.broadcasted_iota(jnp.int32, (1, 1, wk), 2)
    win_mask = (kp <= t_idx) & (kp > t_idx - WINDOW)

    ones_col = jnp.ones((n_keys, LANE), BF16)

    def weights(s, m):
        return jnp.exp2((s - m).astype(BF16)).reshape(R * tq, s.shape[-1])

    def pv(p, v, ones):
        return _dot(p, jnp.concatenate([v, ones], axis=1))

    def normalised(acc):
        return (acc[:, :HEAD_DIM] * (1.0 / acc[:, HEAD_DIM:])).reshape(R, tq, HEAD_DIM)

    def window(g):
        s = _dot_nt(q4s[g], kw_ref[pl.ds(w0, wk), gsl(g)]).reshape(R, tq, wk)
        s = s + jnp.where(win_mask, 0.0, -MASK_BIG)
        p = weights(s, jnp.max(s, axis=-1, keepdims=True))
        return normalised(pv(p, vw_ref[pl.ds(w0, wk), gsl(g)], ones_col[0:wk, :]))

    def front(g):
        q4 = q4s[g]

        s = _dot_nt(q4, kc_ref[g, 0:n_cmp, :].astype(BF16)).reshape(R, tq, n_cmp)
        c_idx = lax.broadcasted_iota(jnp.int32, (1, 1, n_cmp), 2)
        e, l = _masked_exp(s, (c_idx * CMP_STRIDE + (CMP_LEN - 1)) <= t_idx)
        pc = e * (jnp.where(t_idx >= CMP_LEN - 1, 1.0, 0.0) / l)
        o_cmp = _dot(pc.reshape(R * tq, n_cmp).astype(BF16), vc_ref[g, 0:n_cmp, :].astype(BF16))
        o_cmp = o_cmp.reshape(R, tq, HEAD_DIM)
        if n_blk <= top_k:
            return q4, jnp.concatenate([q4, jnp.zeros_like(q4)], axis=1), o_cmp

        pcs = pc[0]
        for r in range(1, R):
            pcs = pcs + pc[r]
        hi = pcs.astype(BF16)
        lo = (pcs - hi.astype(F32)).astype(BF16)
        ovl = ovl_ref[0:n_blk, 0:n_cmp]
        imp = _dot_nt(ovl, hi) + _dot_nt(ovl, lo)
        blk = lax.broadcasted_iota(jnp.int32, (n_blk, tq), 0)
        tcol = t0 + lax.broadcasted_iota(jnp.int32, (n_blk, tq), 1)
        forced = (blk == lax.shift_right_logical(tcol, int(np.log2(SEL_LEN)))) | (blk == 0)
        v = jnp.where(forced, SEL_FORCE, jnp.where(blk * SEL_LEN <= tcol, imp, -SEL_FORCE))
        sub = lax.broadcasted_iota(jnp.int32, (SUBLANE, tq), 0)
        groups = [v[k:k + SUBLANE, :] for k in range(0, n_blk, SUBLANE)]
        ranks = [jnp.zeros((SUBLANE, tq), F32) for _ in groups]
        for sp in range(n_blk):
            other = v[sp:sp + 1, :]
            for gi, vg in enumerate(groups):
                first = gi * SUBLANE
                if first > sp:
                    beats = other >= vg
                elif first + SUBLANE - 1 <= sp:
                    beats = other > vg
                else:
                    beats = (other > vg) | ((other == vg) & (sub > sp - first))
                ranks[gi] = ranks[gi] + jnp.where(beats, 1.0, 0.0)
        rank = jnp.concatenate(ranks, axis=0)
        unsel = jnp.where(rank < float(top_k), 0.0, -1.0)
        unsel = jnp.concatenate([unsel, jnp.zeros((LANE - n_blk, tq), F32)], axis=0).T.astype(BF16)
        q_aug = jnp.concatenate([q4, jnp.concatenate([unsel] * R, axis=0)], axis=1)
        return q4, q_aug, o_cmp

    fronts = [front(g) for g in range(G)]

    n_full = n_keys - SEL_KT
    kp = n_full + lax.broadcasted_iota(jnp.int32, (1, 1, SEL_KT), 2)
    causal_bias = jnp.where(kp <= t_idx, 0.0, -MASK_BIG)

    def selected(g):
        k_aug = jnp.concatenate([ks_ref[:, gsl(g)], oh_ref[...]], axis=1)
        s = _dot_nt(fronts[g][1], k_aug).reshape(R, tq, n_keys)
        s_last = s[:, :, n_full:] + causal_bias
        m = jnp.max(s_last, axis=-1, keepdims=True)
        if n_full:
            s_full = s[:, :, :n_full]
            m = jnp.maximum(m, jnp.max(s_full, axis=-1, keepdims=True))
        acc = pv(weights(s_last, m), vs_ref[n_full:n_keys, gsl(g)], ones_col[n_full:n_keys, :])
        if n_full:
            acc = acc + pv(weights(s_full, m), vs_ref[0:n_full, gsl(g)], ones_col[0:n_full, :])
        return normalised(acc)

    o_sels = [selected(g) for g in range(G)]
    o_wins = [window(g) for g in range(G)]
    heads = []
    for g in range(G):
        o_cmp, o_sel, o_win = fronts[g][2], o_sels[g], o_wins[g]
        gz = gz_ref[:, gsl(g)]
        for r in range(R):
            c = r * N_GATES
            heads.append(gz[:, c:c + 1] * o_cmp[r] + gz[:, c + 1:c + 2] * o_sel[r] + gz[:, c + 2:c + 3] * o_win[r])
    o_ref[...] = _rms(jnp.concatenate(heads, axis=1), ga_ref[...]).astype(o_ref.dtype)


def _attention(za, zb, kcv, ovl_t, onehot, g_attn, batch, seq):
    ncp = seq // CMP_STRIDE
    n_sel = seq // SEL_LEN
    G = NSA_KV_HEADS
    blocks = SEL_KT // Q_BLOCK
    za3 = za.reshape(batch, seq, A_WIDTH)
    zb3 = zb.reshape(batch, seq, B_WIDTH)
    out = jnp.zeros((batch, seq, NSA_WIDTH), BF16)
    for c in range(seq // SEL_KT):
        n_keys = (c + 1) * SEL_KT
        q_row = lambda b, i, c=c: (b, c * blocks + i, 0)
        kv_spec = lambda tile: pl.BlockSpec((None, n_keys, G * LANE), lambda b, i: (b, 0, tile // G))
        operands = (za3, kcv, kcv, za3, za3, za3, za3, zb3, ovl_t, onehot, g_attn, out)
        out = pl.pallas_call(
            functools.partial(_attn_kernel, first_block=c * blocks),
            name=f"attention_{c}",
            grid=(batch, blocks),
            in_specs=[
                pl.BlockSpec((None, Q_BLOCK, NSA_WIDTH), q_row),
                pl.BlockSpec((None, None, G, ncp, HEAD_DIM), lambda b, i: (0, b, 0, 0, 0)),
                pl.BlockSpec((None, None, G, ncp, HEAD_DIM), lambda b, i: (1, b, 0, 0, 0)),
                kv_spec(AT_KS), kv_spec(AT_VS), kv_spec(AT_KW), kv_spec(AT_VW),
                pl.BlockSpec((None, Q_BLOCK, G * LANE), lambda b, i, c=c: (b, c * blocks + i, BT_GATE // G)),
                pl.BlockSpec((n_sel, ncp), lambda b, i: (0, 0)),
                pl.BlockSpec((n_keys, LANE), lambda b, i: (0, 0)),
                pl.BlockSpec((1, NSA_WIDTH), lambda b, i: (0, 0)),
                pl.BlockSpec(memory_space=pl.ANY),
            ],
            out_specs=pl.BlockSpec((None, Q_BLOCK, NSA_WIDTH), q_row),
            out_shape=jax.ShapeDtypeStruct((batch, seq, NSA_WIDTH), BF16),
            input_output_aliases={len(operands) - 1: 0},
            compiler_params=pltpu.CompilerParams(
                dimension_semantics=("arbitrary",) * 2, vmem_limit_bytes=VMEM_LIMIT),
        )(*operands)
    return out.reshape(batch * seq, NSA_WIDTH)


def _lru_kernel(zx_ref, zy_ref, cw_ref, cb_ref, wa_ref, ba_ref, wi_ref, bi_ref, lam_ref, gr_ref,
                o_ref, xpad, a_scr, u_scr, h_scr, carry):
    tt, width = zx_ref.shape
    ti = pl.program_id(1)

    @pl.when(ti == 0)
    def _():
        xpad[0:SUBLANE, :] = jnp.zeros((SUBLANE, width), F32)
        carry[...] = jnp.zeros_like(carry)

    xb = zx_ref[...]
    xpad[SUBLANE:SUBLANE + tt, :] = xb
    xc = cb_ref[...] + cw_ref[CONV_WIDTH - 1:CONV_WIDTH, :] * xb
    for k in range(1, CONV_WIDTH):
        xc = xc + cw_ref[CONV_WIDTH - 1 - k:CONV_WIDTH - k, :] * xpad[SUBLANE - k:SUBLANE - k + tt, :]
    xpad[0:SUBLANE, :] = xb[tt - SUBLANE:tt, :]

    sp = jax.nn.softplus(-lam_ref[...])
    row8 = lax.broadcasted_iota(jnp.int32, (1, SUBLANE, LRU_BLOCK_DIM), 1)
    for hb in range(LRU_BLOCKS):
        sl = slice(hb * LRU_BLOCK_DIM, (hb + 1) * LRU_BLOCK_DIM)
        xs = xc[:, sl]
        xs16 = xs.astype(BF16)
        r = jax.nn.sigmoid(_dot(xs16, wa_ref[hb]) + ba_ref[:, sl])
        ig = jax.nn.sigmoid(_dot(xs16, wi_ref[hb]) + bi_ref[:, sl])
        log_a = -LRU_C * r * sp[:, sl]
        a = jnp.exp(log_a)
        u = jnp.sqrt(-jnp.tanh(log_a) * (a * a + 1.0)) * (ig * xs)
        a = a.reshape(tt // SUBLANE, SUBLANE, LRU_BLOCK_DIM)
        u = u.reshape(tt // SUBLANE, SUBLANE, LRU_BLOCK_DIM)
        for s in (1, 2, 4):
            ok = row8 >= s
            a_sh = pltpu.roll(a, s, axis=1)
            u_sh = pltpu.roll(u, s, axis=1)
            u = jnp.where(ok, u + a * u_sh, u)
            a = jnp.where(ok, a * a_sh, a)
        a_scr[:, sl] = a.reshape(tt, LRU_BLOCK_DIM)
        u_scr[:, sl] = u.reshape(tt, LRU_BLOCK_DIM)

    def group(gi, c):
        r0 = pl.multiple_of(gi * SUBLANE, SUBLANE)
        h = u_scr[pl.ds(r0, SUBLANE), :] + a_scr[pl.ds(r0, SUBLANE), :] * c
        h_scr[pl.ds(r0, SUBLANE), :] = h
        return jnp.broadcast_to(h[SUBLANE - 1:SUBLANE, :], (SUBLANE, width))

    c = lax.fori_loop(0, tt // SUBLANE, group, carry[...], unroll=4)
    carry[...] = c
    o_ref[...] = _rms(h_scr[...] * jax.nn.gelu(zy_ref[...]), gr_ref[...]).astype(o_ref.dtype)


def _rglru(zb, conv_w, conv_b, w_a, b_a, w_i, b_i, lam, g_rec, batch, seq, tt=512):
    nt = seq // tt
    wt = LRU_WIDTH // LANE
    full = lambda a: pl.BlockSpec(a.shape, lambda b, t: (0,) * a.ndim)
    zspec = lambda tile: pl.BlockSpec((tt, LRU_WIDTH), lambda b, t: (b * nt + t, tile // wt))
    return pl.pallas_call(
        _lru_kernel,
        name="rglru",
        grid=(batch, nt),
        in_specs=[zspec(BT_X), zspec(BT_Y), full(conv_w), full(conv_b), full(w_a), full(b_a),
                  full(w_i), full(b_i), full(lam), full(g_rec)],
        out_specs=pl.BlockSpec((tt, LRU_WIDTH), lambda b, t: (b * nt + t, 0)),
        out_shape=jax.ShapeDtypeStruct((batch * seq, LRU_WIDTH), BF16),
        scratch_shapes=[pltpu.VMEM((tt + SUBLANE, LRU_WIDTH), F32), pltpu.VMEM((tt, LRU_WIDTH), F32),
                        pltpu.VMEM((tt, LRU_WIDTH), F32), pltpu.VMEM((tt, LRU_WIDTH), F32),
                        pltpu.VMEM((SUBLANE, LRU_WIDTH), F32)],
        compiler_params=pltpu.CompilerParams(
            dimension_semantics=("arbitrary", "arbitrary"), vmem_limit_bytes=VMEM_LIMIT),
    )(zb, zb, conv_w, conv_b, w_a, b_a, w_i, b_i, lam, g_rec)


def _resident_bf16(w_ref, w16_scr):
    @pl.when(pl.program_id(0) == 0)
    def _():
        w16_scr[...] = w_ref[...].astype(BF16)


def _resident(a):
    return pl.BlockSpec(a.shape, lambda i: (0,) * a.ndim, pipeline_mode=pl.Buffered(1))


def _out_kernel(oa_ref, or_ref, w_ref, post_ref, h_ref, o_ref, w16_scr):
    _resident_bf16(w_ref, w16_scr)
    for c in range(h_ref.shape[0] // OUT_CHUNK):
        rows = slice(c * OUT_CHUNK, (c + 1) * OUT_CHUNK)
        y = (_dot(oa_ref[rows, :], w16_scr[0:NSA_WIDTH, :])
             + _dot(or_ref[rows, :], w16_scr[NSA_WIDTH:NSA_WIDTH + LRU_WIDTH, :]))
        o_ref[rows, :] = h_ref[rows, :] + _rms(y, post_ref[...])


def _out_proj(o_attn, o_rec, w_out, post_g, h, tm=512):
    n, d = h.shape
    row = lambda i: (i, 0)
    return pl.pallas_call(
        _out_kernel,
        name="out_proj",
        grid=(n // tm,),
        in_specs=[pl.BlockSpec((tm, NSA_WIDTH), row), pl.BlockSpec((tm, LRU_WIDTH), row),
                  _resident(w_out), _resident(post_g), pl.BlockSpec((tm, d), row)],
        out_specs=pl.BlockSpec((tm, d), row),
        out_shape=jax.ShapeDtypeStruct((n, d), F32),
        scratch_shapes=[pltpu.VMEM(w_out.shape, BF16)],
        compiler_params=pltpu.CompilerParams(
            dimension_semantics=("arbitrary",), vmem_limit_bytes=VMEM_LIMIT),
    )(o_attn, o_rec, w_out, post_g, h)


def _ple_kernel(h_ref, p_ref, pre_ref, wg_ref, wp_ref, post_ref, o_ref, wg16_scr, wp16_scr):
    _resident_bf16(wg_ref, wg16_scr)
    _resident_bf16(wp_ref, wp16_scr)
    for c in range(h_ref.shape[0] // OUT_CHUNK):
        rows = slice(c * OUT_CHUNK, (c + 1) * OUT_CHUNK)
        h = h_ref[rows, :]
        gate = jax.nn.sigmoid(_dot(_rms(h, pre_ref[...]).astype(BF16), wg16_scr[...]))
        pp = _dot(p_ref[rows, :].astype(BF16), wp16_scr[...])
        o_ref[rows, :] = h + _rms(gate * pp, post_ref[...])


def _ple(h, p, pre_g, w_gate, w_proj, post_g, tm=512):
    n, d = h.shape
    row = lambda i: (i, 0)
    return pl.pallas_call(
        _ple_kernel,
        name="ple",
        grid=(n // tm,),
        in_specs=[pl.BlockSpec((tm, d), row), pl.BlockSpec((tm, p.shape[1]), row),
                  _resident(pre_g), _resident(w_gate), _resident(w_proj), _resident(post_g)],
        out_specs=pl.BlockSpec((tm, d), row),
        out_shape=jax.ShapeDtypeStruct((n, d), F32),
        scratch_shapes=[pltpu.VMEM(w_gate.shape, BF16), pltpu.VMEM(w_proj.shape, BF16)],
        compiler_params=pltpu.CompilerParams(
            dimension_semantics=("arbitrary",), vmem_limit_bytes=VMEM_LIMIT),
    )(h, p, pre_g, w_gate, w_proj, post_g)


def _selection_constants(seq):
    ncp = seq // CMP_STRIDE
    n_cmp = (seq - CMP_LEN) // CMP_STRIDE + 1
    n_sel = seq // SEL_LEN
    c = np.arange(ncp)[None, :]
    s = np.arange(n_sel)[:, None]
    ovl_t = ((c * CMP_STRIDE < s * SEL_LEN + SEL_LEN) & (c * CMP_STRIDE + CMP_LEN - 1 >= s * SEL_LEN) & (c < n_cmp))
    onehot = (np.arange(seq)[:, None] // SEL_LEN) == np.arange(LANE)[None, :]
    return jnp.asarray(ovl_t, BF16), jnp.asarray(onehot * MASK_BIG, BF16)


def _pad_cols(a, width):
    return jnp.pad(a, ((0, 0), (0, width - a.shape[1])))


def kernel(x, p, positions, ff1_pre_g, ff1_post_g, ff1_w_gate, ff1_w_up, ff1_w_down, mix_pre_g, mix_post_g, w_in, cmp_pos_k, cmp_pos_v, cmp_k_w1, cmp_k_w2, cmp_v_w1, cmp_v_w2, nsa_gate_b, conv_w, conv_b, rg_w_a, rg_b_a, rg_w_i, rg_b_i, rg_lambda, attn_out_g, rec_out_g, w_out, ff2_pre_g, ff2_post_g, ff2_w_gate, ff2_w_up, ff2_w_down, ple_pre_g, ple_post_g, w_ple_gate, w_ple_proj):
    batch, seq, d = x.shape
    depth = p.shape[0]
    n = batch * seq
    vec = lambda a: a.reshape(1, -1)
    gate_cols = NSA_HEADS * N_GATES
    grp_gates = GQA_GROUP * N_GATES
    o_q, o_kc, o_vc, o_ks, o_vs, o_kw, o_vw = (NSA_WIDTH * 0,) + tuple(NSA_WIDTH + k * KV_WIDTH for k in range(6))
    o_g = NSA_WIDTH + 6 * KV_WIDTH
    o_x = o_g + gate_cols
    o_y = o_x + LRU_WIDTH

    half = jnp.arange(ROPE_HALF, dtype=F32)
    inv_freq = ROPE_THETA ** (-half / ROPE_HALF)
    invf = jnp.tile(jnp.concatenate([inv_freq, inv_freq]), ROPE_PACK).reshape(1, LANE)
    ovl_t, onehot = _selection_constants(seq)
    sub = PROJ_CHUNK // ROPE_PACK
    pos = positions.reshape(n // PROJ_CHUNK, ROPE_PACK, sub).transpose(0, 2, 1).reshape(n // ROPE_PACK, ROPE_PACK)
    pos = jnp.repeat(pos, ROPE_DIM, axis=1)

    h = x.reshape(n, d)
    for i in range(depth):
        wi = w_in[i]
        cols = lambda o, w: wi[:, o:o + w]
        w_in_z = jnp.concatenate(
            [cols(o_q, NSA_WIDTH), cols(o_ks, KV_WIDTH), cols(o_kw, KV_WIDTH), cols(o_vs, KV_WIDTH),
             cols(o_vw, KV_WIDTH), cols(o_x, 2 * LRU_WIDTH), cols(o_kc, 2 * KV_WIDTH)]
            + [_pad_cols(cols(o_g + g * grp_gates, grp_gates), LANE) for g in range(NSA_KV_HEADS)]
            + [jnp.zeros((d, B_WIDTH - (BT_GATE + NSA_KV_HEADS) * LANE), F32)],
            axis=1).astype(BF16)
        gb = nsa_gate_b[i].reshape(NSA_KV_HEADS, grp_gates)
        gate_bias = _pad_cols(jnp.pad(gb, ((0, 0), (0, LANE - grp_gates))).reshape(1, NSA_KV_HEADS * LANE), PROJ_TN)

        h = _ffn(h, vec(ff1_pre_g[i]), ff1_w_gate[i], ff1_w_up[i], ff1_w_down[i], vec(ff1_post_g[i]))
        za, zb = _proj(pos, h, vec(mix_pre_g[i]), w_in_z, gate_bias, invf)
        kcv = _compress(
            zb, jnp.stack([cmp_pos_k[i], cmp_pos_v[i]]),
            jnp.stack([cmp_k_w1[i], cmp_v_w1[i]]).reshape(
                2, CMP_LEN // CMP_STRIDE, CMP_STRIDE * HEAD_DIM, -1).astype(BF16),
            jnp.stack([cmp_k_w2[i], cmp_v_w2[i]]).astype(BF16), batch, seq)
        o_attn = _attention(za, zb, kcv, ovl_t, onehot, vec(attn_out_g[i]), batch, seq)
        o_rec = _rglru(zb, conv_w[i], vec(conv_b[i]), rg_w_a[i].astype(BF16), vec(rg_b_a[i]),
                       rg_w_i[i].astype(BF16), vec(rg_b_i[i]), vec(rg_lambda[i]), vec(rec_out_g[i]), batch, seq)
        h = _out_proj(o_attn, o_rec, w_out[i], vec(mix_post_g[i]), h)
        h = _ffn(h, vec(ff2_pre_g[i]), ff2_w_gate[i], ff2_w_up[i], ff2_w_down[i], vec(ff2_post_g[i]))
        h = _ple(h, p[i].reshape(n, -1), vec(ple_pre_g[i]), w_ple_gate[i], w_ple_proj[i], vec(ple_post_g[i]))
    return h.reshape(batch, seq, d)
```

```python
import functools

import numpy as np
import jax
import jax.numpy as jnp
from jax import lax
from jax.experimental import pallas as pl
from jax.experimental.pallas import tpu as pltpu

F32 = jnp.float32
BF16 = jnp.bfloat16

D_MODEL = 2048
NSA_HEADS = 8
NSA_KV_HEADS = 2
GQA_GROUP = NSA_HEADS // NSA_KV_HEADS
HEAD_DIM = 128
NSA_WIDTH = NSA_HEADS * HEAD_DIM
KV_WIDTH = NSA_KV_HEADS * HEAD_DIM
ROPE_DIM = HEAD_DIM // 4
ROPE_HALF = ROPE_DIM // 2
ROPE_THETA = 500000.0
CMP_LEN = 32
CMP_STRIDE = 16
SEL_LEN = 64
SEL_TOPK = 16
WINDOW = 512
Q_BLOCK = 128
N_GATES = 3
LRU_WIDTH = 1024
LRU_BLOCKS = 8
LRU_BLOCK_DIM = LRU_WIDTH // LRU_BLOCKS
CONV_WIDTH = 4
LRU_C = 8.0
RMS_EPS = 1e-6
NEG = -1e30
SEL_FORCE = 1e4

LANE = 128
SUBLANE = 8
VMEM_LIMIT = 62 * 1024 * 1024

AT_Q, AT_KS, AT_KW, AT_VS, AT_VW = 0, 8, 10, 12, 14
A_WIDTH = 16 * LANE
BT_X, BT_Y, BT_KC, BT_VC, BT_GATE = 0, 8, 16, 18, 20
B_WIDTH = 24 * LANE
PROJ_TN = 512
PROJ_A_STEPS = A_WIDTH // PROJ_TN
PROJ_STEPS = (A_WIDTH + B_WIDTH) // PROJ_TN
PROJ_HEADS = PROJ_TN // LANE
ROW_CHUNK = 256
PROJ_CHUNK = ROW_CHUNK
ROPE_PACK = LANE // ROPE_DIM
FFN_CHUNK = 4 * ROW_CHUNK
OUT_CHUNK = 2 * ROW_CHUNK

Q_SCALE = HEAD_DIM ** -0.5 * 1.4426950408889634
SEL_KT = 512
ATTN_SUB = 2
MASK_BIG = 2.0 ** 100


def _rms(x, g):
    return x * lax.rsqrt(jnp.mean(x * x, axis=-1, keepdims=True) + RMS_EPS) * g


def _dot(a, b):
    return jnp.dot(a, b, preferred_element_type=F32)


def _dot_nt(a, b):
    return lax.dot_general(a, b, (((1,), (1,)), ((), ())), preferred_element_type=F32)


def _ffn_steps(j, n_steps, h_ref, pre_ref, post_ref, o_ref, u_scr, weights):
    def step(first, last):
        wg, wu, wd = weights()
        for c in range(h_ref.shape[0] // FFN_CHUNK):
            rows = slice(c * FFN_CHUNK, (c + 1) * FFN_CHUNK)
            if first:
                u = _rms(h_ref[rows, :], pre_ref[...]).astype(BF16)
                u_scr[rows, :] = u
            else:
                u = u_scr[rows, :]
            g = _dot(u, wg)
            up = _dot(u, wu)
            acc = _dot(((g * jax.nn.sigmoid(g)) * up).astype(BF16), wd)
            if not first:
                acc = o_ref[rows, :] + acc
            if last:
                acc = h_ref[rows, :] + 0.5 * _rms(acc, post_ref[...])
            o_ref[rows, :] = acc

    if n_steps == 1:
        step(True, True)
    else:
        pl.when(j == 0)(lambda: step(True, False))
        pl.when((j > 0) & (j < n_steps - 1))(lambda: step(False, False))
        pl.when(j == n_steps - 1)(lambda: step(False, True))


def _ffn_head_kernel(h_ref, pre_ref, wg_ref, wu_ref, wd_ref, post_ref, o_ref, wg16_ref, wu16_ref, wd16_ref,
                     u_scr, *, n_steps):
    def weights():
        w16 = []
        for src, dst in ((wg_ref, wg16_ref), (wu_ref, wu16_ref), (wd_ref, wd16_ref)):
            w = src[...].astype(BF16)
            dst[...] = w
            w16.append(w)
        return w16

    _ffn_steps(pl.program_id(0), n_steps, h_ref, pre_ref, post_ref, o_ref, u_scr, weights)


def _ffn_tail_kernel(h_ref, head_ref, pre_ref, wg_ref, wu_ref, wd_ref, post_ref, o_ref, u_scr, *,
                     n_steps, n_copy):
    i, j = pl.program_id(0), pl.program_id(1)
    slab = head_ref.shape[0]

    @pl.when((i == 0) & (j < n_copy))
    def _():
        o_ref[pl.ds(pl.multiple_of(j * slab, slab), slab), :] = head_ref[...]

    @pl.when(i > 0)
    def _():
        _ffn_steps(j, n_steps, h_ref, pre_ref, post_ref, o_ref, u_scr,
                   lambda: (wg_ref[...], wu_ref[...], wd_ref[...]))


def _ffn(h, pre_g, wg, wu, wd, post_g, tm=1024, tf=512, tf_head=256):
    n, d = h.shape
    dff = wg.shape[1]
    tf, tf_head = min(tf, dff), min(tf_head, dff)
    vec1 = pl.BlockSpec((1, d), lambda j: (0, 0))
    head, wg16, wu16, wd16 = pl.pallas_call(
        functools.partial(_ffn_head_kernel, n_steps=dff // tf_head),
        name="ffn_head",
        grid=(dff // tf_head,),
        in_specs=[
            pl.BlockSpec((tm, d), lambda j: (0, 0)), vec1,
            pl.BlockSpec((d, tf_head), lambda j: (0, j)),
            pl.BlockSpec((d, tf_head), lambda j: (0, j)),
            pl.BlockSpec((tf_head, d), lambda j: (j, 0)),
            vec1,
        ],
        out_specs=[pl.BlockSpec((tm, d), lambda j: (0, 0)),
                   pl.BlockSpec((d, tf_head), lambda j: (0, j)),
                   pl.BlockSpec((d, tf_head), lambda j: (0, j)),
                   pl.BlockSpec((tf_head, d), lambda j: (j, 0))],
        out_shape=[jax.ShapeDtypeStruct((tm, d), F32), jax.ShapeDtypeStruct((d, dff), BF16),
                   jax.ShapeDtypeStruct((d, dff), BF16), jax.ShapeDtypeStruct((dff, d), BF16)],
        scratch_shapes=[pltpu.VMEM((tm, d), BF16)],
        compiler_params=pltpu.CompilerParams(
            dimension_semantics=("arbitrary",), vmem_limit_bytes=VMEM_LIMIT),
    )(h, pre_g, wg, wu, wd, post_g)

    n_steps = dff // tf
    n_copy = 1 << (min(n_steps, tm // LANE).bit_length() - 1)
    row = lambda i, j: (i, 0)
    vec = pl.BlockSpec((1, d), lambda i, j: (0, 0))
    wcol = lambda i, j: (0, jnp.where(i > 0, j, 0))
    return pl.pallas_call(
        functools.partial(_ffn_tail_kernel, n_steps=n_steps, n_copy=n_copy),
        name="ffn_tail",
        grid=(n // tm, n_steps),
        in_specs=[
            pl.BlockSpec((tm, d), row),
            pl.BlockSpec((tm // n_copy, d), lambda i, j: (jnp.where(i == 0, jnp.minimum(j, n_copy - 1), n_copy - 1), 0)),
            vec,
            pl.BlockSpec((d, tf), wcol),
            pl.BlockSpec((d, tf), wcol),
            pl.BlockSpec((tf, d), lambda i, j: (jnp.where(i > 0, j, 0), 0)),
            vec,
        ],
        out_specs=pl.BlockSpec((tm, d), row),
        out_shape=jax.ShapeDtypeStruct((n, d), F32),
        scratch_shapes=[pltpu.VMEM((tm, d), BF16)],
        compiler_params=pltpu.CompilerParams(
            dimension_semantics=("arbitrary", "arbitrary"), vmem_limit_bytes=VMEM_LIMIT),
    )(h, head, pre_g, wg16, wu16, wd16, post_g)


def _proj_kernel(pos_ref, h_ref, pre_ref, w_ref, gb_ref, invf_ref, za_ref, zb_ref, u_scr, cos_scr, s1_scr, s2_scr):
    j = pl.program_id(1)

    def prepare(rows):
        u_scr[rows, :] = _rms(h_ref[rows, :], pre_ref[...]).astype(BF16)
        sub = PROJ_CHUNK // ROPE_PACK
        packed = slice(rows.start // ROPE_PACK, rows.start // ROPE_PACK + sub)
        ang = pos_ref[packed, :].astype(F32) * invf_ref[...]
        cos, sin = jnp.cos(ang), jnp.sin(ang)
        lane = lax.broadcasted_iota(jnp.int32, ang.shape, 1)
        for k in range(ROPE_PACK):
            grp = slice(rows.start + k * sub, rows.start + (k + 1) * sub)
            c, s = (cos, sin) if k == 0 else (pltpu.roll(cos, LANE - k * ROPE_DIM, axis=1),
                                              pltpu.roll(sin, LANE - k * ROPE_DIM, axis=1))
            cos_scr[grp, :] = jnp.where(lane < ROPE_DIM, c, 1.0)
            s1_scr[grp, :] = jnp.where((lane >= ROPE_HALF) & (lane < ROPE_DIM), s, 0.0)
            s2_scr[grp, :] = jnp.where(lane < ROPE_HALF, -s, 0.0)

    def head(z, rows, hd, rope, mul=None):
        x = z[:, hd * LANE:(hd + 1) * LANE]
        if rope:
            x = (x * cos_scr[rows, :]
                 + pltpu.roll(x, ROPE_HALF, axis=1) * s1_scr[rows, :]
                 + pltpu.roll(x, LANE - ROPE_HALF, axis=1) * s2_scr[rows, :])
        return x if mul is None else x * mul

    def chunks(first=False):
        for c in range(u_scr.shape[0] // PROJ_CHUNK):
            rows = slice(c * PROJ_CHUNK, (c + 1) * PROJ_CHUNK)
            if first:
                prepare(rows)
            yield rows, _dot(u_scr[rows, :], w_ref[...])

    def store(ref, rope_heads, mul=None, first=False):
        for rows, z in chunks(first):
            for hd in range(PROJ_HEADS):
                ref[rows, hd * LANE:(hd + 1) * LANE] = head(z, rows, hd, hd < rope_heads, mul).astype(ref.dtype)

    @pl.when(j == 0)
    def _():
        store(za_ref, PROJ_HEADS, Q_SCALE, first=True)

    @pl.when((j > 0) & (j < AT_KS // PROJ_HEADS))
    def _():
        store(za_ref, PROJ_HEADS, Q_SCALE)

    @pl.when(j == AT_KS // PROJ_HEADS)
    def _():
        store(za_ref, PROJ_HEADS)

    @pl.when(j == AT_VS // PROJ_HEADS)
    def _():
        store(za_ref, 0)

    @pl.when((j >= PROJ_A_STEPS) & (j < PROJ_A_STEPS + BT_KC // PROJ_HEADS))
    def _():
        store(zb_ref, 0)

    @pl.when(j == PROJ_A_STEPS + BT_KC // PROJ_HEADS)
    def _():
        store(zb_ref, NSA_KV_HEADS)

    @pl.when(j == PROJ_A_STEPS + BT_GATE // PROJ_HEADS)
    def _():
        for rows, z in chunks():
            zb_ref[rows, :] = jax.nn.sigmoid(z + gb_ref[...])


def _proj(pos, h, pre_g, w, gb, invf, tm=1024):
    n, d = h.shape
    return pl.pallas_call(
        _proj_kernel,
        name="mix_proj",
        grid=(n // tm, PROJ_STEPS),
        in_specs=[
            pl.BlockSpec((tm // ROPE_PACK, LANE), lambda i, j: (i, 0)),
            pl.BlockSpec((tm, d), lambda i, j: (i, 0)),
            pl.BlockSpec((1, d), lambda i, j: (0, 0)),
            pl.BlockSpec((d, PROJ_TN), lambda i, j: (0, j)),
            pl.BlockSpec((1, PROJ_TN), lambda i, j: (0, 0)),
            pl.BlockSpec((1, LANE), lambda i, j: (0, 0)),
        ],
        out_specs=[pl.BlockSpec((tm, PROJ_TN), lambda i, j: (i, jnp.minimum(j, PROJ_A_STEPS - 1))),
                   pl.BlockSpec((tm, PROJ_TN), lambda i, j: (i, jnp.maximum(j - PROJ_A_STEPS, 0)))],
        out_shape=[jax.ShapeDtypeStruct((n, A_WIDTH), BF16), jax.ShapeDtypeStruct((n, B_WIDTH), F32)],
        scratch_shapes=[pltpu.VMEM((tm, d), BF16)] + [pltpu.VMEM((tm, LANE), F32)] * 3,
        compiler_params=pltpu.CompilerParams(
            dimension_semantics=("arbitrary", "arbitrary"), vmem_limit_bytes=VMEM_LIMIT),
    )(pos, h, pre_g, w, gb, invf)


def _cmp_kernel(x_ref, pos_ref, w1_ref, w2_ref, o_ref):
    ncp = o_ref.shape[0]
    hidden_w = w1_ref.shape[-1]
    acc_a = jnp.zeros((ncp, hidden_w), F32)
    acc_b = jnp.zeros((ncp, hidden_w), F32)
    for l in range(CMP_STRIDE):
        xl = x_ref[pl.ds(l, ncp, stride=CMP_STRIDE), :]
        acc_a += _dot((xl + pos_ref[l:l + 1, :]).astype(BF16), w1_ref[l])
        acc_b += _dot((xl + pos_ref[CMP_STRIDE + l:CMP_STRIDE + l + 1, :]).astype(BF16), w1_ref[CMP_STRIDE + l])
    hidden = acc_a + pltpu.roll(acc_b, ncp - 1, axis=0)
    o_ref[...] = _dot(jax.nn.gelu(hidden).astype(BF16), w2_ref[...])


def _compress(zb, pos_kv, w1, w2, batch, seq):
    ncp = seq // CMP_STRIDE
    hidden_w = w1.shape[-1]
    return pl.pallas_call(
        _cmp_kernel,
        name="compress",
        grid=(2, batch, NSA_KV_HEADS),
        in_specs=[
            pl.BlockSpec((seq, LANE), lambda k, b, g: (b, BT_KC + 2 * k + g)),
            pl.BlockSpec((None, CMP_LEN, HEAD_DIM), lambda k, b, g: (k, 0, 0)),
            pl.BlockSpec((None, CMP_LEN, HEAD_DIM, hidden_w), lambda k, b, g: (k, 0, 0, 0)),
            pl.BlockSpec((None, hidden_w, HEAD_DIM), lambda k, b, g: (k, 0, 0)),
        ],
        out_specs=pl.BlockSpec((None, None, None, ncp, HEAD_DIM), lambda k, b, g: (k, b, g, 0, 0)),
        out_shape=jax.ShapeDtypeStruct((2, batch, NSA_KV_HEADS, ncp, HEAD_DIM), F32),
        compiler_params=pltpu.CompilerParams(
            dimension_semantics=("arbitrary",) * 3, vmem_limit_bytes=VMEM_LIMIT),
    )(zb, pos_kv, w1, w2)


def _masked_exp(s, mask):
    s = s + jnp.where(mask, 0.0, -MASK_BIG)
    e = jnp.exp2(s - jnp.max(s, axis=-1, keepdims=True))
    return e, jnp.sum(e, axis=-1, keepdims=True)


def _attn_kernel(q_ref, kc_ref, vc_ref, ks_ref, vs_ref, kw_ref, vw_ref, gz_ref, ovl_ref, oh_ref, ga_ref, prev_ref,
                 o_ref, *, first_block):
    del prev_ref
    for sb in range(ATTN_SUB):
        _attn_block(q_ref, kc_ref, vc_ref, ks_ref, vs_ref, kw_ref, vw_ref, gz_ref, ovl_ref, oh_ref, ga_ref, o_ref,
                    rows=slice(sb * Q_BLOCK, (sb + 1) * Q_BLOCK),
                    block=first_block + pl.program_id(1) * ATTN_SUB + sb)


def _attn_block(q_ref, kc_ref, vc_ref, ks_ref, vs_ref, kw_ref, vw_ref, gz_ref, ovl_ref, oh_ref, ga_ref, o_ref,
                *, rows, block):
    G, R, tq = NSA_KV_HEADS, GQA_GROUP, Q_BLOCK
    n_keys = ks_ref.shape[0]
    top_k = min(SEL_TOPK, ovl_ref.shape[0])
    n_blk = n_keys // SEL_LEN
    n_cmp = min(kc_ref.shape[1], -(-(n_keys // CMP_STRIDE) // LANE) * LANE)
    t0 = block * tq
    t_idx = t0 + lax.broadcasted_iota(jnp.int32, (1, tq, 1), 1)
    gsl = lambda g: slice(g * HEAD_DIM, (g + 1) * HEAD_DIM)

    def stacked_q(g):
        q = q_ref[rows, g * R * HEAD_DIM:(g + 1) * R * HEAD_DIM]
        return jnp.concatenate([q[:, r * HEAD_DIM:(r + 1) * HEAD_DIM] for r in range(R)], axis=0)

    q4s = [stacked_q(g) for g in range(G)]

    wk = min(WINDOW + tq, n_keys)
    w0 = pl.multiple_of(jnp.maximum(t0 - WINDOW, 0), tq)
    kp = w0 + lax.broadcasted_iota(jnp.int32, (1, 1, wk), 2)
    win_mask = (kp <= t_idx) & (kp > t_idx - WINDOW)

    ones_col = jnp.ones((n_keys, LANE), BF16)

    def weights(s, m):
        return jnp.exp2((s - m).astype(BF16)).reshape(R * tq, s.shape[-1])

    def pv(p, v, ones):
        return _dot(p, jnp.concatenate([v, ones], axis=1))

    def normalised(acc):
        return (acc[:, :HEAD_DIM] * (1.0 / acc[:, HEAD_DIM:])).reshape(R, tq, HEAD_DIM)

    def window(g):
        s = _dot_nt(q4s[g], kw_ref[pl.ds(w0, wk), gsl(g)]).reshape(R, tq, wk)
        s = s + jnp.where(win_mask, 0.0, -MASK_BIG)
        p = weights(s, jnp.max(s, axis=-1, keepdims=True))
        return normalised(pv(p, vw_ref[pl.ds(w0, wk), gsl(g)], ones_col[0:wk, :]))

    def front(g):
        q4 = q4s[g]

        s = _dot_nt(q4, kc_ref[g, 0:n_cmp, :].astype(BF16)).reshape(R, tq, n_cmp)
        c_idx = lax.broadcasted_iota(jnp.int32, (1, 1, n_cmp), 2)
        e, l = _masked_exp(s, (c_idx * CMP_STRIDE + (CMP_LEN - 1)) <= t_idx)
        pc = e * (jnp.where(t_idx >= CMP_LEN - 1, 1.0, 0.0) / l)
        o_cmp = _dot(pc.reshape(R * tq, n_cmp).astype(BF16), vc_ref[g, 0:n_cmp, :].astype(BF16))
        o_cmp = o_cmp.reshape(R, tq, HEAD_DIM)
        if n_blk <= top_k:
            return q4, jnp.concatenate([q4, jnp.zeros_like(q4)], axis=1), o_cmp

        pcs = pc[0]
        for r in range(1, R):
            pcs = pcs + pc[r]
        hi = pcs.astype(BF16)
        lo = (pcs - hi.astype(F32)).astype(BF16)
        ovl = ovl_ref[0:n_blk, 0:n_cmp]
        imp = _dot_nt(ovl, hi) + _dot_nt(ovl, lo)
        blk = lax.broadcasted_iota(jnp.int32, (n_blk, tq), 0)
        tcol = t0 + lax.broadcasted_iota(jnp.int32, (n_blk, tq), 1)
        forced = (blk == lax.shift_right_logical(tcol, int(np.log2(SEL_LEN)))) | (blk == 0)
        v = jnp.where(forced, SEL_FORCE, jnp.where(blk * SEL_LEN <= tcol, imp, -SEL_FORCE))
        sub = lax.broadcasted_iota(jnp.int32, (SUBLANE, tq), 0)
        groups = [v[k:k + SUBLANE, :] for k in range(0, n_blk, SUBLANE)]
        ranks = [jnp.zeros((SUBLANE, tq), F32) for _ in groups]
        for sp in range(n_blk):
            other = v[sp:sp + 1, :]
            for gi, vg in enumerate(groups):
                first = gi * SUBLANE
                if first > sp:
                    beats = other >= vg
                elif first + SUBLANE - 1 <= sp:
                    beats = other > vg
                else:
                    beats = (other > vg) | ((other == vg) & (sub > sp - first))
                ranks[gi] = ranks[gi] + jnp.where(beats, 1.0, 0.0)
        rank = jnp.concatenate(ranks, axis=0)
        unsel = jnp.where(rank < float(top_k), 0.0, -1.0)
        unsel = jnp.concatenate([unsel, jnp.zeros((LANE - n_blk, tq), F32)], axis=0).T.astype(BF16)
        q_aug = jnp.concatenate([q4, jnp.concatenate([unsel] * R, axis=0)], axis=1)
        return q4, q_aug, o_cmp

    fronts = [front(g) for g in range(G)]

    n_full = n_keys - SEL_KT
    kp = n_full + lax.broadcasted_iota(jnp.int32, (1, 1, SEL_KT), 2)
    causal_bias = jnp.where(kp <= t_idx, 0.0, -MASK_BIG)

    def selected(g):
        k_aug = jnp.concatenate([ks_ref[:, gsl(g)], oh_ref[...]], axis=1)
        s = _dot_nt(fronts[g][1], k_aug).reshape(R, tq, n_keys)
        s_last = s[:, :, n_full:] + causal_bias
        m = jnp.max(s_last, axis=-1, keepdims=True)
        if n_full:
            s_full = s[:, :, :n_full]
            m = jnp.maximum(m, jnp.max(s_full, axis=-1, keepdims=True))
        acc = pv(weights(s_last, m), vs_ref[n_full:n_keys, gsl(g)], ones_col[n_full:n_keys, :])
        if n_full:
            acc = acc + pv(weights(s_full, m), vs_ref[0:n_full, gsl(g)], ones_col[0:n_full, :])
        return normalised(acc)

    o_sels = [selected(g) for g in range(G)]
    o_wins = [window(g) for g in range(G)]
    heads = []
    for g in range(G):
        o_cmp, o_sel, o_win = fronts[g][2], o_sels[g], o_wins[g]
        gz = gz_ref[rows, gsl(g)]
        for r in range(R):
            c = r * N_GATES
            heads.append(gz[:, c:c + 1] * o_cmp[r] + gz[:, c + 1:c + 2] * o_sel[r] + gz[:, c + 2:c + 3] * o_win[r])
    o_ref[rows, :] = _rms(jnp.concatenate(heads, axis=1), ga_ref[...]).astype(o_ref.dtype)


def _attention(za, zb, kcv, ovl_t, onehot, g_attn, batch, seq):
    ncp = seq // CMP_STRIDE
    n_sel = seq // SEL_LEN
    G = NSA_KV_HEADS
    blocks = SEL_KT // Q_BLOCK
    steps = blocks // ATTN_SUB
    za3 = za.reshape(batch, seq, A_WIDTH)
    zb3 = zb.reshape(batch, seq, B_WIDTH)
    out = jnp.zeros((batch, seq, NSA_WIDTH), BF16)
    for c in range(seq // SEL_KT):
        n_keys = (c + 1) * SEL_KT
        q_row = lambda b, i, c=c: (b, c * steps + i, 0)
        kv_spec = lambda tile: pl.BlockSpec((None, n_keys, G * LANE), lambda b, i: (b, 0, tile // G))
        operands = (za3, kcv, kcv, za3, za3, za3, za3, zb3, ovl_t, onehot, g_attn, out)
        out = pl.pallas_call(
            functools.partial(_attn_kernel, first_block=c * blocks),
            name=f"attention_{c}",
            grid=(batch, steps),
            in_specs=[
                pl.BlockSpec((None, ATTN_SUB * Q_BLOCK, NSA_WIDTH), q_row),
                pl.BlockSpec((None, None, G, ncp, HEAD_DIM), lambda b, i: (0, b, 0, 0, 0)),
                pl.BlockSpec((None, None, G, ncp, HEAD_DIM), lambda b, i: (1, b, 0, 0, 0)),
                kv_spec(AT_KS), kv_spec(AT_VS), kv_spec(AT_KW), kv_spec(AT_VW),
                pl.BlockSpec((None, ATTN_SUB * Q_BLOCK, G * LANE), lambda b, i, c=c: (b, c * steps + i, BT_GATE // G)),
                pl.BlockSpec((n_sel, ncp), lambda b, i: (0, 0)),
                pl.BlockSpec((n_keys, LANE), lambda b, i: (0, 0)),
                pl.BlockSpec((1, NSA_WIDTH), lambda b, i: (0, 0)),
                pl.BlockSpec(memory_space=pl.ANY),
            ],
            out_specs=pl.BlockSpec((None, ATTN_SUB * Q_BLOCK, NSA_WIDTH), q_row),
            out_shape=jax.ShapeDtypeStruct((batch, seq, NSA_WIDTH), BF16),
            input_output_aliases={len(operands) - 1: 0},
            compiler_params=pltpu.CompilerParams(
                dimension_semantics=("arbitrary",) * 2, vmem_limit_bytes=VMEM_LIMIT),
        )(*operands)
    return out.reshape(batch * seq, NSA_WIDTH)


def _lru_kernel(zx_ref, zy_ref, cw_ref, cb_ref, wa_ref, ba_ref, wi_ref, bi_ref, lam_ref, gr_ref,
                o_ref, xpad, a_scr, u_scr, h_scr, carry):
    tt, width = zx_ref.shape
    ti = pl.program_id(1)

    @pl.when(ti == 0)
    def _():
        xpad[0:SUBLANE, :] = jnp.zeros((SUBLANE, width), F32)
        carry[...] = jnp.zeros_like(carry)

    xb = zx_ref[...]
    xpad[SUBLANE:SUBLANE + tt, :] = xb
    xc = cb_ref[...] + cw_ref[CONV_WIDTH - 1:CONV_WIDTH, :] * xb
    for k in range(1, CONV_WIDTH):
        xc = xc + cw_ref[CONV_WIDTH - 1 - k:CONV_WIDTH - k, :] * xpad[SUBLANE - k:SUBLANE - k + tt, :]
    xpad[0:SUBLANE, :] = xb[tt - SUBLANE:tt, :]

    sp = jax.nn.softplus(-lam_ref[...])
    row8 = lax.broadcasted_iota(jnp.int32, (1, SUBLANE, LRU_BLOCK_DIM), 1)
    for hb in range(LRU_BLOCKS):
        sl = slice(hb * LRU_BLOCK_DIM, (hb + 1) * LRU_BLOCK_DIM)
        xs = xc[:, sl]
        xs16 = xs.astype(BF16)
        r = jax.nn.sigmoid(_dot(xs16, wa_ref[hb]) + ba_ref[:, sl])
        ig = jax.nn.sigmoid(_dot(xs16, wi_ref[hb]) + bi_ref[:, sl])
        log_a = -LRU_C * r * sp[:, sl]
        a = jnp.exp(log_a)
        u = jnp.sqrt(-jnp.tanh(log_a) * (a * a + 1.0)) * (ig * xs)
        a = a.reshape(tt // SUBLANE, SUBLANE, LRU_BLOCK_DIM)
        u = u.reshape(tt // SUBLANE, SUBLANE, LRU_BLOCK_DIM)
        for s in (1, 2, 4):
            ok = row8 >= s
            a_sh = pltpu.roll(a, s, axis=1)
            u_sh = pltpu.roll(u, s, axis=1)
            u = jnp.where(ok, u + a * u_sh, u)
            a = jnp.where(ok, a * a_sh, a)
        a_scr[:, sl] = a.reshape(tt, LRU_BLOCK_DIM)
        u_scr[:, sl] = u.reshape(tt, LRU_BLOCK_DIM)

    def group(gi, c):
        r0 = pl.multiple_of(gi * SUBLANE, SUBLANE)
        h = u_scr[pl.ds(r0, SUBLANE), :] + a_scr[pl.ds(r0, SUBLANE), :] * c
        h_scr[pl.ds(r0, SUBLANE), :] = h
        return jnp.broadcast_to(h[SUBLANE - 1:SUBLANE, :], (SUBLANE, width))

    c = lax.fori_loop(0, tt // SUBLANE, group, carry[...], unroll=4)
    carry[...] = c
    o_ref[...] = _rms(h_scr[...] * jax.nn.gelu(zy_ref[...]), gr_ref[...]).astype(o_ref.dtype)


def _rglru(zb, conv_w, conv_b, w_a, b_a, w_i, b_i, lam, g_rec, batch, seq, tt=256):
    nt = seq // tt
    wt = LRU_WIDTH // LANE
    full = lambda a: pl.BlockSpec(a.shape, lambda b, t: (0,) * a.ndim)
    zspec = lambda tile: pl.BlockSpec((tt, LRU_WIDTH), lambda b, t: (b * nt + t, tile // wt))
    return pl.pallas_call(
        _lru_kernel,
        name="rglru",
        grid=(batch, nt),
        in_specs=[zspec(BT_X), zspec(BT_Y), full(conv_w), full(conv_b), full(w_a), full(b_a),
                  full(w_i), full(b_i), full(lam), full(g_rec)],
        out_specs=pl.BlockSpec((tt, LRU_WIDTH), lambda b, t: (b * nt + t, 0)),
        out_shape=jax.ShapeDtypeStruct((batch * seq, LRU_WIDTH), BF16),
        scratch_shapes=[pltpu.VMEM((tt + SUBLANE, LRU_WIDTH), F32), pltpu.VMEM((tt, LRU_WIDTH), F32),
                        pltpu.VMEM((tt, LRU_WIDTH), F32), pltpu.VMEM((tt, LRU_WIDTH), F32),
                        pltpu.VMEM((SUBLANE, LRU_WIDTH), F32)],
        compiler_params=pltpu.CompilerParams(
            dimension_semantics=("arbitrary", "arbitrary"), vmem_limit_bytes=VMEM_LIMIT),
    )(zb, zb, conv_w, conv_b, w_a, b_a, w_i, b_i, lam, g_rec)


def _resident_bf16(w_ref, w16_scr):
    @pl.when(pl.program_id(0) == 0)
    def _():
        w16_scr[...] = w_ref[...].astype(BF16)


def _resident(a):
    return pl.BlockSpec(a.shape, lambda i: (0,) * a.ndim, pipeline_mode=pl.Buffered(1))


def _out_kernel(oa_ref, or_ref, w_ref, post_ref, h_ref, o_ref, w16_scr):
    _resident_bf16(w_ref, w16_scr)
    for c in range(h_ref.shape[0] // OUT_CHUNK):
        rows = slice(c * OUT_CHUNK, (c + 1) * OUT_CHUNK)
        y = (_dot(oa_ref[rows, :], w16_scr[0:NSA_WIDTH, :])
             + _dot(or_ref[rows, :], w16_scr[NSA_WIDTH:NSA_WIDTH + LRU_WIDTH, :]))
        o_ref[rows, :] = h_ref[rows, :] + _rms(y, post_ref[...])


def _out_proj(o_attn, o_rec, w_out, post_g, h, tm=512):
    n, d = h.shape
    row = lambda i: (i, 0)
    return pl.pallas_call(
        _out_kernel,
        name="out_proj",
        grid=(n // tm,),
        in_specs=[pl.BlockSpec((tm, NSA_WIDTH), row), pl.BlockSpec((tm, LRU_WIDTH), row),
                  _resident(w_out), _resident(post_g), pl.BlockSpec((tm, d), row)],
        out_specs=pl.BlockSpec((tm, d), row),
        out_shape=jax.ShapeDtypeStruct((n, d), F32),
        scratch_shapes=[pltpu.VMEM(w_out.shape, BF16)],
        compiler_params=pltpu.CompilerParams(
            dimension_semantics=("arbitrary",), vmem_limit_bytes=VMEM_LIMIT),
    )(o_attn, o_rec, w_out, post_g, h)


def _ple_kernel(h_ref, p_ref, pre_ref, wg_ref, wp_ref, post_ref, o_ref, wg16_scr, wp16_scr):
    _resident_bf16(wg_ref, wg16_scr)
    _resident_bf16(wp_ref, wp16_scr)
    for c in range(h_ref.shape[0] // OUT_CHUNK):
        rows = slice(c * OUT_CHUNK, (c + 1) * OUT_CHUNK)
        h = h_ref[rows, :]
        gate = jax.nn.sigmoid(_dot(_rms(h, pre_ref[...]).astype(BF16), wg16_scr[...]))
        pp = _dot(p_ref[rows, :].astype(BF16), wp16_scr[...])
        o_ref[rows, :] = h + _rms(gate * pp, post_ref[...])


def _ple(h, p, pre_g, w_gate, w_proj, post_g, tm=512):
    n, d = h.shape
    row = lambda i: (i, 0)
    return pl.pallas_call(
        _ple_kernel,
        name="ple",
        grid=(n // tm,),
        in_specs=[pl.BlockSpec((tm, d), row), pl.BlockSpec((tm, p.shape[1]), row),
                  _resident(pre_g), _resident(w_gate), _resident(w_proj), _resident(post_g)],
        out_specs=pl.BlockSpec((tm, d), row),
        out_shape=jax.ShapeDtypeStruct((n, d), F32),
        scratch_shapes=[pltpu.VMEM(w_gate.shape, BF16), pltpu.VMEM(w_proj.shape, BF16)],
        compiler_params=pltpu.CompilerParams(
            dimension_semantics=("arbitrary",), vmem_limit_bytes=VMEM_LIMIT),
    )(h, p, pre_g, w_gate, w_proj, post_g)


def _selection_constants(seq):
    ncp = seq // CMP_STRIDE
    n_cmp = (seq - CMP_LEN) // CMP_STRIDE + 1
    n_sel = seq // SEL_LEN
    c = np.arange(ncp)[None, :]
    s = np.arange(n_sel)[:, None]
    ovl_t = ((c * CMP_STRIDE < s * SEL_LEN + SEL_LEN) & (c * CMP_STRIDE + CMP_LEN - 1 >= s * SEL_LEN) & (c < n_cmp))
    onehot = (np.arange(seq)[:, None] // SEL_LEN) == np.arange(LANE)[None, :]
    return jnp.asarray(ovl_t, BF16), jnp.asarray(onehot * MASK_BIG, BF16)


def _pad_cols(a, width):
    return jnp.pad(a, ((0, 0), (0, width - a.shape[1])))


def kernel(x, p, positions, ff1_pre_g, ff1_post_g, ff1_w_gate, ff1_w_up, ff1_w_down, mix_pre_g, mix_post_g, w_in, cmp_pos_k, cmp_pos_v, cmp_k_w1, cmp_k_w2, cmp_v_w1, cmp_v_w2, nsa_gate_b, conv_w, conv_b, rg_w_a, rg_b_a, rg_w_i, rg_b_i, rg_lambda, attn_out_g, rec_out_g, w_out, ff2_pre_g, ff2_post_g, ff2_w_gate, ff2_w_up, ff2_w_down, ple_pre_g, ple_post_g, w_ple_gate, w_ple_proj):
    batch, seq, d = x.shape
    depth = p.shape[0]
    n = batch * seq
    vec = lambda a: a.reshape(1, -1)
    gate_cols = NSA_HEADS * N_GATES
    grp_gates = GQA_GROUP * N_GATES
    o_q, o_kc, o_vc, o_ks, o_vs, o_kw, o_vw = (NSA_WIDTH * 0,) + tuple(NSA_WIDTH + k * KV_WIDTH for k in range(6))
    o_g = NSA_WIDTH + 6 * KV_WIDTH
    o_x = o_g + gate_cols
    o_y = o_x + LRU_WIDTH

    half = jnp.arange(ROPE_HALF, dtype=F32)
    inv_freq = ROPE_THETA ** (-half / ROPE_HALF)
    invf = jnp.tile(jnp.concatenate([inv_freq, inv_freq]), ROPE_PACK).reshape(1, LANE)
    ovl_t, onehot = _selection_constants(seq)
    sub = PROJ_CHUNK // ROPE_PACK
    pos = positions.reshape(n // PROJ_CHUNK, ROPE_PACK, sub).transpose(0, 2, 1).reshape(n // ROPE_PACK, ROPE_PACK)
    pos = jnp.repeat(pos, ROPE_DIM, axis=1)

    h = x.reshape(n, d)
    for i in range(depth):
        wi = w_in[i]
        cols = lambda o, w: wi[:, o:o + w]
        w_in_z = jnp.concatenate(
            [cols(o_q, NSA_WIDTH), cols(o_ks, KV_WIDTH), cols(o_kw, KV_WIDTH), cols(o_vs, KV_WIDTH),
             cols(o_vw, KV_WIDTH), cols(o_x, LRU_WIDTH), cols(o_y, LRU_WIDTH), cols(o_kc, KV_WIDTH),
             cols(o_vc, KV_WIDTH)]
            + [_pad_cols(cols(o_g + g * grp_gates, grp_gates), LANE) for g in range(NSA_KV_HEADS)]
            + [jnp.zeros((d, B_WIDTH - (BT_GATE + NSA_KV_HEADS) * LANE), F32)],
            axis=1).astype(BF16)
        gb = nsa_gate_b[i].reshape(NSA_KV_HEADS, grp_gates)
        gate_bias = _pad_cols(jnp.pad(gb, ((0, 0), (0, LANE - grp_gates))).reshape(1, NSA_KV_HEADS * LANE), PROJ_TN)

        h = _ffn(h, vec(ff1_pre_g[i]), ff1_w_gate[i], ff1_w_up[i], ff1_w_down[i], vec(ff1_post_g[i]))
        za, zb = _proj(pos, h, vec(mix_pre_g[i]), w_in_z, gate_bias, invf)
        kcv = _compress(
            zb, jnp.stack([cmp_pos_k[i], cmp_pos_v[i]]),
            jnp.stack([cmp_k_w1[i], cmp_v_w1[i]]).reshape(2, CMP_LEN, HEAD_DIM, -1).astype(BF16),
            jnp.stack([cmp_k_w2[i], cmp_v_w2[i]]).astype(BF16), batch, seq)
        o_attn = _attention(za, zb, kcv, ovl_t, onehot, vec(attn_out_g[i]), batch, seq)
        o_rec = _rglru(zb, conv_w[i], vec(conv_b[i]), rg_w_a[i].astype(BF16), vec(rg_b_a[i]),
                       rg_w_i[i].astype(BF16), vec(rg_b_i[i]), vec(rg_lambda[i]), vec(rec_out_g[i]), batch, seq)
        h = _out_proj(o_attn, o_rec, w_out[i], vec(mix_post_g[i]), h)
        h = _ffn(h, vec(ff2_pre_g[i]), ff2_w_gate[i], ff2_w_up[i], ff2_w_down[i], vec(ff2_post_g[i]))
        h = _ple(h, p[i].reshape(n, -1), vec(ple_pre_g[i]), w_ple_gate[i], w_ple_proj[i], vec(ple_post_g[i]))
    return h.reshape(batch, seq, d)
```

```python
import functools

import numpy as np
import jax
import jax.numpy as jnp
from jax import lax
from jax.experimental import pallas as pl
from jax.experimental.pallas import tpu as pltpu

F32 = jnp.float32
BF16 = jnp.bfloat16

D_MODEL = 2048
NSA_HEADS = 8
NSA_KV_HEADS = 2
GQA_GROUP = NSA_HEADS // NSA_KV_HEADS
HEAD_DIM = 128
NSA_WIDTH = NSA_HEADS * HEAD_DIM
KV_WIDTH = NSA_KV_HEADS * HEAD_DIM
ROPE_DIM = HEAD_DIM // 4
ROPE_HALF = ROPE_DIM // 2
ROPE_THETA = 500000.0
CMP_LEN = 32
CMP_STRIDE = 16
SEL_LEN = 64
SEL_TOPK = 16
WINDOW = 512
Q_BLOCK = 128
N_GATES = 3
LRU_WIDTH = 1024
LRU_BLOCKS = 8
LRU_BLOCK_DIM = LRU_WIDTH // LRU_BLOCKS
CONV_WIDTH = 4
LRU_C = 8.0
RMS_EPS = 1e-6
NEG = -1e30
SEL_FORCE = 1e4

LANE = 128
SUBLANE = 8
VMEM_LIMIT = 62 * 1024 * 1024

AT_Q, AT_KS, AT_KW, AT_VS, AT_VW = 0, 8, 10, 12, 14
A_WIDTH = 16 * LANE
BT_X, BT_Y, BT_KC, BT_VC, BT_GATE = 0, 8, 16, 18, 20
B_WIDTH = 24 * LANE
PROJ_TN = 512
PROJ_A_STEPS = A_WIDTH // PROJ_TN
PROJ_STEPS = (A_WIDTH + B_WIDTH) // PROJ_TN
PROJ_HEADS = PROJ_TN // LANE
ROW_CHUNK = 256
PROJ_CHUNK = ROW_CHUNK
ROPE_PACK = LANE // ROPE_DIM
FFN_CHUNK = 4 * ROW_CHUNK
OUT_CHUNK = 2 * ROW_CHUNK

Q_SCALE = HEAD_DIM ** -0.5 * 1.4426950408889634
SEL_KT = 512
ATTN_SUB = 4
MASK_BIG = 2.0 ** 100


def _rms(x, g):
    return x * lax.rsqrt(jnp.mean(x * x, axis=-1, keepdims=True) + RMS_EPS) * g


def _dot(a, b):
    return jnp.dot(a, b, preferred_element_type=F32)


def _dot_nt(a, b):
    return lax.dot_general(a, b, (((1,), (1,)), ((), ())), preferred_element_type=F32)


def _ffn_steps(j, n_steps, h_ref, pre_ref, post_ref, o_ref, u_scr, weights):
    def step(first, last):
        wg, wu, wd = weights()
        for c in range(h_ref.shape[0] // FFN_CHUNK):
            rows = slice(c * FFN_CHUNK, (c + 1) * FFN_CHUNK)
            if first:
                u = _rms(h_ref[rows, :], pre_ref[...]).astype(BF16)
                u_scr[rows, :] = u
            else:
                u = u_scr[rows, :]
            g = _dot(u, wg)
            up = _dot(u, wu)
            acc = _dot(((g * jax.nn.sigmoid(g)) * up).astype(BF16), wd)
            if not first:
                acc = o_ref[rows, :] + acc
            if last:
                acc = h_ref[rows, :] + 0.5 * _rms(acc, post_ref[...])
            o_ref[rows, :] = acc

    if n_steps == 1:
        step(True, True)
    else:
        pl.when(j == 0)(lambda: step(True, False))
        pl.when((j > 0) & (j < n_steps - 1))(lambda: step(False, False))
        pl.when(j == n_steps - 1)(lambda: step(False, True))


def _ffn_head_kernel(h_ref, pre_ref, wg_ref, wu_ref, wd_ref, post_ref, o_ref, wg16_ref, wu16_ref, wd16_ref,
                     u_scr, *, n_steps):
    def weights():
        w16 = []
        for src, dst in ((wg_ref, wg16_ref), (wu_ref, wu16_ref), (wd_ref, wd16_ref)):
            w = src[...].astype(BF16)
            dst[...] = w
            w16.append(w)
        return w16

    _ffn_steps(pl.program_id(0), n_steps, h_ref, pre_ref, post_ref, o_ref, u_scr, weights)


def _ffn_tail_kernel(h_ref, head_ref, pre_ref, wg_ref, wu_ref, wd_ref, post_ref, o_ref, u_scr, *,
                     n_steps, n_copy):
    i, j = pl.program_id(0), pl.program_id(1)
    slab = head_ref.shape[0]

    @pl.when((i == 0) & (j < n_copy))
    def _():
        o_ref[pl.ds(pl.multiple_of(j * slab, slab), slab), :] = head_ref[...]

    @pl.when(i > 0)
    def _():
        _ffn_steps(j, n_steps, h_ref, pre_ref, post_ref, o_ref, u_scr,
                   lambda: (wg_ref[...], wu_ref[...], wd_ref[...]))


def _ffn(h, pre_g, wg, wu, wd, post_g, tm=1024, tf=512, tf_head=256):
    n, d = h.shape
    dff = wg.shape[1]
    tf, tf_head = min(tf, dff), min(tf_head, dff)
    vec1 = pl.BlockSpec((1, d), lambda j: (0, 0))
    head, wg16, wu16, wd16 = pl.pallas_call(
        functools.partial(_ffn_head_kernel, n_steps=dff // tf_head),
        name="ffn_head",
        grid=(dff // tf_head,),
        in_specs=[
            pl.BlockSpec((tm, d), lambda j: (0, 0)), vec1,
            pl.BlockSpec((d, tf_head), lambda j: (0, j)),
            pl.BlockSpec((d, tf_head), lambda j: (0, j)),
            pl.BlockSpec((tf_head, d), lambda j: (j, 0)),
            vec1,
        ],
        out_specs=[pl.BlockSpec((tm, d), lambda j: (0, 0)),
                   pl.BlockSpec((d, tf_head), lambda j: (0, j)),
                   pl.BlockSpec((d, tf_head), lambda j: (0, j)),
                   pl.BlockSpec((tf_head, d), lambda j: (j, 0))],
        out_shape=[jax.ShapeDtypeStruct((tm, d), F32), jax.ShapeDtypeStruct((d, dff), BF16),
                   jax.ShapeDtypeStruct((d, dff), BF16), jax.ShapeDtypeStruct((dff, d), BF16)],
        scratch_shapes=[pltpu.VMEM((tm, d), BF16)],
        compiler_params=pltpu.CompilerParams(
            dimension_semantics=("arbitrary",), vmem_limit_bytes=VMEM_LIMIT),
    )(h, pre_g, wg, wu, wd, post_g)

    n_steps = dff // tf
    n_copy = 1 << (min(n_steps, tm // LANE).bit_length() - 1)
    row = lambda i, j: (i, 0)
    vec = pl.BlockSpec((1, d), lambda i, j: (0, 0))
    wcol = lambda i, j: (0, jnp.where(i > 0, j, 0))
    return pl.pallas_call(
        functools.partial(_ffn_tail_kernel, n_steps=n_steps, n_copy=n_copy),
        name="ffn_tail",
        grid=(n // tm, n_steps),
        in_specs=[
            pl.BlockSpec((tm, d), row),
            pl.BlockSpec((tm // n_copy, d), lambda i, j: (jnp.where(i == 0, jnp.minimum(j, n_copy - 1), n_copy - 1), 0)),
            vec,
            pl.BlockSpec((d, tf), wcol),
            pl.BlockSpec((d, tf), wcol),
            pl.BlockSpec((tf, d), lambda i, j: (jnp.where(i > 0, j, 0), 0)),
            vec,
        ],
        out_specs=pl.BlockSpec((tm, d), row),
        out_shape=jax.ShapeDtypeStruct((n, d), F32),
        scratch_shapes=[pltpu.VMEM((tm, d), BF16)],
        compiler_params=pltpu.CompilerParams(
            dimension_semantics=("arbitrary", "arbitrary"), vmem_limit_bytes=VMEM_LIMIT),
    )(h, head, pre_g, wg16, wu16, wd16, post_g)


def _proj_kernel(pos_ref, h_ref, pre_ref, w_ref, gb_ref, invf_ref, za_ref, zb_ref, u_scr, cos_scr, s1_scr, s2_scr):
    j = pl.program_id(1)

    def prepare(rows):
        u_scr[rows, :] = _rms(h_ref[rows, :], pre_ref[...]).astype(BF16)
        sub = PROJ_CHUNK // ROPE_PACK
        packed = slice(rows.start // ROPE_PACK, rows.start // ROPE_PACK + sub)
        ang = pos_ref[packed, :].astype(F32) * invf_ref[...]
        cos, sin = jnp.cos(ang), jnp.sin(ang)
        lane = lax.broadcasted_iota(jnp.int32, ang.shape, 1)
        for k in range(ROPE_PACK):
            grp = slice(rows.start + k * sub, rows.start + (k + 1) * sub)
            c, s = (cos, sin) if k == 0 else (pltpu.roll(cos, LANE - k * ROPE_DIM, axis=1),
                                              pltpu.roll(sin, LANE - k * ROPE_DIM, axis=1))
            cos_scr[grp, :] = jnp.where(lane < ROPE_DIM, c, 1.0)
            s1_scr[grp, :] = jnp.where((lane >= ROPE_HALF) & (lane < ROPE_DIM), s, 0.0)
            s2_scr[grp, :] = jnp.where(lane < ROPE_HALF, -s, 0.0)

    def head(z, rows, hd, rope, mul=None):
        x = z[:, hd * LANE:(hd + 1) * LANE]
        if rope:
            x = (x * cos_scr[rows, :]
                 + pltpu.roll(x, ROPE_HALF, axis=1) * s1_scr[rows, :]
                 + pltpu.roll(x, LANE - ROPE_HALF, axis=1) * s2_scr[rows, :])
        return x if mul is None else x * mul

    def chunks(first=False):
        for c in range(u_scr.shape[0] // PROJ_CHUNK):
            rows = slice(c * PROJ_CHUNK, (c + 1) * PROJ_CHUNK)
            if first:
                prepare(rows)
            yield rows, _dot(u_scr[rows, :], w_ref[...])

    def store(ref, rope_heads, mul=None, first=False):
        for rows, z in chunks(first):
            for hd in range(PROJ_HEADS):
                ref[rows, hd * LANE:(hd + 1) * LANE] = head(z, rows, hd, hd < rope_heads, mul).astype(ref.dtype)

    @pl.when(j == 0)
    def _():
        store(za_ref, PROJ_HEADS, Q_SCALE, first=True)

    @pl.when((j > 0) & (j < AT_KS // PROJ_HEADS))
    def _():
        store(za_ref, PROJ_HEADS, Q_SCALE)

    @pl.when(j == AT_KS // PROJ_HEADS)
    def _():
        store(za_ref, PROJ_HEADS)

    @pl.when(j == AT_VS // PROJ_HEADS)
    def _():
        store(za_ref, 0)

    @pl.when((j >= PROJ_A_STEPS) & (j < PROJ_A_STEPS + BT_KC // PROJ_HEADS))
    def _():
        store(zb_ref, 0)

    @pl.when(j == PROJ_A_STEPS + BT_KC // PROJ_HEADS)
    def _():
        store(zb_ref, NSA_KV_HEADS)

    @pl.when(j == PROJ_A_STEPS + BT_GATE // PROJ_HEADS)
    def _():
        for rows, z in chunks():
            zb_ref[rows, :] = jax.nn.sigmoid(z + gb_ref[...])


def _proj(pos, h, pre_g, w, gb, invf, tm=1024):
    n, d = h.shape
    return pl.pallas_call(
        _proj_kernel,
        name="mix_proj",
        grid=(n // tm, PROJ_STEPS),
        in_specs=[
            pl.BlockSpec((tm // ROPE_PACK, LANE), lambda i, j: (i, 0)),
            pl.BlockSpec((tm, d), lambda i, j: (i, 0)),
            pl.BlockSpec((1, d), lambda i, j: (0, 0)),
            pl.BlockSpec((d, PROJ_TN), lambda i, j: (0, j)),
            pl.BlockSpec((1, PROJ_TN), lambda i, j: (0, 0)),
            pl.BlockSpec((1, LANE), lambda i, j: (0, 0)),
        ],
        out_specs=[pl.BlockSpec((tm, PROJ_TN), lambda i, j: (i, jnp.minimum(j, PROJ_A_STEPS - 1))),
                   pl.BlockSpec((tm, PROJ_TN), lambda i, j: (i, jnp.maximum(j - PROJ_A_STEPS, 0)))],
        out_shape=[jax.ShapeDtypeStruct((n, A_WIDTH), BF16), jax.ShapeDtypeStruct((n, B_WIDTH), F32)],
        scratch_shapes=[pltpu.VMEM((tm, d), BF16)] + [pltpu.VMEM((tm, LANE), F32)] * 3,
        compiler_params=pltpu.CompilerParams(
            dimension_semantics=("arbitrary", "arbitrary"), vmem_limit_bytes=VMEM_LIMIT),
    )(pos, h, pre_g, w, gb, invf)


def _cmp_kernel(x_ref, pos_ref, w1_ref, w2_ref, o_ref):
    ncp = o_ref.shape[0]
    hidden_w = w1_ref.shape[-1]
    acc_a = jnp.zeros((ncp, hidden_w), F32)
    acc_b = jnp.zeros((ncp, hidden_w), F32)
    for l in range(CMP_STRIDE):
        xl = x_ref[pl.ds(l, ncp, stride=CMP_STRIDE), :]
        acc_a += _dot((xl + pos_ref[l:l + 1, :]).astype(BF16), w1_ref[l])
        acc_b += _dot((xl + pos_ref[CMP_STRIDE + l:CMP_STRIDE + l + 1, :]).astype(BF16), w1_ref[CMP_STRIDE + l])
    hidden = acc_a + pltpu.roll(acc_b, ncp - 1, axis=0)
    o_ref[...] = _dot(jax.nn.gelu(hidden).astype(BF16), w2_ref[...])


def _compress(zb, pos_kv, w1, w2, batch, seq):
    ncp = seq // CMP_STRIDE
    hidden_w = w1.shape[-1]
    return pl.pallas_call(
        _cmp_kernel,
        name="compress",
        grid=(2, batch, NSA_KV_HEADS),
        in_specs=[
            pl.BlockSpec((seq, LANE), lambda k, b, g: (b, BT_KC + 2 * k + g)),
            pl.BlockSpec((None, CMP_LEN, HEAD_DIM), lambda k, b, g: (k, 0, 0)),
            pl.BlockSpec((None, CMP_LEN, HEAD_DIM, hidden_w), lambda k, b, g: (k, 0, 0, 0)),
            pl.BlockSpec((None, hidden_w, HEAD_DIM), lambda k, b, g: (k, 0, 0)),
        ],
        out_specs=pl.BlockSpec((None, None, None, ncp, HEAD_DIM), lambda k, b, g: (k, b, g, 0, 0)),
        out_shape=jax.ShapeDtypeStruct((2, batch, NSA_KV_HEADS, ncp, HEAD_DIM), F32),
        compiler_params=pltpu.CompilerParams(
            dimension_semantics=("arbitrary",) * 3, vmem_limit_bytes=VMEM_LIMIT),
    )(zb, pos_kv, w1, w2)


def _masked_exp(s, mask):
    s = s + jnp.where(mask, 0.0, -MASK_BIG)
    e = jnp.exp2(s - jnp.max(s, axis=-1, keepdims=True))
    return e, jnp.sum(e, axis=-1, keepdims=True)


def _attn_kernel(q_ref, kc_ref, vc_ref, ks_ref, vs_ref, kw_ref, vw_ref, gz_ref, ovl_ref, oh_ref, ga_ref, prev_ref,
                 o_ref, *, first_block):
    del prev_ref
    for sb in range(ATTN_SUB):
        _attn_block(q_ref, kc_ref, vc_ref, ks_ref, vs_ref, kw_ref, vw_ref, gz_ref, ovl_ref, oh_ref, ga_ref, o_ref,
                    rows=slice(sb * Q_BLOCK, (sb + 1) * Q_BLOCK),
                    block=first_block + pl.program_id(1) * ATTN_SUB + sb)


def _attn_block(q_ref, kc_ref, vc_ref, ks_ref, vs_ref, kw_ref, vw_ref, gz_ref, ovl_ref, oh_ref, ga_ref, o_ref,
                *, rows, block):
    G, R, tq = NSA_KV_HEADS, GQA_GROUP, Q_BLOCK
    n_keys = ks_ref.shape[0]
    top_k = min(SEL_TOPK, ovl_ref.shape[0])
    n_blk = n_keys // SEL_LEN
    n_cmp = min(kc_ref.shape[1], -(-(n_keys // CMP_STRIDE) // LANE) * LANE)
    t0 = block * tq
    t_idx = t0 + lax.broadcasted_iota(jnp.int32, (1, tq, 1), 1)
    gsl = lambda g: slice(g * HEAD_DIM, (g + 1) * HEAD_DIM)

    def stacked_q(g):
        q = q_ref[rows, g * R * HEAD_DIM:(g + 1) * R * HEAD_DIM]
        return jnp.concatenate([q[:, r * HEAD_DIM:(r + 1) * HEAD_DIM] for r in range(R)], axis=0)

    q4s = [stacked_q(g) for g in range(G)]

    wk = min(WINDOW + tq, n_keys)
    w0 = pl.multiple_of(jnp.maximum(t0 - WINDOW, 0), tq)
    kp = w0 + lax.broadcasted_iota(jnp.int32, (1, 1, wk), 2)
    win_mask = (kp <= t_idx) & (kp > t_idx - WINDOW)

    ones_col = jnp.ones((n_keys, LANE), BF16)

    def weights(s, m):
        return jnp.exp2((s - m).astype(BF16)).reshape(R * tq, s.shape[-1])

    def pv(p, v, ones):
        return _dot(p, jnp.concatenate([v, ones], axis=1))

    def normalised(acc):
        return (acc[:, :HEAD_DIM] * (1.0 / acc[:, HEAD_DIM:])).reshape(R, tq, HEAD_DIM)

    def window(g):
        s = _dot_nt(q4s[g], kw_ref[pl.ds(w0, wk), gsl(g)]).reshape(R, tq, wk)
        s = s + jnp.where(win_mask, 0.0, -MASK_BIG)
        p = weights(s, jnp.max(s, axis=-1, keepdims=True))
        return normalised(pv(p, vw_ref[pl.ds(w0, wk), gsl(g)], ones_col[0:wk, :]))

    def front(g):
        q4 = q4s[g]

        s = _dot_nt(q4, kc_ref[g, 0:n_cmp, :].astype(BF16)).reshape(R, tq, n_cmp)
        c_idx = lax.broadcasted_iota(jnp.int32, (1, 1, n_cmp), 2)
        e, l = _masked_exp(s, (c_idx * CMP_STRIDE + (CMP_LEN - 1)) <= t_idx)
        pc = e * (jnp.where(t_idx >= CMP_LEN - 1, 1.0, 0.0) / l)
        o_cmp = _dot(pc.reshape(R * tq, n_cmp).astype(BF16), vc_ref[g, 0:n_cmp, :].astype(BF16))
        o_cmp = o_cmp.reshape(R, tq, HEAD_DIM)
        if n_blk <= top_k:
            return q4, jnp.concatenate([q4, jnp.zeros_like(q4)], axis=1), o_cmp

        pcs = pc[0]
        for r in range(1, R):
            pcs = pcs + pc[r]
        hi = pcs.astype(BF16)
        lo = (pcs - hi.astype(F32)).astype(BF16)
        ovl = ovl_ref[0:n_blk, 0:n_cmp]
        imp = _dot_nt(ovl, hi) + _dot_nt(ovl, lo)
        blk = lax.broadcasted_iota(jnp.int32, (n_blk, tq), 0)
        tcol = t0 + lax.broadcasted_iota(jnp.int32, (n_blk, tq), 1)
        forced = (blk == lax.shift_right_logical(tcol, int(np.log2(SEL_LEN)))) | (blk == 0)
        v = jnp.where(forced, SEL_FORCE, jnp.where(blk * SEL_LEN <= tcol, imp, -SEL_FORCE))
        sub = lax.broadcasted_iota(jnp.int32, (SUBLANE, tq), 0)
        groups = [v[k:k + SUBLANE, :] for k in range(0, n_blk, SUBLANE)]
        ranks = [jnp.zeros((SUBLANE, tq), F32) for _ in groups]
        for sp in range(n_blk):
            other = v[sp:sp + 1, :]
            for gi, vg in enumerate(groups):
                first = gi * SUBLANE
                if first > sp:
                    beats = other >= vg
                elif first + SUBLANE - 1 <= sp:
                    beats = other > vg
                else:
                    beats = (other > vg) | ((other == vg) & (sub > sp - first))
                ranks[gi] = ranks[gi] + jnp.where(beats, 1.0, 0.0)
        rank = jnp.concatenate(ranks, axis=0)
        unsel = jnp.where(rank < float(top_k), 0.0, -1.0)
        unsel = jnp.concatenate([unsel, jnp.zeros((LANE - n_blk, tq), F32)], axis=0).T.astype(BF16)
        q_aug = jnp.concatenate([q4, jnp.concatenate([unsel] * R, axis=0)], axis=1)
        return q4, q_aug, o_cmp

    fronts = [front(g) for g in range(G)]

    n_full = n_keys - SEL_KT
    kp = n_full + lax.broadcasted_iota(jnp.int32, (1, 1, SEL_KT), 2)
    causal_bias = jnp.where(kp <= t_idx, 0.0, -MASK_BIG)

    def selected(g):
        k_aug = jnp.concatenate([ks_ref[:, gsl(g)], oh_ref[...]], axis=1)
        s = _dot_nt(fronts[g][1], k_aug).reshape(R, tq, n_keys)
        s_last = s[:, :, n_full:] + causal_bias
        m = jnp.max(s_last, axis=-1, keepdims=True)
        if n_full:
            s_full = s[:, :, :n_full]
            m = jnp.maximum(m, jnp.max(s_full, axis=-1, keepdims=True))
        acc = pv(weights(s_last, m), vs_ref[n_full:n_keys, gsl(g)], ones_col[n_full:n_keys, :])
        if n_full:
            acc = acc + pv(weights(s_full, m), vs_ref[0:n_full, gsl(g)], ones_col[0:n_full, :])
        return normalised(acc)

    o_sels = [selected(g) for g in range(G)]
    o_wins = [window(g) for g in range(G)]
    heads = []
    for g in range(G):
        o_cmp, o_sel, o_win = fronts[g][2], o_sels[g], o_wins[g]
        gz = gz_ref[rows, gsl(g)]
        for r in range(R):
            c = r * N_GATES
            heads.append(gz[:, c:c + 1] * o_cmp[r] + gz[:, c + 1:c + 2] * o_sel[r] + gz[:, c + 2:c + 3] * o_win[r])
    o_ref[rows, :] = _rms(jnp.concatenate(heads, axis=1), ga_ref[...]).astype(o_ref.dtype)


def _attention(za, zb, kcv, ovl_t, onehot, g_attn, batch, seq):
    ncp = seq // CMP_STRIDE
    n_sel = seq // SEL_LEN
    G = NSA_KV_HEADS
    blocks = SEL_KT // Q_BLOCK
    steps = blocks // ATTN_SUB
    za3 = za.reshape(batch, seq, A_WIDTH)
    zb3 = zb.reshape(batch, seq, B_WIDTH)
    out = jnp.zeros((batch, seq, NSA_WIDTH), BF16)
    for c in range(seq // SEL_KT):
        n_keys = (c + 1) * SEL_KT
        q_row = lambda b, i, c=c: (b, c * steps + i, 0)
        kv_spec = lambda tile: pl.BlockSpec((None, n_keys, G * LANE), lambda b, i: (b, 0, tile // G))
        operands = (za3, kcv, kcv, za3, za3, za3, za3, zb3, ovl_t, onehot, g_attn, out)
        out = pl.pallas_call(
            functools.partial(_attn_kernel, first_block=c * blocks),
            name=f"attention_{c}",
            grid=(batch, steps),
            in_specs=[
                pl.BlockSpec((None, ATTN_SUB * Q_BLOCK, NSA_WIDTH), q_row),
                pl.BlockSpec((None, None, G, ncp, HEAD_DIM), lambda b, i: (0, b, 0, 0, 0)),
                pl.BlockSpec((None, None, G, ncp, HEAD_DIM), lambda b, i: (1, b, 0, 0, 0)),
                kv_spec(AT_KS), kv_spec(AT_VS), kv_spec(AT_KW), kv_spec(AT_VW),
                pl.BlockSpec((None, ATTN_SUB * Q_BLOCK, G * LANE), lambda b, i, c=c: (b, c * steps + i, BT_GATE // G)),
                pl.BlockSpec((n_sel, ncp), lambda b, i: (0, 0)),
                pl.BlockSpec((n_keys, LANE), lambda b, i: (0, 0)),
                pl.BlockSpec((1, NSA_WIDTH), lambda b, i: (0, 0)),
                pl.BlockSpec(memory_space=pl.ANY),
            ],
            out_specs=pl.BlockSpec((None, ATTN_SUB * Q_BLOCK, NSA_WIDTH), q_row),
            out_shape=jax.ShapeDtypeStruct((batch, seq, NSA_WIDTH), BF16),
            input_output_aliases={len(operands) - 1: 0},
            compiler_params=pltpu.CompilerParams(
                dimension_semantics=("arbitrary",) * 2, vmem_limit_bytes=VMEM_LIMIT),
        )(*operands)
    return out.reshape(batch * seq, NSA_WIDTH)


def _lru_kernel(zx_ref, zy_ref, cw_ref, cb_ref, wa_ref, ba_ref, wi_ref, bi_ref, lam_ref, gr_ref,
                o_ref, xpad, a_scr, u_scr, h_scr, carry):
    tt, width = zx_ref.shape
    ti = pl.program_id(1)

    @pl.when(ti == 0)
    def _():
        xpad[0:SUBLANE, :] = jnp.zeros((SUBLANE, width), F32)
        carry[...] = jnp.zeros_like(carry)

    xb = zx_ref[...]
    xpad[SUBLANE:SUBLANE + tt, :] = xb
    xc = cb_ref[...] + cw_ref[CONV_WIDTH - 1:CONV_WIDTH, :] * xb
    for k in range(1, CONV_WIDTH):
        xc = xc + cw_ref[CONV_WIDTH - 1 - k:CONV_WIDTH - k, :] * xpad[SUBLANE - k:SUBLANE - k + tt, :]
    xpad[0:SUBLANE, :] = xb[tt - SUBLANE:tt, :]

    sp = jax.nn.softplus(-lam_ref[...])
    row8 = lax.broadcasted_iota(jnp.int32, (1, SUBLANE, LRU_BLOCK_DIM), 1)
    for hb in range(LRU_BLOCKS):
        sl = slice(hb * LRU_BLOCK_DIM, (hb + 1) * LRU_BLOCK_DIM)
        xs = xc[:, sl]
        xs16 = xs.astype(BF16)
        r = jax.nn.sigmoid(_dot(xs16, wa_ref[hb]) + ba_ref[:, sl])
        ig = jax.nn.sigmoid(_dot(xs16, wi_ref[hb]) + bi_ref[:, sl])
        log_a = -LRU_C * r * sp[:, sl]
        a = jnp.exp(log_a)
        u = jnp.sqrt(-jnp.tanh(log_a) * (a * a + 1.0)) * (ig * xs)
        a = a.reshape(tt // SUBLANE, SUBLANE, LRU_BLOCK_DIM)
        u = u.reshape(tt // SUBLANE, SUBLANE, LRU_BLOCK_DIM)
        for s in (1, 2, 4):
            ok = row8 >= s
            a_sh = pltpu.roll(a, s, axis=1)
            u_sh = pltpu.roll(u, s, axis=1)
            u = jnp.where(ok, u + a * u_sh, u)
            a = jnp.where(ok, a * a_sh, a)
        a_scr[:, sl] = a.reshape(tt, LRU_BLOCK_DIM)
        u_scr[:, sl] = u.reshape(tt, LRU_BLOCK_DIM)

    def group(gi, c):
        r0 = pl.multiple_of(gi * SUBLANE, SUBLANE)
        h = u_scr[pl.ds(r0, SUBLANE), :] + a_scr[pl.ds(r0, SUBLANE), :] * c
        h_scr[pl.ds(r0, SUBLANE), :] = h
        return jnp.broadcast_to(h[SUBLANE - 1:SUBLANE, :], (SUBLANE, width))

    c = lax.fori_loop(0, tt // SUBLANE, group, carry[...], unroll=4)
    carry[...] = c
    o_ref[...] = _rms(h_scr[...] * jax.nn.gelu(zy_ref[...]), gr_ref[...]).astype(o_ref.dtype)


def _rglru(zb, conv_w, conv_b, w_a, b_a, w_i, b_i, lam, g_rec, batch, seq, tt=256):
    nt = seq // tt
    wt = LRU_WIDTH // LANE
    full = lambda a: pl.BlockSpec(a.shape, lambda b, t: (0,) * a.ndim)
    zspec = lambda tile: pl.BlockSpec((tt, LRU_WIDTH), lambda b, t: (b * nt + t, tile // wt))
    return pl.pallas_call(
        _lru_kernel,
        name="rglru",
        grid=(batch, nt),
        in_specs=[zspec(BT_X), zspec(BT_Y), full(conv_w), full(conv_b), full(w_a), full(b_a),
                  full(w_i), full(b_i), full(lam), full(g_rec)],
        out_specs=pl.BlockSpec((tt, LRU_WIDTH), lambda b, t: (b * nt + t, 0)),
        out_shape=jax.ShapeDtypeStruct((batch * seq, LRU_WIDTH), BF16),
        scratch_shapes=[pltpu.VMEM((tt + SUBLANE, LRU_WIDTH), F32), pltpu.VMEM((tt, LRU_WIDTH), F32),
                        pltpu.VMEM((tt, LRU_WIDTH), F32), pltpu.VMEM((tt, LRU_WIDTH), F32),
                        pltpu.VMEM((SUBLANE, LRU_WIDTH), F32)],
        compiler_params=pltpu.CompilerParams(
            dimension_semantics=("arbitrary", "arbitrary"), vmem_limit_bytes=VMEM_LIMIT),
    )(zb, zb, conv_w, conv_b, w_a, b_a, w_i, b_i, lam, g_rec)


def _resident_bf16(w_ref, w16_scr):
    @pl.when(pl.program_id(0) == 0)
    def _():
        w16_scr[...] = w_ref[...].astype(BF16)


def _resident(a):
    return pl.BlockSpec(a.shape, lambda i: (0,) * a.ndim, pipeline_mode=pl.Buffered(1))


def _out_kernel(oa_ref, or_ref, w_ref, post_ref, h_ref, o_ref, w16_scr):
    _resident_bf16(w_ref, w16_scr)
    for c in range(h_ref.shape[0] // OUT_CHUNK):
        rows = slice(c * OUT_CHUNK, (c + 1) * OUT_CHUNK)
        y = (_dot(oa_ref[rows, :], w16_scr[0:NSA_WIDTH, :])
             + _dot(or_ref[rows, :], w16_scr[NSA_WIDTH:NSA_WIDTH + LRU_WIDTH, :]))
        o_ref[rows, :] = h_ref[rows, :] + _rms(y, post_ref[...])


def _out_proj(o_attn, o_rec, w_out, post_g, h, tm=512):
    n, d = h.shape
    row = lambda i: (i, 0)
    return pl.pallas_call(
        _out_kernel,
        name="out_proj",
        grid=(n // tm,),
        in_specs=[pl.BlockSpec((tm, NSA_WIDTH), row), pl.BlockSpec((tm, LRU_WIDTH), row),
                  _resident(w_out), _resident(post_g), pl.BlockSpec((tm, d), row)],
        out_specs=pl.BlockSpec((tm, d), row),
        out_shape=jax.ShapeDtypeStruct((n, d), F32),
        scratch_shapes=[pltpu.VMEM(w_out.shape, BF16)],
        compiler_params=pltpu.CompilerParams(
            dimension_semantics=("arbitrary",), vmem_limit_bytes=VMEM_LIMIT),
    )(o_attn, o_rec, w_out, post_g, h)


def _ple_kernel(h_ref, p_ref, pre_ref, wg_ref, wp_ref, post_ref, o_ref, wg16_scr, wp16_scr):
    _resident_bf16(wg_ref, wg16_scr)
    _resident_bf16(wp_ref, wp16_scr)
    for c in range(h_ref.shape[0] // OUT_CHUNK):
        rows = slice(c * OUT_CHUNK, (c + 1) * OUT_CHUNK)
        h = h_ref[rows, :]
        gate = jax.nn.sigmoid(_dot(_rms(h, pre_ref[...]).astype(BF16), wg16_scr[...]))
        pp = _dot(p_ref[rows, :].astype(BF16), wp16_scr[...])
        o_ref[rows, :] = h + _rms(gate * pp, post_ref[...])


def _ple(h, p, pre_g, w_gate, w_proj, post_g, tm=512):
    n, d = h.shape
    row = lambda i: (i, 0)
    return pl.pallas_call(
        _ple_kernel,
        name="ple",
        grid=(n // tm,),
        in_specs=[pl.BlockSpec((tm, d), row), pl.BlockSpec((tm, p.shape[1]), row),
                  _resident(pre_g), _resident(w_gate), _resident(w_proj), _resident(post_g)],
        out_specs=pl.BlockSpec((tm, d), row),
        out_shape=jax.ShapeDtypeStruct((n, d), F32),
        scratch_shapes=[pltpu.VMEM(w_gate.shape, BF16), pltpu.VMEM(w_proj.shape, BF16)],
        compiler_params=pltpu.CompilerParams(
            dimension_semantics=("arbitrary",), vmem_limit_bytes=VMEM_LIMIT),
    )(h, p, pre_g, w_gate, w_proj, post_g)


def _selection_constants(seq):
    ncp = seq // CMP_STRIDE
    n_cmp = (seq - CMP_LEN) // CMP_STRIDE + 1
    n_sel = seq // SEL_LEN
    c = np.arange(ncp)[None, :]
    s = np.arange(n_sel)[:, None]
    ovl_t = ((c * CMP_STRIDE < s * SEL_LEN + SEL_LEN) & (c * CMP_STRIDE + CMP_LEN - 1 >= s * SEL_LEN) & (c < n_cmp))
    onehot = (np.arange(seq)[:, None] // SEL_LEN) == np.arange(LANE)[None, :]
    return jnp.asarray(ovl_t, BF16), jnp.asarray(onehot * MASK_BIG, BF16)


def _pad_cols(a, width):
    return jnp.pad(a, ((0, 0), (0, width - a.shape[1])))


def kernel(x, p, positions, ff1_pre_g, ff1_post_g, ff1_w_gate, ff1_w_up, ff1_w_down, mix_pre_g, mix_post_g, w_in, cmp_pos_k, cmp_pos_v, cmp_k_w1, cmp_k_w2, cmp_v_w1, cmp_v_w2, nsa_gate_b, conv_w, conv_b, rg_w_a, rg_b_a, rg_w_i, rg_b_i, rg_lambda, attn_out_g, rec_out_g, w_out, ff2_pre_g, ff2_post_g, ff2_w_gate, ff2_w_up, ff2_w_down, ple_pre_g, ple_post_g, w_ple_gate, w_ple_proj):
    batch, seq, d = x.shape
    depth = p.shape[0]
    n = batch * seq
    vec = lambda a: a.reshape(1, -1)
    gate_cols = NSA_HEADS * N_GATES
    grp_gates = GQA_GROUP * N_GATES
    o_q, o_kc, o_vc, o_ks, o_vs, o_kw, o_vw = (NSA_WIDTH * 0,) + tuple(NSA_WIDTH + k * KV_WIDTH for k in range(6))
    o_g = NSA_WIDTH + 6 * KV_WIDTH
    o_x = o_g + gate_cols
    o_y = o_x + LRU_WIDTH

    half = jnp.arange(ROPE_HALF, dtype=F32)
    inv_freq = ROPE_THETA ** (-half / ROPE_HALF)
    invf = jnp.tile(jnp.concatenate([inv_freq, inv_freq]), ROPE_PACK).reshape(1, LANE)
    ovl_t, onehot = _selection_constants(seq)
    sub = PROJ_CHUNK // ROPE_PACK
    pos = positions.reshape(n // PROJ_CHUNK, ROPE_PACK, sub).transpose(0, 2, 1).reshape(n // ROPE_PACK, ROPE_PACK)
    pos = jnp.repeat(pos, ROPE_DIM, axis=1)

    h = x.reshape(n, d)
    for i in range(depth):
        wi = w_in[i]
        cols = lambda o, w: wi[:, o:o + w]
        w_in_z = jnp.concatenate(
            [cols(o_q, NSA_WIDTH), cols(o_ks, KV_WIDTH), cols(o_kw, KV_WIDTH), cols(o_vs, KV_WIDTH),
             cols(o_vw, KV_WIDTH), cols(o_x, LRU_WIDTH), cols(o_y, LRU_WIDTH), cols(o_kc, KV_WIDTH),
             cols(o_vc, KV_WIDTH)]
            + [_pad_cols(cols(o_g + g * grp_gates, grp_gates), LANE) for g in range(NSA_KV_HEADS)]
            + [jnp.zeros((d, B_WIDTH - (BT_GATE + NSA_KV_HEADS) * LANE), F32)],
            axis=1).astype(BF16)
        gb = nsa_gate_b[i].reshape(NSA_KV_HEADS, grp_gates)
        gate_bias = _pad_cols(jnp.pad(gb, ((0, 0), (0, LANE - grp_gates))).reshape(1, NSA_KV_HEADS * LANE), PROJ_TN)

        h = _ffn(h, vec(ff1_pre_g[i]), ff1_w_gate[i], ff1_w_up[i], ff1_w_down[i], vec(ff1_post_g[i]))
        za, zb = _proj(pos, h, vec(mix_pre_g[i]), w_in_z, gate_bias, invf)
        kcv = _compress(
            zb, jnp.stack([cmp_pos_k[i], cmp_pos_v[i]]),
            jnp.stack([cmp_k_w1[i], cmp_v_w1[i]]).reshape(2, CMP_LEN, HEAD_DIM, -1).astype(BF16),
            jnp.stack([cmp_k_w2[i], cmp_v_w2[i]]).astype(BF16), batch, seq)
        o_attn = _attention(za, zb, kcv, ovl_t, onehot, vec(attn_out_g[i]), batch, seq)
        o_rec = _rglru(zb, conv_w[i], vec(conv_b[i]), rg_w_a[i].astype(BF16), vec(rg_b_a[i]),
                       rg_w_i[i].astype(BF16), vec(rg_b_i[i]), vec(rg_lambda[i]), vec(rec_out_g[i]), batch, seq)
        h = _out_proj(o_attn, o_rec, w_out[i], vec(mix_post_g[i]), h)
        h = _ffn(h, vec(ff2_pre_g[i]), ff2_w_gate[i], ff2_w_up[i], ff2_w_down[i], vec(ff2_post_g[i]))
        h = _ple(h, p[i].reshape(n, -1), vec(ple_pre_g[i]), w_ple_gate[i], w_ple_proj[i], vec(ple_post_g[i]))
    return h.reshape(batch, seq, d)
```
